```python
import jax, jax.numpy as jnp
from jax import lax
import numpy as np

D_MODEL = 1024
BATCH = 8
SEQ = 16384
DEPTH = 4

MEM_LEN = 256
N_HEADS_MLA = 8
QK_NOPE_DIM = 64
QK_ROPE_DIM = 32
QK_HEAD_DIM = QK_NOPE_DIM + QK_ROPE_DIM
V_HEAD_DIM = 64
Q_LORA_RANK = 3 * D_MODEL // 8
KV_LORA_RANK = D_MODEL // 4
MLA_WIDTH = N_HEADS_MLA * V_HEAD_DIM
CONV_WIDTH = D_MODEL // 2
CONV_K = 3
N_HEADS_MEM = 4
MEM_HEAD_DIM = 128
MEM_WIDTH = N_HEADS_MEM * MEM_HEAD_DIM

N_BRANCH = 3
ROPE_BASE = 10000.0
Q_BLOCK = 128
EPS = 1e-6

IN_SIZES = (Q_LORA_RANK, KV_LORA_RANK, QK_ROPE_DIM,
            CONV_WIDTH, CONV_WIDTH, CONV_WIDTH,
            MEM_WIDTH,
            MLA_WIDTH, CONV_WIDTH, MEM_WIDTH,
            N_BRANCH * D_MODEL)
IN_WIDTH = sum(IN_SIZES)

kernel_name = 'hybrid_mla_shortconv_memxattn_encoder'


def rmsnorm(t, g):
    tf = t.astype(jnp.float32)
    tf = tf * lax.rsqrt(jnp.mean(tf * tf, axis=-1, keepdims=True) + EPS)
    return tf.astype(t.dtype) * g


def split_cols(t, sizes):
    return jnp.split(t, np.cumsum(sizes)[:-1].tolist(), axis=-1)


def rope_tables(positions, dtype):
    inv_freq = ROPE_BASE ** (-jnp.arange(0, QK_ROPE_DIM, 2, dtype=jnp.float32) / QK_ROPE_DIM)
    ang = positions.astype(jnp.float32)[..., None] * inv_freq
    return (jnp.cos(ang)[:, :, None, :].astype(dtype),
            jnp.sin(ang)[:, :, None, :].astype(dtype))


def rope_tail(t, cos, sin):
    t_nope, t1, t2 = split_cols(t, (QK_NOPE_DIM, QK_ROPE_DIM // 2, QK_ROPE_DIM // 2))
    return jnp.concatenate([t_nope, t1 * cos - t2 * sin, t2 * cos + t1 * sin], axis=-1)


def blocked_bidirectional_attention(q, k, v):
    B, S, H, Dh = q.shape
    nblk = S // Q_BLOCK
    qb = q.reshape(B, nblk, Q_BLOCK, H, Dh).transpose(1, 0, 2, 3, 4)
    scale = Dh ** -0.5

    def one_block(q_blk):
        s = jnp.einsum('bqhd,bkhd->bhqk', q_blk, k).astype(jnp.float32) * scale
        p = jax.nn.softmax(s, axis=-1).astype(v.dtype)
        return jnp.einsum('bhqk,bkhd->bqhd', p, v)

    out = lax.map(one_block, qb)
    return out.transpose(1, 0, 2, 3, 4).reshape(B, S, H * v.shape[-1])


def centred_short_conv(z, w, b):
    out = lax.conv_general_dilated(
        z, w[:, None, :], window_strides=(1,),
        padding=((CONV_K // 2, CONV_K // 2),),
        dimension_numbers=('NWC', 'WIO', 'NWC'),
        feature_group_count=z.shape[-1])
    return out + b


def hybrid_layer(x, mem, cos, sin, norm_g, w_in, b_gate, q_norm_g, w_uq, kv_norm_g, w_ukv,
                 q_head_g, k_head_g, conv_w, conv_b, mem_norm_g, w_mkv, mem_q_g, mem_k_g,
                 w_br_attn, w_br_conv, w_br_mem, w_out):
    B, S, _ = x.shape
    M = mem.shape[1]
    h = rmsnorm(x, norm_g)
    proj = h @ w_in
    (q_lat, kv_lat, k_pe, c_b, c_c, c_u, q_mem,
     g_attn, g_conv, g_mem, r) = split_cols(proj, IN_SIZES)

    q = (rmsnorm(q_lat, q_norm_g) @ w_uq).reshape(B, S, N_HEADS_MLA, QK_HEAD_DIM)
    kv = (rmsnorm(kv_lat, kv_norm_g) @ w_ukv).reshape(B, S, N_HEADS_MLA, QK_NOPE_DIM + V_HEAD_DIM)
    k_nope, v = split_cols(kv, (QK_NOPE_DIM, V_HEAD_DIM))
    k_rope = jnp.broadcast_to(k_pe[:, :, None, :], (B, S, N_HEADS_MLA, QK_ROPE_DIM))
    k = jnp.concatenate([k_nope, k_rope], axis=-1)
    q = rope_tail(rmsnorm(q, q_head_g), cos, sin)
    k = rope_tail(rmsnorm(k, k_head_g), cos, sin)
    o_attn = blocked_bidirectional_attention(q, k, v) * jax.nn.silu(g_attn)

    o_conv = c_b * centred_short_conv(c_c * c_u, conv_w, conv_b) * jax.nn.silu(g_conv)

    mkv = (rmsnorm(mem, mem_norm_g) @ w_mkv).reshape(B, M, N_HEADS_MEM, 2 * MEM_HEAD_DIM)
    m_k, m_v = split_cols(mkv, (MEM_HEAD_DIM, MEM_HEAD_DIM))
    mq = rmsnorm(q_mem.reshape(B, S, N_HEADS_MEM, MEM_HEAD_DIM), mem_q_g)
    m_k = rmsnorm(m_k, mem_k_g)
    s = jnp.einsum('bshd,bmhd->bhsm', mq, m_k).astype(jnp.float32) * (MEM_HEAD_DIM ** -0.5)
    p = jax.nn.softmax(s, axis=-1).astype(m_v.dtype)
    o_mem = jnp.einsum('bhsm,bmhd->bshd', p, m_v).reshape(B, S, MEM_WIDTH) * jax.nn.silu(g_mem)

    r_attn, r_conv, r_mem = split_cols(jax.nn.sigmoid(r + b_gate), (D_MODEL, D_MODEL, D_MODEL))
    y = r_attn * (o_attn @ w_br_attn) + r_conv * (o_conv @ w_br_conv) + r_mem * (o_mem @ w_br_mem)
    return x + y @ w_out


def _fwd_setup_inputs(seed: int = 0) -> dict:
    key = jax.random.key(seed)
    ks = jax.random.split(key, 24)

    def nrm(k, shape, scale):
        return jax.random.normal(k, shape, jnp.float32) * scale

    def gain(k, shape):
        return 1.0 + 0.1 * jax.random.normal(k, shape, jnp.float32)

    x = nrm(ks[0], (BATCH, SEQ, D_MODEL), 1.0)
    mem = nrm(ks[1], (BATCH, MEM_LEN, D_MODEL), 1.0)
    offset = jax.random.randint(ks[2], (BATCH, 1), 0, 1024, dtype=jnp.int32)
    positions = jnp.arange(SEQ, dtype=jnp.int32)[None, :] + offset
    return {
        'x': x,
        'mem': mem,
        'positions': positions,
        'norm_g': gain(ks[3], (DEPTH, D_MODEL)),
        'w_in': nrm(ks[4], (DEPTH, D_MODEL, IN_WIDTH), D_MODEL ** -0.5),
        'b_gate': nrm(ks[5], (DEPTH, N_BRANCH * D_MODEL), 0.1),
        'q_norm_g': gain(ks[6], (DEPTH, Q_LORA_RANK)),
        'w_uq': nrm(ks[7], (DEPTH, Q_LORA_RANK, N_HEADS_MLA * QK_HEAD_DIM), Q_LORA_RANK ** -0.5),
        'kv_norm_g': gain(ks[8], (DEPTH, KV_LORA_RANK)),
        'w_ukv': nrm(ks[9], (DEPTH, KV_LORA_RANK, N_HEADS_MLA * (QK_NOPE_DIM + V_HEAD_DIM)), KV_LORA_RANK ** -0.5),
        'q_head_g': gain(ks[10], (DEPTH, QK_HEAD_DIM)),
        'k_head_g': gain(ks[11], (DEPTH, QK_HEAD_DIM)),
        'conv_w': nrm(ks[12], (DEPTH, CONV_K, CONV_WIDTH), CONV_K ** -0.5),
        'conv_b': nrm(ks[13], (DEPTH, CONV_WIDTH), 0.1),
        'mem_norm_g': gain(ks[14], (DEPTH, D_MODEL)),
        'w_mkv': nrm(ks[15], (DEPTH, D_MODEL, 2 * MEM_WIDTH), D_MODEL ** -0.5),
        'mem_q_g': gain(ks[16], (DEPTH, MEM_HEAD_DIM)),
        'mem_k_g': gain(ks[17], (DEPTH, MEM_HEAD_DIM)),
        'w_br_attn': nrm(ks[18], (DEPTH, MLA_WIDTH, D_MODEL), MLA_WIDTH ** -0.5),
        'w_br_conv': nrm(ks[19], (DEPTH, CONV_WIDTH, D_MODEL), CONV_WIDTH ** -0.5),
        'w_br_mem': nrm(ks[20], (DEPTH, MEM_WIDTH, D_MODEL), MEM_WIDTH ** -0.5),
        'w_out': nrm(ks[21], (DEPTH, D_MODEL, D_MODEL), D_MODEL ** -0.5),
    }


def _fwd_reference(x, mem, positions, norm_g, w_in, b_gate, q_norm_g, w_uq, kv_norm_g, w_ukv,
              q_head_g, k_head_g, conv_w, conv_b, mem_norm_g, w_mkv, mem_q_g, mem_k_g,
              w_br_attn, w_br_conv, w_br_mem, w_out):
    cos, sin = rope_tables(positions, x.dtype)
    for i in range(DEPTH):
        x = hybrid_layer(x, mem, cos, sin, norm_g[i], w_in[i], b_gate[i], q_norm_g[i], w_uq[i],
                         kv_norm_g[i], w_ukv[i], q_head_g[i], k_head_g[i], conv_w[i], conv_b[i],
                         mem_norm_g[i], w_mkv[i], mem_q_g[i], mem_k_g[i],
                         w_br_attn[i], w_br_conv[i], w_br_mem[i], w_out[i])
    return x


import jax as _jax
import jax.numpy as _jnp

TWIN_FORMAT = 'train_step'
FWD_PARAMS = ['x', 'mem', 'positions', 'norm_g', 'w_in', 'b_gate', 'q_norm_g', 'w_uq', 'kv_norm_g', 'w_ukv', 'q_head_g', 'k_head_g', 'conv_w', 'conv_b', 'mem_norm_g', 'w_mkv', 'mem_q_g', 'mem_k_g', 'w_br_attn', 'w_br_conv', 'w_br_mem', 'w_out']
TWIN_WEIGHTS = ['norm_g', 'w_in', 'b_gate', 'q_norm_g', 'w_uq', 'kv_norm_g', 'w_ukv', 'q_head_g', 'k_head_g', 'conv_w', 'conv_b', 'mem_norm_g', 'w_mkv', 'mem_q_g', 'mem_k_g', 'w_br_attn', 'w_br_conv', 'w_br_mem', 'w_out']
TWIN_DIFF_INPUT = 'x'
TWIN_INPUTS = ['x', 'mem', 'positions', 'norm_g', 'w_in', 'b_gate', 'q_norm_g', 'w_uq', 'kv_norm_g', 'w_ukv', 'q_head_g', 'k_head_g', 'conv_w', 'conv_b', 'mem_norm_g', 'w_mkv', 'mem_q_g', 'mem_k_g', 'w_br_attn', 'w_br_conv', 'w_br_mem', 'w_out', 'loss_target', 'm_norm_g', 'm_w_in', 'm_b_gate', 'm_q_norm_g', 'm_w_uq', 'm_kv_norm_g', 'm_w_ukv', 'm_q_head_g', 'm_k_head_g', 'm_conv_w', 'm_conv_b', 'm_mem_norm_g', 'm_w_mkv', 'm_mem_q_g', 'm_mem_k_g', 'm_w_br_attn', 'm_w_br_conv', 'm_w_br_mem', 'm_w_out', 'v_norm_g', 'v_w_in', 'v_b_gate', 'v_q_norm_g', 'v_w_uq', 'v_kv_norm_g', 'v_w_ukv', 'v_q_head_g', 'v_k_head_g', 'v_conv_w', 'v_conv_b', 'v_mem_norm_g', 'v_w_mkv', 'v_mem_q_g', 'v_mem_k_g', 'v_w_br_attn', 'v_w_br_conv', 'v_w_br_mem', 'v_w_out']
TWIN_OUTPUTS = ['loss', 'grad_x', 'grad_norm_g', 'grad_w_in', 'grad_b_gate', 'grad_q_norm_g', 'grad_w_uq', 'grad_kv_norm_g', 'grad_w_ukv', 'grad_q_head_g', 'grad_k_head_g', 'grad_conv_w', 'grad_conv_b', 'grad_mem_norm_g', 'grad_w_mkv', 'grad_mem_q_g', 'grad_mem_k_g', 'grad_w_br_attn', 'grad_w_br_conv', 'grad_w_br_mem', 'grad_w_out', 'delta_norm_g', 'delta_w_in', 'delta_b_gate', 'delta_q_norm_g', 'delta_w_uq', 'delta_kv_norm_g', 'delta_w_ukv', 'delta_q_head_g', 'delta_k_head_g', 'delta_conv_w', 'delta_conv_b', 'delta_mem_norm_g', 'delta_w_mkv', 'delta_mem_q_g', 'delta_mem_k_g', 'delta_w_br_attn', 'delta_w_br_conv', 'delta_w_br_mem', 'delta_w_out', 'new_m_norm_g', 'new_m_w_in', 'new_m_b_gate', 'new_m_q_norm_g', 'new_m_w_uq', 'new_m_kv_norm_g', 'new_m_w_ukv', 'new_m_q_head_g', 'new_m_k_head_g', 'new_m_conv_w', 'new_m_conv_b', 'new_m_mem_norm_g', 'new_m_w_mkv', 'new_m_mem_q_g', 'new_m_mem_k_g', 'new_m_w_br_attn', 'new_m_w_br_conv', 'new_m_w_br_mem', 'new_m_w_out', 'new_v_norm_g', 'new_v_w_in', 'new_v_b_gate', 'new_v_q_norm_g', 'new_v_w_uq', 'new_v_kv_norm_g', 'new_v_w_ukv', 'new_v_q_head_g', 'new_v_k_head_g', 'new_v_conv_w', 'new_v_conv_b', 'new_v_mem_norm_g', 'new_v_w_mkv', 'new_v_mem_q_g', 'new_v_mem_k_g', 'new_v_w_br_attn', 'new_v_w_br_conv', 'new_v_w_br_mem', 'new_v_w_out']
TWIN_LEAF_KINDS = {'loss': 'loss', 'grad_x': 'grad_x', 'grad_norm_g': 'grad_w', 'grad_w_in': 'grad_w', 'grad_b_gate': 'grad_w', 'grad_q_norm_g': 'grad_w', 'grad_w_uq': 'grad_w', 'grad_kv_norm_g': 'grad_w', 'grad_w_ukv': 'grad_w', 'grad_q_head_g': 'grad_w', 'grad_k_head_g': 'grad_w', 'grad_conv_w': 'grad_w', 'grad_conv_b': 'grad_w', 'grad_mem_norm_g': 'grad_w', 'grad_w_mkv': 'grad_w', 'grad_mem_q_g': 'grad_w', 'grad_mem_k_g': 'grad_w', 'grad_w_br_attn': 'grad_w', 'grad_w_br_conv': 'grad_w', 'grad_w_br_mem': 'grad_w', 'grad_w_out': 'grad_w', 'delta_norm_g': 'delta_w', 'delta_w_in': 'delta_w', 'delta_b_gate': 'delta_w', 'delta_q_norm_g': 'delta_w', 'delta_w_uq': 'delta_w', 'delta_kv_norm_g': 'delta_w', 'delta_w_ukv': 'delta_w', 'delta_q_head_g': 'delta_w', 'delta_k_head_g': 'delta_w', 'delta_conv_w': 'delta_w', 'delta_conv_b': 'delta_w', 'delta_mem_norm_g': 'delta_w', 'delta_w_mkv': 'delta_w', 'delta_mem_q_g': 'delta_w', 'delta_mem_k_g': 'delta_w', 'delta_w_br_attn': 'delta_w', 'delta_w_br_conv': 'delta_w', 'delta_w_br_mem': 'delta_w', 'delta_w_out': 'delta_w', 'new_m_norm_g': 'new_m', 'new_m_w_in': 'new_m', 'new_m_b_gate': 'new_m', 'new_m_q_norm_g': 'new_m', 'new_m_w_uq': 'new_m', 'new_m_kv_norm_g': 'new_m', 'new_m_w_ukv': 'new_m', 'new_m_q_head_g': 'new_m', 'new_m_k_head_g': 'new_m', 'new_m_conv_w': 'new_m', 'new_m_conv_b': 'new_m', 'new_m_mem_norm_g': 'new_m', 'new_m_w_mkv': 'new_m', 'new_m_mem_q_g': 'new_m', 'new_m_mem_k_g': 'new_m', 'new_m_w_br_attn': 'new_m', 'new_m_w_br_conv': 'new_m', 'new_m_w_br_mem': 'new_m', 'new_m_w_out': 'new_m', 'new_v_norm_g': 'new_v', 'new_v_w_in': 'new_v', 'new_v_b_gate': 'new_v', 'new_v_q_norm_g': 'new_v', 'new_v_w_uq': 'new_v', 'new_v_kv_norm_g': 'new_v', 'new_v_w_ukv': 'new_v', 'new_v_q_head_g': 'new_v', 'new_v_k_head_g': 'new_v', 'new_v_conv_w': 'new_v', 'new_v_conv_b': 'new_v', 'new_v_mem_norm_g': 'new_v', 'new_v_w_mkv': 'new_v', 'new_v_mem_q_g': 'new_v', 'new_v_mem_k_g': 'new_v', 'new_v_w_br_attn': 'new_v', 'new_v_w_br_conv': 'new_v', 'new_v_w_br_mem': 'new_v', 'new_v_w_out': 'new_v'}


def _forward(args):
    return _fwd_reference(*[args[k] for k in FWD_PARAMS])


def _output_shape():
    def fwd():
        inp = _fwd_setup_inputs(0)
        return _fwd_reference(*[inp[k] for k in FWD_PARAMS])
    out = _jax.eval_shape(fwd)
    return out.shape, out.dtype

N_MICROBATCH = 1
ADAM_LR = 0.001
ADAM_B1 = 0.9
ADAM_B2 = 0.999
ADAM_EPS = 1e-08
ADAM_WD = 0.01
ADAM_STEP = 10
PER_EXAMPLE_BATCH_AXIS = {'x': 0, 'mem': 0, 'positions': 0, 'loss_target': 0}
SHARED_INPUTS = []
_WEIGHT_DTYPES = {'norm_g': _jnp.float32, 'w_in': _jnp.float32, 'b_gate': _jnp.float32, 'q_norm_g': _jnp.float32, 'w_uq': _jnp.float32, 'kv_norm_g': _jnp.float32, 'w_ukv': _jnp.float32, 'q_head_g': _jnp.float32, 'k_head_g': _jnp.float32, 'conv_w': _jnp.float32, 'conv_b': _jnp.float32, 'mem_norm_g': _jnp.float32, 'w_mkv': _jnp.float32, 'mem_q_g': _jnp.float32, 'mem_k_g': _jnp.float32, 'w_br_attn': _jnp.float32, 'w_br_conv': _jnp.float32, 'w_br_mem': _jnp.float32, 'w_out': _jnp.float32}
MOMENT_SCALE = {'norm_g': 5.811364e+01, 'w_in': 5.811746e-01, 'b_gate': 2.704161e+00, 'q_norm_g': 8.333558e-02, 'w_uq': 6.132450e-02, 'kv_norm_g': 1.795090e-01, 'w_ukv': 7.155106e-02, 'q_head_g': 2.788630e-01, 'k_head_g': 2.782695e-01, 'conv_w': 1.422857e+01, 'conv_b': 1.857876e+00, 'mem_norm_g': 1.036611e-01, 'w_mkv': 7.648650e-02, 'mem_q_g': 8.823434e-01, 'mem_k_g': 8.857394e-01, 'w_br_attn': 5.511148e-02, 'w_br_conv': 6.888968e-01, 'w_br_mem': 5.778067e-02, 'w_out': 6.211973e-01}


def _to_microbatches(a, axis):
    t = _jnp.moveaxis(a, axis, 0)
    t = t.reshape((N_MICROBATCH, t.shape[0] // N_MICROBATCH) + t.shape[1:])
    return _jnp.moveaxis(t, 1, axis + 1)


def setup_inputs(seed: int = 0) -> dict:
    inp = _fwd_setup_inputs(seed)
    key = _jax.random.fold_in(_jax.random.key(seed), 7919)
    shape, _ = _output_shape()
    out = dict(inp)
    out["loss_target"] = _jax.random.normal(_jax.random.fold_in(key, 0), shape, _jnp.float32)
    for i, name in enumerate(TWIN_WEIGHTS):
        w = inp[name].astype(_jnp.float32)
        if MOMENT_SCALE is None:
            s = _jnp.sqrt(_jnp.mean(_jnp.square(w)) + 1e-30)
        else:
            s = MOMENT_SCALE[name]
        km, kv = _jax.random.split(_jax.random.fold_in(key, i + 1))
        out[name] = w
        out["m_" + name] = s * _jax.random.normal(km, w.shape, _jnp.float32)
        out["v_" + name] = (s * s) * _jax.random.uniform(kv, w.shape, _jnp.float32, 0.5, 1.5)
    if N_MICROBATCH > 1:
        for name, axis in PER_EXAMPLE_BATCH_AXIS.items():
            out[name] = _to_microbatches(out[name], axis)
    return {'x': out['x'], 'mem': out['mem'], 'positions': out['positions'], 'norm_g': out['norm_g'], 'w_in': out['w_in'], 'b_gate': out['b_gate'], 'q_norm_g': out['q_norm_g'], 'w_uq': out['w_uq'], 'kv_norm_g': out['kv_norm_g'], 'w_ukv': out['w_ukv'], 'q_head_g': out['q_head_g'], 'k_head_g': out['k_head_g'], 'conv_w': out['conv_w'], 'conv_b': out['conv_b'], 'mem_norm_g': out['mem_norm_g'], 'w_mkv': out['w_mkv'], 'mem_q_g': out['mem_q_g'], 'mem_k_g': out['mem_k_g'], 'w_br_attn': out['w_br_attn'], 'w_br_conv': out['w_br_conv'], 'w_br_mem': out['w_br_mem'], 'w_out': out['w_out'], 'loss_target': out['loss_target'], 'm_norm_g': out['m_norm_g'], 'm_w_in': out['m_w_in'], 'm_b_gate': out['m_b_gate'], 'm_q_norm_g': out['m_q_norm_g'], 'm_w_uq': out['m_w_uq'], 'm_kv_norm_g': out['m_kv_norm_g'], 'm_w_ukv': out['m_w_ukv'], 'm_q_head_g': out['m_q_head_g'], 'm_k_head_g': out['m_k_head_g'], 'm_conv_w': out['m_conv_w'], 'm_conv_b': out['m_conv_b'], 'm_mem_norm_g': out['m_mem_norm_g'], 'm_w_mkv': out['m_w_mkv'], 'm_mem_q_g': out['m_mem_q_g'], 'm_mem_k_g': out['m_mem_k_g'], 'm_w_br_attn': out['m_w_br_attn'], 'm_w_br_conv': out['m_w_br_conv'], 'm_w_br_mem': out['m_w_br_mem'], 'm_w_out': out['m_w_out'], 'v_norm_g': out['v_norm_g'], 'v_w_in': out['v_w_in'], 'v_b_gate': out['v_b_gate'], 'v_q_norm_g': out['v_q_norm_g'], 'v_w_uq': out['v_w_uq'], 'v_kv_norm_g': out['v_kv_norm_g'], 'v_w_ukv': out['v_w_ukv'], 'v_q_head_g': out['v_q_head_g'], 'v_k_head_g': out['v_k_head_g'], 'v_conv_w': out['v_conv_w'], 'v_conv_b': out['v_conv_b'], 'v_mem_norm_g': out['v_mem_norm_g'], 'v_w_mkv': out['v_w_mkv'], 'v_mem_q_g': out['v_mem_q_g'], 'v_mem_k_g': out['v_mem_k_g'], 'v_w_br_attn': out['v_w_br_attn'], 'v_w_br_conv': out['v_w_br_conv'], 'v_w_br_mem': out['v_w_br_mem'], 'v_w_out': out['v_w_out']}


def _loss(weights, diff, rest, loss_target):
    with _jax.named_scope("forward"):
        args = {**rest, TWIN_DIFF_INPUT: diff, **{k: w.astype(_WEIGHT_DTYPES[k]) for k, w in weights.items()}}
        y = _forward(args)
    with _jax.named_scope("loss_head"):
        err = _jnp.square(y.astype(_jnp.float32) - loss_target)
        return 0.5 * _jnp.sum(_jnp.mean(err, axis=-1)) if err.ndim else 0.5 * err


def _adamw(w, g, m, v):
    m = ADAM_B1 * m + (1.0 - ADAM_B1) * g
    v = ADAM_B2 * v + (1.0 - ADAM_B2) * _jnp.square(g)
    m_hat = m / (1.0 - ADAM_B1 ** ADAM_STEP)
    v_hat = v / (1.0 - ADAM_B2 ** ADAM_STEP)
    delta = -ADAM_LR * (m_hat / (_jnp.sqrt(v_hat) + ADAM_EPS) + ADAM_WD * w)
    return delta, m, v


def reference(x, mem, positions, norm_g, w_in, b_gate, q_norm_g, w_uq, kv_norm_g, w_ukv, q_head_g, k_head_g, conv_w, conv_b, mem_norm_g, w_mkv, mem_q_g, mem_k_g, w_br_attn, w_br_conv, w_br_mem, w_out, loss_target, m_norm_g, m_w_in, m_b_gate, m_q_norm_g, m_w_uq, m_kv_norm_g, m_w_ukv, m_q_head_g, m_k_head_g, m_conv_w, m_conv_b, m_mem_norm_g, m_w_mkv, m_mem_q_g, m_mem_k_g, m_w_br_attn, m_w_br_conv, m_w_br_mem, m_w_out, v_norm_g, v_w_in, v_b_gate, v_q_norm_g, v_w_uq, v_kv_norm_g, v_w_ukv, v_q_head_g, v_k_head_g, v_conv_w, v_conv_b, v_mem_norm_g, v_w_mkv, v_mem_q_g, v_mem_k_g, v_w_br_attn, v_w_br_conv, v_w_br_mem, v_w_out):
    given = dict(x=x, mem=mem, positions=positions, norm_g=norm_g, w_in=w_in, b_gate=b_gate, q_norm_g=q_norm_g, w_uq=w_uq, kv_norm_g=kv_norm_g, w_ukv=w_ukv, q_head_g=q_head_g, k_head_g=k_head_g, conv_w=conv_w, conv_b=conv_b, mem_norm_g=mem_norm_g, w_mkv=w_mkv, mem_q_g=mem_q_g, mem_k_g=mem_k_g, w_br_attn=w_br_attn, w_br_conv=w_br_conv, w_br_mem=w_br_mem, w_out=w_out, loss_target=loss_target, m_norm_g=m_norm_g, m_w_in=m_w_in, m_b_gate=m_b_gate, m_q_norm_g=m_q_norm_g, m_w_uq=m_w_uq, m_kv_norm_g=m_kv_norm_g, m_w_ukv=m_w_ukv, m_q_head_g=m_q_head_g, m_k_head_g=m_k_head_g, m_conv_w=m_conv_w, m_conv_b=m_conv_b, m_mem_norm_g=m_mem_norm_g, m_w_mkv=m_w_mkv, m_mem_q_g=m_mem_q_g, m_mem_k_g=m_mem_k_g, m_w_br_attn=m_w_br_attn, m_w_br_conv=m_w_br_conv, m_w_br_mem=m_w_br_mem, m_w_out=m_w_out, v_norm_g=v_norm_g, v_w_in=v_w_in, v_b_gate=v_b_gate, v_q_norm_g=v_q_norm_g, v_w_uq=v_w_uq, v_kv_norm_g=v_kv_norm_g, v_w_ukv=v_w_ukv, v_q_head_g=v_q_head_g, v_k_head_g=v_k_head_g, v_conv_w=v_conv_w, v_conv_b=v_conv_b, v_mem_norm_g=v_mem_norm_g, v_w_mkv=v_w_mkv, v_mem_q_g=v_mem_q_g, v_mem_k_g=v_mem_k_g, v_w_br_attn=v_w_br_attn, v_w_br_conv=v_w_br_conv, v_w_br_mem=v_w_br_mem, v_w_out=v_w_out)
    weights = {n: given[n] for n in TWIN_WEIGHTS}
    shared = {n: given[n] for n in SHARED_INPUTS}
    per_example = {n: given[n] for n in ['x', 'mem', 'positions']}
    grad_fn = _jax.value_and_grad(_loss, argnums=(0, 1))

    def one_microbatch(ex, loss_target):
        ex = dict(ex)
        diff = ex.pop(TWIN_DIFF_INPUT)
        return grad_fn(weights, diff, {**shared, **ex}, loss_target)

    if N_MICROBATCH == 1:
        loss, (grad_w, grad_x) = one_microbatch(per_example, given["loss_target"])
    else:
        def body(carry, xs):
            loss_sum, grad_sum = carry
            l_k, (gw_k, gx_k) = one_microbatch(xs[0], xs[1])
            with _jax.named_scope("update"):
                return (loss_sum + l_k, _jax.tree.map(_jnp.add, grad_sum, gw_k)), gx_k

        init = (_jnp.zeros((), _jnp.float32), _jax.tree.map(_jnp.zeros_like, weights))
        (loss, grad_w), grad_x = _jax.lax.scan(body, init, (per_example, given["loss_target"]))
    with _jax.named_scope("update"):
        delta_w, new_m, new_v = {}, {}, {}
        for n in TWIN_WEIGHTS:
            delta_w[n], new_m[n], new_v[n] = _adamw(weights[n], grad_w[n], given["m_" + n], given["v_" + n])
    return (loss, grad_x, *[grad_w[n] for n in TWIN_WEIGHTS], *[delta_w[n] for n in TWIN_WEIGHTS],
            *[new_m[n] for n in TWIN_WEIGHTS], *[new_v[n] for n in TWIN_WEIGHTS])
```

```python
import functools

import jax
import jax.numpy as jnp
from jax import lax
from jax.experimental import pallas as pl
from jax.experimental.pallas import tpu as pltpu

F32, BF16 = jnp.float32, jnp.bfloat16

N_DEV = 8
DEPTH = 4
D = 1024
QL, KVL, RP = 384, 256, 32
H, NOPE, QKD, VD = 8, 64, 96, 64
HP = 128
CW, MW, AW = 512, 512, 512
HM, MHD = 4, 128
IN_WIDTH = 7328
EPS = 1e-6
ROPE_BASE = 10000.0
ATT_SCALE = QKD ** -0.5
MEM_SCALE = MHD ** -0.5

ADAM_LR, ADAM_B1, ADAM_B2, ADAM_EPS, ADAM_WD, ADAM_STEP = 0.001, 0.9, 0.999, 1e-08, 0.01, 10

LANES = 128
VMEM_LIMIT = 56 * 1024 * 1024

P_CONV, P_MEM, P_R, P_GA, P_L = 0, 2048, 3072, 6144, 6656
P_MAIN = 6656
P_LW = 768
P_W = P_MAIN + P_LW
P_WPAD = 7680

WEIGHTS = ['norm_g', 'w_in', 'b_gate', 'q_norm_g', 'w_uq', 'kv_norm_g', 'w_ukv', 'q_head_g', 'k_head_g',
           'conv_w', 'conv_b', 'mem_norm_g', 'w_mkv', 'mem_q_g', 'mem_k_g', 'w_br_attn', 'w_br_conv',
           'w_br_mem', 'w_out']
INPUTS = ['x', 'mem', 'positions'] + WEIGHTS + ['loss_target'] + ['m_' + n for n in WEIGHTS] + ['v_' + n for n in WEIGHTS]

BIG = {'w_in': ((D, IN_WIDTH), 1), 'w_uq': ((QL, H * QKD), 1), 'w_ukv': ((KVL, H * 128), 1),
       'w_mkv': ((D, 2 * MW), 0), 'w_br_attn': ((AW, D), 1), 'w_br_conv': ((CW, D), 1),
       'w_br_mem': ((MW, D), 1), 'w_out': ((D, D), 0)}
BIG_ORDER = ['w_in', 'w_uq', 'w_ukv', 'w_mkv', 'w_br_attn', 'w_br_conv', 'w_br_mem', 'w_out']
CONVW_PAD = 256
SMALL = {'norm_g': D, 'b_gate': 3 * D, 'q_norm_g': QL, 'kv_norm_g': KVL, 'q_head_g': QKD, 'k_head_g': QKD,
         'conv_b': CW, 'mem_norm_g': D, 'mem_q_g': MHD, 'mem_k_g': MHD}
SMALL_ORDER = list(SMALL)
ROW_ALIGN = 1024


def _call(body, **kw):
    return pl.pallas_call(body, **kw)


def _cp(sem=None):
    return pltpu.CompilerParams(dimension_semantics=sem, vmem_limit_bytes=VMEM_LIMIT)


def _full(shape):
    n = len(shape)
    return pl.BlockSpec(shape, lambda *_: (0,) * n)


def _rms(x, g, n):
    rs = lax.rsqrt(jnp.sum(x * x, axis=-1, keepdims=True) * (1.0 / n) + EPS)
    xh = x * rs
    return xh * g, xh, rs


def _rms_bwd(dy, xh, rs, g, n):
    dxh = dy * g
    dx = rs * (dxh - xh * (jnp.sum(dxh * xh, axis=-1, keepdims=True) * (1.0 / n)))
    return dx, jnp.sum(dy * xh, axis=0, keepdims=True)


def _sigmoid(x):
    return 1.0 / (1.0 + jnp.exp(-x))


def _swap_rope(u):
    lane = lax.broadcasted_iota(jnp.int32, u.shape, 1)
    up = pltpu.roll(u, 16, 1)
    dn = pltpu.roll(u, 112, 1)
    return jnp.where((lane >= 64) & (lane < 80), dn, jnp.where((lane >= 80) & (lane < 96), up, 0.0))


def _rope(u, c, sn):
    return u * c + _swap_rope(u) * sn


def _rope_adj(d, c, sn):
    return d * c + _swap_rope(d * sn)


def _dot(a, b):
    return jnp.dot(a, b, preferred_element_type=F32)


def _dot_nt(a, b):
    return lax.dot_general(a, b, (((1,), (1,)), ((), ())), preferred_element_type=F32)


def _dot_tn(a, b):
    return lax.dot_general(a, b, (((0,), (0,)), ((), ())), preferred_element_type=F32)


def _mm(a, b, out_dtype, tm, tn, name):
    m, k = a.shape
    _, n = b.shape
    tm, tn = min(tm, m), min(tn, n)

    def body(a_ref, b_ref, o_ref):
        o_ref[...] = _dot(a_ref[...].astype(BF16), b_ref[...]).astype(o_ref.dtype)

    return _call(
        body, name=name, grid=(m // tm, n // tn),
        in_specs=[pl.BlockSpec((tm, k), lambda i, j: (i, 0)), pl.BlockSpec((k, tn), lambda i, j: (0, j))],
        out_specs=pl.BlockSpec((tm, tn), lambda i, j: (i, j)),
        out_shape=jax.ShapeDtypeStruct((m, n), out_dtype),
        compiler_params=_cp(("parallel", "arbitrary")),
    )(a, b)


def _tn(a, b, ts, tn, name):
    s, ka = a.shape
    _, n = b.shape
    ts, tn = min(ts, s), min(tn, n)

    def body(a_ref, b_ref, o_ref):
        @pl.when(pl.program_id(1) == 0)
        def _():
            o_ref[...] = jnp.zeros_like(o_ref)

        o_ref[...] += _dot_tn(a_ref[...].astype(BF16), b_ref[...].astype(BF16))

    return _call(
        body, name=name, grid=(n // tn, s // ts),
        in_specs=[pl.BlockSpec((ts, ka), lambda j, i: (i, 0)), pl.BlockSpec((ts, tn), lambda j, i: (i, j))],
        out_specs=pl.BlockSpec((ka, tn), lambda j, i: (0, j)),
        out_shape=jax.ShapeDtypeStruct((ka, n), F32),
        compiler_params=_cp(("parallel", "arbitrary")),
    )(a, b)


def _rms_h(x, g, t, name):
    s = x.shape[0]
    t = min(t, s)

    def body(x_ref, g_ref, h_ref):
        h_ref[...] = _rms(x_ref[...], g_ref[...], D)[0].astype(BF16)

    return _call(
        body, name=name, grid=(s // t,),
        in_specs=[pl.BlockSpec((t, D), lambda i: (i, 0)), _full((1, D))],
        out_specs=pl.BlockSpec((t, D), lambda i: (i, 0)),
        out_shape=jax.ShapeDtypeStruct((s, D), BF16),
        compiler_params=_cp(("parallel",)),
    )(x, g)


def _mla_heads(pl_blk, c, sn, qng, kvng, qhg, khg, wuq, wukv):
    ql = pl_blk[:, 0:QL].astype(F32)
    kvl = pl_blk[:, QL:QL + KVL].astype(F32)
    kpe = pl_blk[:, QL + KVL:P_LW].astype(F32)
    qn, qxh, qrs = _rms(ql, qng, QL)
    kvn, kvxh, kvrs = _rms(kvl, kvng, KVL)
    qn16, kvn16 = qn.astype(BF16), kvn.astype(BF16)
    qp = _dot(qn16, wuq)
    kvp = _dot(kvn16, wukv)
    return ql, kvl, kpe, qxh, qrs, kvxh, kvrs, qn16, kvn16, qp, kvp


def _mla_prep(proj_l, c, sn, qng, kvng, qhg, khg, wuq, wukv, t, name):
    s = proj_l.shape[0]
    t = min(t, s)

    def body(l_ref, c_ref, sn_ref, qng_ref, kvng_ref, qhg_ref, khg_ref, wuq_ref, wukv_ref, q_ref, k_ref, v_ref):
        cc, ss = c_ref[...], sn_ref[...]
        (_, _, kpe, _, _, _, _, _, _, qp, kvp) = _mla_heads(
            l_ref[...], cc, ss, qng_ref[...], kvng_ref[...], qhg_ref[...], khg_ref[...], wuq_ref[...], wukv_ref[...])
        for h in range(H):
            u = qp[:, h * HP:(h + 1) * HP]
            q_ref[h] = _rope(_rms(u, qhg_ref[...], QKD)[0], cc, ss).astype(BF16)
            u = kvp[:, h * HP:(h + 1) * HP] + kpe
            k_ref[h] = _rope(_rms(u, khg_ref[...], QKD)[0], cc, ss).astype(BF16)
            v_ref[h] = kvp[:, (H + h) * HP:(H + h + 1) * HP].astype(BF16)

    hs = pl.BlockSpec((H, t, HP), lambda i: (0, i, 0))
    row = lambda w: pl.BlockSpec((t, w), lambda i: (i, 0))
    return _call(
        body, name=name, grid=(s // t,),
        in_specs=[row(P_LW), row(HP), row(HP), _full((1, QL)), _full((1, KVL)), _full((1, HP)), _full((1, HP)),
                  _full((QL, H * HP)), _full((KVL, 2 * H * HP))],
        out_specs=[hs, hs, hs],
        out_shape=[jax.ShapeDtypeStruct((H, s, HP), BF16)] * 3,
        compiler_params=_cp(("parallel",)),
    )(proj_l, c, sn, qng, kvng, qhg, khg, wuq, wukv)


def _flash_fwd(q, k, v, tq, tk, name):
    _, s, _ = q.shape
    tq, tk = min(tq, s), min(tk, s)
    nk = s // tk

    def body(q_ref, k_ref, v_ref, o_ref, lse_ref):
        qb = q_ref[0]

        def step(j, carry):
            m, l, acc = carry
            off = pl.multiple_of(j * tk, tk)
            kb = k_ref[0, pl.ds(off, tk), :]
            vb = v_ref[0, pl.ds(off, tk), :]
            sc = _dot_nt(qb, kb) * ATT_SCALE
            mn = jnp.maximum(m, jnp.max(sc, axis=-1, keepdims=True))
            p = jnp.exp(sc - mn)
            al = jnp.exp(m - mn)
            l = al * l + jnp.sum(p, axis=-1, keepdims=True)
            acc = al * acc + _dot(p.astype(BF16), vb)
            return mn, l, acc

        m, l, acc = lax.fori_loop(
            0, nk, step, (jnp.full((tq, 1), -1e30, F32), jnp.zeros((tq, 1), F32), jnp.zeros((tq, HP), F32)))
        o_ref[0] = (acc[:, :VD] / l).astype(BF16)
        lse_ref[0] = m + jnp.log(l)

    return _call(
        body, name=name, grid=(H, s // tq),
        in_specs=[pl.BlockSpec((1, tq, HP), lambda h, i: (h, i, 0)),
                  pl.BlockSpec((1, s, HP), lambda h, i: (h, 0, 0)),
                  pl.BlockSpec((1, s, HP), lambda h, i: (h, 0, 0))],
        out_specs=[pl.BlockSpec((1, tq, VD), lambda h, i: (h, i, 0)), pl.BlockSpec((1, tq, 1), lambda h, i: (h, i, 0))],
        out_shape=[jax.ShapeDtypeStruct((H, s, VD), BF16), jax.ShapeDtypeStruct((H, s, 1), F32)],
        compiler_params=_cp(("parallel", "arbitrary")),
    )(q, k, v)


def _mem_prep(mem, mng, wmkv, mkg, name):
    m = mem.shape[0]

    def body(mem_ref, mng_ref, w_ref, mkg_ref, mk_ref, mv_ref):
        mn = _rms(mem_ref[...], mng_ref[...], D)[0].astype(BF16)
        mkv = _dot(mn, w_ref[...])
        for h in range(HM):
            mk_ref[h] = _rms(mkv[:, 2 * h * MHD:(2 * h + 1) * MHD], mkg_ref[...], MHD)[0].astype(BF16)
            mv_ref[h] = mkv[:, (2 * h + 1) * MHD:(2 * h + 2) * MHD].astype(BF16)

    return _call(
        body, name=name,
        in_specs=[_full((m, D)), _full((1, D)), _full((D, 2 * MW)), _full((1, MHD))],
        out_specs=[_full((HM, m, MHD))] * 2,
        out_shape=[jax.ShapeDtypeStruct((HM, m, MHD), BF16)] * 2,
        compiler_params=_cp(),
    )(mem, mng, wmkv, mkg)


def _conv_parts(cv, prev, nxt, first, last, cw, cb):
    t = cv.shape[0]
    c_b, c_c, c_u, g_c = (cv[:, i * CW:(i + 1) * CW].astype(F32) for i in range(4))
    z = c_c * c_u
    zp = jnp.where(first, 0.0, prev[15:16, CW:2 * CW].astype(F32) * prev[15:16, 2 * CW:3 * CW].astype(F32))
    zn = jnp.where(last, 0.0, nxt[0:1, CW:2 * CW].astype(F32) * nxt[0:1, 2 * CW:3 * CW].astype(F32))
    row = lax.broadcasted_iota(jnp.int32, (t, CW), 0)
    z_m1 = jnp.where(row == 0, zp, pltpu.roll(z, 1, 0))
    z_p1 = jnp.where(row == t - 1, zn, pltpu.roll(z, t - 1, 0))
    conv = cw[0:1] * z_m1 + cw[1:2] * z + cw[2:3] * z_p1 + cb
    return c_b, c_c, c_u, g_c, z, z_m1, z_p1, conv


def _mem_attn(qm, mqg, mk_ref, mv_ref):
    outs = []
    for h in range(HM):
        mq, mqxh, mqrs = _rms(qm[:, h * MHD:(h + 1) * MHD], mqg, MHD)
        mq16 = mq.astype(BF16)
        sc = _dot_nt(mq16, mk_ref[h]) * MEM_SCALE
        e = jnp.exp(sc - jnp.max(sc, axis=-1, keepdims=True))
        p = e / jnp.sum(e, axis=-1, keepdims=True)
        o = _dot(p.astype(BF16), mv_ref[h])
        outs.append((mq16, mqxh, mqrs, p, o))
    return outs


def _halo_specs(t, s, width):
    nb = s // 16
    prev = pl.BlockSpec((16, width), lambda i: (jnp.maximum(i * (t // 16) - 1, 0), 0))
    nxt = pl.BlockSpec((16, width), lambda i: (jnp.minimum((i + 1) * (t // 16), nb - 1), 0))
    return prev, nxt


def _branches(proj, o, mk, mv, cw, cb, mqg, t, name):
    s = proj.shape[0]
    t = min(t, s)
    nt = s // t

    def body(cv_ref, pv_ref, nx_ref, mm_ref, ga_ref, o_ref, mk_ref, mv_ref, cw_ref, cb_ref, mqg_ref,
             oa_ref, oc_ref, om_ref):
        i = pl.program_id(0)
        c_b, _, _, g_c, _, _, _, conv = _conv_parts(
            cv_ref[...], pv_ref[...], nx_ref[...], i == 0, i == nt - 1, cw_ref[...], cb_ref[...])
        oc_ref[...] = (c_b * conv * (g_c * _sigmoid(g_c))).astype(BF16)
        ga = ga_ref[...].astype(F32)
        ocat = jnp.concatenate([o_ref[h].astype(F32) for h in range(H)], axis=-1)
        oa_ref[...] = (ocat * (ga * _sigmoid(ga))).astype(BF16)
        mblk = mm_ref[...].astype(F32)
        gm = mblk[:, MW:]
        heads = _mem_attn(mblk[:, :MW], mqg_ref[...], mk_ref, mv_ref)
        om = jnp.concatenate([hh[4] for hh in heads], axis=-1)
        om_ref[...] = (om * (gm * _sigmoid(gm))).astype(BF16)

    pv, nx = _halo_specs(t, s, 4 * CW)
    out = pl.BlockSpec((t, 512), lambda i: (i, 0))
    return _call(
        body, name=name, grid=(nt,),
        in_specs=[pl.BlockSpec((t, 4 * CW), lambda i: (i, 0)), pv, nx,
                  pl.BlockSpec((t, 2 * MW), lambda i: (i, P_MEM // (2 * MW))),
                  pl.BlockSpec((t, AW), lambda i: (i, P_GA // AW)),
                  pl.BlockSpec((H, t, VD), lambda i: (0, i, 0)),
                  _full(mk.shape), _full(mv.shape), _full((3, CW)), _full((1, CW)), _full((1, MHD))],
        out_specs=[out, out, out],
        out_shape=[jax.ShapeDtypeStruct((s, 512), BF16)] * 3,
        compiler_params=_cp(("parallel",)),
    )(proj, proj, proj, proj, proj, o, mk, mv, cw, cb, mqg)


def _merge_fwd(x, proj, oa, oc, om, bg, wa, wc, wm, wo, t, name):
    s = x.shape[0]
    t = min(t, s)

    def body(x_ref, r_ref, oa_ref, oc_ref, om_ref, bg_ref, wa_ref, wc_ref, wm_ref, wo_ref,
             xo_ref, aa_ref, ac_ref, am_ref):
        y = jnp.zeros((t, D), F32)
        for j, (o_ref, w_ref, a_ref) in enumerate(((oa_ref, wa_ref, aa_ref), (oc_ref, wc_ref, ac_ref),
                                                   (om_ref, wm_ref, am_ref))):
            a = _dot(o_ref[...], w_ref[...])
            a_ref[...] = a.astype(BF16)
            rg = _sigmoid(r_ref[:, j * D:(j + 1) * D].astype(F32) + bg_ref[:, j * D:(j + 1) * D])
            y = y + rg * a
        xo_ref[...] = x_ref[...] + _dot(y.astype(BF16), wo_ref[...])

    row = lambda w: pl.BlockSpec((t, w), lambda i: (i, 0))
    return _call(
        body, name=name, grid=(s // t,),
        in_specs=[row(D), pl.BlockSpec((t, 3 * D), lambda i: (i, P_R // (3 * D))), row(512), row(512), row(512),
                  _full((1, 3 * D)), _full((512, D)), _full((512, D)), _full((512, D)), _full((D, D))],
        out_specs=[row(D), row(D), row(D), row(D)],
        out_shape=[jax.ShapeDtypeStruct((s, D), F32)] + [jax.ShapeDtypeStruct((s, D), BF16)] * 3,
        compiler_params=_cp(("parallel",)),
    )(x, proj, oa, oc, om, bg, wa, wc, wm, wo)


def _loss_grad(x, tgt, t, name):
    s = x.shape[0]
    t = min(t, s)

    def body(x_ref, t_ref, g_ref, l_ref):
        @pl.when(pl.program_id(0) == 0)
        def _():
            l_ref[...] = jnp.zeros_like(l_ref)

        e = x_ref[...] - t_ref[...]
        g_ref[...] = e * (1.0 / D)
        sq = e * e
        part = sq[:, 0:LANES]
        for j in range(1, D // LANES):
            part = part + sq[:, j * LANES:(j + 1) * LANES]
        acc = part[0:8]
        for j in range(1, t // 8):
            acc = acc + part[j * 8:(j + 1) * 8]
        l_ref[...] += acc * (0.5 / D)

    row = pl.BlockSpec((t, D), lambda i: (i, 0))
    return _call(
        body, name=name, grid=(s // t,),
        in_specs=[row, row], out_specs=[row, _full((8, LANES))],
        out_shape=[jax.ShapeDtypeStruct((s, D), F32), jax.ShapeDtypeStruct((8, LANES), F32)],
        compiler_params=_cp(("arbitrary",)),
    )(x, tgt)


def _merge_bwd(g, proj, aa, ac, am, bg, wot, wat, wct, wmt, t, name):
    s = g.shape[0]
    t = min(t, s)

    def body(g_ref, r_ref, aa_ref, ac_ref, am_ref, bg_ref, wot_ref, wat_ref, wct_ref, wmt_ref,
             y_ref, daa_ref, dac_ref, dam_ref, dr_ref, doa_ref, doc_ref, dom_ref, dbg_ref):
        @pl.when(pl.program_id(0) == 0)
        def _():
            dbg_ref[...] = jnp.zeros_like(dbg_ref)

        dy = _dot(g_ref[...].astype(BF16), wot_ref[...])
        y = jnp.zeros((t, D), F32)
        for j, (a_ref, wt_ref, da_ref, do_ref) in enumerate(((aa_ref, wat_ref, daa_ref, doa_ref),
                                                             (ac_ref, wct_ref, dac_ref, doc_ref),
                                                             (am_ref, wmt_ref, dam_ref, dom_ref))):
            a = a_ref[...].astype(F32)
            rg = _sigmoid(r_ref[:, j * D:(j + 1) * D].astype(F32) + bg_ref[:, j * D:(j + 1) * D])
            y = y + rg * a
            da = (dy * rg).astype(BF16)
            da_ref[...] = da
            dr = dy * a * rg * (1.0 - rg)
            dr_ref[:, j * D:(j + 1) * D] = dr.astype(BF16)
            dbg_ref[:, j * D:(j + 1) * D] += jnp.sum(dr, axis=0, keepdims=True)
            do_ref[...] = _dot(da, wt_ref[...])
        y_ref[...] = y.astype(BF16)

    row = lambda w: pl.BlockSpec((t, w), lambda i: (i, 0))
    return _call(
        body, name=name, grid=(s // t,),
        in_specs=[row(D), pl.BlockSpec((t, 3 * D), lambda i: (i, P_R // (3 * D))), row(D), row(D), row(D),
                  _full((1, 3 * D)), _full((D, D)), _full((D, 512)), _full((D, 512)), _full((D, 512))],
        out_specs=[row(D), row(D), row(D), row(D), row(3 * D), row(512), row(512), row(512), _full((1, 3 * D))],
        out_shape=[jax.ShapeDtypeStruct((s, D), BF16)] * 4 + [jax.ShapeDtypeStruct((s, 3 * D), BF16)]
        + [jax.ShapeDtypeStruct((s, 512), F32)] * 3 + [jax.ShapeDtypeStruct((1, 3 * D), F32)],
        compiler_params=_cp(("arbitrary",)),
    )(g, proj, aa, ac, am, bg, wot, wat, wct, wmt)


def _branches_bwd(proj, o, mk, mv, cw, cb, mqg, doa, doc, dom, t, name):
    s = proj.shape[0]
    t = min(t, s)
    nt = s // t
    m = mk.shape[1]

    def body(cv_ref, pv_ref, nx_ref, mm_ref, ga_ref, o_ref, mk_ref, mv_ref, cw_ref, cb_ref, mqg_ref,
             doa_ref, doc_ref, dcp_ref, dcn_ref, dom_ref,
             dcv_ref, dmm_ref, dga_ref, do_ref, dl_ref, dcw_ref, dcb_ref, dmk_ref, dmv_ref, dmqg_ref):
        i = pl.program_id(0)

        @pl.when(i == 0)
        def _():
            for r in (dcw_ref, dcb_ref, dmk_ref, dmv_ref, dmqg_ref):
                r[...] = jnp.zeros_like(r)

        first, last = i == 0, i == nt - 1
        cw_, cb_ = cw_ref[...], cb_ref[...]
        pv, nx = pv_ref[...], nx_ref[...]
        c_b, c_c, c_u, g_c, z, z_m1, z_p1, conv = _conv_parts(cv_ref[...], pv, nx, first, last, cw_, cb_)
        sg = _sigmoid(g_c)
        silu = g_c * sg
        dsilu = sg * (1.0 + g_c * (1.0 - sg))
        doc_ = doc_ref[...]
        dconv = doc_ * c_b * silu
        gp = pv[15:16, 3 * CW:4 * CW].astype(F32)
        gn = nx[0:1, 3 * CW:4 * CW].astype(F32)
        dconv_p = jnp.where(first, 0.0, dcp_ref[15:16, :] * pv[15:16, 0:CW].astype(F32) * (gp * _sigmoid(gp)))
        dconv_n = jnp.where(last, 0.0, dcn_ref[0:1, :] * nx[0:1, 0:CW].astype(F32) * (gn * _sigmoid(gn)))
        row = lax.broadcasted_iota(jnp.int32, (t, CW), 0)
        d_m1 = jnp.where(row == 0, dconv_p, pltpu.roll(dconv, 1, 0))
        d_p1 = jnp.where(row == t - 1, dconv_n, pltpu.roll(dconv, t - 1, 0))
        dz = cw_[0:1] * d_p1 + cw_[1:2] * dconv + cw_[2:3] * d_m1
        dcv_ref[:, 0:CW] = (doc_ * conv * silu).astype(BF16)
        dcv_ref[:, CW:2 * CW] = (dz * c_u).astype(BF16)
        dcv_ref[:, 2 * CW:3 * CW] = (dz * c_c).astype(BF16)
        dcv_ref[:, 3 * CW:4 * CW] = (doc_ * c_b * conv * dsilu).astype(BF16)
        dcw_ref[0:1, :] += jnp.sum(dconv * z_m1, axis=0, keepdims=True)
        dcw_ref[1:2, :] += jnp.sum(dconv * z, axis=0, keepdims=True)
        dcw_ref[2:3, :] += jnp.sum(dconv * z_p1, axis=0, keepdims=True)
        dcb_ref[...] += jnp.sum(dconv, axis=0, keepdims=True)
        ga = ga_ref[...].astype(F32)
        sg = _sigmoid(ga)
        doa_ = doa_ref[...]
        ocat = jnp.concatenate([o_ref[h].astype(F32) for h in range(H)], axis=-1)
        dga_ref[...] = (doa_ * ocat * (sg * (1.0 + ga * (1.0 - sg)))).astype(BF16)
        dog = doa_ * (ga * sg)
        zeros = jnp.zeros((t, HP - VD), F32)
        for h in range(H):
            dh = dog[:, h * VD:(h + 1) * VD]
            do_ref[h] = jnp.concatenate([dh, zeros], axis=-1).astype(BF16)
            dl_ref[h] = jnp.sum(dh * ocat[:, h * VD:(h + 1) * VD], axis=-1, keepdims=True)
        mblk = mm_ref[...].astype(F32)
        gm = mblk[:, MW:]
        sg = _sigmoid(gm)
        dom_ = dom_ref[...]
        heads = _mem_attn(mblk[:, :MW], mqg_ref[...], mk_ref, mv_ref)
        om = jnp.concatenate([hh[4] for hh in heads], axis=-1)
        dmm_ref[:, MW:] = (dom_ * om * (sg * (1.0 + gm * (1.0 - sg)))).astype(BF16)
        dmo = dom_ * (gm * sg)
        dmqg = jnp.zeros((1, MHD), F32)
        for h in range(HM):
            mq16, mqxh, mqrs, p, _ = heads[h]
            dmo_h = dmo[:, h * MHD:(h + 1) * MHD].astype(BF16)
            dp = _dot_nt(dmo_h, mv_ref[h])
            ds = (p * (dp - jnp.sum(dp * p, axis=-1, keepdims=True)) * MEM_SCALE).astype(BF16)
            dmq = _dot(ds, mk_ref[h])
            dmk_ref[h] += _dot_tn(ds, mq16)
            dmv_ref[h] += _dot_tn(p.astype(BF16), dmo_h)
            dq, dg = _rms_bwd(dmq, mqxh, mqrs, mqg_ref[...], MHD)
            dmm_ref[:, h * MHD:(h + 1) * MHD] = dq.astype(BF16)
            dmqg = dmqg + dg
        dmqg_ref[...] += dmqg

    pv, nx = _halo_specs(t, s, 4 * CW)
    dpv, dnx = _halo_specs(t, s, CW)
    row = lambda w: pl.BlockSpec((t, w), lambda i: (i, 0))
    hs = lambda w: pl.BlockSpec((H, t, w), lambda i: (0, i, 0))
    return _call(
        body, name=name, grid=(nt,),
        in_specs=[pl.BlockSpec((t, 4 * CW), lambda i: (i, 0)), pv, nx,
                  pl.BlockSpec((t, 2 * MW), lambda i: (i, P_MEM // (2 * MW))),
                  pl.BlockSpec((t, AW), lambda i: (i, P_GA // AW)),
                  hs(VD), _full(mk.shape), _full(mv.shape), _full((3, CW)), _full((1, CW)), _full((1, MHD)),
                  row(512), row(512), dpv, dnx, row(512)],
        out_specs=[row(4 * CW), row(2 * MW), row(AW), hs(HP), hs(1), _full((3, CW)), _full((1, CW)),
                   _full((HM, m, MHD)), _full((HM, m, MHD)), _full((1, MHD))],
        out_shape=[jax.ShapeDtypeStruct((s, 4 * CW), BF16), jax.ShapeDtypeStruct((s, 2 * MW), BF16),
                   jax.ShapeDtypeStruct((s, AW), BF16), jax.ShapeDtypeStruct((H, s, HP), BF16),
                   jax.ShapeDtypeStruct((H, s, 1), F32), jax.ShapeDtypeStruct((3, CW), F32),
                   jax.ShapeDtypeStruct((1, CW), F32), jax.ShapeDtypeStruct((HM, m, MHD), F32),
                   jax.ShapeDtypeStruct((HM, m, MHD), F32), jax.ShapeDtypeStruct((1, MHD), F32)],
        compiler_params=_cp(("arbitrary",)),
    )(proj, proj, proj, proj, proj, o, mk, mv, cw, cb, mqg, doa, doc, doc, doc, dom)


def _flash_bwd(q, k, v, do, lse, dl, tq, tk, name):
    _, s, _ = q.shape
    tq, tk = min(tq, s), min(tk, s)
    nq, nkt = s // tq, s // tk

    def body(q_ref, do_ref, lse_ref, dl_ref, k_ref, v_ref, dq_ref, dk_ref, dv_ref, dq_acc):
        j = pl.program_id(1)

        @pl.when(j == 0)
        def _():
            dq_acc[...] = jnp.zeros_like(dq_acc)

        kb, vb = k_ref[0], v_ref[0]

        def step(i, carry):
            dk, dv = carry
            off = pl.multiple_of(i * tq, tq)
            qb = q_ref[0, pl.ds(off, tq), :]
            dob = do_ref[0, pl.ds(off, tq), :]
            st = _dot_nt(kb, qb) * ATT_SCALE
            pt = jnp.exp(st - lse_ref[0, i])
            dv = dv + _dot(pt.astype(BF16), dob)
            dpt = _dot_nt(vb, dob)
            dst = (pt * (dpt - dl_ref[0, i]) * ATT_SCALE).astype(BF16)
            dk = dk + _dot(dst, qb)
            dq_acc[pl.ds(off, tq), :] += _dot_tn(dst, kb)
            return dk, dv

        dk, dv = lax.fori_loop(0, nq, step, (jnp.zeros((tk, HP), F32), jnp.zeros((tk, HP), F32)))
        dk_ref[0] = dk.astype(BF16)
        dv_ref[0] = dv.astype(BF16)

        @pl.when(j == nkt - 1)
        def _():
            dq_ref[0] = dq_acc[...].astype(BF16)

    whole = pl.BlockSpec((1, s, HP), lambda h, j: (h, 0, 0))
    stat = pl.BlockSpec((1, nq, 1, tq), lambda h, j: (h, 0, 0, 0))
    tile = pl.BlockSpec((1, tk, HP), lambda h, j: (h, j, 0))
    return _call(
        body, name=name, grid=(H, nkt),
        in_specs=[whole, whole, stat, stat, tile, tile],
        out_specs=[whole, tile, tile],
        out_shape=[jax.ShapeDtypeStruct((H, s, HP), BF16)] * 3,
        scratch_shapes=[pltpu.VMEM((s, HP), F32)],
        compiler_params=_cp(("arbitrary", "arbitrary")),
    )(q, do, lse, dl, k, v)


def _mla_prep_bwd(proj_l, c, sn, qng, kvng, qhg, khg, wuq, wukv, wuqt, wukvt, dq, dk, dv, t, name):
    s = proj_l.shape[0]
    t = min(t, s)

    def body(l_ref, c_ref, sn_ref, qng_ref, kvng_ref, qhg_ref, khg_ref, wuq_ref, wukv_ref, wuqt_ref, wukvt_ref,
             dq_ref, dk_ref, dv_ref, dl_ref, dwuq_ref, dwukv_ref, dqng_ref, dkvng_ref, dqhg_ref, dkhg_ref):
        @pl.when(pl.program_id(0) == 0)
        def _():
            for r in (dwuq_ref, dwukv_ref, dqng_ref, dkvng_ref, dqhg_ref, dkhg_ref):
                r[...] = jnp.zeros_like(r)

        cc, ss = c_ref[...], sn_ref[...]
        qhg, khg = qhg_ref[...], khg_ref[...]
        (_, _, kpe, qxh, qrs, kvxh, kvrs, qn16, kvn16, qp, kvp) = _mla_heads(
            l_ref[...], cc, ss, qng_ref[...], kvng_ref[...], qhg, khg, wuq_ref[...], wukv_ref[...])
        lane = lax.broadcasted_iota(jnp.int32, (t, HP), 1)
        dqp, dkp, dvp = [], [], []
        dkpe = jnp.zeros((t, HP), F32)
        dqhg = jnp.zeros((1, HP), F32)
        dkhg = jnp.zeros((1, HP), F32)
        for h in range(H):
            _, xh, rs = _rms(qp[:, h * HP:(h + 1) * HP], qhg, QKD)
            du, dg = _rms_bwd(_rope_adj(dq_ref[h].astype(F32), cc, ss), xh, rs, qhg, QKD)
            dqp.append(du)
            dqhg = dqhg + dg
            _, xh, rs = _rms(kvp[:, h * HP:(h + 1) * HP] + kpe, khg, QKD)
            du, dg = _rms_bwd(_rope_adj(dk_ref[h].astype(F32), cc, ss), xh, rs, khg, QKD)
            dkp.append(du)
            dkhg = dkhg + dg
            dkpe = dkpe + jnp.where((lane >= NOPE) & (lane < QKD), du, 0.0)
            dvp.append(dv_ref[h].astype(F32))
        dqhg_ref[...] += dqhg
        dkhg_ref[...] += dkhg
        dqp16 = jnp.concatenate(dqp, axis=-1).astype(BF16)
        dkvp16 = jnp.concatenate(dkp + dvp, axis=-1).astype(BF16)
        dwuq_ref[...] += _dot_tn(qn16, dqp16)
        dwukv_ref[...] += _dot_tn(kvn16, dkvp16)
        dql, dg = _rms_bwd(_dot(dqp16, wuqt_ref[...]), qxh, qrs, qng_ref[...], QL)
        dqng_ref[...] += dg
        dkvl, dg = _rms_bwd(_dot(dkvp16, wukvt_ref[...]), kvxh, kvrs, kvng_ref[...], KVL)
        dkvng_ref[...] += dg
        dl_ref[:, 0:QL] = dql.astype(BF16)
        dl_ref[:, QL:QL + KVL] = dkvl.astype(BF16)
        dl_ref[:, QL + KVL:P_LW] = dkpe.astype(BF16)

    hs = pl.BlockSpec((H, t, HP), lambda i: (0, i, 0))
    row = lambda w: pl.BlockSpec((t, w), lambda i: (i, 0))
    return _call(
        body, name=name, grid=(s // t,),
        in_specs=[row(P_LW), row(HP), row(HP), _full((1, QL)), _full((1, KVL)), _full((1, HP)), _full((1, HP)),
                  _full((QL, H * HP)), _full((KVL, 2 * H * HP)), _full((H * HP, QL)), _full((2 * H * HP, KVL)),
                  hs, hs, hs],
        out_specs=[row(P_LW), _full((QL, H * HP)), _full((KVL, 2 * H * HP)), _full((1, QL)), _full((1, KVL)),
                   _full((1, HP)), _full((1, HP))],
        out_shape=[jax.ShapeDtypeStruct((s, P_LW), BF16), jax.ShapeDtypeStruct((QL, H * HP), F32),
                   jax.ShapeDtypeStruct((KVL, 2 * H * HP), F32), jax.ShapeDtypeStruct((1, QL), F32),
                   jax.ShapeDtypeStruct((1, KVL), F32), jax.ShapeDtypeStruct((1, HP), F32),
                   jax.ShapeDtypeStruct((1, HP), F32)],
        compiler_params=_cp(("arbitrary",)),
    )(proj_l, c, sn, qng, kvng, qhg, khg, wuq, wukv, wuqt, wukvt, dq, dk, dv)


def _mem_prep_bwd(mem, mng, wmkv, wmkvt, mkg, dmk, dmv, name):
    m = mem.shape[0]

    def body(mem_ref, mng_ref, w_ref, wt_ref, mkg_ref, dmk_ref, dmv_ref, dw_ref, dmng_ref, dmkg_ref):
        mn, xh, _ = _rms(mem_ref[...], mng_ref[...], D)
        mn16 = mn.astype(BF16)
        mkv = _dot(mn16, w_ref[...])
        parts = []
        dmkg = jnp.zeros((1, MHD), F32)
        for h in range(HM):
            _, kxh, krs = _rms(mkv[:, 2 * h * MHD:(2 * h + 1) * MHD], mkg_ref[...], MHD)
            du, dg = _rms_bwd(dmk_ref[h], kxh, krs, mkg_ref[...], MHD)
            dmkg = dmkg + dg
            parts += [du, dmv_ref[h]]
        dmkv = jnp.concatenate(parts, axis=-1).astype(BF16)
        dw_ref[...] = _dot_tn(mn16, dmkv)
        dmn = _dot(dmkv, wt_ref[...])
        dmng_ref[...] = jnp.sum(dmn * xh, axis=0, keepdims=True)
        dmkg_ref[...] = dmkg

    return _call(
        body, name=name,
        in_specs=[_full((m, D)), _full((1, D)), _full((D, 2 * MW)), _full((2 * MW, D)), _full((1, MHD)),
                  _full((HM, m, MHD)), _full((HM, m, MHD))],
        out_specs=[_full((D, 2 * MW)), _full((1, D)), _full((1, MHD))],
        out_shape=[jax.ShapeDtypeStruct((D, 2 * MW), F32), jax.ShapeDtypeStruct((1, D), F32),
                   jax.ShapeDtypeStruct((1, MHD), F32)],
        compiler_params=_cp(),
    )(mem, mng, wmkv, wmkvt, mkg, dmk, dmv)


def _rms_in_bwd(x, g_out, dh, ng, t, name):
    s = x.shape[0]
    t = min(t, s)

    def body(x_ref, go_ref, dh_ref, ng_ref, dx_ref, dng_ref):
        @pl.when(pl.program_id(0) == 0)
        def _():
            dng_ref[...] = jnp.zeros_like(dng_ref)

        _, xh, rs = _rms(x_ref[...], ng_ref[...], D)
        dx, dg = _rms_bwd(dh_ref[...], xh, rs, ng_ref[...], D)
        dx_ref[...] = go_ref[...] + dx
        dng_ref[...] += dg

    row = pl.BlockSpec((t, D), lambda i: (i, 0))
    return _call(
        body, name=name, grid=(s // t,),
        in_specs=[row, row, row, _full((1, D))], out_specs=[row, _full((1, D))],
        out_shape=[jax.ShapeDtypeStruct((s, D), F32), jax.ShapeDtypeStruct((1, D), F32)],
        compiler_params=_cp(("arbitrary",)),
    )(x, g_out, dh, ng)


def _allgather(xs, name):
    r, c_ = xs.shape

    def body(x_ref, out_ref, send_sems, recv_sems, local_sem):
        x, y, c = lax.axis_index("x"), lax.axis_index("y"), lax.axis_index("c")
        me, sibling = (x, y, c), (x, y, 1 - c)
        chips = [(1 - x, y), (x, 1 - y), (1 - x, 1 - y)]

        def slot(px, py, pc):
            return out_ref.at[4 * px + 2 * py + pc]

        def copy(k, block, to, src=None):
            return pltpu.make_async_remote_copy(
                src_ref=slot(*block) if src is None else src, dst_ref=slot(*block),
                send_sem=send_sems.at[k], recv_sem=recv_sems.at[k],
                device_id=to, device_id_type=pl.DeviceIdType.MESH)

        mine = pltpu.make_async_copy(x_ref, slot(*me), local_sem)
        mine.start()
        first = [copy(0, me, sibling, src=x_ref)]
        first += [copy(1 + j, me, (*chip, c), src=x_ref) for j, chip in enumerate(chips)]
        for cp in first:
            cp.start()
        passed = [copy(4 + j, (*chip, c), sibling) for j, chip in enumerate(chips)]
        for j, chip in enumerate(chips):
            copy(1 + j, (*chip, c), me).wait_recv()
            passed[j].start()
        copy(0, sibling, me).wait_recv()
        for j, chip in enumerate(chips):
            copy(4 + j, (*chip, 1 - c), me).wait_recv()
        for cp in first + passed:
            cp.wait_send()
        mine.wait()

    return _call(
        body, name=name,
        in_specs=[pl.BlockSpec(memory_space=pl.ANY)], out_specs=pl.BlockSpec(memory_space=pl.ANY),
        out_shape=jax.ShapeDtypeStruct((N_DEV, r, c_), xs.dtype),
        scratch_shapes=[pltpu.SemaphoreType.DMA((7,)), pltpu.SemaphoreType.DMA((7,)), pltpu.SemaphoreType.DMA],
    )(xs)


def _alltoall(xs, name):
    _, r, c_ = xs.shape

    def body(x_ref, out_ref, send_sems, recv_sems, local_sem):
        x, y, c = lax.axis_index("x"), lax.axis_index("y"), lax.axis_index("c")
        me = 4 * x + 2 * y + c
        mine = pltpu.make_async_copy(x_ref.at[me], out_ref.at[me], local_sem)
        mine.start()
        copies = []
        for k in range(1, N_DEV):
            bx, by, bc = (k >> 2) & 1, (k >> 1) & 1, k & 1
            px = 1 - x if bx else x
            py = 1 - y if by else y
            pc = 1 - c if bc else c
            copies.append(pltpu.make_async_remote_copy(
                src_ref=x_ref.at[4 * px + 2 * py + pc], dst_ref=out_ref.at[me],
                send_sem=send_sems.at[k - 1], recv_sem=recv_sems.at[k - 1],
                device_id=(px, py, pc), device_id_type=pl.DeviceIdType.MESH))
        for cp in copies:
            cp.start()
        for cp in copies:
            cp.wait_recv()
        for cp in copies:
            cp.wait_send()
        mine.wait()

    return _call(
        body, name=name,
        in_specs=[pl.BlockSpec(memory_space=pl.ANY)], out_specs=pl.BlockSpec(memory_space=pl.ANY),
        out_shape=jax.ShapeDtypeStruct(xs.shape, xs.dtype),
        scratch_shapes=[pltpu.SemaphoreType.DMA((7,)), pltpu.SemaphoreType.DMA((7,)), pltpu.SemaphoreType.DMA],
    )(xs)


def _adamw(parts, w, m, v, tr, name):
    r = w.shape[0]
    tr = min(tr, r)
    c1 = 1.0 / (1.0 - ADAM_B1 ** ADAM_STEP)
    c2 = 1.0 / (1.0 - ADAM_B2 ** ADAM_STEP)

    def body(p_ref, w_ref, m_ref, v_ref, g_ref, d_ref, nm_ref, nv_ref):
        g = p_ref[0]
        for j in range(1, N_DEV):
            g = g + p_ref[j]
        nm = ADAM_B1 * m_ref[...] + (1.0 - ADAM_B1) * g
        nv = ADAM_B2 * v_ref[...] + (1.0 - ADAM_B2) * (g * g)
        g_ref[...] = g
        nm_ref[...] = nm
        nv_ref[...] = nv
        d_ref[...] = -ADAM_LR * ((nm * c1) / (jnp.sqrt(nv * c2) + ADAM_EPS) + ADAM_WD * w_ref[...])

    row = pl.BlockSpec((tr, LANES), lambda i: (i, 0))
    return _call(
        body, name=name, grid=(r // tr,),
        in_specs=[pl.BlockSpec((N_DEV, tr, LANES), lambda i: (0, i, 0)), row, row, row],
        out_specs=[row] * 4, out_shape=[jax.ShapeDtypeStruct((r, LANES), F32)] * 4,
        compiler_params=_cp(("parallel",)),
    )(parts, w, m, v)


def _to_rows(flat, align=ROW_ALIGN):
    n = flat.shape[-1]
    rows = -(-n // (LANES * align)) * align
    pad = [(0, 0)] * (flat.ndim - 1) + [(0, rows * LANES - n)]
    return jnp.pad(flat, pad).reshape(flat.shape[:-1] + (rows, LANES))


def _pack_local(get, with_conv):
    parts = []
    for l in range(DEPTH):
        for n in BIG_ORDER:
            parts.append(get(n)[l].reshape(-1))
        if with_conv:
            parts.append(jnp.pad(get('conv_w')[l].reshape(-1), (0, CONVW_PAD - 3 * CW // N_DEV)))
    return jnp.concatenate(parts)


def _shard_blocks(full, axis):
    if axis == 0:
        return full.reshape(N_DEV, -1)
    r, c_ = full.shape
    return full.reshape(r, N_DEV, c_ // N_DEV).transpose(1, 0, 2).reshape(N_DEV, -1)


def _unshard_blocks(blocks, shape, axis):
    if axis == 0:
        return blocks.reshape(shape)
    r, c_ = shape
    return blocks.reshape(N_DEV, r, c_ // N_DEV).transpose(1, 0, 2).reshape(shape)


def _unpack_local(flat, with_conv):
    out = {n: [] for n in BIG_ORDER + (['conv_w'] if with_conv else [])}
    off = 0
    for l in range(DEPTH):
        for n in BIG_ORDER:
            shape, axis = BIG[n]
            loc = tuple(d // N_DEV if a == axis else d for a, d in enumerate(shape))
            sz = loc[0] * loc[1]
            out[n].append(flat[off:off + sz].reshape(loc))
            off += sz
        if with_conv:
            out['conv_w'].append(flat[off:off + 3 * CW // N_DEV].reshape(3, CW // N_DEV))
            off += CONVW_PAD
    return {n: jnp.stack(v) for n, v in out.items()}


def _pad_heads(w, width):
    k = w.shape[0]
    return jnp.pad(w.reshape(k, H, width), ((0, 0), (0, 0), (0, HP - width))).reshape(k, H * HP)


def _unpad_heads(w, width):
    k = w.shape[0]
    return w.reshape(k, H, HP)[:, :, :width].reshape(k, H * width)


IN_SPLIT = {'q_lat': (0, 384), 'kv_lat': (384, 640), 'k_pe': (640, 672), 'c_b': (672, 1184), 'c_c': (1184, 1696),
            'c_u': (1696, 2208), 'q_mem': (2208, 2720), 'g_attn': (2720, 3232), 'g_conv': (3232, 3744),
            'g_mem': (3744, 4256), 'r': (4256, 7328)}
P_ORDER = ['c_b', 'c_c', 'c_u', 'g_conv', 'q_mem', 'g_mem', 'r', 'g_attn', 'q_lat', 'kv_lat']


def _permute_w_in(w):
    k = w.shape[0]
    cols = [w[:, IN_SPLIT[n][0]:IN_SPLIT[n][1]] for n in P_ORDER]
    kpe = w[:, IN_SPLIT['k_pe'][0]:IN_SPLIT['k_pe'][1]]
    cols += [jnp.zeros((k, NOPE), w.dtype), kpe, jnp.zeros((k, HP - QKD), w.dtype)]
    return jnp.concatenate(cols, axis=1)


def _unpermute_w_in(wp):
    off, pos = 0, {}
    for n in P_ORDER:
        wd = IN_SPLIT[n][1] - IN_SPLIT[n][0]
        pos[n] = (off, off + wd)
        off += wd
    pos['k_pe'] = (off + NOPE, off + QKD)
    order = sorted(IN_SPLIT, key=lambda n: IN_SPLIT[n][0])
    return jnp.concatenate([wp[:, pos[n][0]:pos[n][1]] for n in order], axis=1)


def _layer_weights(full, l):
    w = {}
    w_in_p = _permute_w_in(full['w_in'][l])
    w['w_main'] = w_in_p[:, :P_MAIN]
    w['w_l'] = w_in_p[:, P_MAIN:]
    w['w_in_t'] = jnp.pad(w_in_p, ((0, 0), (0, P_WPAD - P_W))).T
    w['w_uq'] = _pad_heads(full['w_uq'][l], QKD)
    wukv = full['w_ukv'][l].reshape(KVL, H, 2, NOPE)
    kpart = jnp.pad(wukv[:, :, 0, :], ((0, 0), (0, 0), (0, HP - NOPE))).reshape(KVL, H * HP)
    vpart = jnp.pad(wukv[:, :, 1, :], ((0, 0), (0, 0), (0, HP - VD))).reshape(KVL, H * HP)
    w['w_ukv'] = jnp.concatenate([kpart, vpart], axis=1)
    w['w_uq_t'] = w['w_uq'].T
    w['w_ukv_t'] = w['w_ukv'].T
    w['w_mkv'] = full['w_mkv'][l]
    w['w_mkv_t'] = w['w_mkv'].T
    for n in ('w_br_attn', 'w_br_conv', 'w_br_mem', 'w_out'):
        w[n] = full[n][l]
        w[n + '_t'] = w[n].T
    return w


def kernel(x, mem, positions, norm_g, w_in, b_gate, q_norm_g, w_uq, kv_norm_g, w_ukv, q_head_g, k_head_g, conv_w, conv_b, mem_norm_g, w_mkv, mem_q_g, mem_k_g, w_br_attn, w_br_conv, w_br_mem, w_out, loss_target, m_norm_g, m_w_in, m_b_gate, m_q_norm_g, m_w_uq, m_kv_norm_g, m_w_ukv, m_q_head_g, m_k_head_g, m_conv_w, m_conv_b, m_mem_norm_g, m_w_mkv, m_mem_q_g, m_mem_k_g, m_w_br_attn, m_w_br_conv, m_w_br_mem, m_w_out, v_norm_g, v_w_in, v_b_gate, v_q_norm_g, v_w_uq, v_kv_norm_g, v_w_ukv, v_q_head_g, v_k_head_g, v_conv_w, v_conv_b, v_mem_norm_g, v_w_mkv, v_mem_q_g, v_mem_k_g, v_w_br_attn, v_w_br_conv, v_w_br_mem, v_w_out):
    a = dict(zip(INPUTS, (x, mem, positions, norm_g, w_in, b_gate, q_norm_g, w_uq, kv_norm_g, w_ukv, q_head_g, k_head_g, conv_w, conv_b, mem_norm_g, w_mkv, mem_q_g, mem_k_g, w_br_attn, w_br_conv, w_br_mem, w_out, loss_target, m_norm_g, m_w_in, m_b_gate, m_q_norm_g, m_w_uq, m_kv_norm_g, m_w_ukv, m_q_head_g, m_k_head_g, m_conv_w, m_conv_b, m_mem_norm_g, m_w_mkv, m_mem_q_g, m_mem_k_g, m_w_br_attn, m_w_br_conv, m_w_br_mem, m_w_out, v_norm_g, v_w_in, v_b_gate, v_q_norm_g, v_w_uq, v_kv_norm_g, v_w_ukv, v_q_head_g, v_k_head_g, v_conv_w, v_conv_b, v_mem_norm_g, v_w_mkv, v_mem_q_g, v_mem_k_g, v_w_br_attn, v_w_br_conv, v_w_br_mem, v_w_out)))
    x = a['x'][0]
    mem = a['mem'][0]
    tgt = a['loss_target'][0]
    s = x.shape[0]
    t_el = 256
    tq_f, tk_f, tq_b, tk_b = 512, 512, 512, 512

    payload = _to_rows(_pack_local(lambda n: a[n], False).astype(BF16))
    gathered = _allgather(payload, "ag_weights").reshape(N_DEV, -1)
    full = {n: [] for n in BIG_ORDER}
    off = 0
    for l in range(DEPTH):
        for n in BIG_ORDER:
            shape, axis = BIG[n]
            sz = shape[0] * shape[1] // N_DEV
            full[n].append(_unshard_blocks(gathered[:, off:off + sz], shape, axis))
            off += sz
    convw_loc = _to_rows(a['conv_w'].reshape(-1), align=8)
    conv_w = _allgather(convw_loc, "ag_conv_w").reshape(N_DEV, -1)[:, :DEPTH * 3 * CW // N_DEV]
    conv_w = conv_w.reshape(N_DEV, DEPTH, 3, CW // N_DEV).transpose(1, 2, 0, 3).reshape(DEPTH, 3, CW)

    inv_freq = ROPE_BASE ** (-jnp.arange(0, RP, 2, dtype=F32) / RP)
    ang = a['positions'][0].astype(F32)[:, None] * inv_freq
    cos, sin = jnp.cos(ang), jnp.sin(ang)
    rc = jnp.concatenate([jnp.ones((s, NOPE), F32), cos, cos, jnp.ones((s, HP - QKD), F32)], axis=1)
    rs = jnp.concatenate([jnp.zeros((s, NOPE), F32), -sin, sin, jnp.zeros((s, HP - QKD), F32)], axis=1)

    def small(n, l, width=None):
        v = a[n][l][None, :]
        return v if width is None else jnp.pad(v, ((0, 0), (0, width - v.shape[1])))

    saved = []
    layer_w = [_layer_weights(full, l) for l in range(DEPTH)]
    for l in range(DEPTH):
        w = layer_w[l]
        tag = f"_l{l}"
        h = _rms_h(x, small('norm_g', l), 512, "rms_h" + tag)
        proj = _mm(h, w['w_main'], BF16, 512, 512, "in_proj" + tag)
        proj_l = _mm(h, w['w_l'], BF16, 512, P_LW, "in_proj_lat" + tag)
        qng, kvng = small('q_norm_g', l), small('kv_norm_g', l)
        qhg, khg = small('q_head_g', l, HP), small('k_head_g', l, HP)
        q, k, v = _mla_prep(proj_l, rc, rs, qng, kvng, qhg, khg, w['w_uq'], w['w_ukv'], t_el, "mla_prep" + tag)
        o, lse = _flash_fwd(q, k, v, tq_f, tk_f, "flash_fwd" + tag)
        mk, mv = _mem_prep(mem, small('mem_norm_g', l), w['w_mkv'], small('mem_k_g', l), "mem_prep" + tag)
        cb, mqg = small('conv_b', l), small('mem_q_g', l)
        oa, oc, om = _branches(proj, o, mk, mv, conv_w[l], cb, mqg, t_el, "branches" + tag)
        x_out, aa, ac, am = _merge_fwd(x, proj, oa, oc, om, small('b_gate', l), w['w_br_attn'], w['w_br_conv'],
                                       w['w_br_mem'], w['w_out'], t_el, "merge" + tag)
        saved.append(dict(x=x, h=h, proj=proj, proj_l=proj_l, q=q, k=k, v=v, o=o, lse=lse, mk=mk, mv=mv,
                          oa=oa, oc=oc, om=om, aa=aa, ac=ac, am=am))
        x = x_out

    g, loss_parts = _loss_grad(x, tgt, 512, "loss")
    loss = lax.psum(jnp.sum(loss_parts), ("x", "y", "c"))

    gw = {n: [None] * DEPTH for n in WEIGHTS}
    for l in reversed(range(DEPTH)):
        w = layer_w[l]
        sv = saved[l]
        tag = f"_l{l}"
        tqb = min(tq_b, s)
        y, daa, dac, dam, dr, doa, doc, dom, dbg = _merge_bwd(
            g, sv['proj'], sv['aa'], sv['ac'], sv['am'], small('b_gate', l), w['w_out_t'], w['w_br_attn_t'],
            w['w_br_conv_t'], w['w_br_mem_t'], t_el, "merge_bwd" + tag)
        gw['b_gate'][l] = dbg[0]
        gw['w_out'][l] = _tn(y, g, 512, 512, "dw_out" + tag)
        gw['w_br_attn'][l] = _tn(sv['oa'], daa, 512, 512, "dw_attn" + tag)
        gw['w_br_conv'][l] = _tn(sv['oc'], dac, 512, 512, "dw_conv" + tag)
        gw['w_br_mem'][l] = _tn(sv['om'], dam, 512, 512, "dw_mem" + tag)
        cb, mqg = small('conv_b', l), small('mem_q_g', l)
        dcv, dmm, dga, do, dl, dcw, dcb, dmk, dmv, dmqg = _branches_bwd(
            sv['proj'], sv['o'], sv['mk'], sv['mv'], conv_w[l], cb, mqg, doa, doc, dom, t_el, "branches_bwd" + tag)
        gw['conv_w'][l], gw['conv_b'][l], gw['mem_q_g'][l] = dcw, dcb[0], dmqg[0]
        dwm, dmng, dmkg = _mem_prep_bwd(mem, small('mem_norm_g', l), w['w_mkv'], w['w_mkv_t'], small('mem_k_g', l),
                                        dmk, dmv, "mem_prep_bwd" + tag)
        gw['w_mkv'][l], gw['mem_norm_g'][l], gw['mem_k_g'][l] = dwm, dmng[0], dmkg[0]
        lse_r = sv['lse'].reshape(H, s // tqb, 1, tqb)
        dl_r = dl.reshape(H, s // tqb, 1, tqb)
        dq, dk, dv = _flash_bwd(sv['q'], sv['k'], sv['v'], do, lse_r, dl_r, tq_b, tk_b, "flash_bwd" + tag)
        qng, kvng = small('q_norm_g', l), small('kv_norm_g', l)
        qhg, khg = small('q_head_g', l, HP), small('k_head_g', l, HP)
        dlat, dwuq, dwukv, dqng, dkvng, dqhg, dkhg = _mla_prep_bwd(
            sv['proj_l'], rc, rs, qng, kvng, qhg, khg, w['w_uq'], w['w_ukv'], w['w_uq_t'], w['w_ukv_t'],
            dq, dk, dv, t_el, "mla_prep_bwd" + tag)
        gw['w_uq'][l] = _unpad_heads(dwuq, QKD)
        dwukv = dwukv.reshape(KVL, 2, H, HP)[:, :, :, :NOPE]
        gw['w_ukv'][l] = dwukv.transpose(0, 2, 1, 3).reshape(KVL, H * 2 * NOPE)
        gw['q_norm_g'][l], gw['kv_norm_g'][l] = dqng[0], dkvng[0]
        gw['q_head_g'][l], gw['k_head_g'][l] = dqhg[0, :QKD], dkhg[0, :QKD]
        dproj = jnp.concatenate([dcv, dmm, dr, dga, dlat, jnp.zeros((s, P_WPAD - P_W), BF16)], axis=1)
        dh = _mm(dproj, w['w_in_t'], F32, 512, 512, "d_h" + tag)
        gw['w_in'][l] = _unpermute_w_in(_tn(sv['h'], dproj, 512, 512, "dw_in" + tag))
        g, dng = _rms_in_bwd(sv['x'], g, dh, small('norm_g', l), 512, "rms_bwd" + tag)
        gw['norm_g'][l] = dng[0]
    grad_x = g[None]

    send = []
    for l in range(DEPTH):
        for n in BIG_ORDER:
            send.append(_shard_blocks(gw[n][l], BIG[n][1]))
        send.append(jnp.pad(_shard_blocks(gw['conv_w'][l], 1), ((0, 0), (0, CONVW_PAD - 3 * CW // N_DEV))))
    send = _to_rows(jnp.concatenate(send, axis=1))
    parts_big = _alltoall(send, "rs_grads")
    small_flat = jnp.concatenate([jnp.stack(gw[n]).reshape(-1) for n in SMALL_ORDER])
    n_small = small_flat.shape[0]
    parts_small = _allgather(_to_rows(small_flat, align=8), "ag_small_grads")

    pk = lambda pre: _to_rows(_pack_local(lambda n: a[pre + n], True))
    res_big = _adamw(parts_big, pk(''), pk('m_'), pk('v_'), 1024, "adamw_big")
    pks = lambda pre: _to_rows(jnp.concatenate([a[pre + n].reshape(-1) for n in SMALL_ORDER]), align=8)
    res_small = _adamw(parts_small, pks(''), pks('m_'), pks('v_'), 1024, "adamw_small")

    outs = []
    for rb, rsm in zip(res_big, res_small):
        d = _unpack_local(rb.reshape(-1), True)
        flat = rsm.reshape(-1)[:n_small]
        off = 0
        for n in SMALL_ORDER:
            d[n] = flat[off:off + DEPTH * SMALL[n]].reshape(DEPTH, SMALL[n])
            off += DEPTH * SMALL[n]
        outs.append(d)
    result = [loss, grad_x]
    for d in outs:
        result += [d[n] for n in WEIGHTS]
    return tuple(result)
```

```python
import functools

import jax
import jax.numpy as jnp
from jax import lax
from jax.experimental import pallas as pl
from jax.experimental.pallas import tpu as pltpu

F32, BF16 = jnp.float32, jnp.bfloat16

N_DEV = 8
DEPTH = 4
D = 1024
QL, KVL, RP = 384, 256, 32
H, NOPE, QKD, VD = 8, 64, 96, 64
HP = 128
CW, MW, AW = 512, 512, 512
HM, MHD = 4, 128
IN_WIDTH = 7328
EPS = 1e-6
ROPE_BASE = 10000.0
ATT_SCALE = QKD ** -0.5
MEM_SCALE = MHD ** -0.5
LOG2E = 1.4426950408889634
QSCALE = ATT_SCALE * LOG2E

ADAM_LR, ADAM_B1, ADAM_B2, ADAM_EPS, ADAM_WD, ADAM_STEP = 0.001, 0.9, 0.999, 1e-08, 0.01, 10

LANES = 128
VMEM_LIMIT = 56 * 1024 * 1024

P_CONV, P_MEM, P_R, P_GA, P_L = 0, 2048, 3072, 6144, 6656
P_MAIN = 6656
P_LW = 768
P_W = P_MAIN + P_LW
P_WPAD = 7680

WEIGHTS = ['norm_g', 'w_in', 'b_gate', 'q_norm_g', 'w_uq', 'kv_norm_g', 'w_ukv', 'q_head_g', 'k_head_g',
           'conv_w', 'conv_b', 'mem_norm_g', 'w_mkv', 'mem_q_g', 'mem_k_g', 'w_br_attn', 'w_br_conv',
           'w_br_mem', 'w_out']
INPUTS = ['x', 'mem', 'positions'] + WEIGHTS + ['loss_target'] + ['m_' + n for n in WEIGHTS] + ['v_' + n for n in WEIGHTS]

BIG = {'w_in': ((D, IN_WIDTH), 1), 'w_uq': ((QL, H * QKD), 1), 'w_ukv': ((KVL, H * 128), 1),
       'w_mkv': ((D, 2 * MW), 0), 'w_br_attn': ((AW, D), 1), 'w_br_conv': ((CW, D), 1),
       'w_br_mem': ((MW, D), 1), 'w_out': ((D, D), 0)}
BIG_ORDER = ['w_in', 'w_uq', 'w_ukv', 'w_mkv', 'w_br_attn', 'w_br_conv', 'w_br_mem', 'w_out']
CONVW_PAD = 256
SMALL = {'norm_g': D, 'b_gate': 3 * D, 'q_norm_g': QL, 'kv_norm_g': KVL, 'q_head_g': QKD, 'k_head_g': QKD,
         'conv_b': CW, 'mem_norm_g': D, 'mem_q_g': MHD, 'mem_k_g': MHD}
SMALL_ORDER = list(SMALL)
ROW_ALIGN = 1024


def _call(body, **kw):
    return pl.pallas_call(body, **kw)


def _cp(sem=None):
    return pltpu.CompilerParams(dimension_semantics=sem, vmem_limit_bytes=VMEM_LIMIT)


def _full(shape):
    n = len(shape)
    return pl.BlockSpec(shape, lambda *_: (0,) * n)


def _rms(x, g, n):
    rs = lax.rsqrt(jnp.sum(x * x, axis=-1, keepdims=True) * (1.0 / n) + EPS)
    xh = x * rs
    return xh * g, xh, rs


def _rms_bwd(dy, xh, rs, g, n):
    dxh = dy * g
    dx = rs * (dxh - xh * (jnp.sum(dxh * xh, axis=-1, keepdims=True) * (1.0 / n)))
    return dx, jnp.sum(dy * xh, axis=0, keepdims=True)


def _sigmoid(x):
    return 1.0 / (1.0 + jnp.exp(-x))


def _swap_rope(u):
    lane = lax.broadcasted_iota(jnp.int32, u.shape, 1)
    up = pltpu.roll(u, 16, 1)
    dn = pltpu.roll(u, 112, 1)
    return jnp.where((lane >= 64) & (lane < 80), dn, jnp.where((lane >= 80) & (lane < 96), up, 0.0))


def _rope(u, c, sn):
    return u * c + _swap_rope(u) * sn


def _rope_adj(d, c, sn):
    return d * c + _swap_rope(d * sn)


def _dot(a, b):
    return jnp.dot(a, b, preferred_element_type=F32)


def _dot_nt(a, b):
    return lax.dot_general(a, b, (((1,), (1,)), ((), ())), preferred_element_type=F32)


def _dot_tn(a, b):
    return lax.dot_general(a, b, (((0,), (0,)), ((), ())), preferred_element_type=F32)


def _mm(a, b, out_dtype, tm, tn, name):
    m, k = a.shape
    _, n = b.shape
    tm, tn = min(tm, m), min(tn, n)

    def body(a_ref, b_ref, o_ref):
        o_ref[...] = _dot(a_ref[...].astype(BF16), b_ref[...]).astype(o_ref.dtype)

    return _call(
        body, name=name, grid=(m // tm, n // tn),
        in_specs=[pl.BlockSpec((tm, k), lambda i, j: (i, 0)), pl.BlockSpec((k, tn), lambda i, j: (0, j))],
        out_specs=pl.BlockSpec((tm, tn), lambda i, j: (i, j)),
        out_shape=jax.ShapeDtypeStruct((m, n), out_dtype),
        compiler_params=_cp(("parallel", "arbitrary")),
    )(a, b)


def _tn(a, b, ts, tn, name):
    s, ka = a.shape
    _, n = b.shape
    ts, tn = min(ts, s), min(tn, n)

    def body(a_ref, b_ref, o_ref):
        @pl.when(pl.program_id(1) == 0)
        def _():
            o_ref[...] = jnp.zeros_like(o_ref)

        o_ref[...] += _dot_tn(a_ref[...].astype(BF16), b_ref[...].astype(BF16))

    return _call(
        body, name=name, grid=(n // tn, s // ts),
        in_specs=[pl.BlockSpec((ts, ka), lambda j, i: (i, 0)), pl.BlockSpec((ts, tn), lambda j, i: (i, j))],
        out_specs=pl.BlockSpec((ka, tn), lambda j, i: (0, j)),
        out_shape=jax.ShapeDtypeStruct((ka, n), F32),
        compiler_params=_cp(("parallel", "arbitrary")),
    )(a, b)


def _rms_h(x, g, t, name):
    s = x.shape[0]
    t = min(t, s)

    def body(x_ref, g_ref, h_ref):
        h_ref[...] = _rms(x_ref[...], g_ref[...], D)[0].astype(BF16)

    return _call(
        body, name=name, grid=(s // t,),
        in_specs=[pl.BlockSpec((t, D), lambda i: (i, 0)), _full((1, D))],
        out_specs=pl.BlockSpec((t, D), lambda i: (i, 0)),
        out_shape=jax.ShapeDtypeStruct((s, D), BF16),
        compiler_params=_cp(("parallel",)),
    )(x, g)


def _mla_heads(pl_blk, c, sn, qng, kvng, qhg, khg, wuq, wukv):
    ql = pl_blk[:, 0:QL].astype(F32)
    kvl = pl_blk[:, QL:QL + KVL].astype(F32)
    kpe = pl_blk[:, QL + KVL:P_LW].astype(F32)
    qn, qxh, qrs = _rms(ql, qng, QL)
    kvn, kvxh, kvrs = _rms(kvl, kvng, KVL)
    qn16, kvn16 = qn.astype(BF16), kvn.astype(BF16)
    qp = _dot(qn16, wuq)
    kvp = _dot(kvn16, wukv)
    return ql, kvl, kpe, qxh, qrs, kvxh, kvrs, qn16, kvn16, qp, kvp


def _mla_prep(proj_l, c, sn, qng, kvng, qhg, khg, wuq, wukv, t, name):
    s = proj_l.shape[0]
    t = min(t, s)

    def body(l_ref, c_ref, sn_ref, qng_ref, kvng_ref, qhg_ref, khg_ref, wuq_ref, wukv_ref, q_ref, k_ref, v_ref):
        cc, ss = c_ref[...], sn_ref[...]
        (_, _, kpe, _, _, _, _, _, _, qp, kvp) = _mla_heads(
            l_ref[...], cc, ss, qng_ref[...], kvng_ref[...], qhg_ref[...], khg_ref[...], wuq_ref[...], wukv_ref[...])
        lane = lax.broadcasted_iota(jnp.int32, cc.shape, 1)
        for h in range(H):
            u = qp[:, h * HP:(h + 1) * HP]
            q_ref[h] = (_rope(_rms(u, qhg_ref[...], QKD)[0], cc, ss) * QSCALE).astype(BF16)
            u = kvp[:, h * HP:(h + 1) * HP] + kpe
            k_ref[h] = _rope(_rms(u, khg_ref[...], QKD)[0], cc, ss).astype(BF16)
            v_ref[h] = jnp.where(lane == VD, 1.0, kvp[:, (H + h) * HP:(H + h + 1) * HP]).astype(BF16)

    hs = pl.BlockSpec((H, t, HP), lambda i: (0, i, 0))
    row = lambda w: pl.BlockSpec((t, w), lambda i: (i, 0))
    return _call(
        body, name=name, grid=(s // t,),
        in_specs=[row(P_LW), row(HP), row(HP), _full((1, QL)), _full((1, KVL)), _full((1, HP)), _full((1, HP)),
                  _full((QL, H * HP)), _full((KVL, 2 * H * HP))],
        out_specs=[hs, hs, hs],
        out_shape=[jax.ShapeDtypeStruct((H, s, HP), BF16)] * 3,
        compiler_params=_cp(("parallel",)),
    )(proj_l, c, sn, qng, kvng, qhg, khg, wuq, wukv)


def _flash_fwd(q, k, v, tq, tk, name):
    _, s, _ = q.shape
    tq, tk = min(tq, s), min(tk, s)
    nk, nc = s // tk, tk // LANES
    un = 4 if nk % 4 == 0 else 1

    def body(q_ref, k_ref, v_ref, o_ref, lse_ref, s_scr):
        qb = q_ref[0]

        def scores(jj, mx):
            for u in range(un):
                j = jj * un + u
                off = pl.multiple_of(j * tk, tk)
                sc = _dot_nt(qb, k_ref[0, pl.ds(off, tk), :])
                s_scr[j] = sc
                for cc in range(nc):
                    mx = jnp.maximum(mx, sc[:, cc * LANES:(cc + 1) * LANES])
            return mx

        mx = lax.fori_loop(0, nk // un, scores, jnp.full((tq, LANES), -jnp.inf, F32))
        m = jnp.max(mx, axis=-1, keepdims=True)
        mb = jnp.broadcast_to(m, (tq, LANES))

        def probs(jj, acc):
            for u in range(un):
                j = jj * un + u
                off = pl.multiple_of(j * tk, tk)
                sc = s_scr[j]
                ps = [jnp.exp2(sc[:, cc * LANES:(cc + 1) * LANES] - mb).astype(BF16) for cc in range(nc)]
                acc = acc + _dot(jnp.concatenate(ps, axis=-1), v_ref[0, pl.ds(off, tk), :])
            return acc

        acc = lax.fori_loop(0, nk // un, probs, jnp.zeros((tq, HP), F32))
        l = acc[:, VD:VD + 1]
        o_ref[0] = (acc[:, :VD] / l).astype(BF16)
        lse = jnp.broadcast_to(m + jnp.log(l) * LOG2E, (tq, LANES))
        lse_ref[0] = lse.T[0:1, :]

    return _call(
        body, name=name, grid=(H, s // tq),
        in_specs=[pl.BlockSpec((1, tq, HP), lambda h, i: (h, i, 0)),
                  pl.BlockSpec((1, s, HP), lambda h, i: (h, 0, 0)),
                  pl.BlockSpec((1, s, HP), lambda h, i: (h, 0, 0))],
        out_specs=[pl.BlockSpec((1, tq, VD), lambda h, i: (h, i, 0)), pl.BlockSpec((1, 1, tq), lambda h, i: (h, 0, i))],
        out_shape=[jax.ShapeDtypeStruct((H, s, VD), BF16), jax.ShapeDtypeStruct((H, 1, s), F32)],
        scratch_shapes=[pltpu.VMEM((nk, tq, tk), F32)],
        compiler_params=_cp(("parallel", "arbitrary")),
    )(q, k, v)


def _mem_prep(mem, mng, wmkv, mkg, name):
    m = mem.shape[0]

    def body(mem_ref, mng_ref, w_ref, mkg_ref, mk_ref, mv_ref):
        mn = _rms(mem_ref[...], mng_ref[...], D)[0].astype(BF16)
        mkv = _dot(mn, w_ref[...])
        for h in range(HM):
            mk_ref[h] = _rms(mkv[:, 2 * h * MHD:(2 * h + 1) * MHD], mkg_ref[...], MHD)[0].astype(BF16)
            mv_ref[h] = mkv[:, (2 * h + 1) * MHD:(2 * h + 2) * MHD].astype(BF16)

    return _call(
        body, name=name,
        in_specs=[_full((m, D)), _full((1, D)), _full((D, 2 * MW)), _full((1, MHD))],
        out_specs=[_full((HM, m, MHD))] * 2,
        out_shape=[jax.ShapeDtypeStruct((HM, m, MHD), BF16)] * 2,
        compiler_params=_cp(),
    )(mem, mng, wmkv, mkg)


def _conv_parts(cv, prev, nxt, first, last, cw, cb):
    t = cv.shape[0]
    c_b, c_c, c_u, g_c = (cv[:, i * CW:(i + 1) * CW].astype(F32) for i in range(4))
    z = c_c * c_u
    zp = jnp.where(first, 0.0, prev[15:16, CW:2 * CW].astype(F32) * prev[15:16, 2 * CW:3 * CW].astype(F32))
    zn = jnp.where(last, 0.0, nxt[0:1, CW:2 * CW].astype(F32) * nxt[0:1, 2 * CW:3 * CW].astype(F32))
    row = lax.broadcasted_iota(jnp.int32, (t, CW), 0)
    z_m1 = jnp.where(row == 0, zp, pltpu.roll(z, 1, 0))
    z_p1 = jnp.where(row == t - 1, zn, pltpu.roll(z, t - 1, 0))
    conv = cw[0:1] * z_m1 + cw[1:2] * z + cw[2:3] * z_p1 + cb
    return c_b, c_c, c_u, g_c, z, z_m1, z_p1, conv


def _mem_attn(qm, mqg, mk_ref, mv_ref):
    outs = []
    for h in range(HM):
        mq, mqxh, mqrs = _rms(qm[:, h * MHD:(h + 1) * MHD], mqg, MHD)
        mq16 = mq.astype(BF16)
        sc = _dot_nt(mq16, mk_ref[h]) * MEM_SCALE
        e = jnp.exp(sc - jnp.max(sc, axis=-1, keepdims=True))
        p = e / jnp.sum(e, axis=-1, keepdims=True)
        o = _dot(p.astype(BF16), mv_ref[h])
        outs.append((mq16, mqxh, mqrs, p, o))
    return outs


def _halo_specs(t, s, width):
    nb = s // 16
    prev = pl.BlockSpec((16, width), lambda i: (jnp.maximum(i * (t // 16) - 1, 0), 0))
    nxt = pl.BlockSpec((16, width), lambda i: (jnp.minimum((i + 1) * (t // 16), nb - 1), 0))
    return prev, nxt


def _branches(proj, o, mk, mv, cw, cb, mqg, t, name):
    s = proj.shape[0]
    t = min(t, s)
    nt = s // t

    def body(cv_ref, pv_ref, nx_ref, mm_ref, ga_ref, o_ref, mk_ref, mv_ref, cw_ref, cb_ref, mqg_ref,
             oa_ref, oc_ref, om_ref):
        i = pl.program_id(0)
        c_b, _, _, g_c, _, _, _, conv = _conv_parts(
            cv_ref[...], pv_ref[...], nx_ref[...], i == 0, i == nt - 1, cw_ref[...], cb_ref[...])
        oc_ref[...] = (c_b * conv * (g_c * _sigmoid(g_c))).astype(BF16)
        ga = ga_ref[...].astype(F32)
        ocat = jnp.concatenate([o_ref[h].astype(F32) for h in range(H)], axis=-1)
        oa_ref[...] = (ocat * (ga * _sigmoid(ga))).astype(BF16)
        mblk = mm_ref[...].astype(F32)
        gm = mblk[:, MW:]
        heads = _mem_attn(mblk[:, :MW], mqg_ref[...], mk_ref, mv_ref)
        om = jnp.concatenate([hh[4] for hh in heads], axis=-1)
        om_ref[...] = (om * (gm * _sigmoid(gm))).astype(BF16)

    pv, nx = _halo_specs(t, s, 4 * CW)
    out = pl.BlockSpec((t, 512), lambda i: (i, 0))
    return _call(
        body, name=name, grid=(nt,),
        in_specs=[pl.BlockSpec((t, 4 * CW), lambda i: (i, 0)), pv, nx,
                  pl.BlockSpec((t, 2 * MW), lambda i: (i, P_MEM // (2 * MW))),
                  pl.BlockSpec((t, AW), lambda i: (i, P_GA // AW)),
                  pl.BlockSpec((H, t, VD), lambda i: (0, i, 0)),
                  _full(mk.shape), _full(mv.shape), _full((3, CW)), _full((1, CW)), _full((1, MHD))],
        out_specs=[out, out, out],
        out_shape=[jax.ShapeDtypeStruct((s, 512), BF16)] * 3,
        compiler_params=_cp(("parallel",)),
    )(proj, proj, proj, proj, proj, o, mk, mv, cw, cb, mqg)


def _merge_fwd(x, proj, oa, oc, om, bg, wa, wc, wm, wo, t, name):
    s = x.shape[0]
    t = min(t, s)

    def body(x_ref, r_ref, oa_ref, oc_ref, om_ref, bg_ref, wa_ref, wc_ref, wm_ref, wo_ref,
             xo_ref, aa_ref, ac_ref, am_ref):
        y = jnp.zeros((t, D), F32)
        for j, (o_ref, w_ref, a_ref) in enumerate(((oa_ref, wa_ref, aa_ref), (oc_ref, wc_ref, ac_ref),
                                                   (om_ref, wm_ref, am_ref))):
            a = _dot(o_ref[...], w_ref[...])
            a_ref[...] = a.astype(BF16)
            rg = _sigmoid(r_ref[:, j * D:(j + 1) * D].astype(F32) + bg_ref[:, j * D:(j + 1) * D])
            y = y + rg * a
        xo_ref[...] = x_ref[...] + _dot(y.astype(BF16), wo_ref[...])

    row = lambda w: pl.BlockSpec((t, w), lambda i: (i, 0))
    return _call(
        body, name=name, grid=(s // t,),
        in_specs=[row(D), pl.BlockSpec((t, 3 * D), lambda i: (i, P_R // (3 * D))), row(512), row(512), row(512),
                  _full((1, 3 * D)), _full((512, D)), _full((512, D)), _full((512, D)), _full((D, D))],
        out_specs=[row(D), row(D), row(D), row(D)],
        out_shape=[jax.ShapeDtypeStruct((s, D), F32)] + [jax.ShapeDtypeStruct((s, D), BF16)] * 3,
        compiler_params=_cp(("parallel",)),
    )(x, proj, oa, oc, om, bg, wa, wc, wm, wo)


def _loss_grad(x, tgt, t, name):
    s = x.shape[0]
    t = min(t, s)

    def body(x_ref, t_ref, g_ref, l_ref):
        @pl.when(pl.program_id(0) == 0)
        def _():
            l_ref[...] = jnp.zeros_like(l_ref)

        e = x_ref[...] - t_ref[...]
        g_ref[...] = e * (1.0 / D)
        sq = e * e
        part = sq[:, 0:LANES]
        for j in range(1, D // LANES):
            part = part + sq[:, j * LANES:(j + 1) * LANES]
        acc = part[0:8]
        for j in range(1, t // 8):
            acc = acc + part[j * 8:(j + 1) * 8]
        l_ref[...] += acc * (0.5 / D)

    row = pl.BlockSpec((t, D), lambda i: (i, 0))
    return _call(
        body, name=name, grid=(s // t,),
        in_specs=[row, row], out_specs=[row, _full((8, LANES))],
        out_shape=[jax.ShapeDtypeStruct((s, D), F32), jax.ShapeDtypeStruct((8, LANES), F32)],
        compiler_params=_cp(("arbitrary",)),
    )(x, tgt)


def _merge_bwd(g, proj, aa, ac, am, bg, wot, wat, wct, wmt, t, name):
    s = g.shape[0]
    t = min(t, s)

    def body(g_ref, r_ref, aa_ref, ac_ref, am_ref, bg_ref, wot_ref, wat_ref, wct_ref, wmt_ref,
             y_ref, daa_ref, dac_ref, dam_ref, dr_ref, doa_ref, doc_ref, dom_ref, dbg_ref):
        @pl.when(pl.program_id(0) == 0)
        def _():
            dbg_ref[...] = jnp.zeros_like(dbg_ref)

        dy = _dot(g_ref[...].astype(BF16), wot_ref[...])
        y = jnp.zeros((t, D), F32)
        for j, (a_ref, wt_ref, da_ref, do_ref) in enumerate(((aa_ref, wat_ref, daa_ref, doa_ref),
                                                             (ac_ref, wct_ref, dac_ref, doc_ref),
                                                             (am_ref, wmt_ref, dam_ref, dom_ref))):
            a = a_ref[...].astype(F32)
            rg = _sigmoid(r_ref[:, j * D:(j + 1) * D].astype(F32) + bg_ref[:, j * D:(j + 1) * D])
            y = y + rg * a
            da = (dy * rg).astype(BF16)
            da_ref[...] = da
            dr = dy * a * rg * (1.0 - rg)
            dr_ref[:, j * D:(j + 1) * D] = dr.astype(BF16)
            dbg_ref[:, j * D:(j + 1) * D] += jnp.sum(dr, axis=0, keepdims=True)
            do_ref[...] = _dot(da, wt_ref[...])
        y_ref[...] = y.astype(BF16)

    row = lambda w: pl.BlockSpec((t, w), lambda i: (i, 0))
    return _call(
        body, name=name, grid=(s // t,),
        in_specs=[row(D), pl.BlockSpec((t, 3 * D), lambda i: (i, P_R // (3 * D))), row(D), row(D), row(D),
                  _full((1, 3 * D)), _full((D, D)), _full((D, 512)), _full((D, 512)), _full((D, 512))],
        out_specs=[row(D), row(D), row(D), row(D), row(3 * D), row(512), row(512), row(512), _full((1, 3 * D))],
        out_shape=[jax.ShapeDtypeStruct((s, D), BF16)] * 4 + [jax.ShapeDtypeStruct((s, 3 * D), BF16)]
        + [jax.ShapeDtypeStruct((s, 512), F32)] * 3 + [jax.ShapeDtypeStruct((1, 3 * D), F32)],
        compiler_params=_cp(("arbitrary",)),
    )(g, proj, aa, ac, am, bg, wot, wat, wct, wmt)


def _branches_bwd(proj, o, mk, mv, cw, cb, mqg, doa, doc, dom, t, name):
    s = proj.shape[0]
    t = min(t, s)
    nt = s // t
    m = mk.shape[1]

    def body(cv_ref, pv_ref, nx_ref, mm_ref, ga_ref, o_ref, mk_ref, mv_ref, cw_ref, cb_ref, mqg_ref,
             doa_ref, doc_ref, dcp_ref, dcn_ref, dom_ref,
             dcv_ref, dmm_ref, dga_ref, do_ref, dl_ref, dcw_ref, dcb_ref, dmk_ref, dmv_ref, dmqg_ref):
        i = pl.program_id(0)

        @pl.when(i == 0)
        def _():
            for r in (dcw_ref, dcb_ref, dmk_ref, dmv_ref, dmqg_ref):
                r[...] = jnp.zeros_like(r)

        first, last = i == 0, i == nt - 1
        cw_, cb_ = cw_ref[...], cb_ref[...]
        pv, nx = pv_ref[...], nx_ref[...]
        c_b, c_c, c_u, g_c, z, z_m1, z_p1, conv = _conv_parts(cv_ref[...], pv, nx, first, last, cw_, cb_)
        sg = _sigmoid(g_c)
        silu = g_c * sg
        dsilu = sg * (1.0 + g_c * (1.0 - sg))
        doc_ = doc_ref[...]
        dconv = doc_ * c_b * silu
        gp = pv[15:16, 3 * CW:4 * CW].astype(F32)
        gn = nx[0:1, 3 * CW:4 * CW].astype(F32)
        dconv_p = jnp.where(first, 0.0, dcp_ref[15:16, :] * pv[15:16, 0:CW].astype(F32) * (gp * _sigmoid(gp)))
        dconv_n = jnp.where(last, 0.0, dcn_ref[0:1, :] * nx[0:1, 0:CW].astype(F32) * (gn * _sigmoid(gn)))
        row = lax.broadcasted_iota(jnp.int32, (t, CW), 0)
        d_m1 = jnp.where(row == 0, dconv_p, pltpu.roll(dconv, 1, 0))
        d_p1 = jnp.where(row == t - 1, dconv_n, pltpu.roll(dconv, t - 1, 0))
        dz = cw_[0:1] * d_p1 + cw_[1:2] * dconv + cw_[2:3] * d_m1
        dcv_ref[:, 0:CW] = (doc_ * conv * silu).astype(BF16)
        dcv_ref[:, CW:2 * CW] = (dz * c_u).astype(BF16)
        dcv_ref[:, 2 * CW:3 * CW] = (dz * c_c).astype(BF16)
        dcv_ref[:, 3 * CW:4 * CW] = (doc_ * c_b * conv * dsilu).astype(BF16)
        dcw_ref[0:1, :] += jnp.sum(dconv * z_m1, axis=0, keepdims=True)
        dcw_ref[1:2, :] += jnp.sum(dconv * z, axis=0, keepdims=True)
        dcw_ref[2:3, :] += jnp.sum(dconv * z_p1, axis=0, keepdims=True)
        dcb_ref[...] += jnp.sum(dconv, axis=0, keepdims=True)
        ga = ga_ref[...].astype(F32)
        sg = _sigmoid(ga)
        doa_ = doa_ref[...]
        ocat = jnp.concatenate([o_ref[h].astype(F32) for h in range(H)], axis=-1)
        dga_ref[...] = (doa_ * ocat * (sg * (1.0 + ga * (1.0 - sg)))).astype(BF16)
        dog = doa_ * (ga * sg)
        zeros = jnp.zeros((t, HP - VD), F32)
        lane = lax.broadcasted_iota(jnp.int32, (t, LANES), 1)
        dmat = jnp.zeros((t, LANES), F32)
        for h in range(H):
            dh = dog[:, h * VD:(h + 1) * VD]
            do_ref[h] = jnp.concatenate([dh, zeros], axis=-1).astype(BF16)
            dmat = jnp.where(lane == h, jnp.sum(dh * ocat[:, h * VD:(h + 1) * VD], axis=-1, keepdims=True), dmat)
        dlt = dmat.T
        for h in range(H):
            dl_ref[h] = dlt[h:h + 1, :]
        mblk = mm_ref[...].astype(F32)
        gm = mblk[:, MW:]
        sg = _sigmoid(gm)
        dom_ = dom_ref[...]
        heads = _mem_attn(mblk[:, :MW], mqg_ref[...], mk_ref, mv_ref)
        om = jnp.concatenate([hh[4] for hh in heads], axis=-1)
        dmm_ref[:, MW:] = (dom_ * om * (sg * (1.0 + gm * (1.0 - sg)))).astype(BF16)
        dmo = dom_ * (gm * sg)
        dmqg = jnp.zeros((1, MHD), F32)
        for h in range(HM):
            mq16, mqxh, mqrs, p, _ = heads[h]
            dmo_h = dmo[:, h * MHD:(h + 1) * MHD].astype(BF16)
            dp = _dot_nt(dmo_h, mv_ref[h])
            ds = (p * (dp - jnp.sum(dp * p, axis=-1, keepdims=True)) * MEM_SCALE).astype(BF16)
            dmq = _dot(ds, mk_ref[h])
            dmk_ref[h] += _dot_tn(ds, mq16)
            dmv_ref[h] += _dot_tn(p.astype(BF16), dmo_h)
            dq, dg = _rms_bwd(dmq, mqxh, mqrs, mqg_ref[...], MHD)
            dmm_ref[:, h * MHD:(h + 1) * MHD] = dq.astype(BF16)
            dmqg = dmqg + dg
        dmqg_ref[...] += dmqg

    pv, nx = _halo_specs(t, s, 4 * CW)
    dpv, dnx = _halo_specs(t, s, CW)
    row = lambda w: pl.BlockSpec((t, w), lambda i: (i, 0))
    hs = lambda w: pl.BlockSpec((H, t, w), lambda i: (0, i, 0))
    return _call(
        body, name=name, grid=(nt,),
        in_specs=[pl.BlockSpec((t, 4 * CW), lambda i: (i, 0)), pv, nx,
                  pl.BlockSpec((t, 2 * MW), lambda i: (i, P_MEM // (2 * MW))),
                  pl.BlockSpec((t, AW), lambda i: (i, P_GA // AW)),
                  hs(VD), _full(mk.shape), _full(mv.shape), _full((3, CW)), _full((1, CW)), _full((1, MHD)),
                  row(512), row(512), dpv, dnx, row(512)],
        out_specs=[row(4 * CW), row(2 * MW), row(AW), hs(HP), pl.BlockSpec((H, 1, t), lambda i: (0, 0, i)),
                   _full((3, CW)), _full((1, CW)),
                   _full((HM, m, MHD)), _full((HM, m, MHD)), _full((1, MHD))],
        out_shape=[jax.ShapeDtypeStruct((s, 4 * CW), BF16), jax.ShapeDtypeStruct((s, 2 * MW), BF16),
                   jax.ShapeDtypeStruct((s, AW), BF16), jax.ShapeDtypeStruct((H, s, HP), BF16),
                   jax.ShapeDtypeStruct((H, 1, s), F32), jax.ShapeDtypeStruct((3, CW), F32),
                   jax.ShapeDtypeStruct((1, CW), F32), jax.ShapeDtypeStruct((HM, m, MHD), F32),
                   jax.ShapeDtypeStruct((HM, m, MHD), F32), jax.ShapeDtypeStruct((1, MHD), F32)],
        compiler_params=_cp(("arbitrary",)),
    )(proj, proj, proj, proj, proj, o, mk, mv, cw, cb, mqg, doa, doc, doc, doc, dom)


def _flash_bwd(q, k, v, do, lse, dl, tq, tk, name):
    _, s, _ = q.shape
    tq, tk = min(tq, s), min(tk, s)
    nq, nkt = s // tq, s // tk
    unroll = 2 if nq % 2 == 0 else 1

    def body(q_ref, do_ref, lse_ref, dl_ref, k_ref, v_ref, dq_ref, dk_ref, dv_ref, dq_acc):
        j = pl.program_id(1)

        @pl.when(j == 0)
        def _():
            dq_acc[...] = jnp.zeros_like(dq_acc)

        kb, vb = k_ref[0], v_ref[0]

        def step(ii, carry):
            dk, dv = carry
            for u in range(unroll):
                i = ii * unroll + u
                off = pl.multiple_of(i * tq, tq)
                qb = q_ref[0, pl.ds(off, tq), :]
                dob = do_ref[0, pl.ds(off, tq), :]
                st = _dot_nt(kb, qb)
                pt = jnp.exp2(st - lse_ref[0, i])
                dv = dv + _dot(pt.astype(BF16), dob)
                dpt = _dot_nt(vb, dob)
                dst = (pt * (dpt - dl_ref[0, i])).astype(BF16)
                dk = dk + _dot(dst, qb)
                dq_acc[pl.ds(off, tq), :] += _dot_tn(dst, kb)
            return dk, dv

        dk, dv = lax.fori_loop(0, nq // unroll, step, (jnp.zeros((tk, HP), F32), jnp.zeros((tk, HP), F32)))
        dk_ref[0] = (dk * (1.0 / LOG2E)).astype(BF16)
        dv_ref[0] = dv.astype(BF16)

        @pl.when(j == nkt - 1)
        def _():
            dq_ref[0] = (dq_acc[...] * ATT_SCALE).astype(BF16)

    whole = pl.BlockSpec((1, s, HP), lambda h, j: (h, 0, 0))
    stat = pl.BlockSpec((1, nq, 1, tq), lambda h, j: (h, 0, 0, 0))
    tile = pl.BlockSpec((1, tk, HP), lambda h, j: (h, j, 0))
    return _call(
        body, name=name, grid=(H, nkt),
        in_specs=[whole, whole, stat, stat, tile, tile],
        out_specs=[whole, tile, tile],
        out_shape=[jax.ShapeDtypeStruct((H, s, HP), BF16)] * 3,
        scratch_shapes=[pltpu.VMEM((s, HP), F32)],
        compiler_params=_cp(("arbitrary", "arbitrary")),
    )(q, do, lse, dl, k, v)


def _mla_prep_bwd(proj_l, c, sn, qng, kvng, qhg, khg, wuq, wukv, wuqt, wukvt, dq, dk, dv, t, name):
    s = proj_l.shape[0]
    t = min(t, s)

    def body(l_ref, c_ref, sn_ref, qng_ref, kvng_ref, qhg_ref, khg_ref, wuq_ref, wukv_ref, wuqt_ref, wukvt_ref,
             dq_ref, dk_ref, dv_ref, dl_ref, dwuq_ref, dwukv_ref, dqng_ref, dkvng_ref, dqhg_ref, dkhg_ref):
        @pl.when(pl.program_id(0) == 0)
        def _():
            for r in (dwuq_ref, dwukv_ref, dqng_ref, dkvng_ref, dqhg_ref, dkhg_ref):
                r[...] = jnp.zeros_like(r)

        cc, ss = c_ref[...], sn_ref[...]
        qhg, khg = qhg_ref[...], khg_ref[...]
        (_, _, kpe, qxh, qrs, kvxh, kvrs, qn16, kvn16, qp, kvp) = _mla_heads(
            l_ref[...], cc, ss, qng_ref[...], kvng_ref[...], qhg, khg, wuq_ref[...], wukv_ref[...])
        lane = lax.broadcasted_iota(jnp.int32, (t, HP), 1)
        dqp, dkp, dvp = [], [], []
        dkpe = jnp.zeros((t, HP), F32)
        dqhg = jnp.zeros((1, HP), F32)
        dkhg = jnp.zeros((1, HP), F32)
        for h in range(H):
            _, xh, rs = _rms(qp[:, h * HP:(h + 1) * HP], qhg, QKD)
            du, dg = _rms_bwd(_rope_adj(dq_ref[h].astype(F32), cc, ss), xh, rs, qhg, QKD)
            dqp.append(du)
            dqhg = dqhg + dg
            _, xh, rs = _rms(kvp[:, h * HP:(h + 1) * HP] + kpe, khg, QKD)
            du, dg = _rms_bwd(_rope_adj(dk_ref[h].astype(F32), cc, ss), xh, rs, khg, QKD)
            dkp.append(du)
            dkhg = dkhg + dg
            dkpe = dkpe + jnp.where((lane >= NOPE) & (lane < QKD), du, 0.0)
            dvp.append(dv_ref[h].astype(F32))
        dqhg_ref[...] += dqhg
        dkhg_ref[...] += dkhg
        dqp16 = jnp.concatenate(dqp, axis=-1).astype(BF16)
        dkvp16 = jnp.concatenate(dkp + dvp, axis=-1).astype(BF16)
        dwuq_ref[...] += _dot_tn(qn16, dqp16)
        dwukv_ref[...] += _dot_tn(kvn16, dkvp16)
        dql, dg = _rms_bwd(_dot(dqp16, wuqt_ref[...]), qxh, qrs, qng_ref[...], QL)
        dqng_ref[...] += dg
        dkvl, dg = _rms_bwd(_dot(dkvp16, wukvt_ref[...]), kvxh, kvrs, kvng_ref[...], KVL)
        dkvng_ref[...] += dg
        dl_ref[:, 0:QL] = dql.astype(BF16)
        dl_ref[:, QL:QL + KVL] = dkvl.astype(BF16)
        dl_ref[:, QL + KVL:P_LW] = dkpe.astype(BF16)

    hs = pl.BlockSpec((H, t, HP), lambda i: (0, i, 0))
    row = lambda w: pl.BlockSpec((t, w), lambda i: (i, 0))
    return _call(
        body, name=name, grid=(s // t,),
        in_specs=[row(P_LW), row(HP), row(HP), _full((1, QL)), _full((1, KVL)), _full((1, HP)), _full((1, HP)),
                  _full((QL, H * HP)), _full((KVL, 2 * H * HP)), _full((H * HP, QL)), _full((2 * H * HP, KVL)),
                  hs, hs, hs],
        out_specs=[row(P_LW), _full((QL, H * HP)), _full((KVL, 2 * H * HP)), _full((1, QL)), _full((1, KVL)),
                   _full((1, HP)), _full((1, HP))],
        out_shape=[jax.ShapeDtypeStruct((s, P_LW), BF16), jax.ShapeDtypeStruct((QL, H * HP), F32),
                   jax.ShapeDtypeStruct((KVL, 2 * H * HP), F32), jax.ShapeDtypeStruct((1, QL), F32),
                   jax.ShapeDtypeStruct((1, KVL), F32), jax.ShapeDtypeStruct((1, HP), F32),
                   jax.ShapeDtypeStruct((1, HP), F32)],
        compiler_params=_cp(("arbitrary",)),
    )(proj_l, c, sn, qng, kvng, qhg, khg, wuq, wukv, wuqt, wukvt, dq, dk, dv)


def _mem_prep_bwd(mem, mng, wmkv, wmkvt, mkg, dmk, dmv, name):
    m = mem.shape[0]

    def body(mem_ref, mng_ref, w_ref, wt_ref, mkg_ref, dmk_ref, dmv_ref, dw_ref, dmng_ref, dmkg_ref):
        mn, xh, _ = _rms(mem_ref[...], mng_ref[...], D)
        mn16 = mn.astype(BF16)
        mkv = _dot(mn16, w_ref[...])
        parts = []
        dmkg = jnp.zeros((1, MHD), F32)
        for h in range(HM):
            _, kxh, krs = _rms(mkv[:, 2 * h * MHD:(2 * h + 1) * MHD], mkg_ref[...], MHD)
            du, dg = _rms_bwd(dmk_ref[h], kxh, krs, mkg_ref[...], MHD)
            dmkg = dmkg + dg
            parts += [du, dmv_ref[h]]
        dmkv = jnp.concatenate(parts, axis=-1).astype(BF16)
        dw_ref[...] = _dot_tn(mn16, dmkv)
        dmn = _dot(dmkv, wt_ref[...])
        dmng_ref[...] = jnp.sum(dmn * xh, axis=0, keepdims=True)
        dmkg_ref[...] = dmkg

    return _call(
        body, name=name,
        in_specs=[_full((m, D)), _full((1, D)), _full((D, 2 * MW)), _full((2 * MW, D)), _full((1, MHD)),
                  _full((HM, m, MHD)), _full((HM, m, MHD))],
        out_specs=[_full((D, 2 * MW)), _full((1, D)), _full((1, MHD))],
        out_shape=[jax.ShapeDtypeStruct((D, 2 * MW), F32), jax.ShapeDtypeStruct((1, D), F32),
                   jax.ShapeDtypeStruct((1, MHD), F32)],
        compiler_params=_cp(),
    )(mem, mng, wmkv, wmkvt, mkg, dmk, dmv)


def _rms_in_bwd(x, g_out, dh, ng, t, name):
    s = x.shape[0]
    t = min(t, s)

    def body(x_ref, go_ref, dh_ref, ng_ref, dx_ref, dng_ref):
        @pl.when(pl.program_id(0) == 0)
        def _():
            dng_ref[...] = jnp.zeros_like(dng_ref)

        _, xh, rs = _rms(x_ref[...], ng_ref[...], D)
        dx, dg = _rms_bwd(dh_ref[...], xh, rs, ng_ref[...], D)
        dx_ref[...] = go_ref[...] + dx
        dng_ref[...] += dg

    row = pl.BlockSpec((t, D), lambda i: (i, 0))
    return _call(
        body, name=name, grid=(s // t,),
        in_specs=[row, row, row, _full((1, D))], out_specs=[row, _full((1, D))],
        out_shape=[jax.ShapeDtypeStruct((s, D), F32), jax.ShapeDtypeStruct((1, D), F32)],
        compiler_params=_cp(("arbitrary",)),
    )(x, g_out, dh, ng)


def _allgather(arrs, name):
    n = len(arrs)

    def body(*refs):
        x_refs, out_refs = refs[:n], refs[n:2 * n]
        send_sems, recv_sems, local_sems = refs[2 * n:]
        x, y, c = lax.axis_index("x"), lax.axis_index("y"), lax.axis_index("c")
        me, sibling = (x, y, c), (x, y, 1 - c)
        chips = [(1 - x, y), (x, 1 - y), (1 - x, 1 - y)]

        def slot(a, px, py, pc):
            return out_refs[a].at[4 * px + 2 * py + pc]

        def copy(a, k, block, to, src=None):
            return pltpu.make_async_remote_copy(
                src_ref=slot(a, *block) if src is None else src, dst_ref=slot(a, *block),
                send_sem=send_sems.at[a, k], recv_sem=recv_sems.at[a, k],
                device_id=to, device_id_type=pl.DeviceIdType.MESH)

        mine = [pltpu.make_async_copy(x_refs[a], slot(a, *me), local_sems.at[a]) for a in range(n)]
        first, passed = [], []
        for a in range(n):
            mine[a].start()
            first.append(copy(a, 0, me, sibling, src=x_refs[a]))
            first += [copy(a, 1 + j, me, (*chip, c), src=x_refs[a]) for j, chip in enumerate(chips)]
        for cp in first:
            cp.start()
        for j, chip in enumerate(chips):
            for a in range(n):
                copy(a, 1 + j, (*chip, c), me).wait_recv()
                passed.append(copy(a, 4 + j, (*chip, c), sibling))
                passed[-1].start()
        for a in range(n):
            copy(a, 0, sibling, me).wait_recv()
        for j, chip in enumerate(chips):
            for a in range(n):
                copy(a, 4 + j, (*chip, 1 - c), me).wait_recv()
        for cp in first + passed:
            cp.wait_send()
        for cp in mine:
            cp.wait()

    any_spec = pl.BlockSpec(memory_space=pl.ANY)
    return _call(
        body, name=name,
        in_specs=[any_spec] * n, out_specs=[any_spec] * n,
        out_shape=[jax.ShapeDtypeStruct((N_DEV,) + a.shape, a.dtype) for a in arrs],
        scratch_shapes=[pltpu.SemaphoreType.DMA((n, 7)), pltpu.SemaphoreType.DMA((n, 7)),
                        pltpu.SemaphoreType.DMA((n,))],
    )(*arrs)


def _alltoall(arrs, name):
    n = len(arrs)

    def body(*refs):
        x_refs, out_refs = refs[:n], refs[n:2 * n]
        send_sems, recv_sems, local_sems = refs[2 * n:]
        x, y, c = lax.axis_index("x"), lax.axis_index("y"), lax.axis_index("c")
        me = 4 * x + 2 * y + c
        mine = [pltpu.make_async_copy(x_refs[a].at[me], out_refs[a].at[me], local_sems.at[a]) for a in range(n)]
        for cp in mine:
            cp.start()
        copies = []
        for k in range(1, N_DEV):
            bx, by, bc = (k >> 2) & 1, (k >> 1) & 1, k & 1
            px = 1 - x if bx else x
            py = 1 - y if by else y
            pc = 1 - c if bc else c
            for a in range(n):
                copies.append(pltpu.make_async_remote_copy(
                    src_ref=x_refs[a].at[4 * px + 2 * py + pc], dst_ref=out_refs[a].at[me],
                    send_sem=send_sems.at[a, k - 1], recv_sem=recv_sems.at[a, k - 1],
                    device_id=(px, py, pc), device_id_type=pl.DeviceIdType.MESH))
        for cp in copies:
            cp.start()
        for cp in copies:
            cp.wait_recv()
        for cp in copies:
            cp.wait_send()
        for cp in mine:
            cp.wait()

    any_spec = pl.BlockSpec(memory_space=pl.ANY)
    return _call(
        body, name=name,
        in_specs=[any_spec] * n, out_specs=[any_spec] * n,
        out_shape=[jax.ShapeDtypeStruct(a.shape, a.dtype) for a in arrs],
        scratch_shapes=[pltpu.SemaphoreType.DMA((n, 7)), pltpu.SemaphoreType.DMA((n, 7)),
                        pltpu.SemaphoreType.DMA((n,))],
    )(*arrs)


ADAMW_BLOCK_BYTES = 4 * 1024 * 1024


def _adamw(parts, w, m, v, name):
    r, c_ = w.shape
    cpad = -(-c_ // LANES) * LANES
    tr = r
    while N_DEV * tr * cpad * 4 > ADAMW_BLOCK_BYTES and tr % 16 == 0:
        tr //= 2
    c1 = 1.0 / (1.0 - ADAM_B1 ** ADAM_STEP)
    c2 = 1.0 / (1.0 - ADAM_B2 ** ADAM_STEP)

    def body(p_ref, w_ref, m_ref, v_ref, g_ref, d_ref, nm_ref, nv_ref):
        g = p_ref[0]
        for j in range(1, N_DEV):
            g = g + p_ref[j]
        nm = ADAM_B1 * m_ref[...] + (1.0 - ADAM_B1) * g
        nv = ADAM_B2 * v_ref[...] + (1.0 - ADAM_B2) * (g * g)
        g_ref[...] = g
        nm_ref[...] = nm
        nv_ref[...] = nv
        d_ref[...] = -ADAM_LR * ((nm * c1) / (jnp.sqrt(nv * c2) + ADAM_EPS) + ADAM_WD * w_ref[...])

    row = pl.BlockSpec((tr, c_), lambda i: (i, 0))
    return _call(
        body, name=name, grid=(r // tr,),
        in_specs=[pl.BlockSpec((N_DEV, tr, c_), lambda i: (0, i, 0)), row, row, row],
        out_specs=[row] * 4, out_shape=[jax.ShapeDtypeStruct((r, c_), F32)] * 4,
        compiler_params=_cp(("parallel",)),
    )(parts, w, m, v)


def _to_rows(flat, align=8):
    n = flat.shape[-1]
    rows = -(-n // (LANES * align)) * align
    return jnp.pad(flat, (0, rows * LANES - n)).reshape(rows, LANES)


def _shard_blocks(full, axis):
    r, c_ = full.shape
    if axis == 0:
        return full.reshape(N_DEV, r // N_DEV, c_)
    return full.reshape(r, N_DEV, c_ // N_DEV).transpose(1, 0, 2)


def _unshard_blocks(blocks, axis):
    _, r, c_ = blocks.shape
    if axis == 0:
        return blocks.reshape(N_DEV * r, c_)
    return blocks.transpose(1, 0, 2).reshape(r, N_DEV * c_)


def _pad_heads(w, width):
    k = w.shape[0]
    return jnp.pad(w.reshape(k, H, width), ((0, 0), (0, 0), (0, HP - width))).reshape(k, H * HP)


def _unpad_heads(w, width):
    k = w.shape[0]
    return w.reshape(k, H, HP)[:, :, :width].reshape(k, H * width)


IN_SPLIT = {'q_lat': (0, 384), 'kv_lat': (384, 640), 'k_pe': (640, 672), 'c_b': (672, 1184), 'c_c': (1184, 1696),
            'c_u': (1696, 2208), 'q_mem': (2208, 2720), 'g_attn': (2720, 3232), 'g_conv': (3232, 3744),
            'g_mem': (3744, 4256), 'r': (4256, 7328)}
P_ORDER = ['c_b', 'c_c', 'c_u', 'g_conv', 'q_mem', 'g_mem', 'r', 'g_attn', 'q_lat', 'kv_lat']


def _permute_w_in(w):
    k = w.shape[0]
    cols = [w[:, IN_SPLIT[n][0]:IN_SPLIT[n][1]] for n in P_ORDER]
    kpe = w[:, IN_SPLIT['k_pe'][0]:IN_SPLIT['k_pe'][1]]
    cols += [jnp.zeros((k, NOPE), w.dtype), kpe, jnp.zeros((k, HP - QKD), w.dtype)]
    return jnp.concatenate(cols, axis=1)


def _unpermute_w_in(wp):
    off, pos = 0, {}
    for n in P_ORDER:
        wd = IN_SPLIT[n][1] - IN_SPLIT[n][0]
        pos[n] = (off, off + wd)
        off += wd
    pos['k_pe'] = (off + NOPE, off + QKD)
    order = sorted(IN_SPLIT, key=lambda n: IN_SPLIT[n][0])
    return jnp.concatenate([wp[:, pos[n][0]:pos[n][1]] for n in order], axis=1)


def _layer_weights(full, l):
    w = {}
    w_in_p = _permute_w_in(full['w_in'][l])
    w['w_main'] = w_in_p[:, :P_MAIN]
    w['w_l'] = w_in_p[:, P_MAIN:]
    w['w_in_t'] = jnp.pad(w_in_p, ((0, 0), (0, P_WPAD - P_W))).T
    w['w_uq'] = _pad_heads(full['w_uq'][l], QKD)
    wukv = full['w_ukv'][l].reshape(KVL, H, 2, NOPE)
    kpart = jnp.pad(wukv[:, :, 0, :], ((0, 0), (0, 0), (0, HP - NOPE))).reshape(KVL, H * HP)
    vpart = jnp.pad(wukv[:, :, 1, :], ((0, 0), (0, 0), (0, HP - VD))).reshape(KVL, H * HP)
    w['w_ukv'] = jnp.concatenate([kpart, vpart], axis=1)
    w['w_uq_t'] = w['w_uq'].T
    w['w_ukv_t'] = w['w_ukv'].T
    w['w_mkv'] = full['w_mkv'][l]
    w['w_mkv_t'] = w['w_mkv'].T
    for n in ('w_br_attn', 'w_br_conv', 'w_br_mem', 'w_out'):
        w[n] = full[n][l]
        w[n + '_t'] = w[n].T
    return w


def kernel(x, mem, positions, norm_g, w_in, b_gate, q_norm_g, w_uq, kv_norm_g, w_ukv, q_head_g, k_head_g, conv_w, conv_b, mem_norm_g, w_mkv, mem_q_g, mem_k_g, w_br_attn, w_br_conv, w_br_mem, w_out, loss_target, m_norm_g, m_w_in, m_b_gate, m_q_norm_g, m_w_uq, m_kv_norm_g, m_w_ukv, m_q_head_g, m_k_head_g, m_conv_w, m_conv_b, m_mem_norm_g, m_w_mkv, m_mem_q_g, m_mem_k_g, m_w_br_attn, m_w_br_conv, m_w_br_mem, m_w_out, v_norm_g, v_w_in, v_b_gate, v_q_norm_g, v_w_uq, v_kv_norm_g, v_w_ukv, v_q_head_g, v_k_head_g, v_conv_w, v_conv_b, v_mem_norm_g, v_w_mkv, v_mem_q_g, v_mem_k_g, v_w_br_attn, v_w_br_conv, v_w_br_mem, v_w_out):
    a = dict(zip(INPUTS, (x, mem, positions, norm_g, w_in, b_gate, q_norm_g, w_uq, kv_norm_g, w_ukv, q_head_g, k_head_g, conv_w, conv_b, mem_norm_g, w_mkv, mem_q_g, mem_k_g, w_br_attn, w_br_conv, w_br_mem, w_out, loss_target, m_norm_g, m_w_in, m_b_gate, m_q_norm_g, m_w_uq, m_kv_norm_g, m_w_ukv, m_q_head_g, m_k_head_g, m_conv_w, m_conv_b, m_mem_norm_g, m_w_mkv, m_mem_q_g, m_mem_k_g, m_w_br_attn, m_w_br_conv, m_w_br_mem, m_w_out, v_norm_g, v_w_in, v_b_gate, v_q_norm_g, v_w_uq, v_kv_norm_g, v_w_ukv, v_q_head_g, v_k_head_g, v_conv_w, v_conv_b, v_mem_norm_g, v_w_mkv, v_mem_q_g, v_mem_k_g, v_w_br_attn, v_w_br_conv, v_w_br_mem, v_w_out)))
    x = a['x'][0]
    mem = a['mem'][0]
    tgt = a['loss_target'][0]
    s = x.shape[0]
    t_el = 256
    tq_f, tk_f, tq_b, tk_b = 256, 1024, 512, 512

    gathered = _allgather([a[n].astype(BF16) for n in BIG_ORDER] + [a['conv_w']], "ag_weights")
    full = {n: [_unshard_blocks(g8[:, l], BIG[n][1]) for l in range(DEPTH)] for n, g8 in zip(BIG_ORDER, gathered)}
    conv_w = gathered[-1].transpose(1, 2, 0, 3).reshape(DEPTH, 3, CW)

    inv_freq = ROPE_BASE ** (-jnp.arange(0, RP, 2, dtype=F32) / RP)
    ang = a['positions'][0].astype(F32)[:, None] * inv_freq
    cos, sin = jnp.cos(ang), jnp.sin(ang)
    rc = jnp.concatenate([jnp.ones((s, NOPE), F32), cos, cos, jnp.ones((s, HP - QKD), F32)], axis=1)
    rs = jnp.concatenate([jnp.zeros((s, NOPE), F32), -sin, sin, jnp.zeros((s, HP - QKD), F32)], axis=1)

    def small(n, l, width=None):
        v = a[n][l][None, :]
        return v if width is None else jnp.pad(v, ((0, 0), (0, width - v.shape[1])))

    saved = []
    layer_w = [_layer_weights(full, l) for l in range(DEPTH)]
    for l in range(DEPTH):
        w = layer_w[l]
        tag = ""
        h = _rms_h(x, small('norm_g', l), 512, "rms_h" + tag)
        proj = _mm(h, w['w_main'], BF16, 512, 512, "in_proj" + tag)
        proj_l = _mm(h, w['w_l'], BF16, 512, P_LW, "in_proj_lat" + tag)
        qng, kvng = small('q_norm_g', l), small('kv_norm_g', l)
        qhg, khg = small('q_head_g', l, HP), small('k_head_g', l, HP)
        q, k, v = _mla_prep(proj_l, rc, rs, qng, kvng, qhg, khg, w['w_uq'], w['w_ukv'], t_el, "mla_prep" + tag)
        o, lse = _flash_fwd(q, k, v, tq_f, tk_f, "flash_fwd" + tag)
        mk, mv = _mem_prep(mem, small('mem_norm_g', l), w['w_mkv'], small('mem_k_g', l), "mem_prep" + tag)
        cb, mqg = small('conv_b', l), small('mem_q_g', l)
        oa, oc, om = _branches(proj, o, mk, mv, conv_w[l], cb, mqg, t_el, "branches" + tag)
        x_out, aa, ac, am = _merge_fwd(x, proj, oa, oc, om, small('b_gate', l), w['w_br_attn'], w['w_br_conv'],
                                       w['w_br_mem'], w['w_out'], t_el, "merge" + tag)
        saved.append(dict(x=x, h=h, proj=proj, proj_l=proj_l, q=q, k=k, v=v, o=o, lse=lse, mk=mk, mv=mv,
                          oa=oa, oc=oc, om=om, aa=aa, ac=ac, am=am))
        x = x_out

    g, loss_parts = _loss_grad(x, tgt, 512, "loss")
    loss = lax.psum(jnp.sum(loss_parts), ("x", "y", "c"))

    gw = {n: [None] * DEPTH for n in WEIGHTS}
    for l in reversed(range(DEPTH)):
        w = layer_w[l]
        sv = saved[l]
        tag = ""
        tqb = min(tq_b, s)
        y, daa, dac, dam, dr, doa, doc, dom, dbg = _merge_bwd(
            g, sv['proj'], sv['aa'], sv['ac'], sv['am'], small('b_gate', l), w['w_out_t'], w['w_br_attn_t'],
            w['w_br_conv_t'], w['w_br_mem_t'], t_el, "merge_bwd" + tag)
        gw['b_gate'][l] = dbg[0]
        gw['w_out'][l] = _tn(y, g, 512, 512, "dw_out" + tag)
        gw['w_br_attn'][l] = _tn(sv['oa'], daa, 512, 512, "dw_attn" + tag)
        gw['w_br_conv'][l] = _tn(sv['oc'], dac, 512, 512, "dw_conv" + tag)
        gw['w_br_mem'][l] = _tn(sv['om'], dam, 512, 512, "dw_mem" + tag)
        cb, mqg = small('conv_b', l), small('mem_q_g', l)
        dcv, dmm, dga, do, dl, dcw, dcb, dmk, dmv, dmqg = _branches_bwd(
            sv['proj'], sv['o'], sv['mk'], sv['mv'], conv_w[l], cb, mqg, doa, doc, dom, t_el, "branches_bwd" + tag)
        gw['conv_w'][l], gw['conv_b'][l], gw['mem_q_g'][l] = dcw, dcb[0], dmqg[0]
        dwm, dmng, dmkg = _mem_prep_bwd(mem, small('mem_norm_g', l), w['w_mkv'], w['w_mkv_t'], small('mem_k_g', l),
                                        dmk, dmv, "mem_prep_bwd" + tag)
        gw['w_mkv'][l], gw['mem_norm_g'][l], gw['mem_k_g'][l] = dwm, dmng[0], dmkg[0]
        lse_r = sv['lse'].reshape(H, s // tqb, 1, tqb)
        dl_r = dl.reshape(H, s // tqb, 1, tqb)
        dq, dk, dv = _flash_bwd(sv['q'], sv['k'], sv['v'], do, lse_r, dl_r, tq_b, tk_b, "flash_bwd" + tag)
        qng, kvng = small('q_norm_g', l), small('kv_norm_g', l)
        qhg, khg = small('q_head_g', l, HP), small('k_head_g', l, HP)
        dlat, dwuq, dwukv, dqng, dkvng, dqhg, dkhg = _mla_prep_bwd(
            sv['proj_l'], rc, rs, qng, kvng, qhg, khg, w['w_uq'], w['w_ukv'], w['w_uq_t'], w['w_ukv_t'],
            dq, dk, dv, t_el, "mla_prep_bwd" + tag)
        gw['w_uq'][l] = _unpad_heads(dwuq, QKD)
        dwukv = dwukv.reshape(KVL, 2, H, HP)[:, :, :, :NOPE]
        gw['w_ukv'][l] = dwukv.transpose(0, 2, 1, 3).reshape(KVL, H * 2 * NOPE)
        gw['q_norm_g'][l], gw['kv_norm_g'][l] = dqng[0], dkvng[0]
        gw['q_head_g'][l], gw['k_head_g'][l] = dqhg[0, :QKD], dkhg[0, :QKD]
        dproj = jnp.concatenate([dcv, dmm, dr, dga, dlat, jnp.zeros((s, P_WPAD - P_W), BF16)], axis=1)
        dh = _mm(dproj, w['w_in_t'], F32, 512, 512, "d_h" + tag)
        gw['w_in'][l] = _unpermute_w_in(_tn(sv['h'], dproj, 512, 512, "dw_in" + tag))
        g, dng = _rms_in_bwd(sv['x'], g, dh, small('norm_g', l), 512, "rms_bwd" + tag)
        gw['norm_g'][l] = dng[0]
    grad_x = g[None]

    sharded = BIG_ORDER + ['conv_w']
    axis_of = lambda n: 1 if n == 'conv_w' else BIG[n][1]
    send = [jnp.stack([_shard_blocks(gw[n][l], axis_of(n)) for l in range(DEPTH)], axis=1) for n in sharded]
    parts_big = _alltoall(send, "rs_grads")
    small_flat = jnp.concatenate([jnp.stack(gw[n]).reshape(-1) for n in SMALL_ORDER])
    n_small = small_flat.shape[0]
    parts_small = _allgather([_to_rows(small_flat)], "ag_small_grads")[0]

    outs = [{} for _ in range(4)]
    for n, parts in zip(sharded, parts_big):
        loc = a[n].shape
        two_d = lambda t: t.reshape(loc[0] * loc[1], loc[2])
        res = _adamw(parts.reshape(N_DEV, loc[0] * loc[1], loc[2]), two_d(a[n]), two_d(a['m_' + n]),
                     two_d(a['v_' + n]), "adamw_" + n)
        for d, r in zip(outs, res):
            d[n] = r.reshape(loc)
    pks = lambda pre: _to_rows(jnp.concatenate([a[pre + n].reshape(-1) for n in SMALL_ORDER]))
    res_small = _adamw(parts_small, pks(''), pks('m_'), pks('v_'), "adamw_small")
    for d, rsm in zip(outs, res_small):
        flat = rsm.reshape(-1)[:n_small]
        off = 0
        for n in SMALL_ORDER:
            d[n] = flat[off:off + DEPTH * SMALL[n]].reshape(DEPTH, SMALL[n])
            off += DEPTH * SMALL[n]
    result = [loss, grad_x]
    for d in outs:
        result += [d[n] for n in WEIGHTS]
    return tuple(result)
```

```python
import functools

import jax
import jax.numpy as jnp
from jax import lax
from jax.experimental import pallas as pl
from jax.experimental.pallas import tpu as pltpu

F32, BF16 = jnp.float32, jnp.bfloat16

N_DEV = 8
DEPTH = 4
D = 1024
QL, KVL, RP = 384, 256, 32
H, NOPE, QKD, VD = 8, 64, 96, 64
HP = 128
CW, MW, AW = 512, 512, 512
HM, MHD = 4, 128
IN_WIDTH = 7328
EPS = 1e-6
ROPE_BASE = 10000.0
ATT_SCALE = QKD ** -0.5
MEM_SCALE = MHD ** -0.5
LOG2E = 1.4426950408889634
QSCALE = ATT_SCALE * LOG2E

ADAM_LR, ADAM_B1, ADAM_B2, ADAM_EPS, ADAM_WD, ADAM_STEP = 0.001, 0.9, 0.999, 1e-08, 0.01, 10

LANES = 128
VMEM_LIMIT = 56 * 1024 * 1024

P_CONV, P_MEM, P_R, P_GA, P_L = 0, 2048, 3072, 6144, 6656
P_MAIN = 6656
P_LW = 768
P_W = P_MAIN + P_LW
P_WPAD = 7680

WEIGHTS = ['norm_g', 'w_in', 'b_gate', 'q_norm_g', 'w_uq', 'kv_norm_g', 'w_ukv', 'q_head_g', 'k_head_g',
           'conv_w', 'conv_b', 'mem_norm_g', 'w_mkv', 'mem_q_g', 'mem_k_g', 'w_br_attn', 'w_br_conv',
           'w_br_mem', 'w_out']
INPUTS = ['x', 'mem', 'positions'] + WEIGHTS + ['loss_target'] + ['m_' + n for n in WEIGHTS] + ['v_' + n for n in WEIGHTS]

BIG = {'w_in': ((D, IN_WIDTH), 1), 'w_uq': ((QL, H * QKD), 1), 'w_ukv': ((KVL, H * 128), 1),
       'w_mkv': ((D, 2 * MW), 0), 'w_br_attn': ((AW, D), 1), 'w_br_conv': ((CW, D), 1),
       'w_br_mem': ((MW, D), 1), 'w_out': ((D, D), 0)}
BIG_ORDER = ['w_in', 'w_uq', 'w_ukv', 'w_mkv', 'w_br_attn', 'w_br_conv', 'w_br_mem', 'w_out']
CONVW_PAD = 256
SMALL = {'norm_g': D, 'b_gate': 3 * D, 'q_norm_g': QL, 'kv_norm_g': KVL, 'q_head_g': QKD, 'k_head_g': QKD,
         'conv_b': CW, 'mem_norm_g': D, 'mem_q_g': MHD, 'mem_k_g': MHD}
SMALL_ORDER = list(SMALL)
ROW_ALIGN = 1024


def _call(body, **kw):
    return pl.pallas_call(body, **kw)


def _cp(sem=None):
    return pltpu.CompilerParams(dimension_semantics=sem, vmem_limit_bytes=VMEM_LIMIT)


def _full(shape):
    n = len(shape)
    return pl.BlockSpec(shape, lambda *_: (0,) * n)


def _rms(x, g, n):
    rs = lax.rsqrt(jnp.sum(x * x, axis=-1, keepdims=True) * (1.0 / n) + EPS)
    xh = x * rs
    return xh * g, xh, rs


def _rms_bwd(dy, xh, rs, g, n):
    dxh = dy * g
    dx = rs * (dxh - xh * (jnp.sum(dxh * xh, axis=-1, keepdims=True) * (1.0 / n)))
    return dx, jnp.sum(dy * xh, axis=0, keepdims=True)


def _sigmoid(x):
    return 1.0 / (1.0 + jnp.exp(-x))


def _swap_rope(u):
    lane = lax.broadcasted_iota(jnp.int32, u.shape, 1)
    up = pltpu.roll(u, 16, 1)
    dn = pltpu.roll(u, 112, 1)
    return jnp.where((lane >= 64) & (lane < 80), dn, jnp.where((lane >= 80) & (lane < 96), up, 0.0))


def _rope(u, c, sn):
    return u * c + _swap_rope(u) * sn


def _rope_adj(d, c, sn):
    return d * c + _swap_rope(d * sn)


def _dot(a, b):
    return jnp.dot(a, b, preferred_element_type=F32)


def _dot_nt(a, b):
    return lax.dot_general(a, b, (((1,), (1,)), ((), ())), preferred_element_type=F32)


def _dot_tn(a, b):
    return lax.dot_general(a, b, (((0,), (0,)), ((), ())), preferred_element_type=F32)


def _mm(a, b, out_dtype, tm, tn, name):
    m, k = a.shape
    _, n = b.shape
    tm, tn = min(tm, m), min(tn, n)

    def body(a_ref, b_ref, o_ref):
        o_ref[...] = _dot(a_ref[...].astype(BF16), b_ref[...]).astype(o_ref.dtype)

    return _call(
        body, name=name, grid=(m // tm, n // tn),
        in_specs=[pl.BlockSpec((tm, k), lambda i, j: (i, 0)), pl.BlockSpec((k, tn), lambda i, j: (0, j))],
        out_specs=pl.BlockSpec((tm, tn), lambda i, j: (i, j)),
        out_shape=jax.ShapeDtypeStruct((m, n), out_dtype),
        compiler_params=_cp(("parallel", "arbitrary")),
    )(a, b)


def _tn(a, b, ts, tn, name):
    s, ka = a.shape
    _, n = b.shape
    ts, tn = min(ts, s), min(tn, n)

    def body(a_ref, b_ref, o_ref):
        @pl.when(pl.program_id(1) == 0)
        def _():
            o_ref[...] = jnp.zeros_like(o_ref)

        o_ref[...] += _dot_tn(a_ref[...].astype(BF16), b_ref[...].astype(BF16))

    return _call(
        body, name=name, grid=(n // tn, s // ts),
        in_specs=[pl.BlockSpec((ts, ka), lambda j, i: (i, 0)), pl.BlockSpec((ts, tn), lambda j, i: (i, j))],
        out_specs=pl.BlockSpec((ka, tn), lambda j, i: (0, j)),
        out_shape=jax.ShapeDtypeStruct((ka, n), F32),
        compiler_params=_cp(("parallel", "arbitrary")),
    )(a, b)


def _mm_acc(a, b, tk, tn, name):
    m, k = a.shape
    _, n = b.shape
    tk, tn = min(tk, k), min(tn, n)

    def body(a_ref, b_ref, o_ref):
        @pl.when(pl.program_id(1) == 0)
        def _():
            o_ref[...] = jnp.zeros_like(o_ref)

        o_ref[...] += _dot(a_ref[...], b_ref[...])

    return _call(
        body, name=name, grid=(n // tn, k // tk),
        in_specs=[pl.BlockSpec((m, tk), lambda j, i: (0, i)), pl.BlockSpec((tk, tn), lambda j, i: (i, j))],
        out_specs=pl.BlockSpec((m, tn), lambda j, i: (0, j)),
        out_shape=jax.ShapeDtypeStruct((m, n), F32),
        compiler_params=_cp(("parallel", "arbitrary")),
    )(a, b)


def _rms_h(x, g, t, name):
    s = x.shape[0]
    t = min(t, s)

    def body(x_ref, g_ref, h_ref, ht_ref):
        h = _rms(x_ref[...], g_ref[...], D)[0]
        h_ref[...] = h.astype(BF16)
        ht_ref[...] = h.T.astype(BF16)

    return _call(
        body, name=name, grid=(s // t,),
        in_specs=[pl.BlockSpec((t, D), lambda i: (i, 0)), _full((1, D))],
        out_specs=[pl.BlockSpec((t, D), lambda i: (i, 0)), pl.BlockSpec((D, t), lambda i: (0, i))],
        out_shape=[jax.ShapeDtypeStruct((s, D), BF16), jax.ShapeDtypeStruct((D, s), BF16)],
        compiler_params=_cp(("parallel",)),
    )(x, g)


def _mla_heads(pl_blk, c, sn, qng, kvng, qhg, khg, wuq, wukv):
    ql = pl_blk[:, 0:QL].astype(F32)
    kvl = pl_blk[:, QL:QL + KVL].astype(F32)
    kpe = pl_blk[:, QL + KVL:P_LW].astype(F32)
    qn, qxh, qrs = _rms(ql, qng, QL)
    kvn, kvxh, kvrs = _rms(kvl, kvng, KVL)
    qn16, kvn16 = qn.astype(BF16), kvn.astype(BF16)
    qp = _dot(qn16, wuq)
    kvp = _dot(kvn16, wukv)
    return ql, kvl, kpe, qxh, qrs, kvxh, kvrs, qn16, kvn16, qp, kvp


def _mla_prep(proj_l, c, sn, qng, kvng, qhg, khg, wuq, wukv, t, name):
    s = proj_l.shape[0]
    t = min(t, s)

    def body(l_ref, c_ref, sn_ref, qng_ref, kvng_ref, qhg_ref, khg_ref, wuq_ref, wukv_ref, q_ref, k_ref, v_ref):
        cc, ss = c_ref[...], sn_ref[...]
        (_, _, kpe, _, _, _, _, _, _, qp, kvp) = _mla_heads(
            l_ref[...], cc, ss, qng_ref[...], kvng_ref[...], qhg_ref[...], khg_ref[...], wuq_ref[...], wukv_ref[...])
        lane = lax.broadcasted_iota(jnp.int32, cc.shape, 1)
        for h in range(H):
            u = qp[:, h * HP:(h + 1) * HP]
            q_ref[h] = (_rope(_rms(u, qhg_ref[...], QKD)[0], cc, ss) * QSCALE).astype(BF16)
            u = kvp[:, h * HP:(h + 1) * HP] + kpe
            k_ref[h] = _rope(_rms(u, khg_ref[...], QKD)[0], cc, ss).astype(BF16)
            v_ref[h] = jnp.where(lane == VD, 1.0, kvp[:, (H + h) * HP:(H + h + 1) * HP]).astype(BF16)

    hs = pl.BlockSpec((H, t, HP), lambda i: (0, i, 0))
    row = lambda w: pl.BlockSpec((t, w), lambda i: (i, 0))
    return _call(
        body, name=name, grid=(s // t,),
        in_specs=[row(P_LW), row(HP), row(HP), _full((1, QL)), _full((1, KVL)), _full((1, HP)), _full((1, HP)),
                  _full((QL, H * HP)), _full((KVL, 2 * H * HP))],
        out_specs=[hs, hs, hs],
        out_shape=[jax.ShapeDtypeStruct((H, s, HP), BF16)] * 3,
        compiler_params=_cp(("parallel",)),
    )(proj_l, c, sn, qng, kvng, qhg, khg, wuq, wukv)


def _flash_fwd(q, k, v, tq, tk, name):
    _, s, _ = q.shape
    tq, tk = min(tq, s), min(tk, s)
    nk, nc = s // tk, tk // LANES
    un = 8 if nk % 8 == 0 else 1

    def body(q_ref, k_ref, v_ref, o_ref, lse_ref, s_scr):
        qb = q_ref[0]

        def scores(jj, mx):
            for u in range(un):
                j = jj * un + u
                off = pl.multiple_of(j * tk, tk)
                sc = _dot_nt(qb, k_ref[0, pl.ds(off, tk), :])
                s_scr[j] = sc
                for cc in range(nc):
                    mx = jnp.maximum(mx, sc[:, cc * LANES:(cc + 1) * LANES])
            return mx

        mx = lax.fori_loop(0, nk // un, scores, jnp.full((tq, LANES), -jnp.inf, F32))
        m = jnp.max(mx, axis=-1, keepdims=True)
        mb = jnp.broadcast_to(m, (tq, LANES))

        def probs(jj, acc):
            for u in range(un):
                j = jj * un + u
                off = pl.multiple_of(j * tk, tk)
                sc = s_scr[j]
                ps = [jnp.exp2(sc[:, cc * LANES:(cc + 1) * LANES] - mb).astype(BF16) for cc in range(nc)]
                acc = acc + _dot(jnp.concatenate(ps, axis=-1), v_ref[0, pl.ds(off, tk), :])
            return acc

        acc = lax.fori_loop(0, nk // un, probs, jnp.zeros((tq, HP), F32))
        l = acc[:, VD:VD + 1]
        o_ref[0] = (acc[:, :VD] / l).astype(BF16)
        lse = jnp.broadcast_to(m + jnp.log(l) * LOG2E, (tq, LANES))
        lse_ref[0] = lse.T[0:1, :]

    return _call(
        body, name=name, grid=(H, s // tq),
        in_specs=[pl.BlockSpec((1, tq, HP), lambda h, i: (h, i, 0)),
                  pl.BlockSpec((1, s, HP), lambda h, i: (h, 0, 0)),
                  pl.BlockSpec((1, s, HP), lambda h, i: (h, 0, 0))],
        out_specs=[pl.BlockSpec((1, tq, VD), lambda h, i: (h, i, 0)), pl.BlockSpec((1, 1, tq), lambda h, i: (h, 0, i))],
        out_shape=[jax.ShapeDtypeStruct((H, s, VD), BF16), jax.ShapeDtypeStruct((H, 1, s), F32)],
        scratch_shapes=[pltpu.VMEM((nk, tq, tk), F32)],
        compiler_params=_cp(("parallel", "arbitrary")),
    )(q, k, v)


def _mem_prep(mem, mng, wmkv, mkg, name):
    m = mem.shape[0]

    def body(mem_ref, mng_ref, w_ref, mkg_ref, mk_ref, mv_ref):
        mn = _rms(mem_ref[...], mng_ref[...], D)[0].astype(BF16)
        mkv = _dot(mn, w_ref[...])
        for h in range(HM):
            mk_ref[h] = _rms(mkv[:, 2 * h * MHD:(2 * h + 1) * MHD], mkg_ref[...], MHD)[0].astype(BF16)
            mv_ref[h] = mkv[:, (2 * h + 1) * MHD:(2 * h + 2) * MHD].astype(BF16)

    return _call(
        body, name=name,
        in_specs=[_full((m, D)), _full((1, D)), _full((D, 2 * MW)), _full((1, MHD))],
        out_specs=[_full((HM, m, MHD))] * 2,
        out_shape=[jax.ShapeDtypeStruct((HM, m, MHD), BF16)] * 2,
        compiler_params=_cp(),
    )(mem, mng, wmkv, mkg)


def _conv_parts(cv, prev, nxt, first, last, cw, cb):
    t = cv.shape[0]
    c_b, c_c, c_u, g_c = (cv[:, i * CW:(i + 1) * CW].astype(F32) for i in range(4))
    z = c_c * c_u
    zp = jnp.where(first, 0.0, prev[15:16, CW:2 * CW].astype(F32) * prev[15:16, 2 * CW:3 * CW].astype(F32))
    zn = jnp.where(last, 0.0, nxt[0:1, CW:2 * CW].astype(F32) * nxt[0:1, 2 * CW:3 * CW].astype(F32))
    row = lax.broadcasted_iota(jnp.int32, (t, CW), 0)
    z_m1 = jnp.where(row == 0, zp, pltpu.roll(z, 1, 0))
    z_p1 = jnp.where(row == t - 1, zn, pltpu.roll(z, t - 1, 0))
    conv = cw[0:1] * z_m1 + cw[1:2] * z + cw[2:3] * z_p1 + cb
    return c_b, c_c, c_u, g_c, z, z_m1, z_p1, conv


def _mem_attn(qm, mqg, mk_ref, mv_ref):
    outs = []
    for h in range(HM):
        mq, mqxh, mqrs = _rms(qm[:, h * MHD:(h + 1) * MHD], mqg, MHD)
        mq16 = mq.astype(BF16)
        sc = _dot_nt(mq16, mk_ref[h]) * MEM_SCALE
        e = jnp.exp(sc - jnp.max(sc, axis=-1, keepdims=True))
        p = e / jnp.sum(e, axis=-1, keepdims=True)
        o = _dot(p.astype(BF16), mv_ref[h])
        outs.append((mq16, mqxh, mqrs, p, o))
    return outs


def _halo_specs(t, s, width):
    nb = s // 16
    prev = pl.BlockSpec((16, width), lambda i: (jnp.maximum(i * (t // 16) - 1, 0), 0))
    nxt = pl.BlockSpec((16, width), lambda i: (jnp.minimum((i + 1) * (t // 16), nb - 1), 0))
    return prev, nxt


def _branches(proj, o, mk, mv, cw, cb, mqg, t, name):
    s = proj.shape[0]
    t = min(t, s)
    nt = s // t

    def body(cv_ref, pv_ref, nx_ref, mm_ref, ga_ref, o_ref, mk_ref, mv_ref, cw_ref, cb_ref, mqg_ref,
             oa_ref, oc_ref, om_ref):
        i = pl.program_id(0)
        c_b, _, _, g_c, _, _, _, conv = _conv_parts(
            cv_ref[...], pv_ref[...], nx_ref[...], i == 0, i == nt - 1, cw_ref[...], cb_ref[...])
        oc_ref[...] = (c_b * conv * (g_c * _sigmoid(g_c))).astype(BF16)
        ga = ga_ref[...].astype(F32)
        ocat = jnp.concatenate([o_ref[h].astype(F32) for h in range(H)], axis=-1)
        oa_ref[...] = (ocat * (ga * _sigmoid(ga))).astype(BF16)
        mblk = mm_ref[...].astype(F32)
        gm = mblk[:, MW:]
        heads = _mem_attn(mblk[:, :MW], mqg_ref[...], mk_ref, mv_ref)
        om = jnp.concatenate([hh[4] for hh in heads], axis=-1)
        om_ref[...] = (om * (gm * _sigmoid(gm))).astype(BF16)

    pv, nx = _halo_specs(t, s, 4 * CW)
    out = pl.BlockSpec((t, 512), lambda i: (i, 0))
    return _call(
        body, name=name, grid=(nt,),
        in_specs=[pl.BlockSpec((t, 4 * CW), lambda i: (i, 0)), pv, nx,
                  pl.BlockSpec((t, 2 * MW), lambda i: (i, P_MEM // (2 * MW))),
                  pl.BlockSpec((t, AW), lambda i: (i, P_GA // AW)),
                  pl.BlockSpec((H, t, VD), lambda i: (0, i, 0)),
                  _full(mk.shape), _full(mv.shape), _full((3, CW)), _full((1, CW)), _full((1, MHD))],
        out_specs=[out, out, out],
        out_shape=[jax.ShapeDtypeStruct((s, 512), BF16)] * 3,
        compiler_params=_cp(("parallel",)),
    )(proj, proj, proj, proj, proj, o, mk, mv, cw, cb, mqg)


def _merge_fwd(x, proj, oa, oc, om, bg, wa, wc, wm, wo, t, name):
    s = x.shape[0]
    t = min(t, s)

    def body(x_ref, r_ref, oa_ref, oc_ref, om_ref, bg_ref, wa_ref, wc_ref, wm_ref, wo_ref,
             xo_ref, aa_ref, ac_ref, am_ref):
        y = jnp.zeros((t, D), F32)
        for j, (o_ref, w_ref, a_ref) in enumerate(((oa_ref, wa_ref, aa_ref), (oc_ref, wc_ref, ac_ref),
                                                   (om_ref, wm_ref, am_ref))):
            a = _dot(o_ref[...], w_ref[...])
            a_ref[...] = a.astype(BF16)
            rg = _sigmoid(r_ref[:, j * D:(j + 1) * D].astype(F32) + bg_ref[:, j * D:(j + 1) * D])
            y = y + rg * a
        xo_ref[...] = x_ref[...] + _dot(y.astype(BF16), wo_ref[...])

    row = lambda w: pl.BlockSpec((t, w), lambda i: (i, 0))
    return _call(
        body, name=name, grid=(s // t,),
        in_specs=[row(D), pl.BlockSpec((t, 3 * D), lambda i: (i, P_R // (3 * D))), row(512), row(512), row(512),
                  _full((1, 3 * D)), _full((512, D)), _full((512, D)), _full((512, D)), _full((D, D))],
        out_specs=[row(D), row(D), row(D), row(D)],
        out_shape=[jax.ShapeDtypeStruct((s, D), F32)] + [jax.ShapeDtypeStruct((s, D), BF16)] * 3,
        compiler_params=_cp(("parallel",)),
    )(x, proj, oa, oc, om, bg, wa, wc, wm, wo)


def _loss_grad(x, tgt, t, name):
    s = x.shape[0]
    t = min(t, s)

    def body(x_ref, t_ref, g_ref, l_ref):
        @pl.when(pl.program_id(0) == 0)
        def _():
            l_ref[...] = jnp.zeros_like(l_ref)

        e = x_ref[...] - t_ref[...]
        g_ref[...] = e * (1.0 / D)
        sq = e * e
        part = sq[:, 0:LANES]
        for j in range(1, D // LANES):
            part = part + sq[:, j * LANES:(j + 1) * LANES]
        acc = part[0:8]
        for j in range(1, t // 8):
            acc = acc + part[j * 8:(j + 1) * 8]
        l_ref[...] += acc * (0.5 / D)

    row = pl.BlockSpec((t, D), lambda i: (i, 0))
    return _call(
        body, name=name, grid=(s // t,),
        in_specs=[row, row], out_specs=[row, _full((8, LANES))],
        out_shape=[jax.ShapeDtypeStruct((s, D), F32), jax.ShapeDtypeStruct((8, LANES), F32)],
        compiler_params=_cp(("arbitrary",)),
    )(x, tgt)


def _merge_bwd(g, proj, aa, ac, am, bg, wot, wat, wct, wmt, t, name):
    s = g.shape[0]
    t = min(t, s)

    def body(g_ref, r_ref, aa_ref, ac_ref, am_ref, bg_ref, wot_ref, wat_ref, wct_ref, wmt_ref,
             y_ref, daa_ref, dac_ref, dam_ref, dr_ref, doa_ref, doc_ref, dom_ref, dbg_ref):
        @pl.when(pl.program_id(0) == 0)
        def _():
            dbg_ref[...] = jnp.zeros_like(dbg_ref)

        dy = _dot(g_ref[...].astype(BF16), wot_ref[...])
        y = jnp.zeros((t, D), F32)
        for j, (a_ref, wt_ref, da_ref, do_ref) in enumerate(((aa_ref, wat_ref, daa_ref, doa_ref),
                                                             (ac_ref, wct_ref, dac_ref, doc_ref),
                                                             (am_ref, wmt_ref, dam_ref, dom_ref))):
            a = a_ref[...].astype(F32)
            rg = _sigmoid(r_ref[:, j * D:(j + 1) * D].astype(F32) + bg_ref[:, j * D:(j + 1) * D])
            y = y + rg * a
            da = (dy * rg).astype(BF16)
            da_ref[...] = da
            dr = dy * a * rg * (1.0 - rg)
            dr_ref[:, j * D:(j + 1) * D] = dr.astype(BF16)
            dbg_ref[:, j * D:(j + 1) * D] += jnp.sum(dr, axis=0, keepdims=True)
            do_ref[...] = _dot(da, wt_ref[...])
        y_ref[...] = y.astype(BF16)

    row = lambda w: pl.BlockSpec((t, w), lambda i: (i, 0))
    return _call(
        body, name=name, grid=(s // t,),
        in_specs=[row(D), pl.BlockSpec((t, 3 * D), lambda i: (i, P_R // (3 * D))), row(D), row(D), row(D),
                  _full((1, 3 * D)), _full((D, D)), _full((D, 512)), _full((D, 512)), _full((D, 512))],
        out_specs=[row(D), row(D), row(D), row(D), row(3 * D), row(512), row(512), row(512), _full((1, 3 * D))],
        out_shape=[jax.ShapeDtypeStruct((s, D), BF16)] * 4 + [jax.ShapeDtypeStruct((s, 3 * D), BF16)]
        + [jax.ShapeDtypeStruct((s, 512), F32)] * 3 + [jax.ShapeDtypeStruct((1, 3 * D), F32)],
        compiler_params=_cp(("arbitrary",)),
    )(g, proj, aa, ac, am, bg, wot, wat, wct, wmt)


def _branches_bwd(proj, o, mk, mv, cw, cb, mqg, doa, doc, dom, t, name):
    s = proj.shape[0]
    t = min(t, s)
    nt = s // t
    m = mk.shape[1]

    def body(cv_ref, pv_ref, nx_ref, mm_ref, ga_ref, o_ref, mk_ref, mv_ref, cw_ref, cb_ref, mqg_ref,
             doa_ref, doc_ref, dcp_ref, dcn_ref, dom_ref,
             dcv_ref, dmm_ref, dga_ref, do_ref, dl_ref, dcw_ref, dcb_ref, dmk_ref, dmv_ref, dmqg_ref):
        i = pl.program_id(0)

        @pl.when(i == 0)
        def _():
            for r in (dcw_ref, dcb_ref, dmk_ref, dmv_ref, dmqg_ref):
                r[...] = jnp.zeros_like(r)

        first, last = i == 0, i == nt - 1
        cw_, cb_ = cw_ref[...], cb_ref[...]
        pv, nx = pv_ref[...], nx_ref[...]
        c_b, c_c, c_u, g_c, z, z_m1, z_p1, conv = _conv_parts(cv_ref[...], pv, nx, first, last, cw_, cb_)
        sg = _sigmoid(g_c)
        silu = g_c * sg
        dsilu = sg * (1.0 + g_c * (1.0 - sg))
        doc_ = doc_ref[...]
        dconv = doc_ * c_b * silu
        gp = pv[15:16, 3 * CW:4 * CW].astype(F32)
        gn = nx[0:1, 3 * CW:4 * CW].astype(F32)
        dconv_p = jnp.where(first, 0.0, dcp_ref[15:16, :] * pv[15:16, 0:CW].astype(F32) * (gp * _sigmoid(gp)))
        dconv_n = jnp.where(last, 0.0, dcn_ref[0:1, :] * nx[0:1, 0:CW].astype(F32) * (gn * _sigmoid(gn)))
        row = lax.broadcasted_iota(jnp.int32, (t, CW), 0)
        d_m1 = jnp.where(row == 0, dconv_p, pltpu.roll(dconv, 1, 0))
        d_p1 = jnp.where(row == t - 1, dconv_n, pltpu.roll(dconv, t - 1, 0))
        dz = cw_[0:1] * d_p1 + cw_[1:2] * dconv + cw_[2:3] * d_m1
        dcv_ref[:, 0:CW] = (doc_ * conv * silu).astype(BF16)
        dcv_ref[:, CW:2 * CW] = (dz * c_u).astype(BF16)
        dcv_ref[:, 2 * CW:3 * CW] = (dz * c_c).astype(BF16)
        dcv_ref[:, 3 * CW:4 * CW] = (doc_ * c_b * conv * dsilu).astype(BF16)
        dcw_ref[0:1, :] += jnp.sum(dconv * z_m1, axis=0, keepdims=True)
        dcw_ref[1:2, :] += jnp.sum(dconv * z, axis=0, keepdims=True)
        dcw_ref[2:3, :] += jnp.sum(dconv * z_p1, axis=0, keepdims=True)
        dcb_ref[...] += jnp.sum(dconv, axis=0, keepdims=True)
        ga = ga_ref[...].astype(F32)
        sg = _sigmoid(ga)
        doa_ = doa_ref[...]
        ocat = jnp.concatenate([o_ref[h].astype(F32) for h in range(H)], axis=-1)
        dga_ref[...] = (doa_ * ocat * (sg * (1.0 + ga * (1.0 - sg)))).astype(BF16)
        dog = doa_ * (ga * sg)
        zeros = jnp.zeros((t, HP - VD), F32)
        lane = lax.broadcasted_iota(jnp.int32, (t, LANES), 1)
        dmat = jnp.zeros((t, LANES), F32)
        for h in range(H):
            dh = dog[:, h * VD:(h + 1) * VD]
            do_ref[h] = jnp.concatenate([dh, zeros], axis=-1).astype(BF16)
            dmat = jnp.where(lane == h, jnp.sum(dh * ocat[:, h * VD:(h + 1) * VD], axis=-1, keepdims=True), dmat)
        dlt = dmat.T
        for h in range(H):
            dl_ref[h] = dlt[h:h + 1, :]
        mblk = mm_ref[...].astype(F32)
        gm = mblk[:, MW:]
        sg = _sigmoid(gm)
        dom_ = dom_ref[...]
        heads = _mem_attn(mblk[:, :MW], mqg_ref[...], mk_ref, mv_ref)
        om = jnp.concatenate([hh[4] for hh in heads], axis=-1)
        dmm_ref[:, MW:] = (dom_ * om * (sg * (1.0 + gm * (1.0 - sg)))).astype(BF16)
        dmo = dom_ * (gm * sg)
        dmqg = jnp.zeros((1, MHD), F32)
        for h in range(HM):
            mq16, mqxh, mqrs, p, _ = heads[h]
            dmo_h = dmo[:, h * MHD:(h + 1) * MHD].astype(BF16)
            dp = _dot_nt(dmo_h, mv_ref[h])
            ds = (p * (dp - jnp.sum(dp * p, axis=-1, keepdims=True)) * MEM_SCALE).astype(BF16)
            dmq = _dot(ds, mk_ref[h])
            dmk_ref[h] += _dot_tn(ds, mq16)
            dmv_ref[h] += _dot_tn(p.astype(BF16), dmo_h)
            dq, dg = _rms_bwd(dmq, mqxh, mqrs, mqg_ref[...], MHD)
            dmm_ref[:, h * MHD:(h + 1) * MHD] = dq.astype(BF16)
            dmqg = dmqg + dg
        dmqg_ref[...] += dmqg

    pv, nx = _halo_specs(t, s, 4 * CW)
    dpv, dnx = _halo_specs(t, s, CW)
    row = lambda w: pl.BlockSpec((t, w), lambda i: (i, 0))
    hs = lambda w: pl.BlockSpec((H, t, w), lambda i: (0, i, 0))
    return _call(
        body, name=name, grid=(nt,),
        in_specs=[pl.BlockSpec((t, 4 * CW), lambda i: (i, 0)), pv, nx,
                  pl.BlockSpec((t, 2 * MW), lambda i: (i, P_MEM // (2 * MW))),
                  pl.BlockSpec((t, AW), lambda i: (i, P_GA // AW)),
                  hs(VD), _full(mk.shape), _full(mv.shape), _full((3, CW)), _full((1, CW)), _full((1, MHD)),
                  row(512), row(512), dpv, dnx, row(512)],
        out_specs=[row(4 * CW), row(2 * MW), row(AW), hs(HP), pl.BlockSpec((H, 1, t), lambda i: (0, 0, i)),
                   _full((3, CW)), _full((1, CW)),
                   _full((HM, m, MHD)), _full((HM, m, MHD)), _full((1, MHD))],
        out_shape=[jax.ShapeDtypeStruct((s, 4 * CW), BF16), jax.ShapeDtypeStruct((s, 2 * MW), BF16),
                   jax.ShapeDtypeStruct((s, AW), BF16), jax.ShapeDtypeStruct((H, s, HP), BF16),
                   jax.ShapeDtypeStruct((H, 1, s), F32), jax.ShapeDtypeStruct((3, CW), F32),
                   jax.ShapeDtypeStruct((1, CW), F32), jax.ShapeDtypeStruct((HM, m, MHD), F32),
                   jax.ShapeDtypeStruct((HM, m, MHD), F32), jax.ShapeDtypeStruct((1, MHD), F32)],
        compiler_params=_cp(("arbitrary",)),
    )(proj, proj, proj, proj, proj, o, mk, mv, cw, cb, mqg, doa, doc, doc, doc, dom)


def _flash_bwd(q, k, v, do, lse, dl, tq, tk, name):
    _, s, _ = q.shape
    tq, tk = min(tq, s), min(tk, s)
    nq, nkt = s // tq, s // tk
    unroll = 4 if nq % 4 == 0 else 1

    def body(q_ref, do_ref, lse_ref, dl_ref, k_ref, v_ref, dq_ref, dk_ref, dv_ref, dq_acc):
        j = pl.program_id(1)

        @pl.when(j == 0)
        def _():
            dq_acc[...] = jnp.zeros_like(dq_acc)

        kb, vb = k_ref[0], v_ref[0]

        def step(ii, carry):
            dk, dv = carry
            for u in range(unroll):
                i = ii * unroll + u
                off = pl.multiple_of(i * tq, tq)
                qb = q_ref[0, pl.ds(off, tq), :]
                dob = do_ref[0, pl.ds(off, tq), :]
                st = _dot_nt(kb, qb)
                pt = jnp.exp2(st - lse_ref[0, i])
                dv = dv + _dot(pt.astype(BF16), dob)
                dpt = _dot_nt(vb, dob)
                dst = (pt * (dpt - dl_ref[0, i])).astype(BF16)
                dk = dk + _dot(dst, qb)
                dq_acc[pl.ds(off, tq), :] += _dot_tn(dst, kb)
            return dk, dv

        dk, dv = lax.fori_loop(0, nq // unroll, step, (jnp.zeros((tk, HP), F32), jnp.zeros((tk, HP), F32)))
        dk_ref[0] = (dk * (1.0 / LOG2E)).astype(BF16)
        dv_ref[0] = dv.astype(BF16)

        @pl.when(j == nkt - 1)
        def _():
            dq_ref[0] = (dq_acc[...] * ATT_SCALE).astype(BF16)

    whole = pl.BlockSpec((1, s, HP), lambda h, j: (h, 0, 0))
    stat = pl.BlockSpec((1, nq, 1, tq), lambda h, j: (h, 0, 0, 0))
    tile = pl.BlockSpec((1, tk, HP), lambda h, j: (h, j, 0))
    return _call(
        body, name=name, grid=(H, nkt),
        in_specs=[whole, whole, stat, stat, tile, tile],
        out_specs=[whole, tile, tile],
        out_shape=[jax.ShapeDtypeStruct((H, s, HP), BF16)] * 3,
        scratch_shapes=[pltpu.VMEM((s, HP), F32)],
        compiler_params=_cp(("arbitrary", "arbitrary")),
    )(q, do, lse, dl, k, v)


def _mla_prep_bwd(proj_l, c, sn, qng, kvng, qhg, khg, wuq, wukv, wuqt, wukvt, dq, dk, dv, t, name):
    s = proj_l.shape[0]
    t = min(t, s)

    def body(l_ref, c_ref, sn_ref, qng_ref, kvng_ref, qhg_ref, khg_ref, wuq_ref, wukv_ref, wuqt_ref, wukvt_ref,
             dq_ref, dk_ref, dv_ref, dl_ref, dwuq_ref, dwukv_ref, dqng_ref, dkvng_ref, dqhg_ref, dkhg_ref):
        @pl.when(pl.program_id(0) == 0)
        def _():
            for r in (dwuq_ref, dwukv_ref, dqng_ref, dkvng_ref, dqhg_ref, dkhg_ref):
                r[...] = jnp.zeros_like(r)

        cc, ss = c_ref[...], sn_ref[...]
        qhg, khg = qhg_ref[...], khg_ref[...]
        (_, _, kpe, qxh, qrs, kvxh, kvrs, qn16, kvn16, qp, kvp) = _mla_heads(
            l_ref[...], cc, ss, qng_ref[...], kvng_ref[...], qhg, khg, wuq_ref[...], wukv_ref[...])
        lane = lax.broadcasted_iota(jnp.int32, (t, HP), 1)
        dqp, dkp, dvp = [], [], []
        dkpe = jnp.zeros((t, HP), F32)
        dqhg = jnp.zeros((1, HP), F32)
        dkhg = jnp.zeros((1, HP), F32)
        for h in range(H):
            _, xh, rs = _rms(qp[:, h * HP:(h + 1) * HP], qhg, QKD)
            du, dg = _rms_bwd(_rope_adj(dq_ref[h].astype(F32), cc, ss), xh, rs, qhg, QKD)
            dqp.append(du)
            dqhg = dqhg + dg
            _, xh, rs = _rms(kvp[:, h * HP:(h + 1) * HP] + kpe, khg, QKD)
            du, dg = _rms_bwd(_rope_adj(dk_ref[h].astype(F32), cc, ss), xh, rs, khg, QKD)
            dkp.append(du)
            dkhg = dkhg + dg
            dkpe = dkpe + jnp.where((lane >= NOPE) & (lane < QKD), du, 0.0)
            dvp.append(dv_ref[h].astype(F32))
        dqhg_ref[...] += dqhg
        dkhg_ref[...] += dkhg
        dqp16 = jnp.concatenate(dqp, axis=-1).astype(BF16)
        dkvp16 = jnp.concatenate(dkp + dvp, axis=-1).astype(BF16)
        dwuq_ref[...] += _dot_tn(qn16, dqp16)
        dwukv_ref[...] += _dot_tn(kvn16, dkvp16)
        dql, dg = _rms_bwd(_dot(dqp16, wuqt_ref[...]), qxh, qrs, qng_ref[...], QL)
        dqng_ref[...] += dg
        dkvl, dg = _rms_bwd(_dot(dkvp16, wukvt_ref[...]), kvxh, kvrs, kvng_ref[...], KVL)
        dkvng_ref[...] += dg
        dl_ref[:, 0:QL] = dql.astype(BF16)
        dl_ref[:, QL:QL + KVL] = dkvl.astype(BF16)
        dl_ref[:, QL + KVL:P_LW] = dkpe.astype(BF16)

    hs = pl.BlockSpec((H, t, HP), lambda i: (0, i, 0))
    row = lambda w: pl.BlockSpec((t, w), lambda i: (i, 0))
    return _call(
        body, name=name, grid=(s // t,),
        in_specs=[row(P_LW), row(HP), row(HP), _full((1, QL)), _full((1, KVL)), _full((1, HP)), _full((1, HP)),
                  _full((QL, H * HP)), _full((KVL, 2 * H * HP)), _full((H * HP, QL)), _full((2 * H * HP, KVL)),
                  hs, hs, hs],
        out_specs=[row(P_LW), _full((QL, H * HP)), _full((KVL, 2 * H * HP)), _full((1, QL)), _full((1, KVL)),
                   _full((1, HP)), _full((1, HP))],
        out_shape=[jax.ShapeDtypeStruct((s, P_LW), BF16), jax.ShapeDtypeStruct((QL, H * HP), F32),
                   jax.ShapeDtypeStruct((KVL, 2 * H * HP), F32), jax.ShapeDtypeStruct((1, QL), F32),
                   jax.ShapeDtypeStruct((1, KVL), F32), jax.ShapeDtypeStruct((1, HP), F32),
                   jax.ShapeDtypeStruct((1, HP), F32)],
        compiler_params=_cp(("arbitrary",)),
    )(proj_l, c, sn, qng, kvng, qhg, khg, wuq, wukv, wuqt, wukvt, dq, dk, dv)


def _mem_prep_bwd(mem, mng, wmkv, wmkvt, mkg, dmk, dmv, name):
    m = mem.shape[0]

    def body(mem_ref, mng_ref, w_ref, wt_ref, mkg_ref, dmk_ref, dmv_ref, dw_ref, dmng_ref, dmkg_ref):
        mn, xh, _ = _rms(mem_ref[...], mng_ref[...], D)
        mn16 = mn.astype(BF16)
        mkv = _dot(mn16, w_ref[...])
        parts = []
        dmkg = jnp.zeros((1, MHD), F32)
        for h in range(HM):
            _, kxh, krs = _rms(mkv[:, 2 * h * MHD:(2 * h + 1) * MHD], mkg_ref[...], MHD)
            du, dg = _rms_bwd(dmk_ref[h], kxh, krs, mkg_ref[...], MHD)
            dmkg = dmkg + dg
            parts += [du, dmv_ref[h]]
        dmkv = jnp.concatenate(parts, axis=-1).astype(BF16)
        dw_ref[...] = _dot_tn(mn16, dmkv)
        dmn = _dot(dmkv, wt_ref[...])
        dmng_ref[...] = jnp.sum(dmn * xh, axis=0, keepdims=True)
        dmkg_ref[...] = dmkg

    return _call(
        body, name=name,
        in_specs=[_full((m, D)), _full((1, D)), _full((D, 2 * MW)), _full((2 * MW, D)), _full((1, MHD)),
                  _full((HM, m, MHD)), _full((HM, m, MHD))],
        out_specs=[_full((D, 2 * MW)), _full((1, D)), _full((1, MHD))],
        out_shape=[jax.ShapeDtypeStruct((D, 2 * MW), F32), jax.ShapeDtypeStruct((1, D), F32),
                   jax.ShapeDtypeStruct((1, MHD), F32)],
        compiler_params=_cp(),
    )(mem, mng, wmkv, wmkvt, mkg, dmk, dmv)


def _rms_in_bwd(x, g_out, dh, ng, t, name):
    s = x.shape[0]
    t = min(t, s)

    def body(x_ref, go_ref, dh_ref, ng_ref, dx_ref, dng_ref):
        @pl.when(pl.program_id(0) == 0)
        def _():
            dng_ref[...] = jnp.zeros_like(dng_ref)

        _, xh, rs = _rms(x_ref[...], ng_ref[...], D)
        dx, dg = _rms_bwd(dh_ref[...], xh, rs, ng_ref[...], D)
        dx_ref[...] = go_ref[...] + dx
        dng_ref[...] += dg

    row = pl.BlockSpec((t, D), lambda i: (i, 0))
    return _call(
        body, name=name, grid=(s // t,),
        in_specs=[row, row, row, _full((1, D))], out_specs=[row, _full((1, D))],
        out_shape=[jax.ShapeDtypeStruct((s, D), F32), jax.ShapeDtypeStruct((1, D), F32)],
        compiler_params=_cp(("arbitrary",)),
    )(x, g_out, dh, ng)


def _allgather(arrs, name):
    n = len(arrs)

    def body(*refs):
        x_refs, out_refs = refs[:n], refs[n:2 * n]
        send_sems, recv_sems, local_sems = refs[2 * n:]
        x, y, c = lax.axis_index("x"), lax.axis_index("y"), lax.axis_index("c")
        me, sibling = (x, y, c), (x, y, 1 - c)
        chips = [(1 - x, y), (x, 1 - y), (1 - x, 1 - y)]

        def slot(a, px, py, pc):
            return out_refs[a].at[4 * px + 2 * py + pc]

        def copy(a, k, block, to, src=None):
            return pltpu.make_async_remote_copy(
                src_ref=slot(a, *block) if src is None else src, dst_ref=slot(a, *block),
                send_sem=send_sems.at[a, k], recv_sem=recv_sems.at[a, k],
                device_id=to, device_id_type=pl.DeviceIdType.MESH)

        mine = [pltpu.make_async_copy(x_refs[a], slot(a, *me), local_sems.at[a]) for a in range(n)]
        first, passed = [], []
        for a in range(n):
            mine[a].start()
            first.append(copy(a, 0, me, sibling, src=x_refs[a]))
            first += [copy(a, 1 + j, me, (*chip, c), src=x_refs[a]) for j, chip in enumerate(chips)]
        for cp in first:
            cp.start()
        for j, chip in enumerate(chips):
            for a in range(n):
                copy(a, 1 + j, (*chip, c), me).wait_recv()
                passed.append(copy(a, 4 + j, (*chip, c), sibling))
                passed[-1].start()
        for a in range(n):
            copy(a, 0, sibling, me).wait_recv()
        for j, chip in enumerate(chips):
            for a in range(n):
                copy(a, 4 + j, (*chip, 1 - c), me).wait_recv()
        for cp in first + passed:
            cp.wait_send()
        for cp in mine:
            cp.wait()

    any_spec = pl.BlockSpec(memory_space=pl.ANY)
    return _call(
        body, name=name,
        in_specs=[any_spec] * n, out_specs=[any_spec] * n,
        out_shape=[jax.ShapeDtypeStruct((N_DEV,) + a.shape, a.dtype) for a in arrs],
        scratch_shapes=[pltpu.SemaphoreType.DMA((n, 7)), pltpu.SemaphoreType.DMA((n, 7)),
                        pltpu.SemaphoreType.DMA((n,))],
    )(*arrs)


def _alltoall(arrs, name):
    n = len(arrs)

    def body(*refs):
        x_refs, out_refs = refs[:n], refs[n:2 * n]
        send_sems, recv_sems, local_sems = refs[2 * n:]
        x, y, c = lax.axis_index("x"), lax.axis_index("y"), lax.axis_index("c")
        me = 4 * x + 2 * y + c
        mine = [pltpu.make_async_copy(x_refs[a].at[me], out_refs[a].at[me], local_sems.at[a]) for a in range(n)]
        for cp in mine:
            cp.start()
        copies = []
        for k in range(1, N_DEV):
            bx, by, bc = (k >> 2) & 1, (k >> 1) & 1, k & 1
            px = 1 - x if bx else x
            py = 1 - y if by else y
            pc = 1 - c if bc else c
            for a in range(n):
                copies.append(pltpu.make_async_remote_copy(
                    src_ref=x_refs[a].at[4 * px + 2 * py + pc], dst_ref=out_refs[a].at[me],
                    send_sem=send_sems.at[a, k - 1], recv_sem=recv_sems.at[a, k - 1],
                    device_id=(px, py, pc), device_id_type=pl.DeviceIdType.MESH))
        for cp in copies:
            cp.start()
        for cp in copies:
            cp.wait_recv()
        for cp in copies:
            cp.wait_send()
        for cp in mine:
            cp.wait()

    any_spec = pl.BlockSpec(memory_space=pl.ANY)
    return _call(
        body, name=name,
        in_specs=[any_spec] * n, out_specs=[any_spec] * n,
        out_shape=[jax.ShapeDtypeStruct(a.shape, a.dtype) for a in arrs],
        scratch_shapes=[pltpu.SemaphoreType.DMA((n, 7)), pltpu.SemaphoreType.DMA((n, 7)),
                        pltpu.SemaphoreType.DMA((n,))],
    )(*arrs)


ADAMW_BLOCK_BYTES = 4 * 1024 * 1024


def _adamw(parts, w, m, v, name):
    r, c_ = w.shape
    cpad = -(-c_ // LANES) * LANES
    tr = r
    while N_DEV * tr * cpad * 4 > ADAMW_BLOCK_BYTES and tr % 16 == 0:
        tr //= 2
    c1 = 1.0 / (1.0 - ADAM_B1 ** ADAM_STEP)
    c2 = 1.0 / (1.0 - ADAM_B2 ** ADAM_STEP)

    def body(p_ref, w_ref, m_ref, v_ref, g_ref, d_ref, nm_ref, nv_ref):
        g = p_ref[0].astype(F32)
        for j in range(1, N_DEV):
            g = g + p_ref[j].astype(F32)
        nm = ADAM_B1 * m_ref[...] + (1.0 - ADAM_B1) * g
        nv = ADAM_B2 * v_ref[...] + (1.0 - ADAM_B2) * (g * g)
        g_ref[...] = g
        nm_ref[...] = nm
        nv_ref[...] = nv
        d_ref[...] = -ADAM_LR * ((nm * c1) / (jnp.sqrt(nv * c2) + ADAM_EPS) + ADAM_WD * w_ref[...])

    row = pl.BlockSpec((tr, c_), lambda i: (i, 0))
    return _call(
        body, name=name, grid=(r // tr,),
        in_specs=[pl.BlockSpec((N_DEV, tr, c_), lambda i: (0, i, 0)), row, row, row],
        out_specs=[row] * 4, out_shape=[jax.ShapeDtypeStruct((r, c_), F32)] * 4,
        compiler_params=_cp(("parallel",)),
    )(parts, w, m, v)


def _to_rows(flat, align=8):
    n = flat.shape[-1]
    rows = -(-n // (LANES * align)) * align
    return jnp.pad(flat, (0, rows * LANES - n)).reshape(rows, LANES)


def _shard_blocks(full, axis):
    r, c_ = full.shape
    if axis == 0:
        return full.reshape(N_DEV, r // N_DEV, c_)
    return full.reshape(r, N_DEV, c_ // N_DEV).transpose(1, 0, 2)


def _unshard_blocks(blocks, axis):
    _, r, c_ = blocks.shape
    if axis == 0:
        return blocks.reshape(N_DEV * r, c_)
    return blocks.transpose(1, 0, 2).reshape(r, N_DEV * c_)


def _pad_heads(w, width):
    k = w.shape[0]
    return jnp.pad(w.reshape(k, H, width), ((0, 0), (0, 0), (0, HP - width))).reshape(k, H * HP)


def _unpad_heads(w, width):
    k = w.shape[0]
    return w.reshape(k, H, HP)[:, :, :width].reshape(k, H * width)


IN_SPLIT = {'q_lat': (0, 384), 'kv_lat': (384, 640), 'k_pe': (640, 672), 'c_b': (672, 1184), 'c_c': (1184, 1696),
            'c_u': (1696, 2208), 'q_mem': (2208, 2720), 'g_attn': (2720, 3232), 'g_conv': (3232, 3744),
            'g_mem': (3744, 4256), 'r': (4256, 7328)}
P_ORDER = ['c_b', 'c_c', 'c_u', 'g_conv', 'q_mem', 'g_mem', 'r', 'g_attn', 'q_lat', 'kv_lat']


def _permute_w_in(w):
    k = w.shape[0]
    cols = [w[:, IN_SPLIT[n][0]:IN_SPLIT[n][1]] for n in P_ORDER]
    kpe = w[:, IN_SPLIT['k_pe'][0]:IN_SPLIT['k_pe'][1]]
    cols += [jnp.zeros((k, NOPE), w.dtype), kpe, jnp.zeros((k, HP - QKD), w.dtype)]
    return jnp.concatenate(cols, axis=1)


def _unpermute_w_in(wp):
    off, pos = 0, {}
    for n in P_ORDER:
        wd = IN_SPLIT[n][1] - IN_SPLIT[n][0]
        pos[n] = (off, off + wd)
        off += wd
    pos['k_pe'] = (off + NOPE, off + QKD)
    order = sorted(IN_SPLIT, key=lambda n: IN_SPLIT[n][0])
    return jnp.concatenate([wp[:, pos[n][0]:pos[n][1]] for n in order], axis=1)


def _layer_weights(full, l):
    w = {}
    w_in_p = _permute_w_in(full['w_in'][l])
    w['w_main'] = w_in_p[:, :P_MAIN]
    w['w_l'] = w_in_p[:, P_MAIN:]
    w['w_in_t'] = jnp.pad(w_in_p, ((0, 0), (0, P_WPAD - P_W))).T
    w['w_uq'] = _pad_heads(full['w_uq'][l], QKD)
    wukv = full['w_ukv'][l].reshape(KVL, H, 2, NOPE)
    kpart = jnp.pad(wukv[:, :, 0, :], ((0, 0), (0, 0), (0, HP - NOPE))).reshape(KVL, H * HP)
    vpart = jnp.pad(wukv[:, :, 1, :], ((0, 0), (0, 0), (0, HP - VD))).reshape(KVL, H * HP)
    w['w_ukv'] = jnp.concatenate([kpart, vpart], axis=1)
    w['w_uq_t'] = w['w_uq'].T
    w['w_ukv_t'] = w['w_ukv'].T
    w['w_mkv'] = full['w_mkv'][l]
    w['w_mkv_t'] = w['w_mkv'].T
    for n in ('w_br_attn', 'w_br_conv', 'w_br_mem', 'w_out'):
        w[n] = full[n][l]
        w[n + '_t'] = w[n].T
    return w


def kernel(x, mem, positions, norm_g, w_in, b_gate, q_norm_g, w_uq, kv_norm_g, w_ukv, q_head_g, k_head_g, conv_w, conv_b, mem_norm_g, w_mkv, mem_q_g, mem_k_g, w_br_attn, w_br_conv, w_br_mem, w_out, loss_target, m_norm_g, m_w_in, m_b_gate, m_q_norm_g, m_w_uq, m_kv_norm_g, m_w_ukv, m_q_head_g, m_k_head_g, m_conv_w, m_conv_b, m_mem_norm_g, m_w_mkv, m_mem_q_g, m_mem_k_g, m_w_br_attn, m_w_br_conv, m_w_br_mem, m_w_out, v_norm_g, v_w_in, v_b_gate, v_q_norm_g, v_w_uq, v_kv_norm_g, v_w_ukv, v_q_head_g, v_k_head_g, v_conv_w, v_conv_b, v_mem_norm_g, v_w_mkv, v_mem_q_g, v_mem_k_g, v_w_br_attn, v_w_br_conv, v_w_br_mem, v_w_out):
    a = dict(zip(INPUTS, (x, mem, positions, norm_g, w_in, b_gate, q_norm_g, w_uq, kv_norm_g, w_ukv, q_head_g, k_head_g, conv_w, conv_b, mem_norm_g, w_mkv, mem_q_g, mem_k_g, w_br_attn, w_br_conv, w_br_mem, w_out, loss_target, m_norm_g, m_w_in, m_b_gate, m_q_norm_g, m_w_uq, m_kv_norm_g, m_w_ukv, m_q_head_g, m_k_head_g, m_conv_w, m_conv_b, m_mem_norm_g, m_w_mkv, m_mem_q_g, m_mem_k_g, m_w_br_attn, m_w_br_conv, m_w_br_mem, m_w_out, v_norm_g, v_w_in, v_b_gate, v_q_norm_g, v_w_uq, v_kv_norm_g, v_w_ukv, v_q_head_g, v_k_head_g, v_conv_w, v_conv_b, v_mem_norm_g, v_w_mkv, v_mem_q_g, v_mem_k_g, v_w_br_attn, v_w_br_conv, v_w_br_mem, v_w_out)))
    x = a['x'][0]
    mem = a['mem'][0]
    tgt = a['loss_target'][0]
    s = x.shape[0]
    t_el = 256
    tq_f, tk_f, tq_b, tk_b = 256, 1024, 512, 512

    gathered = _allgather([a[n].astype(BF16) for n in BIG_ORDER] + [a['conv_w']], "ag_weights")
    full = {n: [_unshard_blocks(g8[:, l], BIG[n][1]) for l in range(DEPTH)] for n, g8 in zip(BIG_ORDER, gathered)}
    conv_w = gathered[-1].transpose(1, 2, 0, 3).reshape(DEPTH, 3, CW)

    inv_freq = ROPE_BASE ** (-jnp.arange(0, RP, 2, dtype=F32) / RP)
    ang = a['positions'][0].astype(F32)[:, None] * inv_freq
    cos, sin = jnp.cos(ang), jnp.sin(ang)
    rc = jnp.concatenate([jnp.ones((s, NOPE), F32), cos, cos, jnp.ones((s, HP - QKD), F32)], axis=1)
    rs = jnp.concatenate([jnp.zeros((s, NOPE), F32), -sin, sin, jnp.zeros((s, HP - QKD), F32)], axis=1)

    def small(n, l, width=None):
        v = a[n][l][None, :]
        return v if width is None else jnp.pad(v, ((0, 0), (0, width - v.shape[1])))

    saved = []
    layer_w = [_layer_weights(full, l) for l in range(DEPTH)]
    for l in range(DEPTH):
        w = layer_w[l]
        tag = ""
        h, ht = _rms_h(x, small('norm_g', l), 512, "rms_h" + tag)
        proj = _mm(h, w['w_main'], BF16, 2048, 512, "in_proj" + tag)
        proj_l = _mm(h, w['w_l'], BF16, 512, P_LW, "in_proj_lat" + tag)
        qng, kvng = small('q_norm_g', l), small('kv_norm_g', l)
        qhg, khg = small('q_head_g', l, HP), small('k_head_g', l, HP)
        q, k, v = _mla_prep(proj_l, rc, rs, qng, kvng, qhg, khg, w['w_uq'], w['w_ukv'], t_el, "mla_prep" + tag)
        o, lse = _flash_fwd(q, k, v, tq_f, tk_f, "flash_fwd" + tag)
        mk, mv = _mem_prep(mem, small('mem_norm_g', l), w['w_mkv'], small('mem_k_g', l), "mem_prep" + tag)
        cb, mqg = small('conv_b', l), small('mem_q_g', l)
        oa, oc, om = _branches(proj, o, mk, mv, conv_w[l], cb, mqg, t_el, "branches" + tag)
        x_out, aa, ac, am = _merge_fwd(x, proj, oa, oc, om, small('b_gate', l), w['w_br_attn'], w['w_br_conv'],
                                       w['w_br_mem'], w['w_out'], t_el, "merge" + tag)
        saved.append(dict(x=x, ht=ht, proj=proj, proj_l=proj_l, q=q, k=k, v=v, o=o, lse=lse, mk=mk, mv=mv,
                          oa=oa, oc=oc, om=om, aa=aa, ac=ac, am=am))
        x = x_out

    g, loss_parts = _loss_grad(x, tgt, 512, "loss")
    loss = lax.psum(jnp.sum(loss_parts), ("x", "y", "c"))

    gw = {n: [None] * DEPTH for n in WEIGHTS}
    for l in reversed(range(DEPTH)):
        w = layer_w[l]
        sv = saved[l]
        tag = ""
        tqb = min(tq_b, s)
        y, daa, dac, dam, dr, doa, doc, dom, dbg = _merge_bwd(
            g, sv['proj'], sv['aa'], sv['ac'], sv['am'], small('b_gate', l), w['w_out_t'], w['w_br_attn_t'],
            w['w_br_conv_t'], w['w_br_mem_t'], t_el, "merge_bwd" + tag)
        gw['b_gate'][l] = dbg[0]
        gw['w_out'][l] = _tn(y, g, 512, 512, "dw_out" + tag)
        gw['w_br_attn'][l] = _tn(sv['oa'], daa, 512, 512, "dw_attn" + tag)
        gw['w_br_conv'][l] = _tn(sv['oc'], dac, 512, 512, "dw_conv" + tag)
        gw['w_br_mem'][l] = _tn(sv['om'], dam, 512, 512, "dw_mem" + tag)
        cb, mqg = small('conv_b', l), small('mem_q_g', l)
        dcv, dmm, dga, do, dl, dcw, dcb, dmk, dmv, dmqg = _branches_bwd(
            sv['proj'], sv['o'], sv['mk'], sv['mv'], conv_w[l], cb, mqg, doa, doc, dom, t_el, "branches_bwd" + tag)
        gw['conv_w'][l], gw['conv_b'][l], gw['mem_q_g'][l] = dcw, dcb[0], dmqg[0]
        dwm, dmng, dmkg = _mem_prep_bwd(mem, small('mem_norm_g', l), w['w_mkv'], w['w_mkv_t'], small('mem_k_g', l),
                                        dmk, dmv, "mem_prep_bwd" + tag)
        gw['w_mkv'][l], gw['mem_norm_g'][l], gw['mem_k_g'][l] = dwm, dmng[0], dmkg[0]
        lse_r = sv['lse'].reshape(H, s // tqb, 1, tqb)
        dl_r = dl.reshape(H, s // tqb, 1, tqb)
        dq, dk, dv = _flash_bwd(sv['q'], sv['k'], sv['v'], do, lse_r, dl_r, tq_b, tk_b, "flash_bwd" + tag)
        qng, kvng = small('q_norm_g', l), small('kv_norm_g', l)
        qhg, khg = small('q_head_g', l, HP), small('k_head_g', l, HP)
        dlat, dwuq, dwukv, dqng, dkvng, dqhg, dkhg = _mla_prep_bwd(
            sv['proj_l'], rc, rs, qng, kvng, qhg, khg, w['w_uq'], w['w_ukv'], w['w_uq_t'], w['w_ukv_t'],
            dq, dk, dv, t_el, "mla_prep_bwd" + tag)
        gw['w_uq'][l] = _unpad_heads(dwuq, QKD)
        dwukv = dwukv.reshape(KVL, 2, H, HP)[:, :, :, :NOPE]
        gw['w_ukv'][l] = dwukv.transpose(0, 2, 1, 3).reshape(KVL, H * 2 * NOPE)
        gw['q_norm_g'][l], gw['kv_norm_g'][l] = dqng[0], dkvng[0]
        gw['q_head_g'][l], gw['k_head_g'][l] = dqhg[0, :QKD], dkhg[0, :QKD]
        dproj = jnp.concatenate([dcv, dmm, dr, dga, dlat, jnp.zeros((s, P_WPAD - P_W), BF16)], axis=1)
        dh = _mm(dproj, w['w_in_t'], F32, 512, 512, "d_h" + tag)
        gw['w_in'][l] = _unpermute_w_in(_mm_acc(sv['ht'], dproj, 1024, 1536, "dw_in" + tag))
        g, dng = _rms_in_bwd(sv['x'], g, dh, small('norm_g', l), 512, "rms_bwd" + tag)
        gw['norm_g'][l] = dng[0]
    grad_x = g[None]

    sharded = BIG_ORDER + ['conv_w']
    axis_of = lambda n: 1 if n == 'conv_w' else BIG[n][1]
    send = [jnp.stack([_shard_blocks(gw[n][l], axis_of(n)) for l in range(DEPTH)], axis=1)
            .astype(F32 if n == 'conv_w' else BF16) for n in sharded]
    parts_big = _alltoall(send, "rs_grads")
    small_flat = jnp.concatenate([jnp.stack(gw[n]).reshape(-1) for n in SMALL_ORDER])
    n_small = small_flat.shape[0]
    parts_small = _allgather([_to_rows(small_flat)], "ag_small_grads")[0]

    outs = [{} for _ in range(4)]
    for n, parts in zip(sharded, parts_big):
        loc = a[n].shape
        two_d = lambda t: t.reshape(loc[0] * loc[1], loc[2])
        res = _adamw(parts.reshape(N_DEV, loc[0] * loc[1], loc[2]), two_d(a[n]), two_d(a['m_' + n]),
                     two_d(a['v_' + n]), "adamw_" + n)
        for d, r in zip(outs, res):
            d[n] = r.reshape(loc)
    pks = lambda pre: _to_rows(jnp.concatenate([a[pre + n].reshape(-1) for n in SMALL_ORDER]))
    res_small = _adamw(parts_small, pks(''), pks('m_'), pks('v_'), "adamw_small")
    for d, rsm in zip(outs, res_small):
        flat = rsm.reshape(-1)[:n_small]
        off = 0
        for n in SMALL_ORDER:
            d[n] = flat[off:off + DEPTH * SMALL[n]].reshape(DEPTH, SMALL[n])
            off += DEPTH * SMALL[n]
    result = [loss, grad_x]
    for d in outs:
        result += [d[n] for n in WEIGHTS]
    return tuple(result)
```

```python
import functools

import jax
import jax.numpy as jnp
from jax import lax
from jax.experimental import pallas as pl
from jax.experimental.pallas import tpu as pltpu

F32, BF16 = jnp.float32, jnp.bfloat16

N_DEV = 8
DEPTH = 4
D = 1024
QL, KVL, RP = 384, 256, 32
H, NOPE, QKD, VD = 8, 64, 96, 64
HP = 128
CW, MW, AW = 512, 512, 512
HM, MHD = 4, 128
IN_WIDTH = 7328
EPS = 1e-6
ROPE_BASE = 10000.0
ATT_SCALE = QKD ** -0.5
MEM_SCALE = MHD ** -0.5
LOG2E = 1.4426950408889634
QSCALE = ATT_SCALE * LOG2E

ADAM_LR, ADAM_B1, ADAM_B2, ADAM_EPS, ADAM_WD, ADAM_STEP = 0.001, 0.9, 0.999, 1e-08, 0.01, 10

LANES = 128
VMEM_LIMIT = 56 * 1024 * 1024

P_CONV, P_MEM, P_R, P_GA, P_L = 0, 2048, 3072, 6144, 6656
P_MAIN = 6656
P_LW = 768
P_W = P_MAIN + P_LW
P_WPAD = 7680

WEIGHTS = ['norm_g', 'w_in', 'b_gate', 'q_norm_g', 'w_uq', 'kv_norm_g', 'w_ukv', 'q_head_g', 'k_head_g',
           'conv_w', 'conv_b', 'mem_norm_g', 'w_mkv', 'mem_q_g', 'mem_k_g', 'w_br_attn', 'w_br_conv',
           'w_br_mem', 'w_out']
INPUTS = ['x', 'mem', 'positions'] + WEIGHTS + ['loss_target'] + ['m_' + n for n in WEIGHTS] + ['v_' + n for n in WEIGHTS]

BIG = {'w_in': ((D, IN_WIDTH), 1), 'w_uq': ((QL, H * QKD), 1), 'w_ukv': ((KVL, H * 128), 1),
       'w_mkv': ((D, 2 * MW), 0), 'w_br_attn': ((AW, D), 1), 'w_br_conv': ((CW, D), 1),
       'w_br_mem': ((MW, D), 1), 'w_out': ((D, D), 0)}
BIG_ORDER = ['w_in', 'w_uq', 'w_ukv', 'w_mkv', 'w_br_attn', 'w_br_conv', 'w_br_mem', 'w_out']
CONVW_PAD = 256
SMALL = {'norm_g': D, 'b_gate': 3 * D, 'q_norm_g': QL, 'kv_norm_g': KVL, 'q_head_g': QKD, 'k_head_g': QKD,
         'conv_b': CW, 'mem_norm_g': D, 'mem_q_g': MHD, 'mem_k_g': MHD}
SMALL_ORDER = list(SMALL)
ROW_ALIGN = 1024


def _call(body, **kw):
    return pl.pallas_call(body, **kw)


def _cp(sem=None):
    return pltpu.CompilerParams(dimension_semantics=sem, vmem_limit_bytes=VMEM_LIMIT)


def _full(shape):
    n = len(shape)
    return pl.BlockSpec(shape, lambda *_: (0,) * n)


def _rms(x, g, n):
    rs = lax.rsqrt(jnp.sum(x * x, axis=-1, keepdims=True) * (1.0 / n) + EPS)
    xh = x * rs
    return xh * g, xh, rs


def _rms_bwd(dy, xh, rs, g, n):
    dxh = dy * g
    dx = rs * (dxh - xh * (jnp.sum(dxh * xh, axis=-1, keepdims=True) * (1.0 / n)))
    return dx, jnp.sum(dy * xh, axis=0, keepdims=True)


def _sigmoid(x):
    return 1.0 / (1.0 + jnp.exp(-x))


def _swap_rope(u):
    lane = lax.broadcasted_iota(jnp.int32, u.shape, 1)
    up = pltpu.roll(u, 16, 1)
    dn = pltpu.roll(u, 112, 1)
    return jnp.where((lane >= 64) & (lane < 80), dn, jnp.where((lane >= 80) & (lane < 96), up, 0.0))


def _rope(u, c, sn):
    return u * c + _swap_rope(u) * sn


def _rope_adj(d, c, sn):
    return d * c + _swap_rope(d * sn)


def _dot(a, b):
    return jnp.dot(a, b, preferred_element_type=F32)


def _dot_nt(a, b):
    return lax.dot_general(a, b, (((1,), (1,)), ((), ())), preferred_element_type=F32)


def _dot_tn(a, b):
    return lax.dot_general(a, b, (((0,), (0,)), ((), ())), preferred_element_type=F32)


def _mm(a, b, out_dtype, tm, tn, name):
    m, k = a.shape
    _, n = b.shape
    tm, tn = min(tm, m), min(tn, n)

    def body(a_ref, b_ref, o_ref):
        o_ref[...] = _dot(a_ref[...].astype(BF16), b_ref[...]).astype(o_ref.dtype)

    return _call(
        body, name=name, grid=(m // tm, n // tn),
        in_specs=[pl.BlockSpec((tm, k), lambda i, j: (i, 0)), pl.BlockSpec((k, tn), lambda i, j: (0, j))],
        out_specs=pl.BlockSpec((tm, tn), lambda i, j: (i, j)),
        out_shape=jax.ShapeDtypeStruct((m, n), out_dtype),
        compiler_params=_cp(("parallel", "arbitrary")),
    )(a, b)


def _tn(a, b, ts, tn, name):
    s, ka = a.shape
    _, n = b.shape
    ts, tn = min(ts, s), min(tn, n)

    def body(a_ref, b_ref, o_ref):
        @pl.when(pl.program_id(1) == 0)
        def _():
            o_ref[...] = jnp.zeros_like(o_ref)

        o_ref[...] += _dot_tn(a_ref[...].astype(BF16), b_ref[...].astype(BF16))

    return _call(
        body, name=name, grid=(n // tn, s // ts),
        in_specs=[pl.BlockSpec((ts, ka), lambda j, i: (i, 0)), pl.BlockSpec((ts, tn), lambda j, i: (i, j))],
        out_specs=pl.BlockSpec((ka, tn), lambda j, i: (0, j)),
        out_shape=jax.ShapeDtypeStruct((ka, n), F32),
        compiler_params=_cp(("parallel", "arbitrary")),
    )(a, b)


def _mm_acc(a, b, tk, tn, name):
    m, k = a.shape
    _, n = b.shape
    tk, tn = min(tk, k), min(tn, n)

    def body(a_ref, b_ref, o_ref):
        @pl.when(pl.program_id(1) == 0)
        def _():
            o_ref[...] = jnp.zeros_like(o_ref)

        o_ref[...] += _dot(a_ref[...], b_ref[...])

    return _call(
        body, name=name, grid=(n // tn, k // tk),
        in_specs=[pl.BlockSpec((m, tk), lambda j, i: (0, i)), pl.BlockSpec((tk, tn), lambda j, i: (i, j))],
        out_specs=pl.BlockSpec((m, tn), lambda j, i: (0, j)),
        out_shape=jax.ShapeDtypeStruct((m, n), F32),
        compiler_params=_cp(("parallel", "arbitrary")),
    )(a, b)


def _rms_h(x, g, t, name):
    s = x.shape[0]
    t = min(t, s)

    def body(x_ref, g_ref, h_ref, ht_ref):
        h = _rms(x_ref[...], g_ref[...], D)[0]
        h_ref[...] = h.astype(BF16)
        ht_ref[...] = h.T.astype(BF16)

    return _call(
        body, name=name, grid=(s // t,),
        in_specs=[pl.BlockSpec((t, D), lambda i: (i, 0)), _full((1, D))],
        out_specs=[pl.BlockSpec((t, D), lambda i: (i, 0)), pl.BlockSpec((D, t), lambda i: (0, i))],
        out_shape=[jax.ShapeDtypeStruct((s, D), BF16), jax.ShapeDtypeStruct((D, s), BF16)],
        compiler_params=_cp(("parallel",)),
    )(x, g)


def _mla_heads(pl_blk, c, sn, qng, kvng, qhg, khg, wuq, wukv):
    ql = pl_blk[:, 0:QL].astype(F32)
    kvl = pl_blk[:, QL:QL + KVL].astype(F32)
    kpe = pl_blk[:, QL + KVL:P_LW].astype(F32)
    qn, qxh, qrs = _rms(ql, qng, QL)
    kvn, kvxh, kvrs = _rms(kvl, kvng, KVL)
    qn16, kvn16 = qn.astype(BF16), kvn.astype(BF16)
    qp = _dot(qn16, wuq)
    kvp = _dot(kvn16, wukv)
    return ql, kvl, kpe, qxh, qrs, kvxh, kvrs, qn16, kvn16, qp, kvp


def _mla_prep(proj_l, c, sn, qng, kvng, qhg, khg, wuq, wukv, t, name):
    s = proj_l.shape[0]
    t = min(t, s)

    def body(l_ref, c_ref, sn_ref, qng_ref, kvng_ref, qhg_ref, khg_ref, wuq_ref, wukv_ref, q_ref, k_ref, v_ref):
        cc, ss = c_ref[...], sn_ref[...]
        (_, _, kpe, _, _, _, _, _, _, qp, kvp) = _mla_heads(
            l_ref[...], cc, ss, qng_ref[...], kvng_ref[...], qhg_ref[...], khg_ref[...], wuq_ref[...], wukv_ref[...])
        lane = lax.broadcasted_iota(jnp.int32, cc.shape, 1)
        for h in range(H):
            u = qp[:, h * HP:(h + 1) * HP]
            q_ref[h] = (_rope(_rms(u, qhg_ref[...], QKD)[0], cc, ss) * QSCALE).astype(BF16)
            u = kvp[:, h * HP:(h + 1) * HP] + kpe
            k_ref[h] = _rope(_rms(u, khg_ref[...], QKD)[0], cc, ss).astype(BF16)
            v_ref[h] = jnp.where(lane == VD, 1.0, kvp[:, (H + h) * HP:(H + h + 1) * HP]).astype(BF16)

    hs = pl.BlockSpec((H, t, HP), lambda i: (0, i, 0))
    row = lambda w: pl.BlockSpec((t, w), lambda i: (i, 0))
    return _call(
        body, name=name, grid=(s // t,),
        in_specs=[row(P_LW), row(HP), row(HP), _full((1, QL)), _full((1, KVL)), _full((1, HP)), _full((1, HP)),
                  _full((QL, H * HP)), _full((KVL, 2 * H * HP))],
        out_specs=[hs, hs, hs],
        out_shape=[jax.ShapeDtypeStruct((H, s, HP), BF16)] * 3,
        compiler_params=_cp(("parallel",)),
    )(proj_l, c, sn, qng, kvng, qhg, khg, wuq, wukv)


def _flash_fwd(q, k, v, tq, tk, name):
    _, s, _ = q.shape
    tq, tk = min(tq, s), min(tk, s)
    nk, nc = s // tk, tk // LANES
    un = 8 if nk % 8 == 0 else 1

    def body(q_ref, k_ref, v_ref, o_ref, lse_ref, s_scr):
        qb = q_ref[0]

        def scores(jj, mx):
            for u in range(un):
                j = jj * un + u
                off = pl.multiple_of(j * tk, tk)
                sc = _dot_nt(qb, k_ref[0, pl.ds(off, tk), :])
                s_scr[j] = sc
                for cc in range(nc):
                    mx = jnp.maximum(mx, sc[:, cc * LANES:(cc + 1) * LANES])
            return mx

        mx = lax.fori_loop(0, nk // un, scores, jnp.full((tq, LANES), -jnp.inf, F32))
        m = jnp.max(mx, axis=-1, keepdims=True)
        mb = jnp.broadcast_to(m, (tq, LANES))

        def probs(jj, acc):
            for u in range(un):
                j = jj * un + u
                off = pl.multiple_of(j * tk, tk)
                sc = s_scr[j]
                ps = [jnp.exp2(sc[:, cc * LANES:(cc + 1) * LANES] - mb).astype(BF16) for cc in range(nc)]
                acc = acc + _dot(jnp.concatenate(ps, axis=-1), v_ref[0, pl.ds(off, tk), :])
            return acc

        acc = lax.fori_loop(0, nk // un, probs, jnp.zeros((tq, HP), F32))
        l = acc[:, VD:VD + 1]
        o_ref[0] = (acc[:, :VD] / l).astype(BF16)
        lse = jnp.broadcast_to(m + jnp.log(l) * LOG2E, (tq, LANES))
        lse_ref[0] = lse.T[0:1, :]

    return _call(
        body, name=name, grid=(H, s // tq),
        in_specs=[pl.BlockSpec((1, tq, HP), lambda h, i: (h, i, 0)),
                  pl.BlockSpec((1, s, HP), lambda h, i: (h, 0, 0)),
                  pl.BlockSpec((1, s, HP), lambda h, i: (h, 0, 0))],
        out_specs=[pl.BlockSpec((1, tq, VD), lambda h, i: (h, i, 0)), pl.BlockSpec((1, 1, tq), lambda h, i: (h, 0, i))],
        out_shape=[jax.ShapeDtypeStruct((H, s, VD), BF16), jax.ShapeDtypeStruct((H, 1, s), F32)],
        scratch_shapes=[pltpu.VMEM((nk, tq, tk), F32)],
        compiler_params=_cp(("parallel", "arbitrary")),
    )(q, k, v)


def _mem_prep(mem, mng, wmkv, mkg, name):
    m = mem.shape[0]

    def body(mem_ref, mng_ref, w_ref, mkg_ref, mk_ref, mv_ref):
        mn = _rms(mem_ref[...], mng_ref[...], D)[0].astype(BF16)
        mkv = _dot(mn, w_ref[...])
        for h in range(HM):
            mk_ref[h] = _rms(mkv[:, 2 * h * MHD:(2 * h + 1) * MHD], mkg_ref[...], MHD)[0].astype(BF16)
            mv_ref[h] = mkv[:, (2 * h + 1) * MHD:(2 * h + 2) * MHD].astype(BF16)

    return _call(
        body, name=name,
        in_specs=[_full((m, D)), _full((1, D)), _full((D, 2 * MW)), _full((1, MHD))],
        out_specs=[_full((HM, m, MHD))] * 2,
        out_shape=[jax.ShapeDtypeStruct((HM, m, MHD), BF16)] * 2,
        compiler_params=_cp(),
    )(mem, mng, wmkv, mkg)


def _conv_parts(cv, prev, nxt, first, last, cw, cb):
    t = cv.shape[0]
    c_b, c_c, c_u, g_c = (cv[:, i * CW:(i + 1) * CW].astype(F32) for i in range(4))
    z = c_c * c_u
    zp = jnp.where(first, 0.0, prev[15:16, CW:2 * CW].astype(F32) * prev[15:16, 2 * CW:3 * CW].astype(F32))
    zn = jnp.where(last, 0.0, nxt[0:1, CW:2 * CW].astype(F32) * nxt[0:1, 2 * CW:3 * CW].astype(F32))
    row = lax.broadcasted_iota(jnp.int32, (t, CW), 0)
    z_m1 = jnp.where(row == 0, zp, pltpu.roll(z, 1, 0))
    z_p1 = jnp.where(row == t - 1, zn, pltpu.roll(z, t - 1, 0))
    conv = cw[0:1] * z_m1 + cw[1:2] * z + cw[2:3] * z_p1 + cb
    return c_b, c_c, c_u, g_c, z, z_m1, z_p1, conv


def _mem_attn(qm, mqg, mk_ref, mv_ref):
    outs = []
    for h in range(HM):
        mq, mqxh, mqrs = _rms(qm[:, h * MHD:(h + 1) * MHD], mqg, MHD)
        mq16 = mq.astype(BF16)
        sc = _dot_nt(mq16, mk_ref[h]) * MEM_SCALE
        e = jnp.exp(sc - jnp.max(sc, axis=-1, keepdims=True))
        p = e / jnp.sum(e, axis=-1, keepdims=True)
        o = _dot(p.astype(BF16), mv_ref[h])
        outs.append((mq16, mqxh, mqrs, p, o))
    return outs


def _halo_specs(t, s, width):
    nb = s // 16
    prev = pl.BlockSpec((16, width), lambda i: (jnp.maximum(i * (t // 16) - 1, 0), 0))
    nxt = pl.BlockSpec((16, width), lambda i: (jnp.minimum((i + 1) * (t // 16), nb - 1), 0))
    return prev, nxt


def _branches(proj, o, mk, mv, cw, cb, mqg, t, name):
    s = proj.shape[0]
    t = min(t, s)
    nt = s // t

    def body(cv_ref, pv_ref, nx_ref, mm_ref, ga_ref, o_ref, mk_ref, mv_ref, cw_ref, cb_ref, mqg_ref,
             oa_ref, oc_ref, om_ref):
        i = pl.program_id(0)
        c_b, _, _, g_c, _, _, _, conv = _conv_parts(
            cv_ref[...], pv_ref[...], nx_ref[...], i == 0, i == nt - 1, cw_ref[...], cb_ref[...])
        oc_ref[...] = (c_b * conv * (g_c * _sigmoid(g_c))).astype(BF16)
        ga = ga_ref[...].astype(F32)
        ocat = jnp.concatenate([o_ref[h].astype(F32) for h in range(H)], axis=-1)
        oa_ref[...] = (ocat * (ga * _sigmoid(ga))).astype(BF16)
        mblk = mm_ref[...].astype(F32)
        gm = mblk[:, MW:]
        heads = _mem_attn(mblk[:, :MW], mqg_ref[...], mk_ref, mv_ref)
        om = jnp.concatenate([hh[4] for hh in heads], axis=-1)
        om_ref[...] = (om * (gm * _sigmoid(gm))).astype(BF16)

    pv, nx = _halo_specs(t, s, 4 * CW)
    out = pl.BlockSpec((t, 512), lambda i: (i, 0))
    return _call(
        body, name=name, grid=(nt,),
        in_specs=[pl.BlockSpec((t, 4 * CW), lambda i: (i, 0)), pv, nx,
                  pl.BlockSpec((t, 2 * MW), lambda i: (i, P_MEM // (2 * MW))),
                  pl.BlockSpec((t, AW), lambda i: (i, P_GA // AW)),
                  pl.BlockSpec((H, t, VD), lambda i: (0, i, 0)),
                  _full(mk.shape), _full(mv.shape), _full((3, CW)), _full((1, CW)), _full((1, MHD))],
        out_specs=[out, out, out],
        out_shape=[jax.ShapeDtypeStruct((s, 512), BF16)] * 3,
        compiler_params=_cp(("parallel",)),
    )(proj, proj, proj, proj, proj, o, mk, mv, cw, cb, mqg)


def _merge_fwd(x, proj, oa, oc, om, bg, wa, wc, wm, wo, t, name):
    s = x.shape[0]
    t = min(t, s)

    def body(x_ref, r_ref, oa_ref, oc_ref, om_ref, bg_ref, wa_ref, wc_ref, wm_ref, wo_ref,
             xo_ref, aa_ref, ac_ref, am_ref):
        y = jnp.zeros((t, D), F32)
        for j, (o_ref, w_ref, a_ref) in enumerate(((oa_ref, wa_ref, aa_ref), (oc_ref, wc_ref, ac_ref),
                                                   (om_ref, wm_ref, am_ref))):
            a = _dot(o_ref[...], w_ref[...])
            a_ref[...] = a.astype(BF16)
            rg = _sigmoid(r_ref[:, j * D:(j + 1) * D].astype(F32) + bg_ref[:, j * D:(j + 1) * D])
            y = y + rg * a
        xo_ref[...] = x_ref[...] + _dot(y.astype(BF16), wo_ref[...])

    row = lambda w: pl.BlockSpec((t, w), lambda i: (i, 0))
    return _call(
        body, name=name, grid=(s // t,),
        in_specs=[row(D), pl.BlockSpec((t, 3 * D), lambda i: (i, P_R // (3 * D))), row(512), row(512), row(512),
                  _full((1, 3 * D)), _full((512, D)), _full((512, D)), _full((512, D)), _full((D, D))],
        out_specs=[row(D), row(D), row(D), row(D)],
        out_shape=[jax.ShapeDtypeStruct((s, D), F32)] + [jax.ShapeDtypeStruct((s, D), BF16)] * 3,
        compiler_params=_cp(("parallel",)),
    )(x, proj, oa, oc, om, bg, wa, wc, wm, wo)


def _loss_grad(x, tgt, t, name):
    s = x.shape[0]
    t = min(t, s)

    def body(x_ref, t_ref, g_ref, l_ref):
        @pl.when(pl.program_id(0) == 0)
        def _():
            l_ref[...] = jnp.zeros_like(l_ref)

        e = x_ref[...] - t_ref[...]
        g_ref[...] = e * (1.0 / D)
        sq = e * e
        part = sq[:, 0:LANES]
        for j in range(1, D // LANES):
            part = part + sq[:, j * LANES:(j + 1) * LANES]
        acc = part[0:8]
        for j in range(1, t // 8):
            acc = acc + part[j * 8:(j + 1) * 8]
        l_ref[...] += acc * (0.5 / D)

    row = pl.BlockSpec((t, D), lambda i: (i, 0))
    return _call(
        body, name=name, grid=(s // t,),
        in_specs=[row, row], out_specs=[row, _full((8, LANES))],
        out_shape=[jax.ShapeDtypeStruct((s, D), F32), jax.ShapeDtypeStruct((8, LANES), F32)],
        compiler_params=_cp(("arbitrary",)),
    )(x, tgt)


def _merge_bwd(g, proj, aa, ac, am, bg, wot, wat, wct, wmt, t, name):
    s = g.shape[0]
    t = min(t, s)

    def body(g_ref, r_ref, aa_ref, ac_ref, am_ref, bg_ref, wot_ref, wat_ref, wct_ref, wmt_ref,
             y_ref, daa_ref, dac_ref, dam_ref, dr_ref, doa_ref, doc_ref, dom_ref, dbg_ref):
        @pl.when(pl.program_id(0) == 0)
        def _():
            dbg_ref[...] = jnp.zeros_like(dbg_ref)

        dy = _dot(g_ref[...].astype(BF16), wot_ref[...])
        y = jnp.zeros((t, D), F32)
        for j, (a_ref, wt_ref, da_ref, do_ref) in enumerate(((aa_ref, wat_ref, daa_ref, doa_ref),
                                                             (ac_ref, wct_ref, dac_ref, doc_ref),
                                                             (am_ref, wmt_ref, dam_ref, dom_ref))):
            a = a_ref[...].astype(F32)
            rg = _sigmoid(r_ref[:, j * D:(j + 1) * D].astype(F32) + bg_ref[:, j * D:(j + 1) * D])
            y = y + rg * a
            da = (dy * rg).astype(BF16)
            da_ref[...] = da
            dr = dy * a * rg * (1.0 - rg)
            dr_ref[:, j * D:(j + 1) * D] = dr.astype(BF16)
            dbg_ref[:, j * D:(j + 1) * D] += jnp.sum(dr, axis=0, keepdims=True)
            do_ref[...] = _dot(da, wt_ref[...])
        y_ref[...] = y.astype(BF16)

    row = lambda w: pl.BlockSpec((t, w), lambda i: (i, 0))
    return _call(
        body, name=name, grid=(s // t,),
        in_specs=[row(D), pl.BlockSpec((t, 3 * D), lambda i: (i, P_R // (3 * D))), row(D), row(D), row(D),
                  _full((1, 3 * D)), _full((D, D)), _full((D, 512)), _full((D, 512)), _full((D, 512))],
        out_specs=[row(D), row(D), row(D), row(D), row(3 * D), row(512), row(512), row(512), _full((1, 3 * D))],
        out_shape=[jax.ShapeDtypeStruct((s, D), BF16)] * 4 + [jax.ShapeDtypeStruct((s, 3 * D), BF16)]
        + [jax.ShapeDtypeStruct((s, 512), F32)] * 3 + [jax.ShapeDtypeStruct((1, 3 * D), F32)],
        compiler_params=_cp(("arbitrary",)),
    )(g, proj, aa, ac, am, bg, wot, wat, wct, wmt)


def _branches_bwd(proj, o, mk, mv, cw, cb, mqg, doa, doc, dom, t, name):
    s = proj.shape[0]
    t = min(t, s)
    nt = s // t
    m = mk.shape[1]

    def body(cv_ref, pv_ref, nx_ref, mm_ref, ga_ref, o_ref, mk_ref, mv_ref, cw_ref, cb_ref, mqg_ref,
             doa_ref, doc_ref, dcp_ref, dcn_ref, dom_ref,
             dcv_ref, dmm_ref, dga_ref, do_ref, dl_ref, dcw_ref, dcb_ref, dmk_ref, dmv_ref, dmqg_ref):
        i = pl.program_id(0)

        @pl.when(i == 0)
        def _():
            for r in (dcw_ref, dcb_ref, dmk_ref, dmv_ref, dmqg_ref):
                r[...] = jnp.zeros_like(r)

        first, last = i == 0, i == nt - 1
        cw_, cb_ = cw_ref[...], cb_ref[...]
        pv, nx = pv_ref[...], nx_ref[...]
        c_b, c_c, c_u, g_c, z, z_m1, z_p1, conv = _conv_parts(cv_ref[...], pv, nx, first, last, cw_, cb_)
        sg = _sigmoid(g_c)
        silu = g_c * sg
        dsilu = sg * (1.0 + g_c * (1.0 - sg))
        doc_ = doc_ref[...]
        dconv = doc_ * c_b * silu
        gp = pv[15:16, 3 * CW:4 * CW].astype(F32)
        gn = nx[0:1, 3 * CW:4 * CW].astype(F32)
        dconv_p = jnp.where(first, 0.0, dcp_ref[15:16, :] * pv[15:16, 0:CW].astype(F32) * (gp * _sigmoid(gp)))
        dconv_n = jnp.where(last, 0.0, dcn_ref[0:1, :] * nx[0:1, 0:CW].astype(F32) * (gn * _sigmoid(gn)))
        row = lax.broadcasted_iota(jnp.int32, (t, CW), 0)
        d_m1 = jnp.where(row == 0, dconv_p, pltpu.roll(dconv, 1, 0))
        d_p1 = jnp.where(row == t - 1, dconv_n, pltpu.roll(dconv, t - 1, 0))
        dz = cw_[0:1] * d_p1 + cw_[1:2] * dconv + cw_[2:3] * d_m1
        dcv_ref[:, 0:CW] = (doc_ * conv * silu).astype(BF16)
        dcv_ref[:, CW:2 * CW] = (dz * c_u).astype(BF16)
        dcv_ref[:, 2 * CW:3 * CW] = (dz * c_c).astype(BF16)
        dcv_ref[:, 3 * CW:4 * CW] = (doc_ * c_b * conv * dsilu).astype(BF16)
        dcw_ref[0:1, :] += jnp.sum(dconv * z_m1, axis=0, keepdims=True)
        dcw_ref[1:2, :] += jnp.sum(dconv * z, axis=0, keepdims=True)
        dcw_ref[2:3, :] += jnp.sum(dconv * z_p1, axis=0, keepdims=True)
        dcb_ref[...] += jnp.sum(dconv, axis=0, keepdims=True)
        ga = ga_ref[...].astype(F32)
        sg = _sigmoid(ga)
        doa_ = doa_ref[...]
        ocat = jnp.concatenate([o_ref[h].astype(F32) for h in range(H)], axis=-1)
        dga_ref[...] = (doa_ * ocat * (sg * (1.0 + ga * (1.0 - sg)))).astype(BF16)
        dog = doa_ * (ga * sg)
        zeros = jnp.zeros((t, HP - VD), F32)
        lane = lax.broadcasted_iota(jnp.int32, (t, LANES), 1)
        dmat = jnp.zeros((t, LANES), F32)
        for h in range(H):
            dh = dog[:, h * VD:(h + 1) * VD]
            do_ref[h] = jnp.concatenate([dh, zeros], axis=-1).astype(BF16)
            dmat = jnp.where(lane == h, jnp.sum(dh * ocat[:, h * VD:(h + 1) * VD], axis=-1, keepdims=True), dmat)
        dlt = dmat.T
        for h in range(H):
            dl_ref[h] = dlt[h:h + 1, :]
        mblk = mm_ref[...].astype(F32)
        gm = mblk[:, MW:]
        sg = _sigmoid(gm)
        dom_ = dom_ref[...]
        heads = _mem_attn(mblk[:, :MW], mqg_ref[...], mk_ref, mv_ref)
        om = jnp.concatenate([hh[4] for hh in heads], axis=-1)
        dmm_ref[:, MW:] = (dom_ * om * (sg * (1.0 + gm * (1.0 - sg)))).astype(BF16)
        dmo = dom_ * (gm * sg)
        dmqg = jnp.zeros((1, MHD), F32)
        for h in range(HM):
            mq16, mqxh, mqrs, p, _ = heads[h]
            dmo_h = dmo[:, h * MHD:(h + 1) * MHD].astype(BF16)
            dp = _dot_nt(dmo_h, mv_ref[h])
            ds = (p * (dp - jnp.sum(dp * p, axis=-1, keepdims=True)) * MEM_SCALE).astype(BF16)
            dmq = _dot(ds, mk_ref[h])
            dmk_ref[h] += _dot_tn(ds, mq16)
            dmv_ref[h] += _dot_tn(p.astype(BF16), dmo_h)
            dq, dg = _rms_bwd(dmq, mqxh, mqrs, mqg_ref[...], MHD)
            dmm_ref[:, h * MHD:(h + 1) * MHD] = dq.astype(BF16)
            dmqg = dmqg + dg
        dmqg_ref[...] += dmqg

    pv, nx = _halo_specs(t, s, 4 * CW)
    dpv, dnx = _halo_specs(t, s, CW)
    row = lambda w: pl.BlockSpec((t, w), lambda i: (i, 0))
    hs = lambda w: pl.BlockSpec((H, t, w), lambda i: (0, i, 0))
    return _call(
        body, name=name, grid=(nt,),
        in_specs=[pl.BlockSpec((t, 4 * CW), lambda i: (i, 0)), pv, nx,
                  pl.BlockSpec((t, 2 * MW), lambda i: (i, P_MEM // (2 * MW))),
                  pl.BlockSpec((t, AW), lambda i: (i, P_GA // AW)),
                  hs(VD), _full(mk.shape), _full(mv.shape), _full((3, CW)), _full((1, CW)), _full((1, MHD)),
                  row(512), row(512), dpv, dnx, row(512)],
        out_specs=[row(4 * CW), row(2 * MW), row(AW), hs(HP), pl.BlockSpec((H, 1, t), lambda i: (0, 0, i)),
                   _full((3, CW)), _full((1, CW)),
                   _full((HM, m, MHD)), _full((HM, m, MHD)), _full((1, MHD))],
        out_shape=[jax.ShapeDtypeStruct((s, 4 * CW), BF16), jax.ShapeDtypeStruct((s, 2 * MW), BF16),
                   jax.ShapeDtypeStruct((s, AW), BF16), jax.ShapeDtypeStruct((H, s, HP), BF16),
                   jax.ShapeDtypeStruct((H, 1, s), F32), jax.ShapeDtypeStruct((3, CW), F32),
                   jax.ShapeDtypeStruct((1, CW), F32), jax.ShapeDtypeStruct((HM, m, MHD), F32),
                   jax.ShapeDtypeStruct((HM, m, MHD), F32), jax.ShapeDtypeStruct((1, MHD), F32)],
        compiler_params=_cp(("arbitrary",)),
    )(proj, proj, proj, proj, proj, o, mk, mv, cw, cb, mqg, doa, doc, doc, doc, dom)


def _flash_bwd(q, k, v, do, lse, dl, tq, tk, name):
    _, s, _ = q.shape
    tq, tk = min(tq, s), min(tk, s)
    nq, nkt, nc = s // tq, s // tk, tk // LANES
    unroll = 4 if nq % 4 == 0 else 1

    def body(q_ref, do_ref, lse_ref, dl_ref, k_ref, v_ref, dq_ref, dk_ref, dv_ref, dq_acc):
        j = pl.program_id(1)

        @pl.when(j == 0)
        def _():
            dq_acc[...] = jnp.zeros_like(dq_acc)

        kb, vb = k_ref[0], v_ref[0]

        def step(ii, carry):
            dkt, dvt = carry
            for u in range(unroll):
                i = ii * unroll + u
                off = pl.multiple_of(i * tq, tq)
                qb = q_ref[0, pl.ds(off, tq), :]
                dob = do_ref[0, pl.ds(off, tq), :]
                lse_b = jnp.broadcast_to(lse_ref[0, i], (LANES, tq)).T
                dl_b = jnp.broadcast_to(dl_ref[0, i], (LANES, tq)).T
                sc = _dot_nt(qb, kb)
                dp = _dot_nt(dob, vb)
                ps, dss = [], []
                for cc in range(nc):
                    p = jnp.exp2(sc[:, cc * LANES:(cc + 1) * LANES] - lse_b)
                    ps.append(p.astype(BF16))
                    dss.append((p * (dp[:, cc * LANES:(cc + 1) * LANES] - dl_b)).astype(BF16))
                p16, ds16 = jnp.concatenate(ps, axis=-1), jnp.concatenate(dss, axis=-1)
                dvt = dvt + _dot_tn(dob, p16)
                dkt = dkt + _dot_tn(qb, ds16)
                dq_acc[pl.ds(off, tq), :] += _dot(ds16, kb)
            return dkt, dvt

        dkt, dvt = lax.fori_loop(0, nq // unroll, step, (jnp.zeros((HP, tk), F32), jnp.zeros((HP, tk), F32)))
        dk_ref[0] = (dkt.T * (1.0 / LOG2E)).astype(BF16)
        dv_ref[0] = dvt.T.astype(BF16)

        @pl.when(j == nkt - 1)
        def _():
            dq_ref[0] = (dq_acc[...] * ATT_SCALE).astype(BF16)

    whole = pl.BlockSpec((1, s, HP), lambda h, j: (h, 0, 0))
    stat = pl.BlockSpec((1, nq, 1, tq), lambda h, j: (h, 0, 0, 0))
    tile = pl.BlockSpec((1, tk, HP), lambda h, j: (h, j, 0))
    return _call(
        body, name=name, grid=(H, nkt),
        in_specs=[whole, whole, stat, stat, tile, tile],
        out_specs=[whole, tile, tile],
        out_shape=[jax.ShapeDtypeStruct((H, s, HP), BF16)] * 3,
        scratch_shapes=[pltpu.VMEM((s, HP), F32)],
        compiler_params=_cp(("arbitrary", "arbitrary")),
    )(q, do, lse, dl, k, v)


def _mla_prep_bwd(proj_l, c, sn, qng, kvng, qhg, khg, wuq, wukv, wuqt, wukvt, dq, dk, dv, t, name):
    s = proj_l.shape[0]
    t = min(t, s)

    def body(l_ref, c_ref, sn_ref, qng_ref, kvng_ref, qhg_ref, khg_ref, wuq_ref, wukv_ref, wuqt_ref, wukvt_ref,
             dq_ref, dk_ref, dv_ref, dl_ref, dwuq_ref, dwukv_ref, dqng_ref, dkvng_ref, dqhg_ref, dkhg_ref):
        @pl.when(pl.program_id(0) == 0)
        def _():
            for r in (dwuq_ref, dwukv_ref, dqng_ref, dkvng_ref, dqhg_ref, dkhg_ref):
                r[...] = jnp.zeros_like(r)

        cc, ss = c_ref[...], sn_ref[...]
        qhg, khg = qhg_ref[...], khg_ref[...]
        (_, _, kpe, qxh, qrs, kvxh, kvrs, qn16, kvn16, qp, kvp) = _mla_heads(
            l_ref[...], cc, ss, qng_ref[...], kvng_ref[...], qhg, khg, wuq_ref[...], wukv_ref[...])
        lane = lax.broadcasted_iota(jnp.int32, (t, HP), 1)
        dqp, dkp, dvp = [], [], []
        dkpe = jnp.zeros((t, HP), F32)
        dqhg = jnp.zeros((1, HP), F32)
        dkhg = jnp.zeros((1, HP), F32)
        for h in range(H):
            _, xh, rs = _rms(qp[:, h * HP:(h + 1) * HP], qhg, QKD)
            du, dg = _rms_bwd(_rope_adj(dq_ref[h].astype(F32), cc, ss), xh, rs, qhg, QKD)
            dqp.append(du)
            dqhg = dqhg + dg
            _, xh, rs = _rms(kvp[:, h * HP:(h + 1) * HP] + kpe, khg, QKD)
            du, dg = _rms_bwd(_rope_adj(dk_ref[h].astype(F32), cc, ss), xh, rs, khg, QKD)
            dkp.append(du)
            dkhg = dkhg + dg
            dkpe = dkpe + jnp.where((lane >= NOPE) & (lane < QKD), du, 0.0)
            dvp.append(dv_ref[h].astype(F32))
        dqhg_ref[...] += dqhg
        dkhg_ref[...] += dkhg
        dqp16 = jnp.concatenate(dqp, axis=-1).astype(BF16)
        dkvp16 = jnp.concatenate(dkp + dvp, axis=-1).astype(BF16)
        dwuq_ref[...] += _dot_tn(qn16, dqp16)
        dwukv_ref[...] += _dot_tn(kvn16, dkvp16)
        dql, dg = _rms_bwd(_dot(dqp16, wuqt_ref[...]), qxh, qrs, qng_ref[...], QL)
        dqng_ref[...] += dg
        dkvl, dg = _rms_bwd(_dot(dkvp16, wukvt_ref[...]), kvxh, kvrs, kvng_ref[...], KVL)
        dkvng_ref[...] += dg
        dl_ref[:, 0:QL] = dql.astype(BF16)
        dl_ref[:, QL:QL + KVL] = dkvl.astype(BF16)
        dl_ref[:, QL + KVL:P_LW] = dkpe.astype(BF16)

    hs = pl.BlockSpec((H, t, HP), lambda i: (0, i, 0))
    row = lambda w: pl.BlockSpec((t, w), lambda i: (i, 0))
    return _call(
        body, name=name, grid=(s // t,),
        in_specs=[row(P_LW), row(HP), row(HP), _full((1, QL)), _full((1, KVL)), _full((1, HP)), _full((1, HP)),
                  _full((QL, H * HP)), _full((KVL, 2 * H * HP)), _full((H * HP, QL)), _full((2 * H * HP, KVL)),
                  hs, hs, hs],
        out_specs=[row(P_LW), _full((QL, H * HP)), _full((KVL, 2 * H * HP)), _full((1, QL)), _full((1, KVL)),
                   _full((1, HP)), _full((1, HP))],
        out_shape=[jax.ShapeDtypeStruct((s, P_LW), BF16), jax.ShapeDtypeStruct((QL, H * HP), F32),
                   jax.ShapeDtypeStruct((KVL, 2 * H * HP), F32), jax.ShapeDtypeStruct((1, QL), F32),
                   jax.ShapeDtypeStruct((1, KVL), F32), jax.ShapeDtypeStruct((1, HP), F32),
                   jax.ShapeDtypeStruct((1, HP), F32)],
        compiler_params=_cp(("arbitrary",)),
    )(proj_l, c, sn, qng, kvng, qhg, khg, wuq, wukv, wuqt, wukvt, dq, dk, dv)


def _mem_prep_bwd(mem, mng, wmkv, wmkvt, mkg, dmk, dmv, name):
    m = mem.shape[0]

    def body(mem_ref, mng_ref, w_ref, wt_ref, mkg_ref, dmk_ref, dmv_ref, dw_ref, dmng_ref, dmkg_ref):
        mn, xh, _ = _rms(mem_ref[...], mng_ref[...], D)
        mn16 = mn.astype(BF16)
        mkv = _dot(mn16, w_ref[...])
        parts = []
        dmkg = jnp.zeros((1, MHD), F32)
        for h in range(HM):
            _, kxh, krs = _rms(mkv[:, 2 * h * MHD:(2 * h + 1) * MHD], mkg_ref[...], MHD)
            du, dg = _rms_bwd(dmk_ref[h], kxh, krs, mkg_ref[...], MHD)
            dmkg = dmkg + dg
            parts += [du, dmv_ref[h]]
        dmkv = jnp.concatenate(parts, axis=-1).astype(BF16)
        dw_ref[...] = _dot_tn(mn16, dmkv)
        dmn = _dot(dmkv, wt_ref[...])
        dmng_ref[...] = jnp.sum(dmn * xh, axis=0, keepdims=True)
        dmkg_ref[...] = dmkg

    return _call(
        body, name=name,
        in_specs=[_full((m, D)), _full((1, D)), _full((D, 2 * MW)), _full((2 * MW, D)), _full((1, MHD)),
                  _full((HM, m, MHD)), _full((HM, m, MHD))],
        out_specs=[_full((D, 2 * MW)), _full((1, D)), _full((1, MHD))],
        out_shape=[jax.ShapeDtypeStruct((D, 2 * MW), F32), jax.ShapeDtypeStruct((1, D), F32),
                   jax.ShapeDtypeStruct((1, MHD), F32)],
        compiler_params=_cp(),
    )(mem, mng, wmkv, wmkvt, mkg, dmk, dmv)


def _rms_in_bwd(x, g_out, dh, ng, t, name):
    s = x.shape[0]
    t = min(t, s)

    def body(x_ref, go_ref, dh_ref, ng_ref, dx_ref, dng_ref):
        @pl.when(pl.program_id(0) == 0)
        def _():
            dng_ref[...] = jnp.zeros_like(dng_ref)

        _, xh, rs = _rms(x_ref[...], ng_ref[...], D)
        dx, dg = _rms_bwd(dh_ref[...], xh, rs, ng_ref[...], D)
        dx_ref[...] = go_ref[...] + dx
        dng_ref[...] += dg

    row = pl.BlockSpec((t, D), lambda i: (i, 0))
    return _call(
        body, name=name, grid=(s // t,),
        in_specs=[row, row, row, _full((1, D))], out_specs=[row, _full((1, D))],
        out_shape=[jax.ShapeDtypeStruct((s, D), F32), jax.ShapeDtypeStruct((1, D), F32)],
        compiler_params=_cp(("arbitrary",)),
    )(x, g_out, dh, ng)


def _allgather(arrs, name):
    n = len(arrs)

    def body(*refs):
        x_refs, out_refs = refs[:n], refs[n:2 * n]
        send_sems, recv_sems, local_sems = refs[2 * n:]
        x, y, c = lax.axis_index("x"), lax.axis_index("y"), lax.axis_index("c")
        me, sibling = (x, y, c), (x, y, 1 - c)
        chips = [(1 - x, y), (x, 1 - y), (1 - x, 1 - y)]

        def slot(a, px, py, pc):
            return out_refs[a].at[4 * px + 2 * py + pc]

        def copy(a, k, block, to, src=None):
            return pltpu.make_async_remote_copy(
                src_ref=slot(a, *block) if src is None else src, dst_ref=slot(a, *block),
                send_sem=send_sems.at[a, k], recv_sem=recv_sems.at[a, k],
                device_id=to, device_id_type=pl.DeviceIdType.MESH)

        mine = [pltpu.make_async_copy(x_refs[a], slot(a, *me), local_sems.at[a]) for a in range(n)]
        first, passed = [], []
        for a in range(n):
            mine[a].start()
            first.append(copy(a, 0, me, sibling, src=x_refs[a]))
            first += [copy(a, 1 + j, me, (*chip, c), src=x_refs[a]) for j, chip in enumerate(chips)]
        for cp in first:
            cp.start()
        for j, chip in enumerate(chips):
            for a in range(n):
                copy(a, 1 + j, (*chip, c), me).wait_recv()
                passed.append(copy(a, 4 + j, (*chip, c), sibling))
                passed[-1].start()
        for a in range(n):
            copy(a, 0, sibling, me).wait_recv()
        for j, chip in enumerate(chips):
            for a in range(n):
                copy(a, 4 + j, (*chip, 1 - c), me).wait_recv()
        for cp in first + passed:
            cp.wait_send()
        for cp in mine:
            cp.wait()

    any_spec = pl.BlockSpec(memory_space=pl.ANY)
    return _call(
        body, name=name,
        in_specs=[any_spec] * n, out_specs=[any_spec] * n,
        out_shape=[jax.ShapeDtypeStruct((N_DEV,) + a.shape, a.dtype) for a in arrs],
        scratch_shapes=[pltpu.SemaphoreType.DMA((n, 7)), pltpu.SemaphoreType.DMA((n, 7)),
                        pltpu.SemaphoreType.DMA((n,))],
    )(*arrs)


def _alltoall(arrs, name):
    n = len(arrs)

    def body(*refs):
        x_refs, out_refs = refs[:n], refs[n:2 * n]
        send_sems, recv_sems, local_sems = refs[2 * n:]
        x, y, c = lax.axis_index("x"), lax.axis_index("y"), lax.axis_index("c")
        me = 4 * x + 2 * y + c
        mine = [pltpu.make_async_copy(x_refs[a].at[me], out_refs[a].at[me], local_sems.at[a]) for a in range(n)]
        for cp in mine:
            cp.start()
        copies = []
        for k in range(1, N_DEV):
            bx, by, bc = (k >> 2) & 1, (k >> 1) & 1, k & 1
            px = 1 - x if bx else x
            py = 1 - y if by else y
            pc = 1 - c if bc else c
            for a in range(n):
                copies.append(pltpu.make_async_remote_copy(
                    src_ref=x_refs[a].at[4 * px + 2 * py + pc], dst_ref=out_refs[a].at[me],
                    send_sem=send_sems.at[a, k - 1], recv_sem=recv_sems.at[a, k - 1],
                    device_id=(px, py, pc), device_id_type=pl.DeviceIdType.MESH))
        for cp in copies:
            cp.start()
        for cp in copies:
            cp.wait_recv()
        for cp in copies:
            cp.wait_send()
        for cp in mine:
            cp.wait()

    any_spec = pl.BlockSpec(memory_space=pl.ANY)
    return _call(
        body, name=name,
        in_specs=[any_spec] * n, out_specs=[any_spec] * n,
        out_shape=[jax.ShapeDtypeStruct(a.shape, a.dtype) for a in arrs],
        scratch_shapes=[pltpu.SemaphoreType.DMA((n, 7)), pltpu.SemaphoreType.DMA((n, 7)),
                        pltpu.SemaphoreType.DMA((n,))],
    )(*arrs)


ADAMW_BLOCK_BYTES = 4 * 1024 * 1024


def _adamw(parts, w, m, v, name):
    r, c_ = w.shape
    cpad = -(-c_ // LANES) * LANES
    tr = r
    while N_DEV * tr * cpad * 4 > ADAMW_BLOCK_BYTES and tr % 16 == 0:
        tr //= 2
    c1 = 1.0 / (1.0 - ADAM_B1 ** ADAM_STEP)
    c2 = 1.0 / (1.0 - ADAM_B2 ** ADAM_STEP)

    def body(p_ref, w_ref, m_ref, v_ref, g_ref, d_ref, nm_ref, nv_ref):
        g = p_ref[0].astype(F32)
        for j in range(1, N_DEV):
            g = g + p_ref[j].astype(F32)
        nm = ADAM_B1 * m_ref[...] + (1.0 - ADAM_B1) * g
        nv = ADAM_B2 * v_ref[...] + (1.0 - ADAM_B2) * (g * g)
        g_ref[...] = g
        nm_ref[...] = nm
        nv_ref[...] = nv
        d_ref[...] = -ADAM_LR * ((nm * c1) / (jnp.sqrt(nv * c2) + ADAM_EPS) + ADAM_WD * w_ref[...])

    row = pl.BlockSpec((tr, c_), lambda i: (i, 0))
    return _call(
        body, name=name, grid=(r // tr,),
        in_specs=[pl.BlockSpec((N_DEV, tr, c_), lambda i: (0, i, 0)), row, row, row],
        out_specs=[row] * 4, out_shape=[jax.ShapeDtypeStruct((r, c_), F32)] * 4,
        compiler_params=_cp(("parallel",)),
    )(parts, w, m, v)


def _to_rows(flat, align=8):
    n = flat.shape[-1]
    rows = -(-n // (LANES * align)) * align
    return jnp.pad(flat, (0, rows * LANES - n)).reshape(rows, LANES)


def _shard_blocks(full, axis):
    r, c_ = full.shape
    if axis == 0:
        return full.reshape(N_DEV, r // N_DEV, c_)
    return full.reshape(r, N_DEV, c_ // N_DEV).transpose(1, 0, 2)


def _unshard_blocks(blocks, axis):
    _, r, c_ = blocks.shape
    if axis == 0:
        return blocks.reshape(N_DEV * r, c_)
    return blocks.transpose(1, 0, 2).reshape(r, N_DEV * c_)


def _pad_heads(w, width):
    k = w.shape[0]
    return jnp.pad(w.reshape(k, H, width), ((0, 0), (0, 0), (0, HP - width))).reshape(k, H * HP)


def _unpad_heads(w, width):
    k = w.shape[0]
    return w.reshape(k, H, HP)[:, :, :width].reshape(k, H * width)


IN_SPLIT = {'q_lat': (0, 384), 'kv_lat': (384, 640), 'k_pe': (640, 672), 'c_b': (672, 1184), 'c_c': (1184, 1696),
            'c_u': (1696, 2208), 'q_mem': (2208, 2720), 'g_attn': (2720, 3232), 'g_conv': (3232, 3744),
            'g_mem': (3744, 4256), 'r': (4256, 7328)}
P_ORDER = ['c_b', 'c_c', 'c_u', 'g_conv', 'q_mem', 'g_mem', 'r', 'g_attn', 'q_lat', 'kv_lat']


def _permute_w_in(w):
    k = w.shape[0]
    cols = [w[:, IN_SPLIT[n][0]:IN_SPLIT[n][1]] for n in P_ORDER]
    kpe = w[:, IN_SPLIT['k_pe'][0]:IN_SPLIT['k_pe'][1]]
    cols += [jnp.zeros((k, NOPE), w.dtype), kpe, jnp.zeros((k, HP - QKD), w.dtype)]
    return jnp.concatenate(cols, axis=1)


def _unpermute_w_in(wp):
    off, pos = 0, {}
    for n in P_ORDER:
        wd = IN_SPLIT[n][1] - IN_SPLIT[n][0]
        pos[n] = (off, off + wd)
        off += wd
    pos['k_pe'] = (off + NOPE, off + QKD)
    order = sorted(IN_SPLIT, key=lambda n: IN_SPLIT[n][0])
    return jnp.concatenate([wp[:, pos[n][0]:pos[n][1]] for n in order], axis=1)


def _layer_weights(full, l):
    w = {}
    w_in_p = _permute_w_in(full['w_in'][l])
    w['w_main'] = w_in_p[:, :P_MAIN]
    w['w_l'] = w_in_p[:, P_MAIN:]
    w['w_in_t'] = jnp.pad(w_in_p, ((0, 0), (0, P_WPAD - P_W))).T
    w['w_uq'] = _pad_heads(full['w_uq'][l], QKD)
    wukv = full['w_ukv'][l].reshape(KVL, H, 2, NOPE)
    kpart = jnp.pad(wukv[:, :, 0, :], ((0, 0), (0, 0), (0, HP - NOPE))).reshape(KVL, H * HP)
    vpart = jnp.pad(wukv[:, :, 1, :], ((0, 0), (0, 0), (0, HP - VD))).reshape(KVL, H * HP)
    w['w_ukv'] = jnp.concatenate([kpart, vpart], axis=1)
    w['w_uq_t'] = w['w_uq'].T
    w['w_ukv_t'] = w['w_ukv'].T
    w['w_mkv'] = full['w_mkv'][l]
    w['w_mkv_t'] = w['w_mkv'].T
    for n in ('w_br_attn', 'w_br_conv', 'w_br_mem', 'w_out'):
        w[n] = full[n][l]
        w[n + '_t'] = w[n].T
    return w


def kernel(x, mem, positions, norm_g, w_in, b_gate, q_norm_g, w_uq, kv_norm_g, w_ukv, q_head_g, k_head_g, conv_w, conv_b, mem_norm_g, w_mkv, mem_q_g, mem_k_g, w_br_attn, w_br_conv, w_br_mem, w_out, loss_target, m_norm_g, m_w_in, m_b_gate, m_q_norm_g, m_w_uq, m_kv_norm_g, m_w_ukv, m_q_head_g, m_k_head_g, m_conv_w, m_conv_b, m_mem_norm_g, m_w_mkv, m_mem_q_g, m_mem_k_g, m_w_br_attn, m_w_br_conv, m_w_br_mem, m_w_out, v_norm_g, v_w_in, v_b_gate, v_q_norm_g, v_w_uq, v_kv_norm_g, v_w_ukv, v_q_head_g, v_k_head_g, v_conv_w, v_conv_b, v_mem_norm_g, v_w_mkv, v_mem_q_g, v_mem_k_g, v_w_br_attn, v_w_br_conv, v_w_br_mem, v_w_out):
    a = dict(zip(INPUTS, (x, mem, positions, norm_g, w_in, b_gate, q_norm_g, w_uq, kv_norm_g, w_ukv, q_head_g, k_head_g, conv_w, conv_b, mem_norm_g, w_mkv, mem_q_g, mem_k_g, w_br_attn, w_br_conv, w_br_mem, w_out, loss_target, m_norm_g, m_w_in, m_b_gate, m_q_norm_g, m_w_uq, m_kv_norm_g, m_w_ukv, m_q_head_g, m_k_head_g, m_conv_w, m_conv_b, m_mem_norm_g, m_w_mkv, m_mem_q_g, m_mem_k_g, m_w_br_attn, m_w_br_conv, m_w_br_mem, m_w_out, v_norm_g, v_w_in, v_b_gate, v_q_norm_g, v_w_uq, v_kv_norm_g, v_w_ukv, v_q_head_g, v_k_head_g, v_conv_w, v_conv_b, v_mem_norm_g, v_w_mkv, v_mem_q_g, v_mem_k_g, v_w_br_attn, v_w_br_conv, v_w_br_mem, v_w_out)))
    x = a['x'][0]
    mem = a['mem'][0]
    tgt = a['loss_target'][0]
    s = x.shape[0]
    t_el = 256
    t_br = 512
    tq_f, tk_f, tq_b, tk_b = 256, 1024, 512, 512

    gathered = _allgather([a[n].astype(BF16) for n in BIG_ORDER] + [a['conv_w']], "ag_weights")
    full = {n: [_unshard_blocks(g8[:, l], BIG[n][1]) for l in range(DEPTH)] for n, g8 in zip(BIG_ORDER, gathered)}
    conv_w = gathered[-1].transpose(1, 2, 0, 3).reshape(DEPTH, 3, CW)

    inv_freq = ROPE_BASE ** (-jnp.arange(0, RP, 2, dtype=F32) / RP)
    ang = a['positions'][0].astype(F32)[:, None] * inv_freq
    cos, sin = jnp.cos(ang), jnp.sin(ang)
    rc = jnp.concatenate([jnp.ones((s, NOPE), F32), cos, cos, jnp.ones((s, HP - QKD), F32)], axis=1)
    rs = jnp.concatenate([jnp.zeros((s, NOPE), F32), -sin, sin, jnp.zeros((s, HP - QKD), F32)], axis=1)

    def small(n, l, width=None):
        v = a[n][l][None, :]
        return v if width is None else jnp.pad(v, ((0, 0), (0, width - v.shape[1])))

    saved = []
    layer_w = [_layer_weights(full, l) for l in range(DEPTH)]
    for l in range(DEPTH):
        w = layer_w[l]
        tag = ""
        h, ht = _rms_h(x, small('norm_g', l), 512, "rms_h" + tag)
        proj = _mm(h, w['w_main'], BF16, 2048, 512, "in_proj" + tag)
        proj_l = _mm(h, w['w_l'], BF16, 512, P_LW, "in_proj_lat" + tag)
        qng, kvng = small('q_norm_g', l), small('kv_norm_g', l)
        qhg, khg = small('q_head_g', l, HP), small('k_head_g', l, HP)
        q, k, v = _mla_prep(proj_l, rc, rs, qng, kvng, qhg, khg, w['w_uq'], w['w_ukv'], t_br, "mla_prep" + tag)
        o, lse = _flash_fwd(q, k, v, tq_f, tk_f, "flash_fwd" + tag)
        mk, mv = _mem_prep(mem, small('mem_norm_g', l), w['w_mkv'], small('mem_k_g', l), "mem_prep" + tag)
        cb, mqg = small('conv_b', l), small('mem_q_g', l)
        oa, oc, om = _branches(proj, o, mk, mv, conv_w[l], cb, mqg, t_br, "branches" + tag)
        x_out, aa, ac, am = _merge_fwd(x, proj, oa, oc, om, small('b_gate', l), w['w_br_attn'], w['w_br_conv'],
                                       w['w_br_mem'], w['w_out'], t_el, "merge" + tag)
        saved.append(dict(x=x, ht=ht, proj=proj, proj_l=proj_l, q=q, k=k, v=v, o=o, lse=lse, mk=mk, mv=mv,
                          oa=oa, oc=oc, om=om, aa=aa, ac=ac, am=am))
        x = x_out

    g, loss_parts = _loss_grad(x, tgt, 512, "loss")
    loss = lax.psum(jnp.sum(loss_parts), ("x", "y", "c"))

    gw = {n: [None] * DEPTH for n in WEIGHTS}
    for l in reversed(range(DEPTH)):
        w = layer_w[l]
        sv = saved[l]
        tag = ""
        tqb = min(tq_b, s)
        y, daa, dac, dam, dr, doa, doc, dom, dbg = _merge_bwd(
            g, sv['proj'], sv['aa'], sv['ac'], sv['am'], small('b_gate', l), w['w_out_t'], w['w_br_attn_t'],
            w['w_br_conv_t'], w['w_br_mem_t'], t_el, "merge_bwd" + tag)
        gw['b_gate'][l] = dbg[0]
        gw['w_out'][l] = _tn(y, g, 512, 512, "dw_out" + tag)
        gw['w_br_attn'][l] = _tn(sv['oa'], daa, 512, 512, "dw_attn" + tag)
        gw['w_br_conv'][l] = _tn(sv['oc'], dac, 512, 512, "dw_conv" + tag)
        gw['w_br_mem'][l] = _tn(sv['om'], dam, 512, 512, "dw_mem" + tag)
        cb, mqg = small('conv_b', l), small('mem_q_g', l)
        dcv, dmm, dga, do, dl, dcw, dcb, dmk, dmv, dmqg = _branches_bwd(
            sv['proj'], sv['o'], sv['mk'], sv['mv'], conv_w[l], cb, mqg, doa, doc, dom, t_br, "branches_bwd" + tag)
        gw['conv_w'][l], gw['conv_b'][l], gw['mem_q_g'][l] = dcw, dcb[0], dmqg[0]
        dwm, dmng, dmkg = _mem_prep_bwd(mem, small('mem_norm_g', l), w['w_mkv'], w['w_mkv_t'], small('mem_k_g', l),
                                        dmk, dmv, "mem_prep_bwd" + tag)
        gw['w_mkv'][l], gw['mem_norm_g'][l], gw['mem_k_g'][l] = dwm, dmng[0], dmkg[0]
        lse_r = sv['lse'].reshape(H, s // tqb, 1, tqb)
        dl_r = dl.reshape(H, s // tqb, 1, tqb)
        dq, dk, dv = _flash_bwd(sv['q'], sv['k'], sv['v'], do, lse_r, dl_r, tq_b, tk_b, "flash_bwd" + tag)
        qng, kvng = small('q_norm_g', l), small('kv_norm_g', l)
        qhg, khg = small('q_head_g', l, HP), small('k_head_g', l, HP)
        dlat, dwuq, dwukv, dqng, dkvng, dqhg, dkhg = _mla_prep_bwd(
            sv['proj_l'], rc, rs, qng, kvng, qhg, khg, w['w_uq'], w['w_ukv'], w['w_uq_t'], w['w_ukv_t'],
            dq, dk, dv, t_br, "mla_prep_bwd" + tag)
        gw['w_uq'][l] = _unpad_heads(dwuq, QKD)
        dwukv = dwukv.reshape(KVL, 2, H, HP)[:, :, :, :NOPE]
        gw['w_ukv'][l] = dwukv.transpose(0, 2, 1, 3).reshape(KVL, H * 2 * NOPE)
        gw['q_norm_g'][l], gw['kv_norm_g'][l] = dqng[0], dkvng[0]
        gw['q_head_g'][l], gw['k_head_g'][l] = dqhg[0, :QKD], dkhg[0, :QKD]
        dproj = jnp.concatenate([dcv, dmm, dr, dga, dlat, jnp.zeros((s, P_WPAD - P_W), BF16)], axis=1)
        dh = _mm(dproj, w['w_in_t'], F32, 512, 512, "d_h" + tag)
        gw['w_in'][l] = _unpermute_w_in(_mm_acc(sv['ht'], dproj, 1024, 1536, "dw_in" + tag))
        g, dng = _rms_in_bwd(sv['x'], g, dh, small('norm_g', l), 512, "rms_bwd" + tag)
        gw['norm_g'][l] = dng[0]
    grad_x = g[None]

    sharded = BIG_ORDER + ['conv_w']
    axis_of = lambda n: 1 if n == 'conv_w' else BIG[n][1]
    send = [jnp.stack([_shard_blocks(gw[n][l], axis_of(n)) for l in range(DEPTH)], axis=1)
            .astype(F32 if n == 'conv_w' else BF16) for n in sharded]
    parts_big = _alltoall(send, "rs_grads")
    small_flat = jnp.concatenate([jnp.stack(gw[n]).reshape(-1) for n in SMALL_ORDER])
    n_small = small_flat.shape[0]
    parts_small = _allgather([_to_rows(small_flat)], "ag_small_grads")[0]

    outs = [{} for _ in range(4)]
    for n, parts in zip(sharded, parts_big):
        loc = a[n].shape
        two_d = lambda t: t.reshape(loc[0] * loc[1], loc[2])
        res = _adamw(parts.reshape(N_DEV, loc[0] * loc[1], loc[2]), two_d(a[n]), two_d(a['m_' + n]),
                     two_d(a['v_' + n]), "adamw_" + n)
        for d, r in zip(outs, res):
            d[n] = r.reshape(loc)
    pks = lambda pre: _to_rows(jnp.concatenate([a[pre + n].reshape(-1) for n in SMALL_ORDER]))
    res_small = _adamw(parts_small, pks(''), pks('m_'), pks('v_'), "adamw_small")
    for d, rsm in zip(outs, res_small):
        flat = rsm.reshape(-1)[:n_small]
        off = 0
        for n in SMALL_ORDER:
            d[n] = flat[off:off + DEPTH * SMALL[n]].reshape(DEPTH, SMALL[n])
            off += DEPTH * SMALL[n]
    result = [loss, grad_x]
    for d in outs:
        result += [d[n] for n in WEIGHTS]
    return tuple(result)
```

```python
import functools

import jax
import jax.numpy as jnp
from jax import lax
from jax.experimental import pallas as pl
from jax.experimental.pallas import tpu as pltpu

F32, BF16 = jnp.float32, jnp.bfloat16

N_DEV = 8
DEPTH = 4
D = 1024
QL, KVL, RP = 384, 256, 32
H, NOPE, QKD, VD = 8, 64, 96, 64
HP = 128
CW, MW, AW = 512, 512, 512
HM, MHD = 4, 128
IN_WIDTH = 7328
EPS = 1e-6
ROPE_BASE = 10000.0
ATT_SCALE = QKD ** -0.5
MEM_SCALE = MHD ** -0.5
LOG2E = 1.4426950408889634
QSCALE = ATT_SCALE * LOG2E

ADAM_LR, ADAM_B1, ADAM_B2, ADAM_EPS, ADAM_WD, ADAM_STEP = 0.001, 0.9, 0.999, 1e-08, 0.01, 10

LANES = 128
VMEM_LIMIT = 56 * 1024 * 1024

P_CONV, P_MEM, P_R, P_GA, P_L = 0, 2048, 3072, 6144, 6656
P_MAIN = 6656
P_LW = 768
P_W = P_MAIN + P_LW
P_WPAD = 7680

WEIGHTS = ['norm_g', 'w_in', 'b_gate', 'q_norm_g', 'w_uq', 'kv_norm_g', 'w_ukv', 'q_head_g', 'k_head_g',
           'conv_w', 'conv_b', 'mem_norm_g', 'w_mkv', 'mem_q_g', 'mem_k_g', 'w_br_attn', 'w_br_conv',
           'w_br_mem', 'w_out']
INPUTS = ['x', 'mem', 'positions'] + WEIGHTS + ['loss_target'] + ['m_' + n for n in WEIGHTS] + ['v_' + n for n in WEIGHTS]

BIG = {'w_in': ((D, IN_WIDTH), 1), 'w_uq': ((QL, H * QKD), 1), 'w_ukv': ((KVL, H * 128), 1),
       'w_mkv': ((D, 2 * MW), 0), 'w_br_attn': ((AW, D), 1), 'w_br_conv': ((CW, D), 1),
       'w_br_mem': ((MW, D), 1), 'w_out': ((D, D), 0)}
BIG_ORDER = ['w_in', 'w_uq', 'w_ukv', 'w_mkv', 'w_br_attn', 'w_br_conv', 'w_br_mem', 'w_out']
CONVW_PAD = 256
SMALL = {'norm_g': D, 'b_gate': 3 * D, 'q_norm_g': QL, 'kv_norm_g': KVL, 'q_head_g': QKD, 'k_head_g': QKD,
         'conv_b': CW, 'mem_norm_g': D, 'mem_q_g': MHD, 'mem_k_g': MHD}
SMALL_ORDER = list(SMALL)
ROW_ALIGN = 1024


def _call(body, **kw):
    return pl.pallas_call(body, **kw)


def _cp(sem=None):
    return pltpu.CompilerParams(dimension_semantics=sem, vmem_limit_bytes=VMEM_LIMIT)


def _full(shape):
    n = len(shape)
    return pl.BlockSpec(shape, lambda *_: (0,) * n)


def _rms(x, g, n):
    rs = lax.rsqrt(jnp.sum(x * x, axis=-1, keepdims=True) * (1.0 / n) + EPS)
    xh = x * rs
    return xh * g, xh, rs


def _rms_bwd(dy, xh, rs, g, n):
    dxh = dy * g
    dx = rs * (dxh - xh * (jnp.sum(dxh * xh, axis=-1, keepdims=True) * (1.0 / n)))
    return dx, jnp.sum(dy * xh, axis=0, keepdims=True)


def _sigmoid(x):
    return 1.0 / (1.0 + jnp.exp(-x))


def _swap_rope(u):
    lane = lax.broadcasted_iota(jnp.int32, u.shape, 1)
    up = pltpu.roll(u, 16, 1)
    dn = pltpu.roll(u, 112, 1)
    return jnp.where((lane >= 64) & (lane < 80), dn, jnp.where((lane >= 80) & (lane < 96), up, 0.0))


def _rope(u, c, sn):
    return u * c + _swap_rope(u) * sn


def _rope_adj(d, c, sn):
    return d * c + _swap_rope(d * sn)


def _dot(a, b):
    return jnp.dot(a, b, preferred_element_type=F32)


def _dot_nt(a, b):
    return lax.dot_general(a, b, (((1,), (1,)), ((), ())), preferred_element_type=F32)


def _dot_tn(a, b):
    return lax.dot_general(a, b, (((0,), (0,)), ((), ())), preferred_element_type=F32)


def _mm(a, b, out_dtype, tm, tn, name):
    m, k = a.shape
    _, n = b.shape
    tm, tn = min(tm, m), min(tn, n)

    def body(a_ref, b_ref, o_ref):
        o_ref[...] = _dot(a_ref[...].astype(BF16), b_ref[...]).astype(o_ref.dtype)

    return _call(
        body, name=name, grid=(m // tm, n // tn),
        in_specs=[pl.BlockSpec((tm, k), lambda i, j: (i, 0)), pl.BlockSpec((k, tn), lambda i, j: (0, j))],
        out_specs=pl.BlockSpec((tm, tn), lambda i, j: (i, j)),
        out_shape=jax.ShapeDtypeStruct((m, n), out_dtype),
        compiler_params=_cp(("parallel", "arbitrary")),
    )(a, b)


def _tn(a, b, ts, tn, name):
    s, ka = a.shape
    _, n = b.shape
    ts, tn = min(ts, s), min(tn, n)

    def body(a_ref, b_ref, o_ref):
        @pl.when(pl.program_id(1) == 0)
        def _():
            o_ref[...] = jnp.zeros_like(o_ref)

        o_ref[...] += _dot_tn(a_ref[...].astype(BF16), b_ref[...].astype(BF16))

    return _call(
        body, name=name, grid=(n // tn, s // ts),
        in_specs=[pl.BlockSpec((ts, ka), lambda j, i: (i, 0)), pl.BlockSpec((ts, tn), lambda j, i: (i, j))],
        out_specs=pl.BlockSpec((ka, tn), lambda j, i: (0, j)),
        out_shape=jax.ShapeDtypeStruct((ka, n), F32),
        compiler_params=_cp(("parallel", "arbitrary")),
    )(a, b)


def _mm_pieces(pieces, b, tm, tn, name):
    n_p = len(pieces)
    m = pieces[0].shape[0]
    k, n = b.shape
    tm, tn = min(tm, m), min(tn, n)
    offs = [sum(p.shape[1] for p in pieces[:i]) for i in range(n_p)]

    def body(*refs):
        a_refs, b_ref, o_ref = refs[:n_p], refs[n_p], refs[n_p + 1]
        acc = None
        for a_ref, off in zip(a_refs, offs):
            d = _dot(a_ref[...], b_ref[off:off + a_ref.shape[1], :])
            acc = d if acc is None else acc + d
        o_ref[...] = acc

    return _call(
        body, name=name, grid=(m // tm, n // tn),
        in_specs=[pl.BlockSpec((tm, p.shape[1]), lambda i, j: (i, 0)) for p in pieces]
        + [pl.BlockSpec((k, tn), lambda i, j: (0, j))],
        out_specs=pl.BlockSpec((tm, tn), lambda i, j: (i, j)),
        out_shape=jax.ShapeDtypeStruct((m, n), F32),
        compiler_params=_cp(("parallel", "arbitrary")),
    )(*pieces, b)


def _mm_acc(a, b, tk, tn, name):
    m, k = a.shape
    _, n = b.shape
    tk, tn = min(tk, k), min(tn, n)

    def body(a_ref, b_ref, o_ref):
        @pl.when(pl.program_id(1) == 0)
        def _():
            o_ref[...] = jnp.zeros_like(o_ref)

        o_ref[...] += _dot(a_ref[...], b_ref[...])

    return _call(
        body, name=name, grid=(n // tn, k // tk),
        in_specs=[pl.BlockSpec((m, tk), lambda j, i: (0, i)), pl.BlockSpec((tk, tn), lambda j, i: (i, j))],
        out_specs=pl.BlockSpec((m, tn), lambda j, i: (0, j)),
        out_shape=jax.ShapeDtypeStruct((m, n), F32),
        compiler_params=_cp(("parallel", "arbitrary")),
    )(a, b)


def _rms_h(x, g, t, name):
    s = x.shape[0]
    t = min(t, s)

    def body(x_ref, g_ref, h_ref, ht_ref):
        h = _rms(x_ref[...], g_ref[...], D)[0]
        h_ref[...] = h.astype(BF16)
        ht_ref[...] = h.T.astype(BF16)

    return _call(
        body, name=name, grid=(s // t,),
        in_specs=[pl.BlockSpec((t, D), lambda i: (i, 0)), _full((1, D))],
        out_specs=[pl.BlockSpec((t, D), lambda i: (i, 0)), pl.BlockSpec((D, t), lambda i: (0, i))],
        out_shape=[jax.ShapeDtypeStruct((s, D), BF16), jax.ShapeDtypeStruct((D, s), BF16)],
        compiler_params=_cp(("parallel",)),
    )(x, g)


def _mla_heads(pl_blk, c, sn, qng, kvng, qhg, khg, wuq, wukv):
    ql = pl_blk[:, 0:QL].astype(F32)
    kvl = pl_blk[:, QL:QL + KVL].astype(F32)
    kpe = pl_blk[:, QL + KVL:P_LW].astype(F32)
    qn, qxh, qrs = _rms(ql, qng, QL)
    kvn, kvxh, kvrs = _rms(kvl, kvng, KVL)
    qn16, kvn16 = qn.astype(BF16), kvn.astype(BF16)
    qp = _dot(qn16, wuq)
    kvp = _dot(kvn16, wukv)
    return ql, kvl, kpe, qxh, qrs, kvxh, kvrs, qn16, kvn16, qp, kvp


def _mla_prep(proj_l, c, sn, qng, kvng, qhg, khg, wuq, wukv, t, name):
    s = proj_l.shape[0]
    t = min(t, s)

    def body(l_ref, c_ref, sn_ref, qng_ref, kvng_ref, qhg_ref, khg_ref, wuq_ref, wukv_ref, q_ref, k_ref, v_ref):
        cc, ss = c_ref[...], sn_ref[...]
        (_, _, kpe, _, _, _, _, _, _, qp, kvp) = _mla_heads(
            l_ref[...], cc, ss, qng_ref[...], kvng_ref[...], qhg_ref[...], khg_ref[...], wuq_ref[...], wukv_ref[...])
        lane = lax.broadcasted_iota(jnp.int32, cc.shape, 1)
        for h in range(H):
            u = qp[:, h * HP:(h + 1) * HP]
            q_ref[h] = (_rope(_rms(u, qhg_ref[...], QKD)[0], cc, ss) * QSCALE).astype(BF16)
            u = kvp[:, h * HP:(h + 1) * HP] + kpe
            k_ref[h] = _rope(_rms(u, khg_ref[...], QKD)[0], cc, ss).astype(BF16)
            v_ref[h] = jnp.where(lane == VD, 1.0, kvp[:, (H + h) * HP:(H + h + 1) * HP]).astype(BF16)

    hs = pl.BlockSpec((H, t, HP), lambda i: (0, i, 0))
    row = lambda w: pl.BlockSpec((t, w), lambda i: (i, 0))
    return _call(
        body, name=name, grid=(s // t,),
        in_specs=[row(P_LW), row(HP), row(HP), _full((1, QL)), _full((1, KVL)), _full((1, HP)), _full((1, HP)),
                  _full((QL, H * HP)), _full((KVL, 2 * H * HP))],
        out_specs=[hs, hs, hs],
        out_shape=[jax.ShapeDtypeStruct((H, s, HP), BF16)] * 3,
        compiler_params=_cp(("parallel",)),
    )(proj_l, c, sn, qng, kvng, qhg, khg, wuq, wukv)


def _flash_fwd(q, k, v, tq, tk, name):
    _, s, _ = q.shape
    tq, tk = min(tq, s), min(tk, s)
    nk, nc = s // tk, tk // LANES
    un = 8 if nk % 8 == 0 else 1

    def body(q_ref, k_ref, v_ref, o_ref, lse_ref, s_scr):
        qb = q_ref[0]

        def scores(jj, mx):
            for u in range(un):
                j = jj * un + u
                off = pl.multiple_of(j * tk, tk)
                sc = _dot_nt(qb, k_ref[0, pl.ds(off, tk), :])
                s_scr[j] = sc
                for cc in range(nc):
                    mx = jnp.maximum(mx, sc[:, cc * LANES:(cc + 1) * LANES])
            return mx

        mx = lax.fori_loop(0, nk // un, scores, jnp.full((tq, LANES), -jnp.inf, F32))
        m = jnp.max(mx, axis=-1, keepdims=True)
        mb = jnp.broadcast_to(m, (tq, LANES))

        def probs(jj, acc):
            for u in range(un):
                j = jj * un + u
                off = pl.multiple_of(j * tk, tk)
                sc = s_scr[j]
                ps = [jnp.exp2(sc[:, cc * LANES:(cc + 1) * LANES] - mb).astype(BF16) for cc in range(nc)]
                acc = acc + _dot(jnp.concatenate(ps, axis=-1), v_ref[0, pl.ds(off, tk), :])
            return acc

        acc = lax.fori_loop(0, nk // un, probs, jnp.zeros((tq, HP), F32))
        l = acc[:, VD:VD + 1]
        o_ref[0] = (acc[:, :VD] / l).astype(BF16)
        lse = jnp.broadcast_to(m + jnp.log(l) * LOG2E, (tq, LANES))
        lse_ref[0] = lse.T[0:1, :]

    return _call(
        body, name=name, grid=(H, s // tq),
        in_specs=[pl.BlockSpec((1, tq, HP), lambda h, i: (h, i, 0)),
                  pl.BlockSpec((1, s, HP), lambda h, i: (h, 0, 0)),
                  pl.BlockSpec((1, s, HP), lambda h, i: (h, 0, 0))],
        out_specs=[pl.BlockSpec((1, tq, VD), lambda h, i: (h, i, 0)), pl.BlockSpec((1, 1, tq), lambda h, i: (h, 0, i))],
        out_shape=[jax.ShapeDtypeStruct((H, s, VD), BF16), jax.ShapeDtypeStruct((H, 1, s), F32)],
        scratch_shapes=[pltpu.VMEM((nk, tq, tk), F32)],
        compiler_params=_cp(("parallel", "arbitrary")),
    )(q, k, v)


def _mem_prep(mem, mng, wmkv, mkg, name):
    m = mem.shape[0]

    def body(mem_ref, mng_ref, w_ref, mkg_ref, mk_ref, mv_ref):
        mn = _rms(mem_ref[...], mng_ref[...], D)[0].astype(BF16)
        mkv = _dot(mn, w_ref[...])
        for h in range(HM):
            mk_ref[h] = _rms(mkv[:, 2 * h * MHD:(2 * h + 1) * MHD], mkg_ref[...], MHD)[0].astype(BF16)
            mv_ref[h] = mkv[:, (2 * h + 1) * MHD:(2 * h + 2) * MHD].astype(BF16)

    return _call(
        body, name=name,
        in_specs=[_full((m, D)), _full((1, D)), _full((D, 2 * MW)), _full((1, MHD))],
        out_specs=[_full((HM, m, MHD))] * 2,
        out_shape=[jax.ShapeDtypeStruct((HM, m, MHD), BF16)] * 2,
        compiler_params=_cp(),
    )(mem, mng, wmkv, mkg)


def _conv_parts(cv, prev, nxt, first, last, cw, cb):
    t = cv.shape[0]
    c_b, c_c, c_u, g_c = (cv[:, i * CW:(i + 1) * CW].astype(F32) for i in range(4))
    z = c_c * c_u
    zp = jnp.where(first, 0.0, prev[15:16, CW:2 * CW].astype(F32) * prev[15:16, 2 * CW:3 * CW].astype(F32))
    zn = jnp.where(last, 0.0, nxt[0:1, CW:2 * CW].astype(F32) * nxt[0:1, 2 * CW:3 * CW].astype(F32))
    row = lax.broadcasted_iota(jnp.int32, (t, CW), 0)
    z_m1 = jnp.where(row == 0, zp, pltpu.roll(z, 1, 0))
    z_p1 = jnp.where(row == t - 1, zn, pltpu.roll(z, t - 1, 0))
    conv = cw[0:1] * z_m1 + cw[1:2] * z + cw[2:3] * z_p1 + cb
    return c_b, c_c, c_u, g_c, z, z_m1, z_p1, conv


def _mem_attn(qm, mqg, mk_ref, mv_ref):
    outs = []
    for h in range(HM):
        mq, mqxh, mqrs = _rms(qm[:, h * MHD:(h + 1) * MHD], mqg, MHD)
        mq16 = mq.astype(BF16)
        sc = _dot_nt(mq16, mk_ref[h]) * MEM_SCALE
        e = jnp.exp(sc - jnp.max(sc, axis=-1, keepdims=True))
        p = e / jnp.sum(e, axis=-1, keepdims=True)
        o = _dot(p.astype(BF16), mv_ref[h])
        outs.append((mq16, mqxh, mqrs, p, o))
    return outs


def _halo_specs(t, s, width):
    nb = s // 16
    prev = pl.BlockSpec((16, width), lambda i: (jnp.maximum(i * (t // 16) - 1, 0), 0))
    nxt = pl.BlockSpec((16, width), lambda i: (jnp.minimum((i + 1) * (t // 16), nb - 1), 0))
    return prev, nxt


def _branches(proj, o, mk, mv, cw, cb, mqg, t, name):
    s = proj.shape[0]
    t = min(t, s)
    nt = s // t

    def body(cv_ref, pv_ref, nx_ref, mm_ref, ga_ref, o_ref, mk_ref, mv_ref, cw_ref, cb_ref, mqg_ref,
             oa_ref, oc_ref, om_ref):
        i = pl.program_id(0)
        c_b, _, _, g_c, _, _, _, conv = _conv_parts(
            cv_ref[...], pv_ref[...], nx_ref[...], i == 0, i == nt - 1, cw_ref[...], cb_ref[...])
        oc_ref[...] = (c_b * conv * (g_c * _sigmoid(g_c))).astype(BF16)
        ga = ga_ref[...].astype(F32)
        ocat = jnp.concatenate([o_ref[h].astype(F32) for h in range(H)], axis=-1)
        oa_ref[...] = (ocat * (ga * _sigmoid(ga))).astype(BF16)
        mblk = mm_ref[...].astype(F32)
        gm = mblk[:, MW:]
        heads = _mem_attn(mblk[:, :MW], mqg_ref[...], mk_ref, mv_ref)
        om = jnp.concatenate([hh[4] for hh in heads], axis=-1)
        om_ref[...] = (om * (gm * _sigmoid(gm))).astype(BF16)

    pv, nx = _halo_specs(t, s, 4 * CW)
    out = pl.BlockSpec((t, 512), lambda i: (i, 0))
    return _call(
        body, name=name, grid=(nt,),
        in_specs=[pl.BlockSpec((t, 4 * CW), lambda i: (i, 0)), pv, nx,
                  pl.BlockSpec((t, 2 * MW), lambda i: (i, P_MEM // (2 * MW))),
                  pl.BlockSpec((t, AW), lambda i: (i, P_GA // AW)),
                  pl.BlockSpec((H, t, VD), lambda i: (0, i, 0)),
                  _full(mk.shape), _full(mv.shape), _full((3, CW)), _full((1, CW)), _full((1, MHD))],
        out_specs=[out, out, out],
        out_shape=[jax.ShapeDtypeStruct((s, 512), BF16)] * 3,
        compiler_params=_cp(("parallel",)),
    )(proj, proj, proj, proj, proj, o, mk, mv, cw, cb, mqg)


def _merge_fwd(x, proj, oa, oc, om, bg, wa, wc, wm, wo, t, name):
    s = x.shape[0]
    t = min(t, s)

    def body(x_ref, r_ref, oa_ref, oc_ref, om_ref, bg_ref, wa_ref, wc_ref, wm_ref, wo_ref,
             xo_ref, aa_ref, ac_ref, am_ref):
        y = jnp.zeros((t, D), F32)
        for j, (o_ref, w_ref, a_ref) in enumerate(((oa_ref, wa_ref, aa_ref), (oc_ref, wc_ref, ac_ref),
                                                   (om_ref, wm_ref, am_ref))):
            a = _dot(o_ref[...], w_ref[...])
            a_ref[...] = a.astype(BF16)
            rg = _sigmoid(r_ref[:, j * D:(j + 1) * D].astype(F32) + bg_ref[:, j * D:(j + 1) * D])
            y = y + rg * a
        xo_ref[...] = x_ref[...] + _dot(y.astype(BF16), wo_ref[...])

    row = lambda w: pl.BlockSpec((t, w), lambda i: (i, 0))
    return _call(
        body, name=name, grid=(s // t,),
        in_specs=[row(D), pl.BlockSpec((t, 3 * D), lambda i: (i, P_R // (3 * D))), row(512), row(512), row(512),
                  _full((1, 3 * D)), _full((512, D)), _full((512, D)), _full((512, D)), _full((D, D))],
        out_specs=[row(D), row(D), row(D), row(D)],
        out_shape=[jax.ShapeDtypeStruct((s, D), F32)] + [jax.ShapeDtypeStruct((s, D), BF16)] * 3,
        compiler_params=_cp(("parallel",)),
    )(x, proj, oa, oc, om, bg, wa, wc, wm, wo)


def _loss_grad(x, tgt, t, name):
    s = x.shape[0]
    t = min(t, s)

    def body(x_ref, t_ref, g_ref, l_ref):
        @pl.when(pl.program_id(0) == 0)
        def _():
            l_ref[...] = jnp.zeros_like(l_ref)

        e = x_ref[...] - t_ref[...]
        g_ref[...] = e * (1.0 / D)
        sq = e * e
        part = sq[:, 0:LANES]
        for j in range(1, D // LANES):
            part = part + sq[:, j * LANES:(j + 1) * LANES]
        acc = part[0:8]
        for j in range(1, t // 8):
            acc = acc + part[j * 8:(j + 1) * 8]
        l_ref[...] += acc * (0.5 / D)

    row = pl.BlockSpec((t, D), lambda i: (i, 0))
    return _call(
        body, name=name, grid=(s // t,),
        in_specs=[row, row], out_specs=[row, _full((8, LANES))],
        out_shape=[jax.ShapeDtypeStruct((s, D), F32), jax.ShapeDtypeStruct((8, LANES), F32)],
        compiler_params=_cp(("arbitrary",)),
    )(x, tgt)


def _merge_bwd(g, proj, aa, ac, am, bg, wot, wat, wct, wmt, t, name):
    s = g.shape[0]
    t = min(t, s)

    def body(g_ref, r_ref, aa_ref, ac_ref, am_ref, bg_ref, wot_ref, wat_ref, wct_ref, wmt_ref,
             y_ref, daa_ref, dac_ref, dam_ref, dr_ref, doa_ref, doc_ref, dom_ref, dbg_ref):
        @pl.when(pl.program_id(0) == 0)
        def _():
            dbg_ref[...] = jnp.zeros_like(dbg_ref)

        dy = _dot(g_ref[...].astype(BF16), wot_ref[...])
        y = jnp.zeros((t, D), F32)
        for j, (a_ref, wt_ref, da_ref, do_ref) in enumerate(((aa_ref, wat_ref, daa_ref, doa_ref),
                                                             (ac_ref, wct_ref, dac_ref, doc_ref),
                                                             (am_ref, wmt_ref, dam_ref, dom_ref))):
            a = a_ref[...].astype(F32)
            rg = _sigmoid(r_ref[:, j * D:(j + 1) * D].astype(F32) + bg_ref[:, j * D:(j + 1) * D])
            y = y + rg * a
            da = (dy * rg).astype(BF16)
            da_ref[...] = da
            dr = dy * a * rg * (1.0 - rg)
            dr_ref[:, j * D:(j + 1) * D] = dr.astype(BF16)
            dbg_ref[:, j * D:(j + 1) * D] += jnp.sum(dr, axis=0, keepdims=True)
            do_ref[...] = _dot(da, wt_ref[...])
        y_ref[...] = y.astype(BF16)

    row = lambda w: pl.BlockSpec((t, w), lambda i: (i, 0))
    return _call(
        body, name=name, grid=(s // t,),
        in_specs=[row(D), pl.BlockSpec((t, 3 * D), lambda i: (i, P_R // (3 * D))), row(D), row(D), row(D),
                  _full((1, 3 * D)), _full((D, D)), _full((D, 512)), _full((D, 512)), _full((D, 512))],
        out_specs=[row(D), row(D), row(D), row(D), row(3 * D), row(512), row(512), row(512), _full((1, 3 * D))],
        out_shape=[jax.ShapeDtypeStruct((s, D), BF16)] * 4 + [jax.ShapeDtypeStruct((s, 3 * D), BF16)]
        + [jax.ShapeDtypeStruct((s, 512), F32)] * 3 + [jax.ShapeDtypeStruct((1, 3 * D), F32)],
        compiler_params=_cp(("arbitrary",)),
    )(g, proj, aa, ac, am, bg, wot, wat, wct, wmt)


def _branches_bwd(proj, o, mk, mv, cw, cb, mqg, doa, doc, dom, t, name):
    s = proj.shape[0]
    t = min(t, s)
    nt = s // t
    m = mk.shape[1]

    def body(cv_ref, pv_ref, nx_ref, mm_ref, ga_ref, o_ref, mk_ref, mv_ref, cw_ref, cb_ref, mqg_ref,
             doa_ref, doc_ref, dcp_ref, dcn_ref, dom_ref,
             dcv_ref, dmm_ref, dga_ref, do_ref, dl_ref, dcw_ref, dcb_ref, dmk_ref, dmv_ref, dmqg_ref):
        i = pl.program_id(0)

        @pl.when(i == 0)
        def _():
            for r in (dcw_ref, dcb_ref, dmk_ref, dmv_ref, dmqg_ref):
                r[...] = jnp.zeros_like(r)

        first, last = i == 0, i == nt - 1
        cw_, cb_ = cw_ref[...], cb_ref[...]
        pv, nx = pv_ref[...], nx_ref[...]
        c_b, c_c, c_u, g_c, z, z_m1, z_p1, conv = _conv_parts(cv_ref[...], pv, nx, first, last, cw_, cb_)
        sg = _sigmoid(g_c)
        silu = g_c * sg
        dsilu = sg * (1.0 + g_c * (1.0 - sg))
        doc_ = doc_ref[...]
        dconv = doc_ * c_b * silu
        gp = pv[15:16, 3 * CW:4 * CW].astype(F32)
        gn = nx[0:1, 3 * CW:4 * CW].astype(F32)
        dconv_p = jnp.where(first, 0.0, dcp_ref[15:16, :] * pv[15:16, 0:CW].astype(F32) * (gp * _sigmoid(gp)))
        dconv_n = jnp.where(last, 0.0, dcn_ref[0:1, :] * nx[0:1, 0:CW].astype(F32) * (gn * _sigmoid(gn)))
        row = lax.broadcasted_iota(jnp.int32, (t, CW), 0)
        d_m1 = jnp.where(row == 0, dconv_p, pltpu.roll(dconv, 1, 0))
        d_p1 = jnp.where(row == t - 1, dconv_n, pltpu.roll(dconv, t - 1, 0))
        dz = cw_[0:1] * d_p1 + cw_[1:2] * dconv + cw_[2:3] * d_m1
        dcv_ref[:, 0:CW] = (doc_ * conv * silu).astype(BF16)
        dcv_ref[:, CW:2 * CW] = (dz * c_u).astype(BF16)
        dcv_ref[:, 2 * CW:3 * CW] = (dz * c_c).astype(BF16)
        dcv_ref[:, 3 * CW:4 * CW] = (doc_ * c_b * conv * dsilu).astype(BF16)
        dcw_ref[0:1, :] += jnp.sum(dconv * z_m1, axis=0, keepdims=True)
        dcw_ref[1:2, :] += jnp.sum(dconv * z, axis=0, keepdims=True)
        dcw_ref[2:3, :] += jnp.sum(dconv * z_p1, axis=0, keepdims=True)
        dcb_ref[...] += jnp.sum(dconv, axis=0, keepdims=True)
        ga = ga_ref[...].astype(F32)
        sg = _sigmoid(ga)
        doa_ = doa_ref[...]
        ocat = jnp.concatenate([o_ref[h].astype(F32) for h in range(H)], axis=-1)
        dga_ref[...] = (doa_ * ocat * (sg * (1.0 + ga * (1.0 - sg)))).astype(BF16)
        dog = doa_ * (ga * sg)
        zeros = jnp.zeros((t, HP - VD), F32)
        lane = lax.broadcasted_iota(jnp.int32, (t, LANES), 1)
        dmat = jnp.zeros((t, LANES), F32)
        for h in range(H):
            dh = dog[:, h * VD:(h + 1) * VD]
            do_ref[h] = jnp.concatenate([dh, zeros], axis=-1).astype(BF16)
            dmat = jnp.where(lane == h, jnp.sum(dh * ocat[:, h * VD:(h + 1) * VD], axis=-1, keepdims=True), dmat)
        dlt = dmat.T
        for h in range(H):
            dl_ref[h] = dlt[h:h + 1, :]
        mblk = mm_ref[...].astype(F32)
        gm = mblk[:, MW:]
        sg = _sigmoid(gm)
        dom_ = dom_ref[...]
        heads = _mem_attn(mblk[:, :MW], mqg_ref[...], mk_ref, mv_ref)
        om = jnp.concatenate([hh[4] for hh in heads], axis=-1)
        dmm_ref[:, MW:] = (dom_ * om * (sg * (1.0 + gm * (1.0 - sg)))).astype(BF16)
        dmo = dom_ * (gm * sg)
        dmqg = jnp.zeros((1, MHD), F32)
        for h in range(HM):
            mq16, mqxh, mqrs, p, _ = heads[h]
            dmo_h = dmo[:, h * MHD:(h + 1) * MHD].astype(BF16)
            dp = _dot_nt(dmo_h, mv_ref[h])
            ds = (p * (dp - jnp.sum(dp * p, axis=-1, keepdims=True)) * MEM_SCALE).astype(BF16)
            dmq = _dot(ds, mk_ref[h])
            dmk_ref[h] += _dot_tn(ds, mq16)
            dmv_ref[h] += _dot_tn(p.astype(BF16), dmo_h)
            dq, dg = _rms_bwd(dmq, mqxh, mqrs, mqg_ref[...], MHD)
            dmm_ref[:, h * MHD:(h + 1) * MHD] = dq.astype(BF16)
            dmqg = dmqg + dg
        dmqg_ref[...] += dmqg

    pv, nx = _halo_specs(t, s, 4 * CW)
    dpv, dnx = _halo_specs(t, s, CW)
    row = lambda w: pl.BlockSpec((t, w), lambda i: (i, 0))
    hs = lambda w: pl.BlockSpec((H, t, w), lambda i: (0, i, 0))
    return _call(
        body, name=name, grid=(nt,),
        in_specs=[pl.BlockSpec((t, 4 * CW), lambda i: (i, 0)), pv, nx,
                  pl.BlockSpec((t, 2 * MW), lambda i: (i, P_MEM // (2 * MW))),
                  pl.BlockSpec((t, AW), lambda i: (i, P_GA // AW)),
                  hs(VD), _full(mk.shape), _full(mv.shape), _full((3, CW)), _full((1, CW)), _full((1, MHD)),
                  row(512), row(512), dpv, dnx, row(512)],
        out_specs=[row(4 * CW), row(2 * MW), row(AW), hs(HP), pl.BlockSpec((H, 1, t), lambda i: (0, 0, i)),
                   _full((3, CW)), _full((1, CW)),
                   _full((HM, m, MHD)), _full((HM, m, MHD)), _full((1, MHD))],
        out_shape=[jax.ShapeDtypeStruct((s, 4 * CW), BF16), jax.ShapeDtypeStruct((s, 2 * MW), BF16),
                   jax.ShapeDtypeStruct((s, AW), BF16), jax.ShapeDtypeStruct((H, s, HP), BF16),
                   jax.ShapeDtypeStruct((H, 1, s), F32), jax.ShapeDtypeStruct((3, CW), F32),
                   jax.ShapeDtypeStruct((1, CW), F32), jax.ShapeDtypeStruct((HM, m, MHD), F32),
                   jax.ShapeDtypeStruct((HM, m, MHD), F32), jax.ShapeDtypeStruct((1, MHD), F32)],
        compiler_params=_cp(("arbitrary",)),
    )(proj, proj, proj, proj, proj, o, mk, mv, cw, cb, mqg, doa, doc, doc, doc, dom)


def _flash_bwd(q, k, v, do, lse, dl, tq, tk, name):
    _, s, _ = q.shape
    tq, tk = min(tq, s), min(tk, s)
    nq, nkt, nc = s // tq, s // tk, tk // LANES
    unroll = 4 if nq % 4 == 0 else 1

    def body(q_ref, do_ref, lse_ref, dl_ref, k_ref, v_ref, dq_ref, dk_ref, dv_ref, dq_acc):
        j = pl.program_id(1)

        @pl.when(j == 0)
        def _():
            dq_acc[...] = jnp.zeros_like(dq_acc)

        kb, vb = k_ref[0], v_ref[0]
        kbt = kb.astype(F32).T.astype(BF16)

        def step(ii, carry):
            dkt, dvt = carry
            for u in range(unroll):
                i = ii * unroll + u
                off = pl.multiple_of(i * tq, tq)
                qb = q_ref[0, pl.ds(off, tq), :]
                dob = do_ref[0, pl.ds(off, tq), :]
                lse_b = jnp.broadcast_to(lse_ref[0, i], (LANES, tq)).T
                dl_b = jnp.broadcast_to(dl_ref[0, i], (LANES, tq)).T
                sc = _dot_nt(qb, kb)
                dp = _dot_nt(dob, vb)
                ps, dss = [], []
                for cc in range(nc):
                    p = jnp.exp2(sc[:, cc * LANES:(cc + 1) * LANES] - lse_b)
                    ps.append(p.astype(BF16))
                    dss.append((p * (dp[:, cc * LANES:(cc + 1) * LANES] - dl_b)).astype(BF16))
                p16, ds16 = jnp.concatenate(ps, axis=-1), jnp.concatenate(dss, axis=-1)
                dvt = dvt + _dot_tn(dob, p16)
                dkt = dkt + _dot_tn(qb, ds16)
                dq_acc[i] += _dot_nt(kbt, ds16)
            return dkt, dvt

        dkt, dvt = lax.fori_loop(0, nq // unroll, step, (jnp.zeros((HP, tk), F32), jnp.zeros((HP, tk), F32)))
        dk_ref[0] = (dkt.T * (1.0 / LOG2E)).astype(BF16)
        dv_ref[0] = dvt.T.astype(BF16)

        @pl.when(j == nkt - 1)
        def _():
            def emit(i, carry):
                off = pl.multiple_of(i * tq, tq)
                dq_ref[0, pl.ds(off, tq), :] = (dq_acc[i].T * ATT_SCALE).astype(BF16)
                return carry

            lax.fori_loop(0, nq, emit, 0)

    whole = pl.BlockSpec((1, s, HP), lambda h, j: (h, 0, 0))
    stat = pl.BlockSpec((1, nq, 1, tq), lambda h, j: (h, 0, 0, 0))
    tile = pl.BlockSpec((1, tk, HP), lambda h, j: (h, j, 0))
    return _call(
        body, name=name, grid=(H, nkt),
        in_specs=[whole, whole, stat, stat, tile, tile],
        out_specs=[whole, tile, tile],
        out_shape=[jax.ShapeDtypeStruct((H, s, HP), BF16)] * 3,
        scratch_shapes=[pltpu.VMEM((nq, HP, tq), F32)],
        compiler_params=_cp(("arbitrary", "arbitrary")),
    )(q, do, lse, dl, k, v)


def _mla_prep_bwd(proj_l, c, sn, qng, kvng, qhg, khg, wuq, wukv, wuqt, wukvt, dq, dk, dv, t, name):
    s = proj_l.shape[0]
    t = min(t, s)

    def body(l_ref, c_ref, sn_ref, qng_ref, kvng_ref, qhg_ref, khg_ref, wuq_ref, wukv_ref, wuqt_ref, wukvt_ref,
             dq_ref, dk_ref, dv_ref, dl_ref, dwuq_ref, dwukv_ref, dqng_ref, dkvng_ref, dqhg_ref, dkhg_ref):
        @pl.when(pl.program_id(0) == 0)
        def _():
            for r in (dwuq_ref, dwukv_ref, dqng_ref, dkvng_ref, dqhg_ref, dkhg_ref):
                r[...] = jnp.zeros_like(r)

        cc, ss = c_ref[...], sn_ref[...]
        qhg, khg = qhg_ref[...], khg_ref[...]
        (_, _, kpe, qxh, qrs, kvxh, kvrs, qn16, kvn16, qp, kvp) = _mla_heads(
            l_ref[...], cc, ss, qng_ref[...], kvng_ref[...], qhg, khg, wuq_ref[...], wukv_ref[...])
        lane = lax.broadcasted_iota(jnp.int32, (t, HP), 1)
        dqp, dkp, dvp = [], [], []
        dkpe = jnp.zeros((t, HP), F32)
        dqhg = jnp.zeros((1, HP), F32)
        dkhg = jnp.zeros((1, HP), F32)
        for h in range(H):
            _, xh, rs = _rms(qp[:, h * HP:(h + 1) * HP], qhg, QKD)
            du, dg = _rms_bwd(_rope_adj(dq_ref[h].astype(F32), cc, ss), xh, rs, qhg, QKD)
            dqp.append(du)
            dqhg = dqhg + dg
            _, xh, rs = _rms(kvp[:, h * HP:(h + 1) * HP] + kpe, khg, QKD)
            du, dg = _rms_bwd(_rope_adj(dk_ref[h].astype(F32), cc, ss), xh, rs, khg, QKD)
            dkp.append(du)
            dkhg = dkhg + dg
            dkpe = dkpe + jnp.where((lane >= NOPE) & (lane < QKD), du, 0.0)
            dvp.append(dv_ref[h].astype(F32))
        dqhg_ref[...] += dqhg
        dkhg_ref[...] += dkhg
        dqp16 = jnp.concatenate(dqp, axis=-1).astype(BF16)
        dkvp16 = jnp.concatenate(dkp + dvp, axis=-1).astype(BF16)
        dwuq_ref[...] += _dot_tn(qn16, dqp16)
        dwukv_ref[...] += _dot_tn(kvn16, dkvp16)
        dql, dg = _rms_bwd(_dot(dqp16, wuqt_ref[...]), qxh, qrs, qng_ref[...], QL)
        dqng_ref[...] += dg
        dkvl, dg = _rms_bwd(_dot(dkvp16, wukvt_ref[...]), kvxh, kvrs, kvng_ref[...], KVL)
        dkvng_ref[...] += dg
        dl_ref[:, 0:QL] = dql.astype(BF16)
        dl_ref[:, QL:QL + KVL] = dkvl.astype(BF16)
        dl_ref[:, QL + KVL:P_LW] = dkpe.astype(BF16)

    hs = pl.BlockSpec((H, t, HP), lambda i: (0, i, 0))
    row = lambda w: pl.BlockSpec((t, w), lambda i: (i, 0))
    return _call(
        body, name=name, grid=(s // t,),
        in_specs=[row(P_LW), row(HP), row(HP), _full((1, QL)), _full((1, KVL)), _full((1, HP)), _full((1, HP)),
                  _full((QL, H * HP)), _full((KVL, 2 * H * HP)), _full((H * HP, QL)), _full((2 * H * HP, KVL)),
                  hs, hs, hs],
        out_specs=[row(P_LW), _full((QL, H * HP)), _full((KVL, 2 * H * HP)), _full((1, QL)), _full((1, KVL)),
                   _full((1, HP)), _full((1, HP))],
        out_shape=[jax.ShapeDtypeStruct((s, P_LW), BF16), jax.ShapeDtypeStruct((QL, H * HP), F32),
                   jax.ShapeDtypeStruct((KVL, 2 * H * HP), F32), jax.ShapeDtypeStruct((1, QL), F32),
                   jax.ShapeDtypeStruct((1, KVL), F32), jax.ShapeDtypeStruct((1, HP), F32),
                   jax.ShapeDtypeStruct((1, HP), F32)],
        compiler_params=_cp(("arbitrary",)),
    )(proj_l, c, sn, qng, kvng, qhg, khg, wuq, wukv, wuqt, wukvt, dq, dk, dv)


def _mem_prep_bwd(mem, mng, wmkv, wmkvt, mkg, dmk, dmv, name):
    m = mem.shape[0]

    def body(mem_ref, mng_ref, w_ref, wt_ref, mkg_ref, dmk_ref, dmv_ref, dw_ref, dmng_ref, dmkg_ref):
        mn, xh, _ = _rms(mem_ref[...], mng_ref[...], D)
        mn16 = mn.astype(BF16)
        mkv = _dot(mn16, w_ref[...])
        parts = []
        dmkg = jnp.zeros((1, MHD), F32)
        for h in range(HM):
            _, kxh, krs = _rms(mkv[:, 2 * h * MHD:(2 * h + 1) * MHD], mkg_ref[...], MHD)
            du, dg = _rms_bwd(dmk_ref[h], kxh, krs, mkg_ref[...], MHD)
            dmkg = dmkg + dg
            parts += [du, dmv_ref[h]]
        dmkv = jnp.concatenate(parts, axis=-1).astype(BF16)
        dw_ref[...] = _dot_tn(mn16, dmkv)
        dmn = _dot(dmkv, wt_ref[...])
        dmng_ref[...] = jnp.sum(dmn * xh, axis=0, keepdims=True)
        dmkg_ref[...] = dmkg

    return _call(
        body, name=name,
        in_specs=[_full((m, D)), _full((1, D)), _full((D, 2 * MW)), _full((2 * MW, D)), _full((1, MHD)),
                  _full((HM, m, MHD)), _full((HM, m, MHD))],
        out_specs=[_full((D, 2 * MW)), _full((1, D)), _full((1, MHD))],
        out_shape=[jax.ShapeDtypeStruct((D, 2 * MW), F32), jax.ShapeDtypeStruct((1, D), F32),
                   jax.ShapeDtypeStruct((1, MHD), F32)],
        compiler_params=_cp(),
    )(mem, mng, wmkv, wmkvt, mkg, dmk, dmv)


def _rms_in_bwd(x, g_out, dh, ng, t, name):
    s = x.shape[0]
    t = min(t, s)

    def body(x_ref, go_ref, dh_ref, ng_ref, dx_ref, dng_ref):
        @pl.when(pl.program_id(0) == 0)
        def _():
            dng_ref[...] = jnp.zeros_like(dng_ref)

        _, xh, rs = _rms(x_ref[...], ng_ref[...], D)
        dx, dg = _rms_bwd(dh_ref[...], xh, rs, ng_ref[...], D)
        dx_ref[...] = go_ref[...] + dx
        dng_ref[...] += dg

    row = pl.BlockSpec((t, D), lambda i: (i, 0))
    return _call(
        body, name=name, grid=(s // t,),
        in_specs=[row, row, row, _full((1, D))], out_specs=[row, _full((1, D))],
        out_shape=[jax.ShapeDtypeStruct((s, D), F32), jax.ShapeDtypeStruct((1, D), F32)],
        compiler_params=_cp(("arbitrary",)),
    )(x, g_out, dh, ng)


def _allgather(arrs, name):
    n = len(arrs)

    def body(*refs):
        x_refs, out_refs = refs[:n], refs[n:2 * n]
        send_sems, recv_sems, local_sems = refs[2 * n:]
        x, y, c = lax.axis_index("x"), lax.axis_index("y"), lax.axis_index("c")
        me, sibling = (x, y, c), (x, y, 1 - c)
        chips = [(1 - x, y), (x, 1 - y), (1 - x, 1 - y)]

        def slot(a, px, py, pc):
            return out_refs[a].at[4 * px + 2 * py + pc]

        def copy(a, k, block, to, src=None):
            return pltpu.make_async_remote_copy(
                src_ref=slot(a, *block) if src is None else src, dst_ref=slot(a, *block),
                send_sem=send_sems.at[a, k], recv_sem=recv_sems.at[a, k],
                device_id=to, device_id_type=pl.DeviceIdType.MESH)

        mine = [pltpu.make_async_copy(x_refs[a], slot(a, *me), local_sems.at[a]) for a in range(n)]
        first, passed = [], []
        for a in range(n):
            mine[a].start()
            first.append(copy(a, 0, me, sibling, src=x_refs[a]))
            first += [copy(a, 1 + j, me, (*chip, c), src=x_refs[a]) for j, chip in enumerate(chips)]
        for cp in first:
            cp.start()
        for j, chip in enumerate(chips):
            for a in range(n):
                copy(a, 1 + j, (*chip, c), me).wait_recv()
                passed.append(copy(a, 4 + j, (*chip, c), sibling))
                passed[-1].start()
        for a in range(n):
            copy(a, 0, sibling, me).wait_recv()
        for j, chip in enumerate(chips):
            for a in range(n):
                copy(a, 4 + j, (*chip, 1 - c), me).wait_recv()
        for cp in first + passed:
            cp.wait_send()
        for cp in mine:
            cp.wait()

    any_spec = pl.BlockSpec(memory_space=pl.ANY)
    return _call(
        body, name=name,
        in_specs=[any_spec] * n, out_specs=[any_spec] * n,
        out_shape=[jax.ShapeDtypeStruct((N_DEV,) + a.shape, a.dtype) for a in arrs],
        scratch_shapes=[pltpu.SemaphoreType.DMA((n, 7)), pltpu.SemaphoreType.DMA((n, 7)),
                        pltpu.SemaphoreType.DMA((n,))],
    )(*arrs)


def _alltoall(arrs, name):
    n = len(arrs)

    def body(*refs):
        x_refs, out_refs = refs[:n], refs[n:2 * n]
        send_sems, recv_sems, local_sems = refs[2 * n:]
        x, y, c = lax.axis_index("x"), lax.axis_index("y"), lax.axis_index("c")
        me = 4 * x + 2 * y + c
        mine = [pltpu.make_async_copy(x_refs[a].at[me], out_refs[a].at[me], local_sems.at[a]) for a in range(n)]
        for cp in mine:
            cp.start()
        copies = []
        for k in range(1, N_DEV):
            bx, by, bc = (k >> 2) & 1, (k >> 1) & 1, k & 1
            px = 1 - x if bx else x
            py = 1 - y if by else y
            pc = 1 - c if bc else c
            for a in range(n):
                copies.append(pltpu.make_async_remote_copy(
                    src_ref=x_refs[a].at[4 * px + 2 * py + pc], dst_ref=out_refs[a].at[me],
                    send_sem=send_sems.at[a, k - 1], recv_sem=recv_sems.at[a, k - 1],
                    device_id=(px, py, pc), device_id_type=pl.DeviceIdType.MESH))
        for cp in copies:
            cp.start()
        for cp in copies:
            cp.wait_recv()
        for cp in copies:
            cp.wait_send()
        for cp in mine:
            cp.wait()

    any_spec = pl.BlockSpec(memory_space=pl.ANY)
    return _call(
        body, name=name,
        in_specs=[any_spec] * n, out_specs=[any_spec] * n,
        out_shape=[jax.ShapeDtypeStruct(a.shape, a.dtype) for a in arrs],
        scratch_shapes=[pltpu.SemaphoreType.DMA((n, 7)), pltpu.SemaphoreType.DMA((n, 7)),
                        pltpu.SemaphoreType.DMA((n,))],
    )(*arrs)


ADAMW_BLOCK_BYTES = 4 * 1024 * 1024


def _adamw(parts, w, m, v, name):
    r, c_ = w.shape
    cpad = -(-c_ // LANES) * LANES
    tr = r
    while N_DEV * tr * cpad * 4 > ADAMW_BLOCK_BYTES and tr % 16 == 0:
        tr //= 2
    c1 = 1.0 / (1.0 - ADAM_B1 ** ADAM_STEP)
    c2 = 1.0 / (1.0 - ADAM_B2 ** ADAM_STEP)

    def body(p_ref, w_ref, m_ref, v_ref, g_ref, d_ref, nm_ref, nv_ref):
        g = p_ref[0].astype(F32)
        for j in range(1, N_DEV):
            g = g + p_ref[j].astype(F32)
        nm = ADAM_B1 * m_ref[...] + (1.0 - ADAM_B1) * g
        nv = ADAM_B2 * v_ref[...] + (1.0 - ADAM_B2) * (g * g)
        g_ref[...] = g
        nm_ref[...] = nm
        nv_ref[...] = nv
        d_ref[...] = -ADAM_LR * ((nm * c1) / (jnp.sqrt(nv * c2) + ADAM_EPS) + ADAM_WD * w_ref[...])

    row = pl.BlockSpec((tr, c_), lambda i: (i, 0))
    return _call(
        body, name=name, grid=(r // tr,),
        in_specs=[pl.BlockSpec((N_DEV, tr, c_), lambda i: (0, i, 0)), row, row, row],
        out_specs=[row] * 4, out_shape=[jax.ShapeDtypeStruct((r, c_), F32)] * 4,
        compiler_params=_cp(("parallel",)),
    )(parts, w, m, v)


def _to_rows(flat, align=8):
    n = flat.shape[-1]
    rows = -(-n // (LANES * align)) * align
    return jnp.pad(flat, (0, rows * LANES - n)).reshape(rows, LANES)


def _shard_blocks(full, axis):
    r, c_ = full.shape
    if axis == 0:
        return full.reshape(N_DEV, r // N_DEV, c_)
    return full.reshape(r, N_DEV, c_ // N_DEV).transpose(1, 0, 2)


def _unshard_blocks(blocks, axis):
    _, r, c_ = blocks.shape
    if axis == 0:
        return blocks.reshape(N_DEV * r, c_)
    return blocks.transpose(1, 0, 2).reshape(r, N_DEV * c_)


def _pad_heads(w, width):
    k = w.shape[0]
    return jnp.pad(w.reshape(k, H, width), ((0, 0), (0, 0), (0, HP - width))).reshape(k, H * HP)


def _unpad_heads(w, width):
    k = w.shape[0]
    return w.reshape(k, H, HP)[:, :, :width].reshape(k, H * width)


IN_SPLIT = {'q_lat': (0, 384), 'kv_lat': (384, 640), 'k_pe': (640, 672), 'c_b': (672, 1184), 'c_c': (1184, 1696),
            'c_u': (1696, 2208), 'q_mem': (2208, 2720), 'g_attn': (2720, 3232), 'g_conv': (3232, 3744),
            'g_mem': (3744, 4256), 'r': (4256, 7328)}
P_ORDER = ['c_b', 'c_c', 'c_u', 'g_conv', 'q_mem', 'g_mem', 'r', 'g_attn', 'q_lat', 'kv_lat']


def _permute_w_in(w):
    k = w.shape[0]
    cols = [w[:, IN_SPLIT[n][0]:IN_SPLIT[n][1]] for n in P_ORDER]
    kpe = w[:, IN_SPLIT['k_pe'][0]:IN_SPLIT['k_pe'][1]]
    cols += [jnp.zeros((k, NOPE), w.dtype), kpe, jnp.zeros((k, HP - QKD), w.dtype)]
    return jnp.concatenate(cols, axis=1)


def _unpermute_w_in(pieces):
    bounds, off = [], 0
    for p in pieces:
        bounds.append((off, off + p.shape[1]))
        off += p.shape[1]

    def cols(lo, hi):
        for p, (b0, b1) in zip(pieces, bounds):
            if b0 <= lo and hi <= b1:
                return p[:, lo - b0:hi - b0]
        raise ValueError("a column range straddles two pieces")

    off, pos = 0, {}
    for n in P_ORDER:
        wd = IN_SPLIT[n][1] - IN_SPLIT[n][0]
        pos[n] = (off, off + wd)
        off += wd
    pos['k_pe'] = (off + NOPE, off + QKD)
    order = sorted(IN_SPLIT, key=lambda n: IN_SPLIT[n][0])
    return jnp.concatenate([cols(*pos[n]) for n in order], axis=1)


def _layer_weights(full, l):
    w = {}
    w_in_p = _permute_w_in(full['w_in'][l])
    w['w_main'] = w_in_p[:, :P_MAIN]
    w['w_l'] = w_in_p[:, P_MAIN:]
    w['w_in_t'] = w_in_p.T
    w['w_uq'] = _pad_heads(full['w_uq'][l], QKD)
    wukv = full['w_ukv'][l].reshape(KVL, H, 2, NOPE)
    kpart = jnp.pad(wukv[:, :, 0, :], ((0, 0), (0, 0), (0, HP - NOPE))).reshape(KVL, H * HP)
    vpart = jnp.pad(wukv[:, :, 1, :], ((0, 0), (0, 0), (0, HP - VD))).reshape(KVL, H * HP)
    w['w_ukv'] = jnp.concatenate([kpart, vpart], axis=1)
    w['w_uq_t'] = w['w_uq'].T
    w['w_ukv_t'] = w['w_ukv'].T
    w['w_mkv'] = full['w_mkv'][l]
    w['w_mkv_t'] = w['w_mkv'].T
    for n in ('w_br_attn', 'w_br_conv', 'w_br_mem', 'w_out'):
        w[n] = full[n][l]
        w[n + '_t'] = w[n].T
    return w


def kernel(x, mem, positions, norm_g, w_in, b_gate, q_norm_g, w_uq, kv_norm_g, w_ukv, q_head_g, k_head_g, conv_w, conv_b, mem_norm_g, w_mkv, mem_q_g, mem_k_g, w_br_attn, w_br_conv, w_br_mem, w_out, loss_target, m_norm_g, m_w_in, m_b_gate, m_q_norm_g, m_w_uq, m_kv_norm_g, m_w_ukv, m_q_head_g, m_k_head_g, m_conv_w, m_conv_b, m_mem_norm_g, m_w_mkv, m_mem_q_g, m_mem_k_g, m_w_br_attn, m_w_br_conv, m_w_br_mem, m_w_out, v_norm_g, v_w_in, v_b_gate, v_q_norm_g, v_w_uq, v_kv_norm_g, v_w_ukv, v_q_head_g, v_k_head_g, v_conv_w, v_conv_b, v_mem_norm_g, v_w_mkv, v_mem_q_g, v_mem_k_g, v_w_br_attn, v_w_br_conv, v_w_br_mem, v_w_out):
    a = dict(zip(INPUTS, (x, mem, positions, norm_g, w_in, b_gate, q_norm_g, w_uq, kv_norm_g, w_ukv, q_head_g, k_head_g, conv_w, conv_b, mem_norm_g, w_mkv, mem_q_g, mem_k_g, w_br_attn, w_br_conv, w_br_mem, w_out, loss_target, m_norm_g, m_w_in, m_b_gate, m_q_norm_g, m_w_uq, m_kv_norm_g, m_w_ukv, m_q_head_g, m_k_head_g, m_conv_w, m_conv_b, m_mem_norm_g, m_w_mkv, m_mem_q_g, m_mem_k_g, m_w_br_attn, m_w_br_conv, m_w_br_mem, m_w_out, v_norm_g, v_w_in, v_b_gate, v_q_norm_g, v_w_uq, v_kv_norm_g, v_w_ukv, v_q_head_g, v_k_head_g, v_conv_w, v_conv_b, v_mem_norm_g, v_w_mkv, v_mem_q_g, v_mem_k_g, v_w_br_attn, v_w_br_conv, v_w_br_mem, v_w_out)))
    x = a['x'][0]
    mem = a['mem'][0]
    tgt = a['loss_target'][0]
    s = x.shape[0]
    t_el = 256
    t_br = 512
    tq_f, tk_f, tq_b, tk_b = 256, 1024, 512, 512

    gathered = _allgather([a[n].astype(BF16) for n in BIG_ORDER] + [a['conv_w']], "ag_weights")
    full = {n: [_unshard_blocks(g8[:, l], BIG[n][1]) for l in range(DEPTH)] for n, g8 in zip(BIG_ORDER, gathered)}
    conv_w = gathered[-1].transpose(1, 2, 0, 3).reshape(DEPTH, 3, CW)

    inv_freq = ROPE_BASE ** (-jnp.arange(0, RP, 2, dtype=F32) / RP)
    ang = a['positions'][0].astype(F32)[:, None] * inv_freq
    cos, sin = jnp.cos(ang), jnp.sin(ang)
    rc = jnp.concatenate([jnp.ones((s, NOPE), F32), cos, cos, jnp.ones((s, HP - QKD), F32)], axis=1)
    rs = jnp.concatenate([jnp.zeros((s, NOPE), F32), -sin, sin, jnp.zeros((s, HP - QKD), F32)], axis=1)

    def small(n, l, width=None):
        v = a[n][l][None, :]
        return v if width is None else jnp.pad(v, ((0, 0), (0, width - v.shape[1])))

    saved = []
    layer_w = [_layer_weights(full, l) for l in range(DEPTH)]
    for l in range(DEPTH):
        w = layer_w[l]
        tag = ""
        h, ht = _rms_h(x, small('norm_g', l), 512, "rms_h" + tag)
        proj = _mm(h, w['w_main'], BF16, 2048, 512, "in_proj" + tag)
        proj_l = _mm(h, w['w_l'], BF16, 512, P_LW, "in_proj_lat" + tag)
        qng, kvng = small('q_norm_g', l), small('kv_norm_g', l)
        qhg, khg = small('q_head_g', l, HP), small('k_head_g', l, HP)
        q, k, v = _mla_prep(proj_l, rc, rs, qng, kvng, qhg, khg, w['w_uq'], w['w_ukv'], t_br, "mla_prep" + tag)
        o, lse = _flash_fwd(q, k, v, tq_f, tk_f, "flash_fwd" + tag)
        mk, mv = _mem_prep(mem, small('mem_norm_g', l), w['w_mkv'], small('mem_k_g', l), "mem_prep" + tag)
        cb, mqg = small('conv_b', l), small('mem_q_g', l)
        oa, oc, om = _branches(proj, o, mk, mv, conv_w[l], cb, mqg, t_br, "branches" + tag)
        x_out, aa, ac, am = _merge_fwd(x, proj, oa, oc, om, small('b_gate', l), w['w_br_attn'], w['w_br_conv'],
                                       w['w_br_mem'], w['w_out'], t_el, "merge" + tag)
        saved.append(dict(x=x, ht=ht, proj=proj, proj_l=proj_l, q=q, k=k, v=v, o=o, lse=lse, mk=mk, mv=mv,
                          oa=oa, oc=oc, om=om, aa=aa, ac=ac, am=am))
        x = x_out

    g, loss_parts = _loss_grad(x, tgt, 512, "loss")
    loss = lax.psum(jnp.sum(loss_parts), ("x", "y", "c"))

    gw = {n: [None] * DEPTH for n in WEIGHTS}
    for l in reversed(range(DEPTH)):
        w = layer_w[l]
        sv = saved[l]
        tag = ""
        tqb = min(tq_b, s)
        y, daa, dac, dam, dr, doa, doc, dom, dbg = _merge_bwd(
            g, sv['proj'], sv['aa'], sv['ac'], sv['am'], small('b_gate', l), w['w_out_t'], w['w_br_attn_t'],
            w['w_br_conv_t'], w['w_br_mem_t'], t_el, "merge_bwd" + tag)
        gw['b_gate'][l] = dbg[0]
        gw['w_out'][l] = _tn(y, g, 512, D, "dw_out" + tag)
        gw['w_br_attn'][l] = _tn(sv['oa'], daa, 512, D, "dw_attn" + tag)
        gw['w_br_conv'][l] = _tn(sv['oc'], dac, 512, D, "dw_conv" + tag)
        gw['w_br_mem'][l] = _tn(sv['om'], dam, 512, D, "dw_mem" + tag)
        cb, mqg = small('conv_b', l), small('mem_q_g', l)
        dcv, dmm, dga, do, dl, dcw, dcb, dmk, dmv, dmqg = _branches_bwd(
            sv['proj'], sv['o'], sv['mk'], sv['mv'], conv_w[l], cb, mqg, doa, doc, dom, t_br, "branches_bwd" + tag)
        gw['conv_w'][l], gw['conv_b'][l], gw['mem_q_g'][l] = dcw, dcb[0], dmqg[0]
        dwm, dmng, dmkg = _mem_prep_bwd(mem, small('mem_norm_g', l), w['w_mkv'], w['w_mkv_t'], small('mem_k_g', l),
                                        dmk, dmv, "mem_prep_bwd" + tag)
        gw['w_mkv'][l], gw['mem_norm_g'][l], gw['mem_k_g'][l] = dwm, dmng[0], dmkg[0]
        lse_r = sv['lse'].reshape(H, s // tqb, 1, tqb)
        dl_r = dl.reshape(H, s // tqb, 1, tqb)
        dq, dk, dv = _flash_bwd(sv['q'], sv['k'], sv['v'], do, lse_r, dl_r, tq_b, tk_b, "flash_bwd" + tag)
        qng, kvng = small('q_norm_g', l), small('kv_norm_g', l)
        qhg, khg = small('q_head_g', l, HP), small('k_head_g', l, HP)
        dlat, dwuq, dwukv, dqng, dkvng, dqhg, dkhg = _mla_prep_bwd(
            sv['proj_l'], rc, rs, qng, kvng, qhg, khg, w['w_uq'], w['w_ukv'], w['w_uq_t'], w['w_ukv_t'],
            dq, dk, dv, t_br, "mla_prep_bwd" + tag)
        gw['w_uq'][l] = _unpad_heads(dwuq, QKD)
        dwukv = dwukv.reshape(KVL, 2, H, HP)[:, :, :, :NOPE]
        gw['w_ukv'][l] = dwukv.transpose(0, 2, 1, 3).reshape(KVL, H * 2 * NOPE)
        gw['q_norm_g'][l], gw['kv_norm_g'][l] = dqng[0], dkvng[0]
        gw['q_head_g'][l], gw['k_head_g'][l] = dqhg[0, :QKD], dkhg[0, :QKD]
        dpieces = [dcv, dmm, dr, dga, dlat]
        dh = _mm_pieces(dpieces, w['w_in_t'], 512, 512, "d_h" + tag)
        gw['w_in'][l] = _unpermute_w_in([
            _mm_acc(sv['ht'], p, 1024, tn, f"dw_in_{i}" + tag)
            for i, (p, tn) in enumerate(zip(dpieces, (1024, 1024, 1536, 512, P_LW)))])
        g, dng = _rms_in_bwd(sv['x'], g, dh, small('norm_g', l), 512, "rms_bwd" + tag)
        gw['norm_g'][l] = dng[0]
    grad_x = g[None]

    sharded = BIG_ORDER + ['conv_w']
    axis_of = lambda n: 1 if n == 'conv_w' else BIG[n][1]
    send = [jnp.stack([_shard_blocks(gw[n][l], axis_of(n)) for l in range(DEPTH)], axis=1)
            .astype(F32 if n == 'conv_w' else BF16) for n in sharded]
    parts_big = _alltoall(send, "rs_grads")
    small_flat = jnp.concatenate([jnp.stack(gw[n]).reshape(-1) for n in SMALL_ORDER])
    n_small = small_flat.shape[0]
    parts_small = _allgather([_to_rows(small_flat)], "ag_small_grads")[0]

    outs = [{} for _ in range(4)]
    for n, parts in zip(sharded, parts_big):
        loc = a[n].shape
        two_d = lambda t: t.reshape(loc[0] * loc[1], loc[2])
        res = _adamw(parts.reshape(N_DEV, loc[0] * loc[1], loc[2]), two_d(a[n]), two_d(a['m_' + n]),
                     two_d(a['v_' + n]), "adamw_" + n)
        for d, r in zip(outs, res):
            d[n] = r.reshape(loc)
    pks = lambda pre: _to_rows(jnp.concatenate([a[pre + n].reshape(-1) for n in SMALL_ORDER]))
    res_small = _adamw(parts_small, pks(''), pks('m_'), pks('v_'), "adamw_small")
    for d, rsm in zip(outs, res_small):
        flat = rsm.reshape(-1)[:n_small]
        off = 0
        for n in SMALL_ORDER:
            d[n] = flat[off:off + DEPTH * SMALL[n]].reshape(DEPTH, SMALL[n])
            off += DEPTH * SMALL[n]
    result = [loss, grad_x]
    for d in outs:
        result += [d[n] for n in WEIGHTS]
    return tuple(result)
```

```python
import functools

import jax
import jax.numpy as jnp
from jax import lax
from jax.experimental import pallas as pl
from jax.experimental.pallas import tpu as pltpu

F32, BF16 = jnp.float32, jnp.bfloat16

N_DEV = 8
DEPTH = 4
D = 1024
QL, KVL, RP = 384, 256, 32
H, NOPE, QKD, VD = 8, 64, 96, 64
HP = 128
CW, MW, AW = 512, 512, 512
HM, MHD = 4, 128
IN_WIDTH = 7328
EPS = 1e-6
ROPE_BASE = 10000.0
ATT_SCALE = QKD ** -0.5
MEM_SCALE = MHD ** -0.5
LOG2E = 1.4426950408889634
QSCALE = ATT_SCALE * LOG2E

ADAM_LR, ADAM_B1, ADAM_B2, ADAM_EPS, ADAM_WD, ADAM_STEP = 0.001, 0.9, 0.999, 1e-08, 0.01, 10

LANES = 128
VMEM_LIMIT = 56 * 1024 * 1024

P_CONV, P_MEM, P_R, P_GA, P_L = 0, 2048, 3072, 6144, 6656
P_MAIN = 6656
P_LW = 768
P_W = P_MAIN + P_LW
P_WPAD = 7680

WEIGHTS = ['norm_g', 'w_in', 'b_gate', 'q_norm_g', 'w_uq', 'kv_norm_g', 'w_ukv', 'q_head_g', 'k_head_g',
           'conv_w', 'conv_b', 'mem_norm_g', 'w_mkv', 'mem_q_g', 'mem_k_g', 'w_br_attn', 'w_br_conv',
           'w_br_mem', 'w_out']
INPUTS = ['x', 'mem', 'positions'] + WEIGHTS + ['loss_target'] + ['m_' + n for n in WEIGHTS] + ['v_' + n for n in WEIGHTS]

BIG = {'w_in': ((D, IN_WIDTH), 1), 'w_uq': ((QL, H * QKD), 1), 'w_ukv': ((KVL, H * 128), 1),
       'w_mkv': ((D, 2 * MW), 0), 'w_br_attn': ((AW, D), 1), 'w_br_conv': ((CW, D), 1),
       'w_br_mem': ((MW, D), 1), 'w_out': ((D, D), 0)}
BIG_ORDER = ['w_in', 'w_uq', 'w_ukv', 'w_mkv', 'w_br_attn', 'w_br_conv', 'w_br_mem', 'w_out']
CONVW_PAD = 256
SMALL = {'norm_g': D, 'b_gate': 3 * D, 'q_norm_g': QL, 'kv_norm_g': KVL, 'q_head_g': QKD, 'k_head_g': QKD,
         'conv_b': CW, 'mem_norm_g': D, 'mem_q_g': MHD, 'mem_k_g': MHD}
SMALL_ORDER = list(SMALL)
ROW_ALIGN = 1024


def _call(body, **kw):
    return pl.pallas_call(body, **kw)


def _cp(sem=None):
    return pltpu.CompilerParams(dimension_semantics=sem, vmem_limit_bytes=VMEM_LIMIT)


def _full(shape):
    n = len(shape)
    return pl.BlockSpec(shape, lambda *_: (0,) * n)


def _rms(x, g, n):
    rs = lax.rsqrt(jnp.sum(x * x, axis=-1, keepdims=True) * (1.0 / n) + EPS)
    xh = x * rs
    return xh * g, xh, rs


def _rms_bwd(dy, xh, rs, g, n):
    dxh = dy * g
    dx = rs * (dxh - xh * (jnp.sum(dxh * xh, axis=-1, keepdims=True) * (1.0 / n)))
    return dx, jnp.sum(dy * xh, axis=0, keepdims=True)


def _sigmoid(x):
    return 1.0 / (1.0 + jnp.exp(-x))


def _swap_rope(u):
    lane = lax.broadcasted_iota(jnp.int32, u.shape, 1)
    up = pltpu.roll(u, 16, 1)
    dn = pltpu.roll(u, 112, 1)
    return jnp.where((lane >= 64) & (lane < 80), dn, jnp.where((lane >= 80) & (lane < 96), up, 0.0))


def _rope(u, c, sn):
    return u * c + _swap_rope(u) * sn


def _rope_adj(d, c, sn):
    return d * c + _swap_rope(d * sn)


def _dot(a, b):
    return jnp.dot(a, b, preferred_element_type=F32)


def _dot_nt(a, b):
    return lax.dot_general(a, b, (((1,), (1,)), ((), ())), preferred_element_type=F32)


def _dot_tn(a, b):
    return lax.dot_general(a, b, (((0,), (0,)), ((), ())), preferred_element_type=F32)


def _mm(a, b, out_dtype, tm, tn, name):
    m, k = a.shape
    _, n = b.shape
    tm, tn = min(tm, m), min(tn, n)

    def body(a_ref, b_ref, o_ref):
        o_ref[...] = _dot(a_ref[...].astype(BF16), b_ref[...]).astype(o_ref.dtype)

    return _call(
        body, name=name, grid=(m // tm, n // tn),
        in_specs=[pl.BlockSpec((tm, k), lambda i, j: (i, 0)), pl.BlockSpec((k, tn), lambda i, j: (0, j))],
        out_specs=pl.BlockSpec((tm, tn), lambda i, j: (i, j)),
        out_shape=jax.ShapeDtypeStruct((m, n), out_dtype),
        compiler_params=_cp(("parallel", "arbitrary")),
    )(a, b)


def _tn(a, b, ts, tn, name):
    s, ka = a.shape
    _, n = b.shape
    ts, tn = min(ts, s), min(tn, n)

    def body(a_ref, b_ref, o_ref):
        @pl.when(pl.program_id(1) == 0)
        def _():
            o_ref[...] = jnp.zeros_like(o_ref)

        o_ref[...] += _dot_tn(a_ref[...].astype(BF16), b_ref[...].astype(BF16))

    return _call(
        body, name=name, grid=(n // tn, s // ts),
        in_specs=[pl.BlockSpec((ts, ka), lambda j, i: (i, 0)), pl.BlockSpec((ts, tn), lambda j, i: (i, j))],
        out_specs=pl.BlockSpec((ka, tn), lambda j, i: (0, j)),
        out_shape=jax.ShapeDtypeStruct((ka, n), F32),
        compiler_params=_cp(("parallel", "arbitrary")),
    )(a, b)


def _mm_pieces(pieces, b, tm, tn, name):
    n_p = len(pieces)
    m = pieces[0].shape[0]
    k, n = b.shape
    tm, tn = min(tm, m), min(tn, n)
    offs = [sum(p.shape[1] for p in pieces[:i]) for i in range(n_p)]

    def body(*refs):
        a_refs, b_ref, o_ref = refs[:n_p], refs[n_p], refs[n_p + 1]
        acc = None
        for a_ref, off in zip(a_refs, offs):
            d = _dot(a_ref[...], b_ref[off:off + a_ref.shape[1], :])
            acc = d if acc is None else acc + d
        o_ref[...] = acc

    return _call(
        body, name=name, grid=(m // tm, n // tn),
        in_specs=[pl.BlockSpec((tm, p.shape[1]), lambda i, j: (i, 0)) for p in pieces]
        + [pl.BlockSpec((k, tn), lambda i, j: (0, j))],
        out_specs=pl.BlockSpec((tm, tn), lambda i, j: (i, j)),
        out_shape=jax.ShapeDtypeStruct((m, n), F32),
        compiler_params=_cp(("parallel", "arbitrary")),
    )(*pieces, b)


def _mm_acc(a, b, tk, tn, name):
    m, k = a.shape
    _, n = b.shape
    tk, tn = min(tk, k), min(tn, n)

    def body(a_ref, b_ref, o_ref):
        @pl.when(pl.program_id(1) == 0)
        def _():
            o_ref[...] = jnp.zeros_like(o_ref)

        o_ref[...] += _dot(a_ref[...], b_ref[...])

    return _call(
        body, name=name, grid=(n // tn, k // tk),
        in_specs=[pl.BlockSpec((m, tk), lambda j, i: (0, i)), pl.BlockSpec((tk, tn), lambda j, i: (i, j))],
        out_specs=pl.BlockSpec((m, tn), lambda j, i: (0, j)),
        out_shape=jax.ShapeDtypeStruct((m, n), F32),
        compiler_params=_cp(("parallel", "arbitrary")),
    )(a, b)


def _rms_h(x, g, t, name):
    s = x.shape[0]
    t = min(t, s)

    def body(x_ref, g_ref, h_ref, ht_ref):
        h = _rms(x_ref[...], g_ref[...], D)[0]
        h_ref[...] = h.astype(BF16)
        ht_ref[...] = h.T.astype(BF16)

    return _call(
        body, name=name, grid=(s // t,),
        in_specs=[pl.BlockSpec((t, D), lambda i: (i, 0)), _full((1, D))],
        out_specs=[pl.BlockSpec((t, D), lambda i: (i, 0)), pl.BlockSpec((D, t), lambda i: (0, i))],
        out_shape=[jax.ShapeDtypeStruct((s, D), BF16), jax.ShapeDtypeStruct((D, s), BF16)],
        compiler_params=_cp(("parallel",)),
    )(x, g)


def _mla_heads(pl_blk, c, sn, qng, kvng, qhg, khg, wuq, wukv):
    ql = pl_blk[:, 0:QL].astype(F32)
    kvl = pl_blk[:, QL:QL + KVL].astype(F32)
    kpe = pl_blk[:, QL + KVL:P_LW].astype(F32)
    qn, qxh, qrs = _rms(ql, qng, QL)
    kvn, kvxh, kvrs = _rms(kvl, kvng, KVL)
    qn16, kvn16 = qn.astype(BF16), kvn.astype(BF16)
    qp = _dot(qn16, wuq)
    kvp = _dot(kvn16, wukv)
    return ql, kvl, kpe, qxh, qrs, kvxh, kvrs, qn16, kvn16, qp, kvp


def _mla_prep(proj_l, c, sn, qng, kvng, qhg, khg, wuq, wukv, t, name):
    s = proj_l.shape[0]
    t = min(t, s)

    def body(l_ref, c_ref, sn_ref, qng_ref, kvng_ref, qhg_ref, khg_ref, wuq_ref, wukv_ref, q_ref, k_ref, v_ref):
        cc, ss = c_ref[...], sn_ref[...]
        (_, _, kpe, _, _, _, _, _, _, qp, kvp) = _mla_heads(
            l_ref[...], cc, ss, qng_ref[...], kvng_ref[...], qhg_ref[...], khg_ref[...], wuq_ref[...], wukv_ref[...])
        lane = lax.broadcasted_iota(jnp.int32, cc.shape, 1)
        for h in range(H):
            u = qp[:, h * HP:(h + 1) * HP]
            q_ref[h] = (_rope(_rms(u, qhg_ref[...], QKD)[0], cc, ss) * QSCALE).astype(BF16)
            u = kvp[:, h * HP:(h + 1) * HP] + kpe
            k_ref[h] = _rope(_rms(u, khg_ref[...], QKD)[0], cc, ss).astype(BF16)
            v_ref[h] = jnp.where(lane == VD, 1.0, kvp[:, (H + h) * HP:(H + h + 1) * HP]).astype(BF16)

    hs = pl.BlockSpec((H, t, HP), lambda i: (0, i, 0))
    row = lambda w: pl.BlockSpec((t, w), lambda i: (i, 0))
    return _call(
        body, name=name, grid=(s // t,),
        in_specs=[row(P_LW), row(HP), row(HP), _full((1, QL)), _full((1, KVL)), _full((1, HP)), _full((1, HP)),
                  _full((QL, H * HP)), _full((KVL, 2 * H * HP))],
        out_specs=[hs, hs, hs],
        out_shape=[jax.ShapeDtypeStruct((H, s, HP), BF16)] * 3,
        compiler_params=_cp(("parallel",)),
    )(proj_l, c, sn, qng, kvng, qhg, khg, wuq, wukv)


def _flash_fwd(q, k, v, tq, tk, name):
    _, s, _ = q.shape
    tq, tk = min(tq, s), min(tk, s)
    nk, nc = s // tk, tk // LANES
    un = 8 if nk % 8 == 0 else 1

    def body(q_ref, k_ref, v_ref, o_ref, lse_ref, s_scr):
        qb = q_ref[0]

        def scores(jj, mx):
            for u in range(un):
                j = jj * un + u
                off = pl.multiple_of(j * tk, tk)
                sc = _dot_nt(qb, k_ref[0, pl.ds(off, tk), :])
                s_scr[j] = sc
                for cc in range(nc):
                    mx = jnp.maximum(mx, sc[:, cc * LANES:(cc + 1) * LANES])
            return mx

        mx = lax.fori_loop(0, nk // un, scores, jnp.full((tq, LANES), -jnp.inf, F32))
        m = jnp.max(mx, axis=-1, keepdims=True)
        mb = jnp.broadcast_to(m, (tq, LANES))

        def probs(jj, acc):
            for u in range(un):
                j = jj * un + u
                off = pl.multiple_of(j * tk, tk)
                sc = s_scr[j]
                ps = [jnp.exp2(sc[:, cc * LANES:(cc + 1) * LANES] - mb).astype(BF16) for cc in range(nc)]
                acc = acc + _dot(jnp.concatenate(ps, axis=-1), v_ref[0, pl.ds(off, tk), :])
            return acc

        acc = lax.fori_loop(0, nk // un, probs, jnp.zeros((tq, HP), F32))
        l = acc[:, VD:VD + 1]
        o_ref[0] = (acc[:, :VD] / l).astype(BF16)
        lse = jnp.broadcast_to(m + jnp.log(l) * LOG2E, (tq, LANES))
        lse_ref[0] = lse.T[0:1, :]

    return _call(
        body, name=name, grid=(H, s // tq),
        in_specs=[pl.BlockSpec((1, tq, HP), lambda h, i: (h, i, 0)),
                  pl.BlockSpec((1, s, HP), lambda h, i: (h, 0, 0)),
                  pl.BlockSpec((1, s, HP), lambda h, i: (h, 0, 0))],
        out_specs=[pl.BlockSpec((1, tq, VD), lambda h, i: (h, i, 0)), pl.BlockSpec((1, 1, tq), lambda h, i: (h, 0, i))],
        out_shape=[jax.ShapeDtypeStruct((H, s, VD), BF16), jax.ShapeDtypeStruct((H, 1, s), F32)],
        scratch_shapes=[pltpu.VMEM((nk, tq, tk), F32)],
        compiler_params=_cp(("parallel", "arbitrary")),
    )(q, k, v)


def _mem_prep(mem, mng, wmkv, mkg, name):
    m = mem.shape[0]

    def body(mem_ref, mng_ref, w_ref, mkg_ref, mk_ref, mv_ref):
        mn = _rms(mem_ref[...], mng_ref[...], D)[0].astype(BF16)
        mkv = _dot(mn, w_ref[...])
        for h in range(HM):
            mk_ref[h] = _rms(mkv[:, 2 * h * MHD:(2 * h + 1) * MHD], mkg_ref[...], MHD)[0].astype(BF16)
            mv_ref[h] = mkv[:, (2 * h + 1) * MHD:(2 * h + 2) * MHD].astype(BF16)

    return _call(
        body, name=name,
        in_specs=[_full((m, D)), _full((1, D)), _full((D, 2 * MW)), _full((1, MHD))],
        out_specs=[_full((HM, m, MHD))] * 2,
        out_shape=[jax.ShapeDtypeStruct((HM, m, MHD), BF16)] * 2,
        compiler_params=_cp(),
    )(mem, mng, wmkv, mkg)


def _conv_parts(cv, prev, nxt, first, last, cw, cb):
    t = cv.shape[0]
    c_b, c_c, c_u, g_c = (cv[:, i * CW:(i + 1) * CW].astype(F32) for i in range(4))
    z = c_c * c_u
    zp = jnp.where(first, 0.0, prev[15:16, CW:2 * CW].astype(F32) * prev[15:16, 2 * CW:3 * CW].astype(F32))
    zn = jnp.where(last, 0.0, nxt[0:1, CW:2 * CW].astype(F32) * nxt[0:1, 2 * CW:3 * CW].astype(F32))
    row = lax.broadcasted_iota(jnp.int32, (t, CW), 0)
    z_m1 = jnp.where(row == 0, zp, pltpu.roll(z, 1, 0))
    z_p1 = jnp.where(row == t - 1, zn, pltpu.roll(z, t - 1, 0))
    conv = cw[0:1] * z_m1 + cw[1:2] * z + cw[2:3] * z_p1 + cb
    return c_b, c_c, c_u, g_c, z, z_m1, z_p1, conv


def _mem_attn(qm, mqg, mk_ref, mv_ref):
    outs = []
    for h in range(HM):
        mq, mqxh, mqrs = _rms(qm[:, h * MHD:(h + 1) * MHD], mqg, MHD)
        mq16 = mq.astype(BF16)
        sc = _dot_nt(mq16, mk_ref[h]) * MEM_SCALE
        e = jnp.exp(sc - jnp.max(sc, axis=-1, keepdims=True))
        p = e / jnp.sum(e, axis=-1, keepdims=True)
        o = _dot(p.astype(BF16), mv_ref[h])
        outs.append((mq16, mqxh, mqrs, p, o))
    return outs


def _halo_specs(t, s, width):
    nb = s // 16
    prev = pl.BlockSpec((16, width), lambda i: (jnp.maximum(i * (t // 16) - 1, 0), 0))
    nxt = pl.BlockSpec((16, width), lambda i: (jnp.minimum((i + 1) * (t // 16), nb - 1), 0))
    return prev, nxt


def _branches(proj, o, mk, mv, cw, cb, mqg, t, name):
    s = proj.shape[0]
    t = min(t, s)
    nt = s // t

    def body(cv_ref, pv_ref, nx_ref, mm_ref, ga_ref, o_ref, mk_ref, mv_ref, cw_ref, cb_ref, mqg_ref,
             oa_ref, oc_ref, om_ref):
        i = pl.program_id(0)
        c_b, _, _, g_c, _, _, _, conv = _conv_parts(
            cv_ref[...], pv_ref[...], nx_ref[...], i == 0, i == nt - 1, cw_ref[...], cb_ref[...])
        oc_ref[...] = (c_b * conv * (g_c * _sigmoid(g_c))).astype(BF16)
        ga = ga_ref[...].astype(F32)
        ocat = jnp.concatenate([o_ref[h].astype(F32) for h in range(H)], axis=-1)
        oa_ref[...] = (ocat * (ga * _sigmoid(ga))).astype(BF16)
        mblk = mm_ref[...].astype(F32)
        gm = mblk[:, MW:]
        heads = _mem_attn(mblk[:, :MW], mqg_ref[...], mk_ref, mv_ref)
        om = jnp.concatenate([hh[4] for hh in heads], axis=-1)
        om_ref[...] = (om * (gm * _sigmoid(gm))).astype(BF16)

    pv, nx = _halo_specs(t, s, 4 * CW)
    out = pl.BlockSpec((t, 512), lambda i: (i, 0))
    return _call(
        body, name=name, grid=(nt,),
        in_specs=[pl.BlockSpec((t, 4 * CW), lambda i: (i, 0)), pv, nx,
                  pl.BlockSpec((t, 2 * MW), lambda i: (i, P_MEM // (2 * MW))),
                  pl.BlockSpec((t, AW), lambda i: (i, P_GA // AW)),
                  pl.BlockSpec((H, t, VD), lambda i: (0, i, 0)),
                  _full(mk.shape), _full(mv.shape), _full((3, CW)), _full((1, CW)), _full((1, MHD))],
        out_specs=[out, out, out],
        out_shape=[jax.ShapeDtypeStruct((s, 512), BF16)] * 3,
        compiler_params=_cp(("parallel",)),
    )(proj, proj, proj, proj, proj, o, mk, mv, cw, cb, mqg)


def _merge_fwd(x, proj, oa, oc, om, bg, wa, wc, wm, wo, t, name):
    s = x.shape[0]
    t = min(t, s)

    def body(x_ref, r_ref, oa_ref, oc_ref, om_ref, bg_ref, wa_ref, wc_ref, wm_ref, wo_ref,
             xo_ref, aa_ref, ac_ref, am_ref):
        y = jnp.zeros((t, D), F32)
        for j, (o_ref, w_ref, a_ref) in enumerate(((oa_ref, wa_ref, aa_ref), (oc_ref, wc_ref, ac_ref),
                                                   (om_ref, wm_ref, am_ref))):
            a = _dot(o_ref[...], w_ref[...])
            a_ref[...] = a.astype(BF16)
            rg = _sigmoid(r_ref[:, j * D:(j + 1) * D].astype(F32) + bg_ref[:, j * D:(j + 1) * D])
            y = y + rg * a
        xo_ref[...] = x_ref[...] + _dot(y.astype(BF16), wo_ref[...])

    row = lambda w: pl.BlockSpec((t, w), lambda i: (i, 0))
    return _call(
        body, name=name, grid=(s // t,),
        in_specs=[row(D), pl.BlockSpec((t, 3 * D), lambda i: (i, P_R // (3 * D))), row(512), row(512), row(512),
                  _full((1, 3 * D)), _full((512, D)), _full((512, D)), _full((512, D)), _full((D, D))],
        out_specs=[row(D), row(D), row(D), row(D)],
        out_shape=[jax.ShapeDtypeStruct((s, D), F32)] + [jax.ShapeDtypeStruct((s, D), BF16)] * 3,
        compiler_params=_cp(("parallel",)),
    )(x, proj, oa, oc, om, bg, wa, wc, wm, wo)


def _loss_grad(x, tgt, t, name):
    s = x.shape[0]
    t = min(t, s)

    def body(x_ref, t_ref, g_ref, l_ref):
        @pl.when(pl.program_id(0) == 0)
        def _():
            l_ref[...] = jnp.zeros_like(l_ref)

        e = x_ref[...] - t_ref[...]
        g_ref[...] = e * (1.0 / D)
        sq = e * e
        part = sq[:, 0:LANES]
        for j in range(1, D // LANES):
            part = part + sq[:, j * LANES:(j + 1) * LANES]
        acc = part[0:8]
        for j in range(1, t // 8):
            acc = acc + part[j * 8:(j + 1) * 8]
        l_ref[...] += acc * (0.5 / D)

    row = pl.BlockSpec((t, D), lambda i: (i, 0))
    return _call(
        body, name=name, grid=(s // t,),
        in_specs=[row, row], out_specs=[row, _full((8, LANES))],
        out_shape=[jax.ShapeDtypeStruct((s, D), F32), jax.ShapeDtypeStruct((8, LANES), F32)],
        compiler_params=_cp(("arbitrary",)),
    )(x, tgt)


def _merge_bwd(g, proj, aa, ac, am, bg, wot, wat, wct, wmt, t, name):
    s = g.shape[0]
    t = min(t, s)

    def body(g_ref, r_ref, aa_ref, ac_ref, am_ref, bg_ref, wot_ref, wat_ref, wct_ref, wmt_ref,
             y_ref, daa_ref, dac_ref, dam_ref, dr_ref, doa_ref, doc_ref, dom_ref, dbg_ref):
        @pl.when(pl.program_id(0) == 0)
        def _():
            dbg_ref[...] = jnp.zeros_like(dbg_ref)

        dy = _dot(g_ref[...].astype(BF16), wot_ref[...])
        y = jnp.zeros((t, D), F32)
        for j, (a_ref, wt_ref, da_ref, do_ref) in enumerate(((aa_ref, wat_ref, daa_ref, doa_ref),
                                                             (ac_ref, wct_ref, dac_ref, doc_ref),
                                                             (am_ref, wmt_ref, dam_ref, dom_ref))):
            a = a_ref[...].astype(F32)
            rg = _sigmoid(r_ref[:, j * D:(j + 1) * D].astype(F32) + bg_ref[:, j * D:(j + 1) * D])
            y = y + rg * a
            da = (dy * rg).astype(BF16)
            da_ref[...] = da
            dr = dy * a * rg * (1.0 - rg)
            dr_ref[:, j * D:(j + 1) * D] = dr.astype(BF16)
            dbg_ref[:, j * D:(j + 1) * D] += jnp.sum(dr, axis=0, keepdims=True)
            do_ref[...] = _dot(da, wt_ref[...])
        y_ref[...] = y.astype(BF16)

    row = lambda w: pl.BlockSpec((t, w), lambda i: (i, 0))
    return _call(
        body, name=name, grid=(s // t,),
        in_specs=[row(D), pl.BlockSpec((t, 3 * D), lambda i: (i, P_R // (3 * D))), row(D), row(D), row(D),
                  _full((1, 3 * D)), _full((D, D)), _full((D, 512)), _full((D, 512)), _full((D, 512))],
        out_specs=[row(D), row(D), row(D), row(D), row(3 * D), row(512), row(512), row(512), _full((1, 3 * D))],
        out_shape=[jax.ShapeDtypeStruct((s, D), BF16)] * 4 + [jax.ShapeDtypeStruct((s, 3 * D), BF16)]
        + [jax.ShapeDtypeStruct((s, 512), F32)] * 3 + [jax.ShapeDtypeStruct((1, 3 * D), F32)],
        compiler_params=_cp(("arbitrary",)),
    )(g, proj, aa, ac, am, bg, wot, wat, wct, wmt)


def _branches_bwd(proj, o, mk, mv, cw, cb, mqg, doa, doc, dom, t, name):
    s = proj.shape[0]
    t = min(t, s)
    nt = s // t
    m = mk.shape[1]

    def body(cv_ref, pv_ref, nx_ref, mm_ref, ga_ref, o_ref, mk_ref, mv_ref, cw_ref, cb_ref, mqg_ref,
             doa_ref, doc_ref, dcp_ref, dcn_ref, dom_ref,
             dcv_ref, dmm_ref, dga_ref, do_ref, dl_ref, dcw_ref, dcb_ref, dmk_ref, dmv_ref, dmqg_ref):
        i = pl.program_id(0)

        @pl.when(i == 0)
        def _():
            for r in (dcw_ref, dcb_ref, dmk_ref, dmv_ref, dmqg_ref):
                r[...] = jnp.zeros_like(r)

        first, last = i == 0, i == nt - 1
        cw_, cb_ = cw_ref[...], cb_ref[...]
        pv, nx = pv_ref[...], nx_ref[...]
        c_b, c_c, c_u, g_c, z, z_m1, z_p1, conv = _conv_parts(cv_ref[...], pv, nx, first, last, cw_, cb_)
        sg = _sigmoid(g_c)
        silu = g_c * sg
        dsilu = sg * (1.0 + g_c * (1.0 - sg))
        doc_ = doc_ref[...]
        dconv = doc_ * c_b * silu
        gp = pv[15:16, 3 * CW:4 * CW].astype(F32)
        gn = nx[0:1, 3 * CW:4 * CW].astype(F32)
        dconv_p = jnp.where(first, 0.0, dcp_ref[15:16, :] * pv[15:16, 0:CW].astype(F32) * (gp * _sigmoid(gp)))
        dconv_n = jnp.where(last, 0.0, dcn_ref[0:1, :] * nx[0:1, 0:CW].astype(F32) * (gn * _sigmoid(gn)))
        row = lax.broadcasted_iota(jnp.int32, (t, CW), 0)
        d_m1 = jnp.where(row == 0, dconv_p, pltpu.roll(dconv, 1, 0))
        d_p1 = jnp.where(row == t - 1, dconv_n, pltpu.roll(dconv, t - 1, 0))
        dz = cw_[0:1] * d_p1 + cw_[1:2] * dconv + cw_[2:3] * d_m1
        dcv_ref[:, 0:CW] = (doc_ * conv * silu).astype(BF16)
        dcv_ref[:, CW:2 * CW] = (dz * c_u).astype(BF16)
        dcv_ref[:, 2 * CW:3 * CW] = (dz * c_c).astype(BF16)
        dcv_ref[:, 3 * CW:4 * CW] = (doc_ * c_b * conv * dsilu).astype(BF16)
        dcw_ref[0:1, :] += jnp.sum(dconv * z_m1, axis=0, keepdims=True)
        dcw_ref[1:2, :] += jnp.sum(dconv * z, axis=0, keepdims=True)
        dcw_ref[2:3, :] += jnp.sum(dconv * z_p1, axis=0, keepdims=True)
        dcb_ref[...] += jnp.sum(dconv, axis=0, keepdims=True)
        ga = ga_ref[...].astype(F32)
        sg = _sigmoid(ga)
        doa_ = doa_ref[...]
        ocat = jnp.concatenate([o_ref[h].astype(F32) for h in range(H)], axis=-1)
        dga_ref[...] = (doa_ * ocat * (sg * (1.0 + ga * (1.0 - sg)))).astype(BF16)
        dog = doa_ * (ga * sg)
        zeros = jnp.zeros((t, HP - VD), F32)
        lane = lax.broadcasted_iota(jnp.int32, (t, LANES), 1)
        dmat = jnp.zeros((t, LANES), F32)
        for h in range(H):
            dh = dog[:, h * VD:(h + 1) * VD]
            do_ref[h] = jnp.concatenate([dh, zeros], axis=-1).astype(BF16)
            dmat = jnp.where(lane == h, jnp.sum(dh * ocat[:, h * VD:(h + 1) * VD], axis=-1, keepdims=True), dmat)
        dlt = dmat.T
        for h in range(H):
            dl_ref[h] = dlt[h:h + 1, :]
        mblk = mm_ref[...].astype(F32)
        gm = mblk[:, MW:]
        sg = _sigmoid(gm)
        dom_ = dom_ref[...]
        heads = _mem_attn(mblk[:, :MW], mqg_ref[...], mk_ref, mv_ref)
        om = jnp.concatenate([hh[4] for hh in heads], axis=-1)
        dmm_ref[:, MW:] = (dom_ * om * (sg * (1.0 + gm * (1.0 - sg)))).astype(BF16)
        dmo = dom_ * (gm * sg)
        dmqg = jnp.zeros((1, MHD), F32)
        for h in range(HM):
            mq16, mqxh, mqrs, p, _ = heads[h]
            dmo_h = dmo[:, h * MHD:(h + 1) * MHD].astype(BF16)
            dp = _dot_nt(dmo_h, mv_ref[h])
            ds = (p * (dp - jnp.sum(dp * p, axis=-1, keepdims=True)) * MEM_SCALE).astype(BF16)
            dmq = _dot(ds, mk_ref[h])
            dmk_ref[h] += _dot_tn(ds, mq16)
            dmv_ref[h] += _dot_tn(p.astype(BF16), dmo_h)
            dq, dg = _rms_bwd(dmq, mqxh, mqrs, mqg_ref[...], MHD)
            dmm_ref[:, h * MHD:(h + 1) * MHD] = dq.astype(BF16)
            dmqg = dmqg + dg
        dmqg_ref[...] += dmqg

    pv, nx = _halo_specs(t, s, 4 * CW)
    dpv, dnx = _halo_specs(t, s, CW)
    row = lambda w: pl.BlockSpec((t, w), lambda i: (i, 0))
    hs = lambda w: pl.BlockSpec((H, t, w), lambda i: (0, i, 0))
    return _call(
        body, name=name, grid=(nt,),
        in_specs=[pl.BlockSpec((t, 4 * CW), lambda i: (i, 0)), pv, nx,
                  pl.BlockSpec((t, 2 * MW), lambda i: (i, P_MEM // (2 * MW))),
                  pl.BlockSpec((t, AW), lambda i: (i, P_GA // AW)),
                  hs(VD), _full(mk.shape), _full(mv.shape), _full((3, CW)), _full((1, CW)), _full((1, MHD)),
                  row(512), row(512), dpv, dnx, row(512)],
        out_specs=[row(4 * CW), row(2 * MW), row(AW), hs(HP), pl.BlockSpec((H, 1, t), lambda i: (0, 0, i)),
                   _full((3, CW)), _full((1, CW)),
                   _full((HM, m, MHD)), _full((HM, m, MHD)), _full((1, MHD))],
        out_shape=[jax.ShapeDtypeStruct((s, 4 * CW), BF16), jax.ShapeDtypeStruct((s, 2 * MW), BF16),
                   jax.ShapeDtypeStruct((s, AW), BF16), jax.ShapeDtypeStruct((H, s, HP), BF16),
                   jax.ShapeDtypeStruct((H, 1, s), F32), jax.ShapeDtypeStruct((3, CW), F32),
                   jax.ShapeDtypeStruct((1, CW), F32), jax.ShapeDtypeStruct((HM, m, MHD), F32),
                   jax.ShapeDtypeStruct((HM, m, MHD), F32), jax.ShapeDtypeStruct((1, MHD), F32)],
        compiler_params=_cp(("arbitrary",)),
    )(proj, proj, proj, proj, proj, o, mk, mv, cw, cb, mqg, doa, doc, doc, doc, dom)


def _flash_bwd(q, k, v, do, lse, dl, tq, tk, name):
    _, s, _ = q.shape
    tq, tk = min(tq, s), min(tk, s)
    nq, nkt, nc = s // tq, s // tk, tk // LANES
    unroll = 4 if nq % 4 == 0 else 1

    def body(q_ref, do_ref, lse_ref, dl_ref, k_ref, v_ref, dq_ref, dk_ref, dv_ref, dq_acc):
        j = pl.program_id(1)

        @pl.when(j == 0)
        def _():
            dq_acc[...] = jnp.zeros_like(dq_acc)

        kb, vb = k_ref[0], v_ref[0]

        def step(ii, carry):
            dkt, dvt = carry
            for u in range(unroll):
                i = ii * unroll + u
                off = pl.multiple_of(i * tq, tq)
                qb = q_ref[0, pl.ds(off, tq), :]
                dob = do_ref[0, pl.ds(off, tq), :]
                lse_b = jnp.broadcast_to(lse_ref[0, i], (LANES, tq)).T
                dl_b = jnp.broadcast_to(dl_ref[0, i], (LANES, tq)).T
                sc = _dot_nt(qb, kb)
                dp = _dot_nt(dob, vb)
                ps, dss = [], []
                for cc in range(nc):
                    p = jnp.exp2(sc[:, cc * LANES:(cc + 1) * LANES] - lse_b)
                    ps.append(p.astype(BF16))
                    dss.append((p * (dp[:, cc * LANES:(cc + 1) * LANES] - dl_b)).astype(BF16))
                p16, ds16 = jnp.concatenate(ps, axis=-1), jnp.concatenate(dss, axis=-1)
                dvt = dvt + _dot_tn(dob, p16)
                dkt = dkt + _dot_tn(qb, ds16)
                dq_acc[pl.ds(off, tq), :] += _dot(ds16, kb)
            return dkt, dvt

        dkt, dvt = lax.fori_loop(0, nq // unroll, step, (jnp.zeros((HP, tk), F32), jnp.zeros((HP, tk), F32)))
        dk_ref[0] = (dkt.T * (1.0 / LOG2E)).astype(BF16)
        dv_ref[0] = dvt.T.astype(BF16)

        @pl.when(j == nkt - 1)
        def _():
            dq_ref[0] = (dq_acc[...] * ATT_SCALE).astype(BF16)

    whole = pl.BlockSpec((1, s, HP), lambda h, j: (h, 0, 0))
    stat = pl.BlockSpec((1, nq, 1, tq), lambda h, j: (h, 0, 0, 0))
    tile = pl.BlockSpec((1, tk, HP), lambda h, j: (h, j, 0))
    return _call(
        body, name=name, grid=(H, nkt),
        in_specs=[whole, whole, stat, stat, tile, tile],
        out_specs=[whole, tile, tile],
        out_shape=[jax.ShapeDtypeStruct((H, s, HP), BF16)] * 3,
        scratch_shapes=[pltpu.VMEM((s, HP), F32)],
        compiler_params=_cp(("arbitrary", "arbitrary")),
    )(q, do, lse, dl, k, v)


def _mla_prep_bwd(proj_l, c, sn, qng, kvng, qhg, khg, wuq, wukv, wuqt, wukvt, dq, dk, dv, t, name):
    s = proj_l.shape[0]
    t = min(t, s)

    def body(l_ref, c_ref, sn_ref, qng_ref, kvng_ref, qhg_ref, khg_ref, wuq_ref, wukv_ref, wuqt_ref, wukvt_ref,
             dq_ref, dk_ref, dv_ref, dl_ref, dwuq_ref, dwukv_ref, dqng_ref, dkvng_ref, dqhg_ref, dkhg_ref):
        @pl.when(pl.program_id(0) == 0)
        def _():
            for r in (dwuq_ref, dwukv_ref, dqng_ref, dkvng_ref, dqhg_ref, dkhg_ref):
                r[...] = jnp.zeros_like(r)

        cc, ss = c_ref[...], sn_ref[...]
        qhg, khg = qhg_ref[...], khg_ref[...]
        (_, _, kpe, qxh, qrs, kvxh, kvrs, qn16, kvn16, qp, kvp) = _mla_heads(
            l_ref[...], cc, ss, qng_ref[...], kvng_ref[...], qhg, khg, wuq_ref[...], wukv_ref[...])
        lane = lax.broadcasted_iota(jnp.int32, (t, HP), 1)
        dqp, dkp, dvp = [], [], []
        dkpe = jnp.zeros((t, HP), F32)
        dqhg = jnp.zeros((1, HP), F32)
        dkhg = jnp.zeros((1, HP), F32)
        for h in range(H):
            _, xh, rs = _rms(qp[:, h * HP:(h + 1) * HP], qhg, QKD)
            du, dg = _rms_bwd(_rope_adj(dq_ref[h].astype(F32), cc, ss), xh, rs, qhg, QKD)
            dqp.append(du)
            dqhg = dqhg + dg
            _, xh, rs = _rms(kvp[:, h * HP:(h + 1) * HP] + kpe, khg, QKD)
            du, dg = _rms_bwd(_rope_adj(dk_ref[h].astype(F32), cc, ss), xh, rs, khg, QKD)
            dkp.append(du)
            dkhg = dkhg + dg
            dkpe = dkpe + jnp.where((lane >= NOPE) & (lane < QKD), du, 0.0)
            dvp.append(dv_ref[h].astype(F32))
        dqhg_ref[...] += dqhg
        dkhg_ref[...] += dkhg
        dqp16 = jnp.concatenate(dqp, axis=-1).astype(BF16)
        dkvp16 = jnp.concatenate(dkp + dvp, axis=-1).astype(BF16)
        dwuq_ref[...] += _dot_tn(qn16, dqp16)
        dwukv_ref[...] += _dot_tn(kvn16, dkvp16)
        dql, dg = _rms_bwd(_dot(dqp16, wuqt_ref[...]), qxh, qrs, qng_ref[...], QL)
        dqng_ref[...] += dg
        dkvl, dg = _rms_bwd(_dot(dkvp16, wukvt_ref[...]), kvxh, kvrs, kvng_ref[...], KVL)
        dkvng_ref[...] += dg
        dl_ref[:, 0:QL] = dql.astype(BF16)
        dl_ref[:, QL:QL + KVL] = dkvl.astype(BF16)
        dl_ref[:, QL + KVL:P_LW] = dkpe.astype(BF16)

    hs = pl.BlockSpec((H, t, HP), lambda i: (0, i, 0))
    row = lambda w: pl.BlockSpec((t, w), lambda i: (i, 0))
    return _call(
        body, name=name, grid=(s // t,),
        in_specs=[row(P_LW), row(HP), row(HP), _full((1, QL)), _full((1, KVL)), _full((1, HP)), _full((1, HP)),
                  _full((QL, H * HP)), _full((KVL, 2 * H * HP)), _full((H * HP, QL)), _full((2 * H * HP, KVL)),
                  hs, hs, hs],
        out_specs=[row(P_LW), _full((QL, H * HP)), _full((KVL, 2 * H * HP)), _full((1, QL)), _full((1, KVL)),
                   _full((1, HP)), _full((1, HP))],
        out_shape=[jax.ShapeDtypeStruct((s, P_LW), BF16), jax.ShapeDtypeStruct((QL, H * HP), F32),
                   jax.ShapeDtypeStruct((KVL, 2 * H * HP), F32), jax.ShapeDtypeStruct((1, QL), F32),
                   jax.ShapeDtypeStruct((1, KVL), F32), jax.ShapeDtypeStruct((1, HP), F32),
                   jax.ShapeDtypeStruct((1, HP), F32)],
        compiler_params=_cp(("arbitrary",)),
    )(proj_l, c, sn, qng, kvng, qhg, khg, wuq, wukv, wuqt, wukvt, dq, dk, dv)


def _mem_prep_bwd(mem, mng, wmkv, wmkvt, mkg, dmk, dmv, name):
    m = mem.shape[0]

    def body(mem_ref, mng_ref, w_ref, wt_ref, mkg_ref, dmk_ref, dmv_ref, dw_ref, dmng_ref, dmkg_ref):
        mn, xh, _ = _rms(mem_ref[...], mng_ref[...], D)
        mn16 = mn.astype(BF16)
        mkv = _dot(mn16, w_ref[...])
        parts = []
        dmkg = jnp.zeros((1, MHD), F32)
        for h in range(HM):
            _, kxh, krs = _rms(mkv[:, 2 * h * MHD:(2 * h + 1) * MHD], mkg_ref[...], MHD)
            du, dg = _rms_bwd(dmk_ref[h], kxh, krs, mkg_ref[...], MHD)
            dmkg = dmkg + dg
            parts += [du, dmv_ref[h]]
        dmkv = jnp.concatenate(parts, axis=-1).astype(BF16)
        dw_ref[...] = _dot_tn(mn16, dmkv)
        dmn = _dot(dmkv, wt_ref[...])
        dmng_ref[...] = jnp.sum(dmn * xh, axis=0, keepdims=True)
        dmkg_ref[...] = dmkg

    return _call(
        body, name=name,
        in_specs=[_full((m, D)), _full((1, D)), _full((D, 2 * MW)), _full((2 * MW, D)), _full((1, MHD)),
                  _full((HM, m, MHD)), _full((HM, m, MHD))],
        out_specs=[_full((D, 2 * MW)), _full((1, D)), _full((1, MHD))],
        out_shape=[jax.ShapeDtypeStruct((D, 2 * MW), F32), jax.ShapeDtypeStruct((1, D), F32),
                   jax.ShapeDtypeStruct((1, MHD), F32)],
        compiler_params=_cp(),
    )(mem, mng, wmkv, wmkvt, mkg, dmk, dmv)


def _rms_in_bwd(x, g_out, dh, ng, t, name):
    s = x.shape[0]
    t = min(t, s)

    def body(x_ref, go_ref, dh_ref, ng_ref, dx_ref, dng_ref):
        @pl.when(pl.program_id(0) == 0)
        def _():
            dng_ref[...] = jnp.zeros_like(dng_ref)

        _, xh, rs = _rms(x_ref[...], ng_ref[...], D)
        dx, dg = _rms_bwd(dh_ref[...], xh, rs, ng_ref[...], D)
        dx_ref[...] = go_ref[...] + dx
        dng_ref[...] += dg

    row = pl.BlockSpec((t, D), lambda i: (i, 0))
    return _call(
        body, name=name, grid=(s // t,),
        in_specs=[row, row, row, _full((1, D))], out_specs=[row, _full((1, D))],
        out_shape=[jax.ShapeDtypeStruct((s, D), F32), jax.ShapeDtypeStruct((1, D), F32)],
        compiler_params=_cp(("arbitrary",)),
    )(x, g_out, dh, ng)


def _allgather(arrs, name):
    n = len(arrs)

    def body(*refs):
        x_refs, out_refs = refs[:n], refs[n:2 * n]
        send_sems, recv_sems, local_sems = refs[2 * n:]
        x, y, c = lax.axis_index("x"), lax.axis_index("y"), lax.axis_index("c")
        me, sibling = (x, y, c), (x, y, 1 - c)
        chips = [(1 - x, y), (x, 1 - y), (1 - x, 1 - y)]

        def slot(a, px, py, pc):
            return out_refs[a].at[4 * px + 2 * py + pc]

        def copy(a, k, block, to, src=None):
            return pltpu.make_async_remote_copy(
                src_ref=slot(a, *block) if src is None else src, dst_ref=slot(a, *block),
                send_sem=send_sems.at[a, k], recv_sem=recv_sems.at[a, k],
                device_id=to, device_id_type=pl.DeviceIdType.MESH)

        mine = [pltpu.make_async_copy(x_refs[a], slot(a, *me), local_sems.at[a]) for a in range(n)]
        first, passed = [], []
        for a in range(n):
            mine[a].start()
            first.append(copy(a, 0, me, sibling, src=x_refs[a]))
            first += [copy(a, 1 + j, me, (*chip, c), src=x_refs[a]) for j, chip in enumerate(chips)]
        for cp in first:
            cp.start()
        for j, chip in enumerate(chips):
            for a in range(n):
                copy(a, 1 + j, (*chip, c), me).wait_recv()
                passed.append(copy(a, 4 + j, (*chip, c), sibling))
                passed[-1].start()
        for a in range(n):
            copy(a, 0, sibling, me).wait_recv()
        for j, chip in enumerate(chips):
            for a in range(n):
                copy(a, 4 + j, (*chip, 1 - c), me).wait_recv()
        for cp in first + passed:
            cp.wait_send()
        for cp in mine:
            cp.wait()

    any_spec = pl.BlockSpec(memory_space=pl.ANY)
    return _call(
        body, name=name,
        in_specs=[any_spec] * n, out_specs=[any_spec] * n,
        out_shape=[jax.ShapeDtypeStruct((N_DEV,) + a.shape, a.dtype) for a in arrs],
        scratch_shapes=[pltpu.SemaphoreType.DMA((n, 7)), pltpu.SemaphoreType.DMA((n, 7)),
                        pltpu.SemaphoreType.DMA((n,))],
    )(*arrs)


def _alltoall(arrs, name):
    n = len(arrs)

    def body(*refs):
        x_refs, out_refs = refs[:n], refs[n:2 * n]
        send_sems, recv_sems, local_sems = refs[2 * n:]
        x, y, c = lax.axis_index("x"), lax.axis_index("y"), lax.axis_index("c")
        me = 4 * x + 2 * y + c
        mine = [pltpu.make_async_copy(x_refs[a].at[me], out_refs[a].at[me], local_sems.at[a]) for a in range(n)]
        for cp in mine:
            cp.start()
        copies = []
        for k in range(1, N_DEV):
            bx, by, bc = (k >> 2) & 1, (k >> 1) & 1, k & 1
            px = 1 - x if bx else x
            py = 1 - y if by else y
            pc = 1 - c if bc else c
            for a in range(n):
                copies.append(pltpu.make_async_remote_copy(
                    src_ref=x_refs[a].at[4 * px + 2 * py + pc], dst_ref=out_refs[a].at[me],
                    send_sem=send_sems.at[a, k - 1], recv_sem=recv_sems.at[a, k - 1],
                    device_id=(px, py, pc), device_id_type=pl.DeviceIdType.MESH))
        for cp in copies:
            cp.start()
        for cp in copies:
            cp.wait_recv()
        for cp in copies:
            cp.wait_send()
        for cp in mine:
            cp.wait()

    any_spec = pl.BlockSpec(memory_space=pl.ANY)
    return _call(
        body, name=name,
        in_specs=[any_spec] * n, out_specs=[any_spec] * n,
        out_shape=[jax.ShapeDtypeStruct(a.shape, a.dtype) for a in arrs],
        scratch_shapes=[pltpu.SemaphoreType.DMA((n, 7)), pltpu.SemaphoreType.DMA((n, 7)),
                        pltpu.SemaphoreType.DMA((n,))],
    )(*arrs)


ADAMW_BLOCK_BYTES = 4 * 1024 * 1024


def _adamw(parts, w, m, v, name):
    r, c_ = w.shape
    cpad = -(-c_ // LANES) * LANES
    tr = r
    while N_DEV * tr * cpad * 4 > ADAMW_BLOCK_BYTES and tr % 16 == 0:
        tr //= 2
    c1 = 1.0 / (1.0 - ADAM_B1 ** ADAM_STEP)
    c2 = 1.0 / (1.0 - ADAM_B2 ** ADAM_STEP)

    def body(p_ref, w_ref, m_ref, v_ref, g_ref, d_ref, nm_ref, nv_ref):
        g = p_ref[0].astype(F32)
        for j in range(1, N_DEV):
            g = g + p_ref[j].astype(F32)
        nm = ADAM_B1 * m_ref[...] + (1.0 - ADAM_B1) * g
        nv = ADAM_B2 * v_ref[...] + (1.0 - ADAM_B2) * (g * g)
        g_ref[...] = g
        nm_ref[...] = nm
        nv_ref[...] = nv
        d_ref[...] = -ADAM_LR * ((nm * c1) / (jnp.sqrt(nv * c2) + ADAM_EPS) + ADAM_WD * w_ref[...])

    row = pl.BlockSpec((tr, c_), lambda i: (i, 0))
    return _call(
        body, name=name, grid=(r // tr,),
        in_specs=[pl.BlockSpec((N_DEV, tr, c_), lambda i: (0, i, 0)), row, row, row],
        out_specs=[row] * 4, out_shape=[jax.ShapeDtypeStruct((r, c_), F32)] * 4,
        compiler_params=_cp(("parallel",)),
    )(parts, w, m, v)


def _to_rows(flat, align=8):
    n = flat.shape[-1]
    rows = -(-n // (LANES * align)) * align
    return jnp.pad(flat, (0, rows * LANES - n)).reshape(rows, LANES)


def _shard_blocks(full, axis):
    r, c_ = full.shape
    if axis == 0:
        return full.reshape(N_DEV, r // N_DEV, c_)
    return full.reshape(r, N_DEV, c_ // N_DEV).transpose(1, 0, 2)


def _unshard_blocks(blocks, axis):
    _, r, c_ = blocks.shape
    if axis == 0:
        return blocks.reshape(N_DEV * r, c_)
    return blocks.transpose(1, 0, 2).reshape(r, N_DEV * c_)


def _pad_heads(w, width):
    k = w.shape[0]
    return jnp.pad(w.reshape(k, H, width), ((0, 0), (0, 0), (0, HP - width))).reshape(k, H * HP)


def _unpad_heads(w, width):
    k = w.shape[0]
    return w.reshape(k, H, HP)[:, :, :width].reshape(k, H * width)


IN_SPLIT = {'q_lat': (0, 384), 'kv_lat': (384, 640), 'k_pe': (640, 672), 'c_b': (672, 1184), 'c_c': (1184, 1696),
            'c_u': (1696, 2208), 'q_mem': (2208, 2720), 'g_attn': (2720, 3232), 'g_conv': (3232, 3744),
            'g_mem': (3744, 4256), 'r': (4256, 7328)}
P_ORDER = ['c_b', 'c_c', 'c_u', 'g_conv', 'q_mem', 'g_mem', 'r', 'g_attn', 'q_lat', 'kv_lat']


def _permute_w_in(w):
    k = w.shape[0]
    cols = [w[:, IN_SPLIT[n][0]:IN_SPLIT[n][1]] for n in P_ORDER]
    kpe = w[:, IN_SPLIT['k_pe'][0]:IN_SPLIT['k_pe'][1]]
    cols += [jnp.zeros((k, NOPE), w.dtype), kpe, jnp.zeros((k, HP - QKD), w.dtype)]
    return jnp.concatenate(cols, axis=1)


def _unpermute_w_in(pieces):
    bounds, off = [], 0
    for p in pieces:
        bounds.append((off, off + p.shape[1]))
        off += p.shape[1]

    def cols(lo, hi):
        for p, (b0, b1) in zip(pieces, bounds):
            if b0 <= lo and hi <= b1:
                return p[:, lo - b0:hi - b0]
        raise ValueError("a column range straddles two pieces")

    off, pos = 0, {}
    for n in P_ORDER:
        wd = IN_SPLIT[n][1] - IN_SPLIT[n][0]
        pos[n] = (off, off + wd)
        off += wd
    pos['k_pe'] = (off + NOPE, off + QKD)
    order = sorted(IN_SPLIT, key=lambda n: IN_SPLIT[n][0])
    return jnp.concatenate([cols(*pos[n]) for n in order], axis=1)


def _layer_weights(full, l):
    w = {}
    w_in_p = _permute_w_in(full['w_in'][l])
    w['w_main'] = w_in_p[:, :P_MAIN]
    w['w_l'] = w_in_p[:, P_MAIN:]
    w['w_in_t'] = w_in_p.T
    w['w_uq'] = _pad_heads(full['w_uq'][l], QKD)
    wukv = full['w_ukv'][l].reshape(KVL, H, 2, NOPE)
    kpart = jnp.pad(wukv[:, :, 0, :], ((0, 0), (0, 0), (0, HP - NOPE))).reshape(KVL, H * HP)
    vpart = jnp.pad(wukv[:, :, 1, :], ((0, 0), (0, 0), (0, HP - VD))).reshape(KVL, H * HP)
    w['w_ukv'] = jnp.concatenate([kpart, vpart], axis=1)
    w['w_uq_t'] = w['w_uq'].T
    w['w_ukv_t'] = w['w_ukv'].T
    w['w_mkv'] = full['w_mkv'][l]
    w['w_mkv_t'] = w['w_mkv'].T
    for n in ('w_br_attn', 'w_br_conv', 'w_br_mem', 'w_out'):
        w[n] = full[n][l]
        w[n + '_t'] = w[n].T
    return w


def kernel(x, mem, positions, norm_g, w_in, b_gate, q_norm_g, w_uq, kv_norm_g, w_ukv, q_head_g, k_head_g, conv_w, conv_b, mem_norm_g, w_mkv, mem_q_g, mem_k_g, w_br_attn, w_br_conv, w_br_mem, w_out, loss_target, m_norm_g, m_w_in, m_b_gate, m_q_norm_g, m_w_uq, m_kv_norm_g, m_w_ukv, m_q_head_g, m_k_head_g, m_conv_w, m_conv_b, m_mem_norm_g, m_w_mkv, m_mem_q_g, m_mem_k_g, m_w_br_attn, m_w_br_conv, m_w_br_mem, m_w_out, v_norm_g, v_w_in, v_b_gate, v_q_norm_g, v_w_uq, v_kv_norm_g, v_w_ukv, v_q_head_g, v_k_head_g, v_conv_w, v_conv_b, v_mem_norm_g, v_w_mkv, v_mem_q_g, v_mem_k_g, v_w_br_attn, v_w_br_conv, v_w_br_mem, v_w_out):
    a = dict(zip(INPUTS, (x, mem, positions, norm_g, w_in, b_gate, q_norm_g, w_uq, kv_norm_g, w_ukv, q_head_g, k_head_g, conv_w, conv_b, mem_norm_g, w_mkv, mem_q_g, mem_k_g, w_br_attn, w_br_conv, w_br_mem, w_out, loss_target, m_norm_g, m_w_in, m_b_gate, m_q_norm_g, m_w_uq, m_kv_norm_g, m_w_ukv, m_q_head_g, m_k_head_g, m_conv_w, m_conv_b, m_mem_norm_g, m_w_mkv, m_mem_q_g, m_mem_k_g, m_w_br_attn, m_w_br_conv, m_w_br_mem, m_w_out, v_norm_g, v_w_in, v_b_gate, v_q_norm_g, v_w_uq, v_kv_norm_g, v_w_ukv, v_q_head_g, v_k_head_g, v_conv_w, v_conv_b, v_mem_norm_g, v_w_mkv, v_mem_q_g, v_mem_k_g, v_w_br_attn, v_w_br_conv, v_w_br_mem, v_w_out)))
    x = a['x'][0]
    mem = a['mem'][0]
    tgt = a['loss_target'][0]
    s = x.shape[0]
    t_el = 256
    t_br = 512
    tq_f, tk_f, tq_b, tk_b = 256, 1024, 512, 512

    gathered = _allgather([a[n].astype(BF16) for n in BIG_ORDER] + [a['conv_w']], "ag_weights")
    full = {n: [_unshard_blocks(g8[:, l], BIG[n][1]) for l in range(DEPTH)] for n, g8 in zip(BIG_ORDER, gathered)}
    conv_w = gathered[-1].transpose(1, 2, 0, 3).reshape(DEPTH, 3, CW)

    inv_freq = ROPE_BASE ** (-jnp.arange(0, RP, 2, dtype=F32) / RP)
    ang = a['positions'][0].astype(F32)[:, None] * inv_freq
    cos, sin = jnp.cos(ang), jnp.sin(ang)
    rc = jnp.concatenate([jnp.ones((s, NOPE), F32), cos, cos, jnp.ones((s, HP - QKD), F32)], axis=1)
    rs = jnp.concatenate([jnp.zeros((s, NOPE), F32), -sin, sin, jnp.zeros((s, HP - QKD), F32)], axis=1)

    def small(n, l, width=None):
        v = a[n][l][None, :]
        return v if width is None else jnp.pad(v, ((0, 0), (0, width - v.shape[1])))

    saved = []
    layer_w = [_layer_weights(full, l) for l in range(DEPTH)]
    for l in range(DEPTH):
        w = layer_w[l]
        tag = ""
        h, ht = _rms_h(x, small('norm_g', l), 512, "rms_h" + tag)
        proj = _mm(h, w['w_main'], BF16, 2048, 512, "in_proj" + tag)
        proj_l = _mm(h, w['w_l'], BF16, 512, P_LW, "in_proj_lat" + tag)
        qng, kvng = small('q_norm_g', l), small('kv_norm_g', l)
        qhg, khg = small('q_head_g', l, HP), small('k_head_g', l, HP)
        q, k, v = _mla_prep(proj_l, rc, rs, qng, kvng, qhg, khg, w['w_uq'], w['w_ukv'], t_br, "mla_prep" + tag)
        o, lse = _flash_fwd(q, k, v, tq_f, tk_f, "flash_fwd" + tag)
        mk, mv = _mem_prep(mem, small('mem_norm_g', l), w['w_mkv'], small('mem_k_g', l), "mem_prep" + tag)
        cb, mqg = small('conv_b', l), small('mem_q_g', l)
        oa, oc, om = _branches(proj, o, mk, mv, conv_w[l], cb, mqg, t_br, "branches" + tag)
        x_out, aa, ac, am = _merge_fwd(x, proj, oa, oc, om, small('b_gate', l), w['w_br_attn'], w['w_br_conv'],
                                       w['w_br_mem'], w['w_out'], t_el, "merge" + tag)
        saved.append(dict(x=x, ht=ht, proj=proj, proj_l=proj_l, q=q, k=k, v=v, o=o, lse=lse, mk=mk, mv=mv,
                          oa=oa, oc=oc, om=om, aa=aa, ac=ac, am=am))
        x = x_out

    g, loss_parts = _loss_grad(x, tgt, 512, "loss")
    loss = lax.psum(jnp.sum(loss_parts), ("x", "y", "c"))

    gw = {n: [None] * DEPTH for n in WEIGHTS}
    for l in reversed(range(DEPTH)):
        w = layer_w[l]
        sv = saved[l]
        tag = ""
        tqb = min(tq_b, s)
        y, daa, dac, dam, dr, doa, doc, dom, dbg = _merge_bwd(
            g, sv['proj'], sv['aa'], sv['ac'], sv['am'], small('b_gate', l), w['w_out_t'], w['w_br_attn_t'],
            w['w_br_conv_t'], w['w_br_mem_t'], t_el, "merge_bwd" + tag)
        gw['b_gate'][l] = dbg[0]
        gw['w_out'][l] = _tn(y, g, 512, D, "dw_out" + tag)
        gw['w_br_attn'][l] = _tn(sv['oa'], daa, 512, D, "dw_attn" + tag)
        gw['w_br_conv'][l] = _tn(sv['oc'], dac, 512, D, "dw_conv" + tag)
        gw['w_br_mem'][l] = _tn(sv['om'], dam, 512, D, "dw_mem" + tag)
        cb, mqg = small('conv_b', l), small('mem_q_g', l)
        dcv, dmm, dga, do, dl, dcw, dcb, dmk, dmv, dmqg = _branches_bwd(
            sv['proj'], sv['o'], sv['mk'], sv['mv'], conv_w[l], cb, mqg, doa, doc, dom, t_br, "branches_bwd" + tag)
        gw['conv_w'][l], gw['conv_b'][l], gw['mem_q_g'][l] = dcw, dcb[0], dmqg[0]
        dwm, dmng, dmkg = _mem_prep_bwd(mem, small('mem_norm_g', l), w['w_mkv'], w['w_mkv_t'], small('mem_k_g', l),
                                        dmk, dmv, "mem_prep_bwd" + tag)
        gw['w_mkv'][l], gw['mem_norm_g'][l], gw['mem_k_g'][l] = dwm, dmng[0], dmkg[0]
        lse_r = sv['lse'].reshape(H, s // tqb, 1, tqb)
        dl_r = dl.reshape(H, s // tqb, 1, tqb)
        dq, dk, dv = _flash_bwd(sv['q'], sv['k'], sv['v'], do, lse_r, dl_r, tq_b, tk_b, "flash_bwd" + tag)
        qng, kvng = small('q_norm_g', l), small('kv_norm_g', l)
        qhg, khg = small('q_head_g', l, HP), small('k_head_g', l, HP)
        dlat, dwuq, dwukv, dqng, dkvng, dqhg, dkhg = _mla_prep_bwd(
            sv['proj_l'], rc, rs, qng, kvng, qhg, khg, w['w_uq'], w['w_ukv'], w['w_uq_t'], w['w_ukv_t'],
            dq, dk, dv, t_br, "mla_prep_bwd" + tag)
        gw['w_uq'][l] = _unpad_heads(dwuq, QKD)
        dwukv = dwukv.reshape(KVL, 2, H, HP)[:, :, :, :NOPE]
        gw['w_ukv'][l] = dwukv.transpose(0, 2, 1, 3).reshape(KVL, H * 2 * NOPE)
        gw['q_norm_g'][l], gw['kv_norm_g'][l] = dqng[0], dkvng[0]
        gw['q_head_g'][l], gw['k_head_g'][l] = dqhg[0, :QKD], dkhg[0, :QKD]
        dpieces = [dcv, dmm, dr, dga, dlat]
        dh = _mm_pieces(dpieces, w['w_in_t'], 512, 512, "d_h" + tag)
        gw['w_in'][l] = _unpermute_w_in([
            _mm_acc(sv['ht'], p, 1024, tn, f"dw_in_{i}" + tag)
            for i, (p, tn) in enumerate(zip(dpieces, (1024, 1024, 1536, 512, P_LW)))])
        g, dng = _rms_in_bwd(sv['x'], g, dh, small('norm_g', l), 512, "rms_bwd" + tag)
        gw['norm_g'][l] = dng[0]
    grad_x = g[None]

    sharded = BIG_ORDER + ['conv_w']
    axis_of = lambda n: 1 if n == 'conv_w' else BIG[n][1]
    send = [jnp.stack([_shard_blocks(gw[n][l], axis_of(n)) for l in range(DEPTH)], axis=1)
            .astype(F32 if n == 'conv_w' else BF16) for n in sharded]
    parts_big = _alltoall(send, "rs_grads")
    small_flat = jnp.concatenate([jnp.stack(gw[n]).reshape(-1) for n in SMALL_ORDER])
    n_small = small_flat.shape[0]
    parts_small = _allgather([_to_rows(small_flat)], "ag_small_grads")[0]

    outs = [{} for _ in range(4)]
    for n, parts in zip(sharded, parts_big):
        loc = a[n].shape
        two_d = lambda t: t.reshape(loc[0] * loc[1], loc[2])
        res = _adamw(parts.reshape(N_DEV, loc[0] * loc[1], loc[2]), two_d(a[n]), two_d(a['m_' + n]),
                     two_d(a['v_' + n]), "adamw_" + n)
        for d, r in zip(outs, res):
            d[n] = r.reshape(loc)
    pks = lambda pre: _to_rows(jnp.concatenate([a[pre + n].reshape(-1) for n in SMALL_ORDER]))
    res_small = _adamw(parts_small, pks(''), pks('m_'), pks('v_'), "adamw_small")
    for d, rsm in zip(outs, res_small):
        flat = rsm.reshape(-1)[:n_small]
        off = 0
        for n in SMALL_ORDER:
            d[n] = flat[off:off + DEPTH * SMALL[n]].reshape(DEPTH, SMALL[n])
            off += DEPTH * SMALL[n]
    result = [loss, grad_x]
    for d in outs:
        result += [d[n] for n in WEIGHTS]
    return tuple(result)
```

```python
import functools

import jax
import jax.numpy as jnp
from jax import lax
from jax.experimental import pallas as pl
from jax.experimental.pallas import tpu as pltpu

F32, BF16 = jnp.float32, jnp.bfloat16

N_DEV = 8
DEPTH = 4
D = 1024
QL, KVL, RP = 384, 256, 32
H, NOPE, QKD, VD = 8, 64, 96, 64
HP = 128
CW, MW, AW = 512, 512, 512
HM, MHD = 4, 128
IN_WIDTH = 7328
EPS = 1e-6
ROPE_BASE = 10000.0
ATT_SCALE = QKD ** -0.5
MEM_SCALE = MHD ** -0.5
LOG2E = 1.4426950408889634
QSCALE = ATT_SCALE * LOG2E

ADAM_LR, ADAM_B1, ADAM_B2, ADAM_EPS, ADAM_WD, ADAM_STEP = 0.001, 0.9, 0.999, 1e-08, 0.01, 10

LANES = 128
VMEM_LIMIT = 56 * 1024 * 1024

P_CONV, P_MEM, P_R, P_GA, P_L = 0, 2048, 3072, 6144, 6656
P_MAIN = 6656
P_LW = 768
P_W = P_MAIN + P_LW
P_WPAD = 7680

WEIGHTS = ['norm_g', 'w_in', 'b_gate', 'q_norm_g', 'w_uq', 'kv_norm_g', 'w_ukv', 'q_head_g', 'k_head_g',
           'conv_w', 'conv_b', 'mem_norm_g', 'w_mkv', 'mem_q_g', 'mem_k_g', 'w_br_attn', 'w_br_conv',
           'w_br_mem', 'w_out']
INPUTS = ['x', 'mem', 'positions'] + WEIGHTS + ['loss_target'] + ['m_' + n for n in WEIGHTS] + ['v_' + n for n in WEIGHTS]

BIG = {'w_in': ((D, IN_WIDTH), 1), 'w_uq': ((QL, H * QKD), 1), 'w_ukv': ((KVL, H * 128), 1),
       'w_mkv': ((D, 2 * MW), 0), 'w_br_attn': ((AW, D), 1), 'w_br_conv': ((CW, D), 1),
       'w_br_mem': ((MW, D), 1), 'w_out': ((D, D), 0)}
BIG_ORDER = ['w_in', 'w_uq', 'w_ukv', 'w_mkv', 'w_br_attn', 'w_br_conv', 'w_br_mem', 'w_out']
CONVW_PAD = 256
SMALL = {'norm_g': D, 'b_gate': 3 * D, 'q_norm_g': QL, 'kv_norm_g': KVL, 'q_head_g': QKD, 'k_head_g': QKD,
         'conv_b': CW, 'mem_norm_g': D, 'mem_q_g': MHD, 'mem_k_g': MHD}
SMALL_ORDER = list(SMALL)
ROW_ALIGN = 1024


def _call(body, **kw):
    return pl.pallas_call(body, **kw)


def _cp(sem=None):
    return pltpu.CompilerParams(dimension_semantics=sem, vmem_limit_bytes=VMEM_LIMIT)


def _full(shape):
    n = len(shape)
    return pl.BlockSpec(shape, lambda *_: (0,) * n)


def _rms(x, g, n):
    rs = lax.rsqrt(jnp.sum(x * x, axis=-1, keepdims=True) * (1.0 / n) + EPS)
    xh = x * rs
    return xh * g, xh, rs


def _rms_bwd(dy, xh, rs, g, n):
    dxh = dy * g
    dx = rs * (dxh - xh * (jnp.sum(dxh * xh, axis=-1, keepdims=True) * (1.0 / n)))
    return dx, jnp.sum(dy * xh, axis=0, keepdims=True)


def _sigmoid(x):
    return 1.0 / (1.0 + jnp.exp(-x))


def _swap_rope(u):
    lane = lax.broadcasted_iota(jnp.int32, u.shape, 1)
    up = pltpu.roll(u, 16, 1)
    dn = pltpu.roll(u, 112, 1)
    return jnp.where((lane >= 64) & (lane < 80), dn, jnp.where((lane >= 80) & (lane < 96), up, 0.0))


def _rope(u, c, sn):
    return u * c + _swap_rope(u) * sn


def _rope_adj(d, c, sn):
    return d * c + _swap_rope(d * sn)


def _dot(a, b):
    return jnp.dot(a, b, preferred_element_type=F32)


def _dot_nt(a, b):
    return lax.dot_general(a, b, (((1,), (1,)), ((), ())), preferred_element_type=F32)


def _dot_tn(a, b):
    return lax.dot_general(a, b, (((0,), (0,)), ((), ())), preferred_element_type=F32)


def _mm(a, b, out_dtype, tm, tn, name):
    m, k = a.shape
    _, n = b.shape
    tm, tn = min(tm, m), min(tn, n)

    def body(a_ref, b_ref, o_ref):
        o_ref[...] = _dot(a_ref[...].astype(BF16), b_ref[...]).astype(o_ref.dtype)

    return _call(
        body, name=name, grid=(m // tm, n // tn),
        in_specs=[pl.BlockSpec((tm, k), lambda i, j: (i, 0)), pl.BlockSpec((k, tn), lambda i, j: (0, j))],
        out_specs=pl.BlockSpec((tm, tn), lambda i, j: (i, j)),
        out_shape=jax.ShapeDtypeStruct((m, n), out_dtype),
        compiler_params=_cp(("parallel", "arbitrary")),
    )(a, b)


def _tn(a, b, ts, tn, name):
    s, ka = a.shape
    _, n = b.shape
    ts, tn = min(ts, s), min(tn, n)

    def body(a_ref, b_ref, o_ref):
        @pl.when(pl.program_id(1) == 0)
        def _():
            o_ref[...] = jnp.zeros_like(o_ref)

        o_ref[...] += _dot_tn(a_ref[...].astype(BF16), b_ref[...].astype(BF16))

    return _call(
        body, name=name, grid=(n // tn, s // ts),
        in_specs=[pl.BlockSpec((ts, ka), lambda j, i: (i, 0)), pl.BlockSpec((ts, tn), lambda j, i: (i, j))],
        out_specs=pl.BlockSpec((ka, tn), lambda j, i: (0, j)),
        out_shape=jax.ShapeDtypeStruct((ka, n), F32),
        compiler_params=_cp(("parallel", "arbitrary")),
    )(a, b)


def _mm_pieces(pieces, b, tm, tn, name):
    n_p = len(pieces)
    m = pieces[0].shape[0]
    k, n = b.shape
    tm, tn = min(tm, m), min(tn, n)
    offs = [sum(p.shape[1] for p in pieces[:i]) for i in range(n_p)]

    def body(*refs):
        a_refs, b_ref, o_ref = refs[:n_p], refs[n_p], refs[n_p + 1]
        acc = None
        for a_ref, off in zip(a_refs, offs):
            d = _dot(a_ref[...], b_ref[off:off + a_ref.shape[1], :])
            acc = d if acc is None else acc + d
        o_ref[...] = acc

    return _call(
        body, name=name, grid=(m // tm, n // tn),
        in_specs=[pl.BlockSpec((tm, p.shape[1]), lambda i, j: (i, 0)) for p in pieces]
        + [pl.BlockSpec((k, tn), lambda i, j: (0, j))],
        out_specs=pl.BlockSpec((tm, tn), lambda i, j: (i, j)),
        out_shape=jax.ShapeDtypeStruct((m, n), F32),
        compiler_params=_cp(("parallel", "arbitrary")),
    )(*pieces, b)


def _mm_acc(a, b, tk, tn, name):
    m, k = a.shape
    _, n = b.shape
    tk, tn = min(tk, k), min(tn, n)

    def body(a_ref, b_ref, o_ref):
        @pl.when(pl.program_id(1) == 0)
        def _():
            o_ref[...] = jnp.zeros_like(o_ref)

        o_ref[...] += _dot(a_ref[...], b_ref[...])

    return _call(
        body, name=name, grid=(n // tn, k // tk),
        in_specs=[pl.BlockSpec((m, tk), lambda j, i: (0, i)), pl.BlockSpec((tk, tn), lambda j, i: (i, j))],
        out_specs=pl.BlockSpec((m, tn), lambda j, i: (0, j)),
        out_shape=jax.ShapeDtypeStruct((m, n), F32),
        compiler_params=_cp(("parallel", "arbitrary")),
    )(a, b)


def _rms_h(x, g, t, name):
    s = x.shape[0]
    t = min(t, s)

    def body(x_ref, g_ref, h_ref, ht_ref):
        h = _rms(x_ref[...], g_ref[...], D)[0]
        h_ref[...] = h.astype(BF16)
        ht_ref[...] = h.T.astype(BF16)

    return _call(
        body, name=name, grid=(s // t,),
        in_specs=[pl.BlockSpec((t, D), lambda i: (i, 0)), _full((1, D))],
        out_specs=[pl.BlockSpec((t, D), lambda i: (i, 0)), pl.BlockSpec((D, t), lambda i: (0, i))],
        out_shape=[jax.ShapeDtypeStruct((s, D), BF16), jax.ShapeDtypeStruct((D, s), BF16)],
        compiler_params=_cp(("parallel",)),
    )(x, g)


def _mla_heads(pl_blk, c, sn, qng, kvng, qhg, khg, wuq, wukv):
    ql = pl_blk[:, 0:QL].astype(F32)
    kvl = pl_blk[:, QL:QL + KVL].astype(F32)
    kpe = pl_blk[:, QL + KVL:P_LW].astype(F32)
    qn, qxh, qrs = _rms(ql, qng, QL)
    kvn, kvxh, kvrs = _rms(kvl, kvng, KVL)
    qn16, kvn16 = qn.astype(BF16), kvn.astype(BF16)
    qp = _dot(qn16, wuq)
    kvp = _dot(kvn16, wukv)
    return ql, kvl, kpe, qxh, qrs, kvxh, kvrs, qn16, kvn16, qp, kvp


def _mla_prep(proj_l, c, sn, qng, kvng, qhg, khg, wuq, wukv, t, name):
    s = proj_l.shape[0]
    t = min(t, s)

    def body(l_ref, c_ref, sn_ref, qng_ref, kvng_ref, qhg_ref, khg_ref, wuq_ref, wukv_ref, q_ref, k_ref, v_ref):
        cc, ss = c_ref[...], sn_ref[...]
        (_, _, kpe, _, _, _, _, _, _, qp, kvp) = _mla_heads(
            l_ref[...], cc, ss, qng_ref[...], kvng_ref[...], qhg_ref[...], khg_ref[...], wuq_ref[...], wukv_ref[...])
        lane = lax.broadcasted_iota(jnp.int32, cc.shape, 1)
        for h in range(H):
            u = qp[:, h * HP:(h + 1) * HP]
            q_ref[h] = (_rope(_rms(u, qhg_ref[...], QKD)[0], cc, ss) * QSCALE).astype(BF16)
            u = kvp[:, h * HP:(h + 1) * HP] + kpe
            k_ref[h] = _rope(_rms(u, khg_ref[...], QKD)[0], cc, ss).astype(BF16)
            v_ref[h] = jnp.where(lane == VD, 1.0, kvp[:, (H + h) * HP:(H + h + 1) * HP]).astype(BF16)

    hs = pl.BlockSpec((H, t, HP), lambda i: (0, i, 0))
    row = lambda w: pl.BlockSpec((t, w), lambda i: (i, 0))
    return _call(
        body, name=name, grid=(s // t,),
        in_specs=[row(P_LW), row(HP), row(HP), _full((1, QL)), _full((1, KVL)), _full((1, HP)), _full((1, HP)),
                  _full((QL, H * HP)), _full((KVL, 2 * H * HP))],
        out_specs=[hs, hs, hs],
        out_shape=[jax.ShapeDtypeStruct((H, s, HP), BF16)] * 3,
        compiler_params=_cp(("parallel",)),
    )(proj_l, c, sn, qng, kvng, qhg, khg, wuq, wukv)


def _flash_fwd(q, k, v, tq, tk, name):
    _, s, _ = q.shape
    tq, tk = min(tq, s), min(tk, s)
    nk, nc = s // tk, tk // LANES
    un = 8 if nk % 8 == 0 else 1

    def body(q_ref, k_ref, v_ref, o_ref, lse_ref, s_scr):
        qb = q_ref[0]

        def scores(jj, mx):
            for u in range(un):
                j = jj * un + u
                off = pl.multiple_of(j * tk, tk)
                sc = _dot_nt(qb, k_ref[0, pl.ds(off, tk), :])
                s_scr[j] = sc
                for cc in range(nc):
                    mx = jnp.maximum(mx, sc[:, cc * LANES:(cc + 1) * LANES])
            return mx

        mx = lax.fori_loop(0, nk // un, scores, jnp.full((tq, LANES), -jnp.inf, F32))
        m = jnp.max(mx, axis=-1, keepdims=True)
        mb = jnp.broadcast_to(m, (tq, LANES))

        def probs(jj, acc):
            for u in range(un):
                j = jj * un + u
                off = pl.multiple_of(j * tk, tk)
                sc = s_scr[j]
                ps = [jnp.exp2(sc[:, cc * LANES:(cc + 1) * LANES] - mb).astype(BF16) for cc in range(nc)]
                acc = acc + _dot(jnp.concatenate(ps, axis=-1), v_ref[0, pl.ds(off, tk), :])
            return acc

        acc = lax.fori_loop(0, nk // un, probs, jnp.zeros((tq, HP), F32))
        l = acc[:, VD:VD + 1]
        o_ref[0] = (acc[:, :VD] / l).astype(BF16)
        lse = jnp.broadcast_to(m + jnp.log(l) * LOG2E, (tq, LANES))
        lse_ref[0] = lse.T[0:1, :]

    return _call(
        body, name=name, grid=(H, s // tq),
        in_specs=[pl.BlockSpec((1, tq, HP), lambda h, i: (h, i, 0)),
                  pl.BlockSpec((1, s, HP), lambda h, i: (h, 0, 0), pipeline_mode=pl.Buffered(1)),
                  pl.BlockSpec((1, s, HP), lambda h, i: (h, 0, 0), pipeline_mode=pl.Buffered(1))],
        out_specs=[pl.BlockSpec((1, tq, VD), lambda h, i: (h, i, 0)), pl.BlockSpec((1, 1, tq), lambda h, i: (h, 0, i))],
        out_shape=[jax.ShapeDtypeStruct((H, s, VD), BF16), jax.ShapeDtypeStruct((H, 1, s), F32)],
        scratch_shapes=[pltpu.VMEM((nk, tq, tk), F32)],
        compiler_params=_cp(("parallel", "arbitrary")),
    )(q, k, v)


def _mem_prep(mem, mng, wmkv, mkg, name):
    m = mem.shape[0]

    def body(mem_ref, mng_ref, w_ref, mkg_ref, mk_ref, mv_ref):
        mn = _rms(mem_ref[...], mng_ref[...], D)[0].astype(BF16)
        mkv = _dot(mn, w_ref[...])
        for h in range(HM):
            mk_ref[h] = _rms(mkv[:, 2 * h * MHD:(2 * h + 1) * MHD], mkg_ref[...], MHD)[0].astype(BF16)
            mv_ref[h] = mkv[:, (2 * h + 1) * MHD:(2 * h + 2) * MHD].astype(BF16)

    return _call(
        body, name=name,
        in_specs=[_full((m, D)), _full((1, D)), _full((D, 2 * MW)), _full((1, MHD))],
        out_specs=[_full((HM, m, MHD))] * 2,
        out_shape=[jax.ShapeDtypeStruct((HM, m, MHD), BF16)] * 2,
        compiler_params=_cp(),
    )(mem, mng, wmkv, mkg)


def _conv_parts(cv, prev, nxt, first, last, cw, cb):
    t = cv.shape[0]
    c_b, c_c, c_u, g_c = (cv[:, i * CW:(i + 1) * CW].astype(F32) for i in range(4))
    z = c_c * c_u
    zp = jnp.where(first, 0.0, prev[15:16, CW:2 * CW].astype(F32) * prev[15:16, 2 * CW:3 * CW].astype(F32))
    zn = jnp.where(last, 0.0, nxt[0:1, CW:2 * CW].astype(F32) * nxt[0:1, 2 * CW:3 * CW].astype(F32))
    row = lax.broadcasted_iota(jnp.int32, (t, CW), 0)
    z_m1 = jnp.where(row == 0, zp, pltpu.roll(z, 1, 0))
    z_p1 = jnp.where(row == t - 1, zn, pltpu.roll(z, t - 1, 0))
    conv = cw[0:1] * z_m1 + cw[1:2] * z + cw[2:3] * z_p1 + cb
    return c_b, c_c, c_u, g_c, z, z_m1, z_p1, conv


def _mem_attn(qm, mqg, mk_ref, mv_ref):
    outs = []
    for h in range(HM):
        mq, mqxh, mqrs = _rms(qm[:, h * MHD:(h + 1) * MHD], mqg, MHD)
        mq16 = mq.astype(BF16)
        sc = _dot_nt(mq16, mk_ref[h]) * MEM_SCALE
        e = jnp.exp(sc - jnp.max(sc, axis=-1, keepdims=True))
        p = e / jnp.sum(e, axis=-1, keepdims=True)
        o = _dot(p.astype(BF16), mv_ref[h])
        outs.append((mq16, mqxh, mqrs, p, o))
    return outs


def _halo_specs(t, s, width):
    nb = s // 16
    prev = pl.BlockSpec((16, width), lambda i: (jnp.maximum(i * (t // 16) - 1, 0), 0))
    nxt = pl.BlockSpec((16, width), lambda i: (jnp.minimum((i + 1) * (t // 16), nb - 1), 0))
    return prev, nxt


def _branches(proj, o, mk, mv, cw, cb, mqg, t, name):
    s = proj.shape[0]
    t = min(t, s)
    nt = s // t

    def body(cv_ref, pv_ref, nx_ref, mm_ref, ga_ref, o_ref, mk_ref, mv_ref, cw_ref, cb_ref, mqg_ref,
             oa_ref, oc_ref, om_ref):
        i = pl.program_id(0)
        c_b, _, _, g_c, _, _, _, conv = _conv_parts(
            cv_ref[...], pv_ref[...], nx_ref[...], i == 0, i == nt - 1, cw_ref[...], cb_ref[...])
        oc_ref[...] = (c_b * conv * (g_c * _sigmoid(g_c))).astype(BF16)
        ga = ga_ref[...].astype(F32)
        ocat = jnp.concatenate([o_ref[h].astype(F32) for h in range(H)], axis=-1)
        oa_ref[...] = (ocat * (ga * _sigmoid(ga))).astype(BF16)
        mblk = mm_ref[...].astype(F32)
        gm = mblk[:, MW:]
        heads = _mem_attn(mblk[:, :MW], mqg_ref[...], mk_ref, mv_ref)
        om = jnp.concatenate([hh[4] for hh in heads], axis=-1)
        om_ref[...] = (om * (gm * _sigmoid(gm))).astype(BF16)

    pv, nx = _halo_specs(t, s, 4 * CW)
    out = pl.BlockSpec((t, 512), lambda i: (i, 0))
    return _call(
        body, name=name, grid=(nt,),
        in_specs=[pl.BlockSpec((t, 4 * CW), lambda i: (i, 0)), pv, nx,
                  pl.BlockSpec((t, 2 * MW), lambda i: (i, P_MEM // (2 * MW))),
                  pl.BlockSpec((t, AW), lambda i: (i, P_GA // AW)),
                  pl.BlockSpec((H, t, VD), lambda i: (0, i, 0)),
                  _full(mk.shape), _full(mv.shape), _full((3, CW)), _full((1, CW)), _full((1, MHD))],
        out_specs=[out, out, out],
        out_shape=[jax.ShapeDtypeStruct((s, 512), BF16)] * 3,
        compiler_params=_cp(("parallel",)),
    )(proj, proj, proj, proj, proj, o, mk, mv, cw, cb, mqg)


def _merge_fwd(x, proj, oa, oc, om, bg, wa, wc, wm, wo, t, name):
    s = x.shape[0]
    t = min(t, s)

    def body(x_ref, r_ref, oa_ref, oc_ref, om_ref, bg_ref, wa_ref, wc_ref, wm_ref, wo_ref,
             xo_ref, aa_ref, ac_ref, am_ref):
        y = jnp.zeros((t, D), F32)
        for j, (o_ref, w_ref, a_ref) in enumerate(((oa_ref, wa_ref, aa_ref), (oc_ref, wc_ref, ac_ref),
                                                   (om_ref, wm_ref, am_ref))):
            a = _dot(o_ref[...], w_ref[...])
            a_ref[...] = a.astype(BF16)
            rg = _sigmoid(r_ref[:, j * D:(j + 1) * D].astype(F32) + bg_ref[:, j * D:(j + 1) * D])
            y = y + rg * a
        xo_ref[...] = x_ref[...] + _dot(y.astype(BF16), wo_ref[...])

    row = lambda w: pl.BlockSpec((t, w), lambda i: (i, 0))
    return _call(
        body, name=name, grid=(s // t,),
        in_specs=[row(D), pl.BlockSpec((t, 3 * D), lambda i: (i, P_R // (3 * D))), row(512), row(512), row(512),
                  _full((1, 3 * D)), _full((512, D)), _full((512, D)), _full((512, D)), _full((D, D))],
        out_specs=[row(D), row(D), row(D), row(D)],
        out_shape=[jax.ShapeDtypeStruct((s, D), F32)] + [jax.ShapeDtypeStruct((s, D), BF16)] * 3,
        compiler_params=_cp(("parallel",)),
    )(x, proj, oa, oc, om, bg, wa, wc, wm, wo)


def _loss_grad(x, tgt, t, name):
    s = x.shape[0]
    t = min(t, s)

    def body(x_ref, t_ref, g_ref, l_ref):
        @pl.when(pl.program_id(0) == 0)
        def _():
            l_ref[...] = jnp.zeros_like(l_ref)

        e = x_ref[...] - t_ref[...]
        g_ref[...] = e * (1.0 / D)
        sq = e * e
        part = sq[:, 0:LANES]
        for j in range(1, D // LANES):
            part = part + sq[:, j * LANES:(j + 1) * LANES]
        acc = part[0:8]
        for j in range(1, t // 8):
            acc = acc + part[j * 8:(j + 1) * 8]
        l_ref[...] += acc * (0.5 / D)

    row = pl.BlockSpec((t, D), lambda i: (i, 0))
    return _call(
        body, name=name, grid=(s // t,),
        in_specs=[row, row], out_specs=[row, _full((8, LANES))],
        out_shape=[jax.ShapeDtypeStruct((s, D), F32), jax.ShapeDtypeStruct((8, LANES), F32)],
        compiler_params=_cp(("arbitrary",)),
    )(x, tgt)


def _merge_bwd(g, proj, aa, ac, am, bg, wot, wat, wct, wmt, t, name):
    s = g.shape[0]
    t = min(t, s)

    def body(g_ref, r_ref, aa_ref, ac_ref, am_ref, bg_ref, wot_ref, wat_ref, wct_ref, wmt_ref,
             y_ref, daa_ref, dac_ref, dam_ref, dr_ref, doa_ref, doc_ref, dom_ref, dbg_ref):
        @pl.when(pl.program_id(0) == 0)
        def _():
            dbg_ref[...] = jnp.zeros_like(dbg_ref)

        dy = _dot(g_ref[...].astype(BF16), wot_ref[...])
        y = jnp.zeros((t, D), F32)
        for j, (a_ref, wt_ref, da_ref, do_ref) in enumerate(((aa_ref, wat_ref, daa_ref, doa_ref),
                                                             (ac_ref, wct_ref, dac_ref, doc_ref),
                                                             (am_ref, wmt_ref, dam_ref, dom_ref))):
            a = a_ref[...].astype(F32)
            rg = _sigmoid(r_ref[:, j * D:(j + 1) * D].astype(F32) + bg_ref[:, j * D:(j + 1) * D])
            y = y + rg * a
            da = (dy * rg).astype(BF16)
            da_ref[...] = da
            dr = dy * a * rg * (1.0 - rg)
            dr_ref[:, j * D:(j + 1) * D] = dr.astype(BF16)
            dbg_ref[:, j * D:(j + 1) * D] += jnp.sum(dr, axis=0, keepdims=True)
            do_ref[...] = _dot(da, wt_ref[...])
        y_ref[...] = y.astype(BF16)

    row = lambda w: pl.BlockSpec((t, w), lambda i: (i, 0))
    return _call(
        body, name=name, grid=(s // t,),
        in_specs=[row(D), pl.BlockSpec((t, 3 * D), lambda i: (i, P_R // (3 * D))), row(D), row(D), row(D),
                  _full((1, 3 * D)), _full((D, D)), _full((D, 512)), _full((D, 512)), _full((D, 512))],
        out_specs=[row(D), row(D), row(D), row(D), row(3 * D), row(512), row(512), row(512), _full((1, 3 * D))],
        out_shape=[jax.ShapeDtypeStruct((s, D), BF16)] * 4 + [jax.ShapeDtypeStruct((s, 3 * D), BF16)]
        + [jax.ShapeDtypeStruct((s, 512), F32)] * 3 + [jax.ShapeDtypeStruct((1, 3 * D), F32)],
        compiler_params=_cp(("arbitrary",)),
    )(g, proj, aa, ac, am, bg, wot, wat, wct, wmt)


def _branches_bwd(proj, o, mk, mv, cw, cb, mqg, doa, doc, dom, t, name):
    s = proj.shape[0]
    t = min(t, s)
    nt = s // t
    m = mk.shape[1]

    def body(cv_ref, pv_ref, nx_ref, mm_ref, ga_ref, o_ref, mk_ref, mv_ref, cw_ref, cb_ref, mqg_ref,
             doa_ref, doc_ref, dcp_ref, dcn_ref, dom_ref,
             dcv_ref, dmm_ref, dga_ref, do_ref, dl_ref, dcw_ref, dcb_ref, dmk_ref, dmv_ref, dmqg_ref):
        i = pl.program_id(0)

        @pl.when(i == 0)
        def _():
            for r in (dcw_ref, dcb_ref, dmk_ref, dmv_ref, dmqg_ref):
                r[...] = jnp.zeros_like(r)

        first, last = i == 0, i == nt - 1
        cw_, cb_ = cw_ref[...], cb_ref[...]
        pv, nx = pv_ref[...], nx_ref[...]
        c_b, c_c, c_u, g_c, z, z_m1, z_p1, conv = _conv_parts(cv_ref[...], pv, nx, first, last, cw_, cb_)
        sg = _sigmoid(g_c)
        silu = g_c * sg
        dsilu = sg * (1.0 + g_c * (1.0 - sg))
        doc_ = doc_ref[...]
        dconv = doc_ * c_b * silu
        gp = pv[15:16, 3 * CW:4 * CW].astype(F32)
        gn = nx[0:1, 3 * CW:4 * CW].astype(F32)
        dconv_p = jnp.where(first, 0.0, dcp_ref[15:16, :] * pv[15:16, 0:CW].astype(F32) * (gp * _sigmoid(gp)))
        dconv_n = jnp.where(last, 0.0, dcn_ref[0:1, :] * nx[0:1, 0:CW].astype(F32) * (gn * _sigmoid(gn)))
        row = lax.broadcasted_iota(jnp.int32, (t, CW), 0)
        d_m1 = jnp.where(row == 0, dconv_p, pltpu.roll(dconv, 1, 0))
        d_p1 = jnp.where(row == t - 1, dconv_n, pltpu.roll(dconv, t - 1, 0))
        dz = cw_[0:1] * d_p1 + cw_[1:2] * dconv + cw_[2:3] * d_m1
        dcv_ref[:, 0:CW] = (doc_ * conv * silu).astype(BF16)
        dcv_ref[:, CW:2 * CW] = (dz * c_u).astype(BF16)
        dcv_ref[:, 2 * CW:3 * CW] = (dz * c_c).astype(BF16)
        dcv_ref[:, 3 * CW:4 * CW] = (doc_ * c_b * conv * dsilu).astype(BF16)
        dcw_ref[0:1, :] += jnp.sum(dconv * z_m1, axis=0, keepdims=True)
        dcw_ref[1:2, :] += jnp.sum(dconv * z, axis=0, keepdims=True)
        dcw_ref[2:3, :] += jnp.sum(dconv * z_p1, axis=0, keepdims=True)
        dcb_ref[...] += jnp.sum(dconv, axis=0, keepdims=True)
        ga = ga_ref[...].astype(F32)
        sg = _sigmoid(ga)
        doa_ = doa_ref[...]
        ocat = jnp.concatenate([o_ref[h].astype(F32) for h in range(H)], axis=-1)
        dga_ref[...] = (doa_ * ocat * (sg * (1.0 + ga * (1.0 - sg)))).astype(BF16)
        dog = doa_ * (ga * sg)
        zeros = jnp.zeros((t, HP - VD), F32)
        lane = lax.broadcasted_iota(jnp.int32, (t, LANES), 1)
        dmat = jnp.zeros((t, LANES), F32)
        for h in range(H):
            dh = dog[:, h * VD:(h + 1) * VD]
            do_ref[h] = jnp.concatenate([dh, zeros], axis=-1).astype(BF16)
            dmat = jnp.where(lane == h, jnp.sum(dh * ocat[:, h * VD:(h + 1) * VD], axis=-1, keepdims=True), dmat)
        dlt = dmat.T
        for h in range(H):
            dl_ref[h] = dlt[h:h + 1, :]
        mblk = mm_ref[...].astype(F32)
        gm = mblk[:, MW:]
        sg = _sigmoid(gm)
        dom_ = dom_ref[...]
        heads = _mem_attn(mblk[:, :MW], mqg_ref[...], mk_ref, mv_ref)
        om = jnp.concatenate([hh[4] for hh in heads], axis=-1)
        dmm_ref[:, MW:] = (dom_ * om * (sg * (1.0 + gm * (1.0 - sg)))).astype(BF16)
        dmo = dom_ * (gm * sg)
        dmqg = jnp.zeros((1, MHD), F32)
        for h in range(HM):
            mq16, mqxh, mqrs, p, _ = heads[h]
            dmo_h = dmo[:, h * MHD:(h + 1) * MHD].astype(BF16)
            dp = _dot_nt(dmo_h, mv_ref[h])
            ds = (p * (dp - jnp.sum(dp * p, axis=-1, keepdims=True)) * MEM_SCALE).astype(BF16)
            dmq = _dot(ds, mk_ref[h])
            dmk_ref[h] += _dot_tn(ds, mq16)
            dmv_ref[h] += _dot_tn(p.astype(BF16), dmo_h)
            dq, dg = _rms_bwd(dmq, mqxh, mqrs, mqg_ref[...], MHD)
            dmm_ref[:, h * MHD:(h + 1) * MHD] = dq.astype(BF16)
            dmqg = dmqg + dg
        dmqg_ref[...] += dmqg

    pv, nx = _halo_specs(t, s, 4 * CW)
    dpv, dnx = _halo_specs(t, s, CW)
    row = lambda w: pl.BlockSpec((t, w), lambda i: (i, 0))
    hs = lambda w: pl.BlockSpec((H, t, w), lambda i: (0, i, 0))
    return _call(
        body, name=name, grid=(nt,),
        in_specs=[pl.BlockSpec((t, 4 * CW), lambda i: (i, 0)), pv, nx,
                  pl.BlockSpec((t, 2 * MW), lambda i: (i, P_MEM // (2 * MW))),
                  pl.BlockSpec((t, AW), lambda i: (i, P_GA // AW)),
                  hs(VD), _full(mk.shape), _full(mv.shape), _full((3, CW)), _full((1, CW)), _full((1, MHD)),
                  row(512), row(512), dpv, dnx, row(512)],
        out_specs=[row(4 * CW), row(2 * MW), row(AW), hs(HP), pl.BlockSpec((H, 1, t), lambda i: (0, 0, i)),
                   _full((3, CW)), _full((1, CW)),
                   _full((HM, m, MHD)), _full((HM, m, MHD)), _full((1, MHD))],
        out_shape=[jax.ShapeDtypeStruct((s, 4 * CW), BF16), jax.ShapeDtypeStruct((s, 2 * MW), BF16),
                   jax.ShapeDtypeStruct((s, AW), BF16), jax.ShapeDtypeStruct((H, s, HP), BF16),
                   jax.ShapeDtypeStruct((H, 1, s), F32), jax.ShapeDtypeStruct((3, CW), F32),
                   jax.ShapeDtypeStruct((1, CW), F32), jax.ShapeDtypeStruct((HM, m, MHD), F32),
                   jax.ShapeDtypeStruct((HM, m, MHD), F32), jax.ShapeDtypeStruct((1, MHD), F32)],
        compiler_params=_cp(("arbitrary",)),
    )(proj, proj, proj, proj, proj, o, mk, mv, cw, cb, mqg, doa, doc, doc, doc, dom)


def _flash_bwd(q, k, v, do, lse, dl, tq, tk, name):
    _, s, _ = q.shape
    tq, tk = min(tq, s), min(tk, s)
    nq, nkt, nc = s // tq, s // tk, tk // LANES
    unroll = 4 if nq % 4 == 0 else 1

    def body(q_ref, do_ref, lse_ref, dl_ref, k_ref, v_ref, dq_ref, dk_ref, dv_ref, dq_acc):
        j = pl.program_id(1)

        @pl.when(j == 0)
        def _():
            dq_acc[...] = jnp.zeros_like(dq_acc)

        kb, vb = k_ref[0], v_ref[0]

        def step(ii, carry):
            dkt, dvt = carry
            for u in range(unroll):
                i = ii * unroll + u
                off = pl.multiple_of(i * tq, tq)
                qb = q_ref[0, pl.ds(off, tq), :]
                dob = do_ref[0, pl.ds(off, tq), :]
                lse_b = jnp.broadcast_to(lse_ref[0, i], (LANES, tq)).T
                dl_b = jnp.broadcast_to(dl_ref[0, i], (LANES, tq)).T
                sc = _dot_nt(qb, kb)
                dp = _dot_nt(dob, vb)
                ps, dss = [], []
                for cc in range(nc):
                    p = jnp.exp2(sc[:, cc * LANES:(cc + 1) * LANES] - lse_b)
                    ps.append(p.astype(BF16))
                    dss.append((p * (dp[:, cc * LANES:(cc + 1) * LANES] - dl_b)).astype(BF16))
                p16, ds16 = jnp.concatenate(ps, axis=-1), jnp.concatenate(dss, axis=-1)
                dvt = dvt + _dot_tn(dob, p16)
                dkt = dkt + _dot_tn(qb, ds16)
                dq_acc[pl.ds(off, tq), :] += _dot(ds16, kb)
            return dkt, dvt

        dkt, dvt = lax.fori_loop(0, nq // unroll, step, (jnp.zeros((HP, tk), F32), jnp.zeros((HP, tk), F32)))
        dk_ref[0] = (dkt.T * (1.0 / LOG2E)).astype(BF16)
        dv_ref[0] = dvt.T.astype(BF16)

        @pl.when(j == nkt - 1)
        def _():
            dq_ref[0] = (dq_acc[...] * ATT_SCALE).astype(BF16)

    whole = pl.BlockSpec((1, s, HP), lambda h, j: (h, 0, 0))
    stat = pl.BlockSpec((1, nq, 1, tq), lambda h, j: (h, 0, 0, 0))
    tile = pl.BlockSpec((1, tk, HP), lambda h, j: (h, j, 0))
    return _call(
        body, name=name, grid=(H, nkt),
        in_specs=[whole, whole, stat, stat, tile, tile],
        out_specs=[whole, tile, tile],
        out_shape=[jax.ShapeDtypeStruct((H, s, HP), BF16)] * 3,
        scratch_shapes=[pltpu.VMEM((s, HP), F32)],
        compiler_params=_cp(("arbitrary", "arbitrary")),
    )(q, do, lse, dl, k, v)


def _mla_prep_bwd(proj_l, c, sn, qng, kvng, qhg, khg, wuq, wukv, wuqt, wukvt, dq, dk, dv, t, name):
    s = proj_l.shape[0]
    t = min(t, s)

    def body(l_ref, c_ref, sn_ref, qng_ref, kvng_ref, qhg_ref, khg_ref, wuq_ref, wukv_ref, wuqt_ref, wukvt_ref,
             dq_ref, dk_ref, dv_ref, dl_ref, dwuq_ref, dwukv_ref, dqng_ref, dkvng_ref, dqhg_ref, dkhg_ref):
        @pl.when(pl.program_id(0) == 0)
        def _():
            for r in (dwuq_ref, dwukv_ref, dqng_ref, dkvng_ref, dqhg_ref, dkhg_ref):
                r[...] = jnp.zeros_like(r)

        cc, ss = c_ref[...], sn_ref[...]
        qhg, khg = qhg_ref[...], khg_ref[...]
        (_, _, kpe, qxh, qrs, kvxh, kvrs, qn16, kvn16, qp, kvp) = _mla_heads(
            l_ref[...], cc, ss, qng_ref[...], kvng_ref[...], qhg, khg, wuq_ref[...], wukv_ref[...])
        lane = lax.broadcasted_iota(jnp.int32, (t, HP), 1)
        dqp, dkp, dvp = [], [], []
        dkpe = jnp.zeros((t, HP), F32)
        dqhg = jnp.zeros((1, HP), F32)
        dkhg = jnp.zeros((1, HP), F32)
        for h in range(H):
            _, xh, rs = _rms(qp[:, h * HP:(h + 1) * HP], qhg, QKD)
            du, dg = _rms_bwd(_rope_adj(dq_ref[h].astype(F32), cc, ss), xh, rs, qhg, QKD)
            dqp.append(du)
            dqhg = dqhg + dg
            _, xh, rs = _rms(kvp[:, h * HP:(h + 1) * HP] + kpe, khg, QKD)
            du, dg = _rms_bwd(_rope_adj(dk_ref[h].astype(F32), cc, ss), xh, rs, khg, QKD)
            dkp.append(du)
            dkhg = dkhg + dg
            dkpe = dkpe + jnp.where((lane >= NOPE) & (lane < QKD), du, 0.0)
            dvp.append(dv_ref[h].astype(F32))
        dqhg_ref[...] += dqhg
        dkhg_ref[...] += dkhg
        dqp16 = jnp.concatenate(dqp, axis=-1).astype(BF16)
        dkvp16 = jnp.concatenate(dkp + dvp, axis=-1).astype(BF16)
        dwuq_ref[...] += _dot_tn(qn16, dqp16)
        dwukv_ref[...] += _dot_tn(kvn16, dkvp16)
        dql, dg = _rms_bwd(_dot(dqp16, wuqt_ref[...]), qxh, qrs, qng_ref[...], QL)
        dqng_ref[...] += dg
        dkvl, dg = _rms_bwd(_dot(dkvp16, wukvt_ref[...]), kvxh, kvrs, kvng_ref[...], KVL)
        dkvng_ref[...] += dg
        dl_ref[:, 0:QL] = dql.astype(BF16)
        dl_ref[:, QL:QL + KVL] = dkvl.astype(BF16)
        dl_ref[:, QL + KVL:P_LW] = dkpe.astype(BF16)

    hs = pl.BlockSpec((H, t, HP), lambda i: (0, i, 0))
    row = lambda w: pl.BlockSpec((t, w), lambda i: (i, 0))
    return _call(
        body, name=name, grid=(s // t,),
        in_specs=[row(P_LW), row(HP), row(HP), _full((1, QL)), _full((1, KVL)), _full((1, HP)), _full((1, HP)),
                  _full((QL, H * HP)), _full((KVL, 2 * H * HP)), _full((H * HP, QL)), _full((2 * H * HP, KVL)),
                  hs, hs, hs],
        out_specs=[row(P_LW), _full((QL, H * HP)), _full((KVL, 2 * H * HP)), _full((1, QL)), _full((1, KVL)),
                   _full((1, HP)), _full((1, HP))],
        out_shape=[jax.ShapeDtypeStruct((s, P_LW), BF16), jax.ShapeDtypeStruct((QL, H * HP), F32),
                   jax.ShapeDtypeStruct((KVL, 2 * H * HP), F32), jax.ShapeDtypeStruct((1, QL), F32),
                   jax.ShapeDtypeStruct((1, KVL), F32), jax.ShapeDtypeStruct((1, HP), F32),
                   jax.ShapeDtypeStruct((1, HP), F32)],
        compiler_params=_cp(("arbitrary",)),
    )(proj_l, c, sn, qng, kvng, qhg, khg, wuq, wukv, wuqt, wukvt, dq, dk, dv)


def _mem_prep_bwd(mem, mng, wmkv, wmkvt, mkg, dmk, dmv, name):
    m = mem.shape[0]

    def body(mem_ref, mng_ref, w_ref, wt_ref, mkg_ref, dmk_ref, dmv_ref, dw_ref, dmng_ref, dmkg_ref):
        mn, xh, _ = _rms(mem_ref[...], mng_ref[...], D)
        mn16 = mn.astype(BF16)
        mkv = _dot(mn16, w_ref[...])
        parts = []
        dmkg = jnp.zeros((1, MHD), F32)
        for h in range(HM):
            _, kxh, krs = _rms(mkv[:, 2 * h * MHD:(2 * h + 1) * MHD], mkg_ref[...], MHD)
            du, dg = _rms_bwd(dmk_ref[h], kxh, krs, mkg_ref[...], MHD)
            dmkg = dmkg + dg
            parts += [du, dmv_ref[h]]
        dmkv = jnp.concatenate(parts, axis=-1).astype(BF16)
        dw_ref[...] = _dot_tn(mn16, dmkv)
        dmn = _dot(dmkv, wt_ref[...])
        dmng_ref[...] = jnp.sum(dmn * xh, axis=0, keepdims=True)
        dmkg_ref[...] = dmkg

    return _call(
        body, name=name,
        in_specs=[_full((m, D)), _full((1, D)), _full((D, 2 * MW)), _full((2 * MW, D)), _full((1, MHD)),
                  _full((HM, m, MHD)), _full((HM, m, MHD))],
        out_specs=[_full((D, 2 * MW)), _full((1, D)), _full((1, MHD))],
        out_shape=[jax.ShapeDtypeStruct((D, 2 * MW), F32), jax.ShapeDtypeStruct((1, D), F32),
                   jax.ShapeDtypeStruct((1, MHD), F32)],
        compiler_params=_cp(),
    )(mem, mng, wmkv, wmkvt, mkg, dmk, dmv)


def _rms_in_bwd(x, g_out, dh, ng, t, name):
    s = x.shape[0]
    t = min(t, s)

    def body(x_ref, go_ref, dh_ref, ng_ref, dx_ref, dng_ref):
        @pl.when(pl.program_id(0) == 0)
        def _():
            dng_ref[...] = jnp.zeros_like(dng_ref)

        _, xh, rs = _rms(x_ref[...], ng_ref[...], D)
        dx, dg = _rms_bwd(dh_ref[...], xh, rs, ng_ref[...], D)
        dx_ref[...] = go_ref[...] + dx
        dng_ref[...] += dg

    row = pl.BlockSpec((t, D), lambda i: (i, 0))
    return _call(
        body, name=name, grid=(s // t,),
        in_specs=[row, row, row, _full((1, D))], out_specs=[row, _full((1, D))],
        out_shape=[jax.ShapeDtypeStruct((s, D), F32), jax.ShapeDtypeStruct((1, D), F32)],
        compiler_params=_cp(("arbitrary",)),
    )(x, g_out, dh, ng)


def _allgather(arrs, name):
    n = len(arrs)

    def body(*refs):
        x_refs, out_refs = refs[:n], refs[n:2 * n]
        send_sems, recv_sems, local_sems = refs[2 * n:]
        x, y, c = lax.axis_index("x"), lax.axis_index("y"), lax.axis_index("c")
        me, sibling = (x, y, c), (x, y, 1 - c)
        chips = [(1 - x, y), (x, 1 - y), (1 - x, 1 - y)]

        def slot(a, px, py, pc):
            return out_refs[a].at[4 * px + 2 * py + pc]

        def copy(a, k, block, to, src=None):
            return pltpu.make_async_remote_copy(
                src_ref=slot(a, *block) if src is None else src, dst_ref=slot(a, *block),
                send_sem=send_sems.at[a, k], recv_sem=recv_sems.at[a, k],
                device_id=to, device_id_type=pl.DeviceIdType.MESH)

        mine = [pltpu.make_async_copy(x_refs[a], slot(a, *me), local_sems.at[a]) for a in range(n)]
        first, passed = [], []
        for a in range(n):
            mine[a].start()
            first.append(copy(a, 0, me, sibling, src=x_refs[a]))
            first += [copy(a, 1 + j, me, (*chip, c), src=x_refs[a]) for j, chip in enumerate(chips)]
        for cp in first:
            cp.start()
        for j, chip in enumerate(chips):
            for a in range(n):
                copy(a, 1 + j, (*chip, c), me).wait_recv()
                passed.append(copy(a, 4 + j, (*chip, c), sibling))
                passed[-1].start()
        for a in range(n):
            copy(a, 0, sibling, me).wait_recv()
        for j, chip in enumerate(chips):
            for a in range(n):
                copy(a, 4 + j, (*chip, 1 - c), me).wait_recv()
        for cp in first + passed:
            cp.wait_send()
        for cp in mine:
            cp.wait()

    any_spec = pl.BlockSpec(memory_space=pl.ANY)
    return _call(
        body, name=name,
        in_specs=[any_spec] * n, out_specs=[any_spec] * n,
        out_shape=[jax.ShapeDtypeStruct((N_DEV,) + a.shape, a.dtype) for a in arrs],
        scratch_shapes=[pltpu.SemaphoreType.DMA((n, 7)), pltpu.SemaphoreType.DMA((n, 7)),
                        pltpu.SemaphoreType.DMA((n,))],
    )(*arrs)


def _alltoall(arrs, name):
    n = len(arrs)

    def body(*refs):
        x_refs, out_refs = refs[:n], refs[n:2 * n]
        send_sems, recv_sems, local_sems = refs[2 * n:]
        x, y, c = lax.axis_index("x"), lax.axis_index("y"), lax.axis_index("c")
        me = 4 * x + 2 * y + c
        mine = [pltpu.make_async_copy(x_refs[a].at[me], out_refs[a].at[me], local_sems.at[a]) for a in range(n)]
        for cp in mine:
            cp.start()
        copies = []
        for k in range(1, N_DEV):
            bx, by, bc = (k >> 2) & 1, (k >> 1) & 1, k & 1
            px = 1 - x if bx else x
            py = 1 - y if by else y
            pc = 1 - c if bc else c
            for a in range(n):
                copies.append(pltpu.make_async_remote_copy(
                    src_ref=x_refs[a].at[4 * px + 2 * py + pc], dst_ref=out_refs[a].at[me],
                    send_sem=send_sems.at[a, k - 1], recv_sem=recv_sems.at[a, k - 1],
                    device_id=(px, py, pc), device_id_type=pl.DeviceIdType.MESH))
        for cp in copies:
            cp.start()
        for cp in copies:
            cp.wait_recv()
        for cp in copies:
            cp.wait_send()
        for cp in mine:
            cp.wait()

    any_spec = pl.BlockSpec(memory_space=pl.ANY)
    return _call(
        body, name=name,
        in_specs=[any_spec] * n, out_specs=[any_spec] * n,
        out_shape=[jax.ShapeDtypeStruct(a.shape, a.dtype) for a in arrs],
        scratch_shapes=[pltpu.SemaphoreType.DMA((n, 7)), pltpu.SemaphoreType.DMA((n, 7)),
                        pltpu.SemaphoreType.DMA((n,))],
    )(*arrs)


ADAMW_BLOCK_BYTES = 4 * 1024 * 1024


def _adamw(parts, w, m, v, name):
    r, c_ = w.shape
    cpad = -(-c_ // LANES) * LANES
    tr = r
    while N_DEV * tr * cpad * 4 > ADAMW_BLOCK_BYTES and tr % 16 == 0:
        tr //= 2
    c1 = 1.0 / (1.0 - ADAM_B1 ** ADAM_STEP)
    c2 = 1.0 / (1.0 - ADAM_B2 ** ADAM_STEP)

    def body(p_ref, w_ref, m_ref, v_ref, g_ref, d_ref, nm_ref, nv_ref):
        g = p_ref[0].astype(F32)
        for j in range(1, N_DEV):
            g = g + p_ref[j].astype(F32)
        nm = ADAM_B1 * m_ref[...] + (1.0 - ADAM_B1) * g
        nv = ADAM_B2 * v_ref[...] + (1.0 - ADAM_B2) * (g * g)
        g_ref[...] = g
        nm_ref[...] = nm
        nv_ref[...] = nv
        d_ref[...] = -ADAM_LR * ((nm * c1) / (jnp.sqrt(nv * c2) + ADAM_EPS) + ADAM_WD * w_ref[...])

    row = pl.BlockSpec((tr, c_), lambda i: (i, 0))
    return _call(
        body, name=name, grid=(r // tr,),
        in_specs=[pl.BlockSpec((N_DEV, tr, c_), lambda i: (0, i, 0)), row, row, row],
        out_specs=[row] * 4, out_shape=[jax.ShapeDtypeStruct((r, c_), F32)] * 4,
        compiler_params=_cp(("parallel",)),
    )(parts, w, m, v)


def _to_rows(flat, align=8):
    n = flat.shape[-1]
    rows = -(-n // (LANES * align)) * align
    return jnp.pad(flat, (0, rows * LANES - n)).reshape(rows, LANES)


def _shard_blocks(full, axis):
    r, c_ = full.shape
    if axis == 0:
        return full.reshape(N_DEV, r // N_DEV, c_)
    return full.reshape(r, N_DEV, c_ // N_DEV).transpose(1, 0, 2)


def _unshard_blocks(blocks, axis):
    _, r, c_ = blocks.shape
    if axis == 0:
        return blocks.reshape(N_DEV * r, c_)
    return blocks.transpose(1, 0, 2).reshape(r, N_DEV * c_)


def _pad_heads(w, width):
    k = w.shape[0]
    return jnp.pad(w.reshape(k, H, width), ((0, 0), (0, 0), (0, HP - width))).reshape(k, H * HP)


def _unpad_heads(w, width):
    k = w.shape[0]
    return w.reshape(k, H, HP)[:, :, :width].reshape(k, H * width)


IN_SPLIT = {'q_lat': (0, 384), 'kv_lat': (384, 640), 'k_pe': (640, 672), 'c_b': (672, 1184), 'c_c': (1184, 1696),
            'c_u': (1696, 2208), 'q_mem': (2208, 2720), 'g_attn': (2720, 3232), 'g_conv': (3232, 3744),
            'g_mem': (3744, 4256), 'r': (4256, 7328)}
P_ORDER = ['c_b', 'c_c', 'c_u', 'g_conv', 'q_mem', 'g_mem', 'r', 'g_attn', 'q_lat', 'kv_lat']


def _permute_w_in(w):
    k = w.shape[0]
    cols = [w[:, IN_SPLIT[n][0]:IN_SPLIT[n][1]] for n in P_ORDER]
    kpe = w[:, IN_SPLIT['k_pe'][0]:IN_SPLIT['k_pe'][1]]
    cols += [jnp.zeros((k, NOPE), w.dtype), kpe, jnp.zeros((k, HP - QKD), w.dtype)]
    return jnp.concatenate(cols, axis=1)


def _unpermute_w_in(pieces):
    bounds, off = [], 0
    for p in pieces:
        bounds.append((off, off + p.shape[1]))
        off += p.shape[1]

    def cols(lo, hi):
        for p, (b0, b1) in zip(pieces, bounds):
            if b0 <= lo and hi <= b1:
                return p[:, lo - b0:hi - b0]
        raise ValueError("a column range straddles two pieces")

    off, pos = 0, {}
    for n in P_ORDER:
        wd = IN_SPLIT[n][1] - IN_SPLIT[n][0]
        pos[n] = (off, off + wd)
        off += wd
    pos['k_pe'] = (off + NOPE, off + QKD)
    order = sorted(IN_SPLIT, key=lambda n: IN_SPLIT[n][0])
    return jnp.concatenate([cols(*pos[n]) for n in order], axis=1)


def _layer_weights(full, l):
    w = {}
    w_in_p = _permute_w_in(full['w_in'][l])
    w['w_main'] = w_in_p[:, :P_MAIN]
    w['w_l'] = w_in_p[:, P_MAIN:]
    w['w_in_t'] = w_in_p.T
    w['w_uq'] = _pad_heads(full['w_uq'][l], QKD)
    wukv = full['w_ukv'][l].reshape(KVL, H, 2, NOPE)
    kpart = jnp.pad(wukv[:, :, 0, :], ((0, 0), (0, 0), (0, HP - NOPE))).reshape(KVL, H * HP)
    vpart = jnp.pad(wukv[:, :, 1, :], ((0, 0), (0, 0), (0, HP - VD))).reshape(KVL, H * HP)
    w['w_ukv'] = jnp.concatenate([kpart, vpart], axis=1)
    w['w_uq_t'] = w['w_uq'].T
    w['w_ukv_t'] = w['w_ukv'].T
    w['w_mkv'] = full['w_mkv'][l]
    w['w_mkv_t'] = w['w_mkv'].T
    for n in ('w_br_attn', 'w_br_conv', 'w_br_mem', 'w_out'):
        w[n] = full[n][l]
        w[n + '_t'] = w[n].T
    return w


def kernel(x, mem, positions, norm_g, w_in, b_gate, q_norm_g, w_uq, kv_norm_g, w_ukv, q_head_g, k_head_g, conv_w, conv_b, mem_norm_g, w_mkv, mem_q_g, mem_k_g, w_br_attn, w_br_conv, w_br_mem, w_out, loss_target, m_norm_g, m_w_in, m_b_gate, m_q_norm_g, m_w_uq, m_kv_norm_g, m_w_ukv, m_q_head_g, m_k_head_g, m_conv_w, m_conv_b, m_mem_norm_g, m_w_mkv, m_mem_q_g, m_mem_k_g, m_w_br_attn, m_w_br_conv, m_w_br_mem, m_w_out, v_norm_g, v_w_in, v_b_gate, v_q_norm_g, v_w_uq, v_kv_norm_g, v_w_ukv, v_q_head_g, v_k_head_g, v_conv_w, v_conv_b, v_mem_norm_g, v_w_mkv, v_mem_q_g, v_mem_k_g, v_w_br_attn, v_w_br_conv, v_w_br_mem, v_w_out):
    a = dict(zip(INPUTS, (x, mem, positions, norm_g, w_in, b_gate, q_norm_g, w_uq, kv_norm_g, w_ukv, q_head_g, k_head_g, conv_w, conv_b, mem_norm_g, w_mkv, mem_q_g, mem_k_g, w_br_attn, w_br_conv, w_br_mem, w_out, loss_target, m_norm_g, m_w_in, m_b_gate, m_q_norm_g, m_w_uq, m_kv_norm_g, m_w_ukv, m_q_head_g, m_k_head_g, m_conv_w, m_conv_b, m_mem_norm_g, m_w_mkv, m_mem_q_g, m_mem_k_g, m_w_br_attn, m_w_br_conv, m_w_br_mem, m_w_out, v_norm_g, v_w_in, v_b_gate, v_q_norm_g, v_w_uq, v_kv_norm_g, v_w_ukv, v_q_head_g, v_k_head_g, v_conv_w, v_conv_b, v_mem_norm_g, v_w_mkv, v_mem_q_g, v_mem_k_g, v_w_br_attn, v_w_br_conv, v_w_br_mem, v_w_out)))
    x = a['x'][0]
    mem = a['mem'][0]
    tgt = a['loss_target'][0]
    s = x.shape[0]
    t_el = 256
    t_br = 512
    tq_f, tk_f, tq_b, tk_b = 512, 1024, 512, 512

    gathered = _allgather([a[n].astype(BF16) for n in BIG_ORDER] + [a['conv_w']], "ag_weights")
    full = {n: [_unshard_blocks(g8[:, l], BIG[n][1]) for l in range(DEPTH)] for n, g8 in zip(BIG_ORDER, gathered)}
    conv_w = gathered[-1].transpose(1, 2, 0, 3).reshape(DEPTH, 3, CW)

    inv_freq = ROPE_BASE ** (-jnp.arange(0, RP, 2, dtype=F32) / RP)
    ang = a['positions'][0].astype(F32)[:, None] * inv_freq
    cos, sin = jnp.cos(ang), jnp.sin(ang)
    rc = jnp.concatenate([jnp.ones((s, NOPE), F32), cos, cos, jnp.ones((s, HP - QKD), F32)], axis=1)
    rs = jnp.concatenate([jnp.zeros((s, NOPE), F32), -sin, sin, jnp.zeros((s, HP - QKD), F32)], axis=1)

    def small(n, l, width=None):
        v = a[n][l][None, :]
        return v if width is None else jnp.pad(v, ((0, 0), (0, width - v.shape[1])))

    saved = []
    layer_w = [_layer_weights(full, l) for l in range(DEPTH)]
    for l in range(DEPTH):
        w = layer_w[l]
        tag = ""
        h, ht = _rms_h(x, small('norm_g', l), 512, "rms_h" + tag)
        proj = _mm(h, w['w_main'], BF16, 2048, 512, "in_proj" + tag)
        proj_l = _mm(h, w['w_l'], BF16, 512, P_LW, "in_proj_lat" + tag)
        qng, kvng = small('q_norm_g', l), small('kv_norm_g', l)
        qhg, khg = small('q_head_g', l, HP), small('k_head_g', l, HP)
        q, k, v = _mla_prep(proj_l, rc, rs, qng, kvng, qhg, khg, w['w_uq'], w['w_ukv'], t_br, "mla_prep" + tag)
        o, lse = _flash_fwd(q, k, v, tq_f, tk_f, "flash_fwd" + tag)
        mk, mv = _mem_prep(mem, small('mem_norm_g', l), w['w_mkv'], small('mem_k_g', l), "mem_prep" + tag)
        cb, mqg = small('conv_b', l), small('mem_q_g', l)
        oa, oc, om = _branches(proj, o, mk, mv, conv_w[l], cb, mqg, t_br, "branches" + tag)
        x_out, aa, ac, am = _merge_fwd(x, proj, oa, oc, om, small('b_gate', l), w['w_br_attn'], w['w_br_conv'],
                                       w['w_br_mem'], w['w_out'], t_el, "merge" + tag)
        saved.append(dict(x=x, ht=ht, proj=proj, proj_l=proj_l, q=q, k=k, v=v, o=o, lse=lse, mk=mk, mv=mv,
                          oa=oa, oc=oc, om=om, aa=aa, ac=ac, am=am))
        x = x_out

    g, loss_parts = _loss_grad(x, tgt, 512, "loss")
    loss = lax.psum(jnp.sum(loss_parts), ("x", "y", "c"))

    gw = {n: [None] * DEPTH for n in WEIGHTS}
    for l in reversed(range(DEPTH)):
        w = layer_w[l]
        sv = saved[l]
        tag = ""
        tqb = min(tq_b, s)
        y, daa, dac, dam, dr, doa, doc, dom, dbg = _merge_bwd(
            g, sv['proj'], sv['aa'], sv['ac'], sv['am'], small('b_gate', l), w['w_out_t'], w['w_br_attn_t'],
            w['w_br_conv_t'], w['w_br_mem_t'], t_el, "merge_bwd" + tag)
        gw['b_gate'][l] = dbg[0]
        gw['w_out'][l] = _tn(y, g, 512, D, "dw_out" + tag)
        gw['w_br_attn'][l] = _tn(sv['oa'], daa, 512, D, "dw_attn" + tag)
        gw['w_br_conv'][l] = _tn(sv['oc'], dac, 512, D, "dw_conv" + tag)
        gw['w_br_mem'][l] = _tn(sv['om'], dam, 512, D, "dw_mem" + tag)
        cb, mqg = small('conv_b', l), small('mem_q_g', l)
        dcv, dmm, dga, do, dl, dcw, dcb, dmk, dmv, dmqg = _branches_bwd(
            sv['proj'], sv['o'], sv['mk'], sv['mv'], conv_w[l], cb, mqg, doa, doc, dom, t_br, "branches_bwd" + tag)
        gw['conv_w'][l], gw['conv_b'][l], gw['mem_q_g'][l] = dcw, dcb[0], dmqg[0]
        dwm, dmng, dmkg = _mem_prep_bwd(mem, small('mem_norm_g', l), w['w_mkv'], w['w_mkv_t'], small('mem_k_g', l),
                                        dmk, dmv, "mem_prep_bwd" + tag)
        gw['w_mkv'][l], gw['mem_norm_g'][l], gw['mem_k_g'][l] = dwm, dmng[0], dmkg[0]
        lse_r = sv['lse'].reshape(H, s // tqb, 1, tqb)
        dl_r = dl.reshape(H, s // tqb, 1, tqb)
        dq, dk, dv = _flash_bwd(sv['q'], sv['k'], sv['v'], do, lse_r, dl_r, tq_b, tk_b, "flash_bwd" + tag)
        qng, kvng = small('q_norm_g', l), small('kv_norm_g', l)
        qhg, khg = small('q_head_g', l, HP), small('k_head_g', l, HP)
        dlat, dwuq, dwukv, dqng, dkvng, dqhg, dkhg = _mla_prep_bwd(
            sv['proj_l'], rc, rs, qng, kvng, qhg, khg, w['w_uq'], w['w_ukv'], w['w_uq_t'], w['w_ukv_t'],
            dq, dk, dv, t_br, "mla_prep_bwd" + tag)
        gw['w_uq'][l] = _unpad_heads(dwuq, QKD)
        dwukv = dwukv.reshape(KVL, 2, H, HP)[:, :, :, :NOPE]
        gw['w_ukv'][l] = dwukv.transpose(0, 2, 1, 3).reshape(KVL, H * 2 * NOPE)
        gw['q_norm_g'][l], gw['kv_norm_g'][l] = dqng[0], dkvng[0]
        gw['q_head_g'][l], gw['k_head_g'][l] = dqhg[0, :QKD], dkhg[0, :QKD]
        dpieces = [dcv, dmm, dr, dga, dlat]
        dh = _mm_pieces(dpieces, w['w_in_t'], 512, 512, "d_h" + tag)
        gw['w_in'][l] = _unpermute_w_in([
            _mm_acc(sv['ht'], p, 1024, tn, f"dw_in_{i}" + tag)
            for i, (p, tn) in enumerate(zip(dpieces, (1024, 1024, 1536, 512, P_LW)))])
        g, dng = _rms_in_bwd(sv['x'], g, dh, small('norm_g', l), 512, "rms_bwd" + tag)
        gw['norm_g'][l] = dng[0]
    grad_x = g[None]

    sharded = BIG_ORDER + ['conv_w']
    axis_of = lambda n: 1 if n == 'conv_w' else BIG[n][1]
    send = [jnp.stack([_shard_blocks(gw[n][l], axis_of(n)) for l in range(DEPTH)], axis=1)
            .astype(F32 if n == 'conv_w' else BF16) for n in sharded]
    parts_big = _alltoall(send, "rs_grads")
    small_flat = jnp.concatenate([jnp.stack(gw[n]).reshape(-1) for n in SMALL_ORDER])
    n_small = small_flat.shape[0]
    parts_small = _allgather([_to_rows(small_flat)], "ag_small_grads")[0]

    outs = [{} for _ in range(4)]
    for n, parts in zip(sharded, parts_big):
        loc = a[n].shape
        two_d = lambda t: t.reshape(loc[0] * loc[1], loc[2])
        res = _adamw(parts.reshape(N_DEV, loc[0] * loc[1], loc[2]), two_d(a[n]), two_d(a['m_' + n]),
                     two_d(a['v_' + n]), "adamw_" + n)
        for d, r in zip(outs, res):
            d[n] = r.reshape(loc)
    pks = lambda pre: _to_rows(jnp.concatenate([a[pre + n].reshape(-1) for n in SMALL_ORDER]))
    res_small = _adamw(parts_small, pks(''), pks('m_'), pks('v_'), "adamw_small")
    for d, rsm in zip(outs, res_small):
        flat = rsm.reshape(-1)[:n_small]
        off = 0
        for n in SMALL_ORDER:
            d[n] = flat[off:off + DEPTH * SMALL[n]].reshape(DEPTH, SMALL[n])
            off += DEPTH * SMALL[n]
    result = [loss, grad_x]
    for d in outs:
        result += [d[n] for n in WEIGHTS]
    return tuple(result)
```

```python
import functools

import jax
import jax.numpy as jnp
from jax import lax
from jax.experimental import pallas as pl
from jax.experimental.pallas import tpu as pltpu

F32, BF16 = jnp.float32, jnp.bfloat16

N_DEV = 8
DEPTH = 4
D = 1024
QL, KVL, RP = 384, 256, 32
H, NOPE, QKD, VD = 8, 64, 96, 64
HP = 128
CW, MW, AW = 512, 512, 512
HM, MHD = 4, 128
IN_WIDTH = 7328
EPS = 1e-6
ROPE_BASE = 10000.0
ATT_SCALE = QKD ** -0.5
MEM_SCALE = MHD ** -0.5
LOG2E = 1.4426950408889634
QSCALE = ATT_SCALE * LOG2E

ADAM_LR, ADAM_B1, ADAM_B2, ADAM_EPS, ADAM_WD, ADAM_STEP = 0.001, 0.9, 0.999, 1e-08, 0.01, 10

LANES = 128
VMEM_LIMIT = 56 * 1024 * 1024

P_CONV, P_MEM, P_R, P_GA, P_L = 0, 2048, 3072, 6144, 6656
P_MAIN = 6656
P_LW = 768
P_W = P_MAIN + P_LW
P_WPAD = 7680

WEIGHTS = ['norm_g', 'w_in', 'b_gate', 'q_norm_g', 'w_uq', 'kv_norm_g', 'w_ukv', 'q_head_g', 'k_head_g',
           'conv_w', 'conv_b', 'mem_norm_g', 'w_mkv', 'mem_q_g', 'mem_k_g', 'w_br_attn', 'w_br_conv',
           'w_br_mem', 'w_out']
INPUTS = ['x', 'mem', 'positions'] + WEIGHTS + ['loss_target'] + ['m_' + n for n in WEIGHTS] + ['v_' + n for n in WEIGHTS]

BIG = {'w_in': ((D, IN_WIDTH), 1), 'w_uq': ((QL, H * QKD), 1), 'w_ukv': ((KVL, H * 128), 1),
       'w_mkv': ((D, 2 * MW), 0), 'w_br_attn': ((AW, D), 1), 'w_br_conv': ((CW, D), 1),
       'w_br_mem': ((MW, D), 1), 'w_out': ((D, D), 0)}
BIG_ORDER = ['w_in', 'w_uq', 'w_ukv', 'w_mkv', 'w_br_attn', 'w_br_conv', 'w_br_mem', 'w_out']
CONVW_PAD = 256
SMALL = {'norm_g': D, 'b_gate': 3 * D, 'q_norm_g': QL, 'kv_norm_g': KVL, 'q_head_g': QKD, 'k_head_g': QKD,
         'conv_b': CW, 'mem_norm_g': D, 'mem_q_g': MHD, 'mem_k_g': MHD}
SMALL_ORDER = list(SMALL)
ROW_ALIGN = 1024


def _call(body, **kw):
    return pl.pallas_call(body, **kw)


def _cp(sem=None):
    return pltpu.CompilerParams(dimension_semantics=sem, vmem_limit_bytes=VMEM_LIMIT)


def _full(shape):
    n = len(shape)
    return pl.BlockSpec(shape, lambda *_: (0,) * n)


def _rms(x, g, n):
    rs = lax.rsqrt(jnp.sum(x * x, axis=-1, keepdims=True) * (1.0 / n) + EPS)
    xh = x * rs
    return xh * g, xh, rs


def _rms_bwd(dy, xh, rs, g, n):
    dxh = dy * g
    dx = rs * (dxh - xh * (jnp.sum(dxh * xh, axis=-1, keepdims=True) * (1.0 / n)))
    return dx, jnp.sum(dy * xh, axis=0, keepdims=True)


def _sigmoid(x):
    return 1.0 / (1.0 + jnp.exp(-x))


def _swap_rope(u):
    lane = lax.broadcasted_iota(jnp.int32, u.shape, 1)
    up = pltpu.roll(u, 16, 1)
    dn = pltpu.roll(u, 112, 1)
    return jnp.where((lane >= 64) & (lane < 80), dn, jnp.where((lane >= 80) & (lane < 96), up, 0.0))


def _rope(u, c, sn):
    return u * c + _swap_rope(u) * sn


def _rope_adj(d, c, sn):
    return d * c + _swap_rope(d * sn)


def _dot(a, b):
    return jnp.dot(a, b, preferred_element_type=F32)


def _dot_nt(a, b):
    return lax.dot_general(a, b, (((1,), (1,)), ((), ())), preferred_element_type=F32)


def _dot_tn(a, b):
    return lax.dot_general(a, b, (((0,), (0,)), ((), ())), preferred_element_type=F32)


def _mm(a, b, out_dtype, tm, tn, name):
    m, k = a.shape
    _, n = b.shape
    tm, tn = min(tm, m), min(tn, n)

    def body(a_ref, b_ref, o_ref):
        o_ref[...] = _dot(a_ref[...].astype(BF16), b_ref[...]).astype(o_ref.dtype)

    return _call(
        body, name=name, grid=(m // tm, n // tn),
        in_specs=[pl.BlockSpec((tm, k), lambda i, j: (i, 0)), pl.BlockSpec((k, tn), lambda i, j: (0, j))],
        out_specs=pl.BlockSpec((tm, tn), lambda i, j: (i, j)),
        out_shape=jax.ShapeDtypeStruct((m, n), out_dtype),
        compiler_params=_cp(("parallel", "arbitrary")),
    )(a, b)


def _tn(a, b, ts, tn, name):
    s, ka = a.shape
    _, n = b.shape
    ts, tn = min(ts, s), min(tn, n)

    def body(a_ref, b_ref, o_ref):
        @pl.when(pl.program_id(1) == 0)
        def _():
            o_ref[...] = jnp.zeros_like(o_ref)

        o_ref[...] += _dot_tn(a_ref[...].astype(BF16), b_ref[...].astype(BF16))

    return _call(
        body, name=name, grid=(n // tn, s // ts),
        in_specs=[pl.BlockSpec((ts, ka), lambda j, i: (i, 0)), pl.BlockSpec((ts, tn), lambda j, i: (i, j))],
        out_specs=pl.BlockSpec((ka, tn), lambda j, i: (0, j)),
        out_shape=jax.ShapeDtypeStruct((ka, n), F32),
        compiler_params=_cp(("parallel", "arbitrary")),
    )(a, b)


def _mm_pieces(pieces, b, tm, tn, name):
    n_p = len(pieces)
    m = pieces[0].shape[0]
    k, n = b.shape
    tm, tn = min(tm, m), min(tn, n)
    offs = [sum(p.shape[1] for p in pieces[:i]) for i in range(n_p)]

    def body(*refs):
        a_refs, b_ref, o_ref = refs[:n_p], refs[n_p], refs[n_p + 1]
        acc = None
        for a_ref, off in zip(a_refs, offs):
            d = _dot(a_ref[...], b_ref[off:off + a_ref.shape[1], :])
            acc = d if acc is None else acc + d
        o_ref[...] = acc

    return _call(
        body, name=name, grid=(m // tm, n // tn),
        in_specs=[pl.BlockSpec((tm, p.shape[1]), lambda i, j: (i, 0)) for p in pieces]
        + [pl.BlockSpec((k, tn), lambda i, j: (0, j))],
        out_specs=pl.BlockSpec((tm, tn), lambda i, j: (i, j)),
        out_shape=jax.ShapeDtypeStruct((m, n), F32),
        compiler_params=_cp(("parallel", "arbitrary")),
    )(*pieces, b)


def _mm_acc(a, b, tk, tn, name):
    m, k = a.shape
    _, n = b.shape
    tk, tn = min(tk, k), min(tn, n)

    def body(a_ref, b_ref, o_ref):
        @pl.when(pl.program_id(1) == 0)
        def _():
            o_ref[...] = jnp.zeros_like(o_ref)

        o_ref[...] += _dot(a_ref[...], b_ref[...])

    return _call(
        body, name=name, grid=(n // tn, k // tk),
        in_specs=[pl.BlockSpec((m, tk), lambda j, i: (0, i)), pl.BlockSpec((tk, tn), lambda j, i: (i, j))],
        out_specs=pl.BlockSpec((m, tn), lambda j, i: (0, j)),
        out_shape=jax.ShapeDtypeStruct((m, n), F32),
        compiler_params=_cp(("parallel", "arbitrary")),
    )(a, b)


def _rms_h(x, g, t, name):
    s = x.shape[0]
    t = min(t, s)

    def body(x_ref, g_ref, h_ref, ht_ref):
        h = _rms(x_ref[...], g_ref[...], D)[0]
        h_ref[...] = h.astype(BF16)
        ht_ref[...] = h.T.astype(BF16)

    return _call(
        body, name=name, grid=(s // t,),
        in_specs=[pl.BlockSpec((t, D), lambda i: (i, 0)), _full((1, D))],
        out_specs=[pl.BlockSpec((t, D), lambda i: (i, 0)), pl.BlockSpec((D, t), lambda i: (0, i))],
        out_shape=[jax.ShapeDtypeStruct((s, D), BF16), jax.ShapeDtypeStruct((D, s), BF16)],
        compiler_params=_cp(("parallel",)),
    )(x, g)


def _mla_heads(pl_blk, c, sn, qng, kvng, qhg, khg, wuq, wukv):
    ql = pl_blk[:, 0:QL].astype(F32)
    kvl = pl_blk[:, QL:QL + KVL].astype(F32)
    kpe = pl_blk[:, QL + KVL:P_LW].astype(F32)
    qn, qxh, qrs = _rms(ql, qng, QL)
    kvn, kvxh, kvrs = _rms(kvl, kvng, KVL)
    qn16, kvn16 = qn.astype(BF16), kvn.astype(BF16)
    qp = _dot(qn16, wuq)
    kvp = _dot(kvn16, wukv)
    return ql, kvl, kpe, qxh, qrs, kvxh, kvrs, qn16, kvn16, qp, kvp


def _mla_prep(proj_l, c, sn, qng, kvng, qhg, khg, wuq, wukv, t, name):
    s = proj_l.shape[0]
    t = min(t, s)

    def body(l_ref, c_ref, sn_ref, qng_ref, kvng_ref, qhg_ref, khg_ref, wuq_ref, wukv_ref, q_ref, k_ref, v_ref):
        cc, ss = c_ref[...], sn_ref[...]
        (_, _, kpe, _, _, _, _, _, _, qp, kvp) = _mla_heads(
            l_ref[...], cc, ss, qng_ref[...], kvng_ref[...], qhg_ref[...], khg_ref[...], wuq_ref[...], wukv_ref[...])
        lane = lax.broadcasted_iota(jnp.int32, cc.shape, 1)
        for h in range(H):
            u = qp[:, h * HP:(h + 1) * HP]
            q_ref[h] = (_rope(_rms(u, qhg_ref[...], QKD)[0], cc, ss) * QSCALE).astype(BF16)
            u = kvp[:, h * HP:(h + 1) * HP] + kpe
            k_ref[h] = _rope(_rms(u, khg_ref[...], QKD)[0], cc, ss).astype(BF16)
            v_ref[h] = jnp.where(lane == VD, 1.0, kvp[:, (H + h) * HP:(H + h + 1) * HP]).astype(BF16)

    hs = pl.BlockSpec((H, t, HP), lambda i: (0, i, 0))
    row = lambda w: pl.BlockSpec((t, w), lambda i: (i, 0))
    return _call(
        body, name=name, grid=(s // t,),
        in_specs=[row(P_LW), row(HP), row(HP), _full((1, QL)), _full((1, KVL)), _full((1, HP)), _full((1, HP)),
                  _full((QL, H * HP)), _full((KVL, 2 * H * HP))],
        out_specs=[hs, hs, hs],
        out_shape=[jax.ShapeDtypeStruct((H, s, HP), BF16)] * 3,
        compiler_params=_cp(("parallel",)),
    )(proj_l, c, sn, qng, kvng, qhg, khg, wuq, wukv)


def _flash_fwd(q, k, v, tq, tk, name):
    _, s, _ = q.shape
    tq, tk = min(tq, s), min(tk, s)
    nk, nc = s // tk, tk // LANES
    un = 8 if nk % 8 == 0 else 1

    def body(q_ref, k_ref, v_ref, o_ref, lse_ref, s_scr):
        qb = q_ref[0]

        def scores(jj, mx):
            for u in range(un):
                j = jj * un + u
                off = pl.multiple_of(j * tk, tk)
                sc = _dot_nt(qb, k_ref[0, pl.ds(off, tk), :])
                s_scr[j] = sc
                for cc in range(nc):
                    mx = jnp.maximum(mx, sc[:, cc * LANES:(cc + 1) * LANES])
            return mx

        mx = lax.fori_loop(0, nk // un, scores, jnp.full((tq, LANES), -jnp.inf, F32))
        m = jnp.max(mx, axis=-1, keepdims=True)
        mb = jnp.broadcast_to(m, (tq, LANES))

        def probs(jj, acc):
            for u in range(un):
                j = jj * un + u
                off = pl.multiple_of(j * tk, tk)
                sc = s_scr[j]
                ps = [jnp.exp2(sc[:, cc * LANES:(cc + 1) * LANES] - mb).astype(BF16) for cc in range(nc)]
                acc = acc + _dot(jnp.concatenate(ps, axis=-1), v_ref[0, pl.ds(off, tk), :])
            return acc

        acc = lax.fori_loop(0, nk // un, probs, jnp.zeros((tq, HP), F32))
        l = acc[:, VD:VD + 1]
        o_ref[0] = (acc[:, :VD] / l).astype(BF16)
        lse = jnp.broadcast_to(m + jnp.log(l) * LOG2E, (tq, LANES))
        lse_ref[0] = lse.T[0:1, :]

    return _call(
        body, name=name, grid=(H, s // tq),
        in_specs=[pl.BlockSpec((1, tq, HP), lambda h, i: (h, i, 0)),
                  pl.BlockSpec((1, s, HP), lambda h, i: (h, 0, 0), pipeline_mode=pl.Buffered(1)),
                  pl.BlockSpec((1, s, HP), lambda h, i: (h, 0, 0), pipeline_mode=pl.Buffered(1))],
        out_specs=[pl.BlockSpec((1, tq, VD), lambda h, i: (h, i, 0)), pl.BlockSpec((1, 1, tq), lambda h, i: (h, 0, i))],
        out_shape=[jax.ShapeDtypeStruct((H, s, VD), BF16), jax.ShapeDtypeStruct((H, 1, s), F32)],
        scratch_shapes=[pltpu.VMEM((nk, tq, tk), F32)],
        compiler_params=_cp(("parallel", "arbitrary")),
    )(q, k, v)


def _mem_prep(mem, mng, wmkv, mkg, name):
    m = mem.shape[0]

    def body(mem_ref, mng_ref, w_ref, mkg_ref, mk_ref, mv_ref):
        mn = _rms(mem_ref[...], mng_ref[...], D)[0].astype(BF16)
        mkv = _dot(mn, w_ref[...])
        for h in range(HM):
            mk_ref[h] = _rms(mkv[:, 2 * h * MHD:(2 * h + 1) * MHD], mkg_ref[...], MHD)[0].astype(BF16)
            mv_ref[h] = mkv[:, (2 * h + 1) * MHD:(2 * h + 2) * MHD].astype(BF16)

    return _call(
        body, name=name,
        in_specs=[_full((m, D)), _full((1, D)), _full((D, 2 * MW)), _full((1, MHD))],
        out_specs=[_full((HM, m, MHD))] * 2,
        out_shape=[jax.ShapeDtypeStruct((HM, m, MHD), BF16)] * 2,
        compiler_params=_cp(),
    )(mem, mng, wmkv, mkg)


def _conv_parts(cv, prev, nxt, first, last, cw, cb):
    t = cv.shape[0]
    c_b, c_c, c_u, g_c = (cv[:, i * CW:(i + 1) * CW].astype(F32) for i in range(4))
    z = c_c * c_u
    zp = jnp.where(first, 0.0, prev[15:16, CW:2 * CW].astype(F32) * prev[15:16, 2 * CW:3 * CW].astype(F32))
    zn = jnp.where(last, 0.0, nxt[0:1, CW:2 * CW].astype(F32) * nxt[0:1, 2 * CW:3 * CW].astype(F32))
    row = lax.broadcasted_iota(jnp.int32, (t, CW), 0)
    z_m1 = jnp.where(row == 0, zp, pltpu.roll(z, 1, 0))
    z_p1 = jnp.where(row == t - 1, zn, pltpu.roll(z, t - 1, 0))
    conv = cw[0:1] * z_m1 + cw[1:2] * z + cw[2:3] * z_p1 + cb
    return c_b, c_c, c_u, g_c, z, z_m1, z_p1, conv


def _mem_attn(qm, mqg, mk_ref, mv_ref):
    outs = []
    for h in range(HM):
        mq, mqxh, mqrs = _rms(qm[:, h * MHD:(h + 1) * MHD], mqg, MHD)
        mq16 = mq.astype(BF16)
        sc = _dot_nt(mq16, mk_ref[h]) * MEM_SCALE
        e = jnp.exp(sc - jnp.max(sc, axis=-1, keepdims=True))
        p = e / jnp.sum(e, axis=-1, keepdims=True)
        o = _dot(p.astype(BF16), mv_ref[h])
        outs.append((mq16, mqxh, mqrs, p, o))
    return outs


def _halo_specs(t, s, width):
    nb = s // 16
    prev = pl.BlockSpec((16, width), lambda i: (jnp.maximum(i * (t // 16) - 1, 0), 0))
    nxt = pl.BlockSpec((16, width), lambda i: (jnp.minimum((i + 1) * (t // 16), nb - 1), 0))
    return prev, nxt


def _branches(proj, o, mk, mv, cw, cb, mqg, t, name):
    s = proj.shape[0]
    t = min(t, s)
    nt = s // t

    def body(cv_ref, pv_ref, nx_ref, mm_ref, ga_ref, o_ref, mk_ref, mv_ref, cw_ref, cb_ref, mqg_ref,
             oa_ref, oc_ref, om_ref):
        i = pl.program_id(0)
        c_b, _, _, g_c, _, _, _, conv = _conv_parts(
            cv_ref[...], pv_ref[...], nx_ref[...], i == 0, i == nt - 1, cw_ref[...], cb_ref[...])
        oc_ref[...] = (c_b * conv * (g_c * _sigmoid(g_c))).astype(BF16)
        ga = ga_ref[...].astype(F32)
        ocat = jnp.concatenate([o_ref[h].astype(F32) for h in range(H)], axis=-1)
        oa_ref[...] = (ocat * (ga * _sigmoid(ga))).astype(BF16)
        mblk = mm_ref[...].astype(F32)
        gm = mblk[:, MW:]
        heads = _mem_attn(mblk[:, :MW], mqg_ref[...], mk_ref, mv_ref)
        om = jnp.concatenate([hh[4] for hh in heads], axis=-1)
        om_ref[...] = (om * (gm * _sigmoid(gm))).astype(BF16)

    pv, nx = _halo_specs(t, s, 4 * CW)
    out = pl.BlockSpec((t, 512), lambda i: (i, 0))
    return _call(
        body, name=name, grid=(nt,),
        in_specs=[pl.BlockSpec((t, 4 * CW), lambda i: (i, 0)), pv, nx,
                  pl.BlockSpec((t, 2 * MW), lambda i: (i, P_MEM // (2 * MW))),
                  pl.BlockSpec((t, AW), lambda i: (i, P_GA // AW)),
                  pl.BlockSpec((H, t, VD), lambda i: (0, i, 0)),
                  _full(mk.shape), _full(mv.shape), _full((3, CW)), _full((1, CW)), _full((1, MHD))],
        out_specs=[out, out, out],
        out_shape=[jax.ShapeDtypeStruct((s, 512), BF16)] * 3,
        compiler_params=_cp(("parallel",)),
    )(proj, proj, proj, proj, proj, o, mk, mv, cw, cb, mqg)


def _merge_fwd(x, proj, oa, oc, om, bg, wa, wc, wm, wo, t, name):
    s = x.shape[0]
    t = min(t, s)

    def body(x_ref, r_ref, oa_ref, oc_ref, om_ref, bg_ref, wa_ref, wc_ref, wm_ref, wo_ref,
             xo_ref, aa_ref, ac_ref, am_ref):
        y = jnp.zeros((t, D), F32)
        for j, (o_ref, w_ref, a_ref) in enumerate(((oa_ref, wa_ref, aa_ref), (oc_ref, wc_ref, ac_ref),
                                                   (om_ref, wm_ref, am_ref))):
            a = _dot(o_ref[...], w_ref[...])
            a_ref[...] = a.astype(BF16)
            rg = _sigmoid(r_ref[:, j * D:(j + 1) * D].astype(F32) + bg_ref[:, j * D:(j + 1) * D])
            y = y + rg * a
        xo_ref[...] = x_ref[...] + _dot(y.astype(BF16), wo_ref[...])

    row = lambda w: pl.BlockSpec((t, w), lambda i: (i, 0))
    return _call(
        body, name=name, grid=(s // t,),
        in_specs=[row(D), pl.BlockSpec((t, 3 * D), lambda i: (i, P_R // (3 * D))), row(512), row(512), row(512),
                  _full((1, 3 * D)), _full((512, D)), _full((512, D)), _full((512, D)), _full((D, D))],
        out_specs=[row(D), row(D), row(D), row(D)],
        out_shape=[jax.ShapeDtypeStruct((s, D), F32)] + [jax.ShapeDtypeStruct((s, D), BF16)] * 3,
        compiler_params=_cp(("parallel",)),
    )(x, proj, oa, oc, om, bg, wa, wc, wm, wo)


def _loss_grad(x, tgt, t, name):
    s = x.shape[0]
    t = min(t, s)

    def body(x_ref, t_ref, g_ref, l_ref):
        @pl.when(pl.program_id(0) == 0)
        def _():
            l_ref[...] = jnp.zeros_like(l_ref)

        e = x_ref[...] - t_ref[...]
        g_ref[...] = e * (1.0 / D)
        sq = e * e
        part = sq[:, 0:LANES]
        for j in range(1, D // LANES):
            part = part + sq[:, j * LANES:(j + 1) * LANES]
        acc = part[0:8]
        for j in range(1, t // 8):
            acc = acc + part[j * 8:(j + 1) * 8]
        l_ref[...] += acc * (0.5 / D)

    row = pl.BlockSpec((t, D), lambda i: (i, 0))
    return _call(
        body, name=name, grid=(s // t,),
        in_specs=[row, row], out_specs=[row, _full((8, LANES))],
        out_shape=[jax.ShapeDtypeStruct((s, D), F32), jax.ShapeDtypeStruct((8, LANES), F32)],
        compiler_params=_cp(("arbitrary",)),
    )(x, tgt)


def _merge_bwd(g, proj, aa, ac, am, bg, wot, wat, wct, wmt, t, name):
    s = g.shape[0]
    t = min(t, s)

    def body(g_ref, r_ref, aa_ref, ac_ref, am_ref, bg_ref, wot_ref, wat_ref, wct_ref, wmt_ref,
             y_ref, daa_ref, dac_ref, dam_ref, dr_ref, doa_ref, doc_ref, dom_ref, dbg_ref):
        @pl.when(pl.program_id(0) == 0)
        def _():
            dbg_ref[...] = jnp.zeros_like(dbg_ref)

        dy = _dot(g_ref[...].astype(BF16), wot_ref[...])
        y = jnp.zeros((t, D), F32)
        for j, (a_ref, wt_ref, da_ref, do_ref) in enumerate(((aa_ref, wat_ref, daa_ref, doa_ref),
                                                             (ac_ref, wct_ref, dac_ref, doc_ref),
                                                             (am_ref, wmt_ref, dam_ref, dom_ref))):
            a = a_ref[...].astype(F32)
            rg = _sigmoid(r_ref[:, j * D:(j + 1) * D].astype(F32) + bg_ref[:, j * D:(j + 1) * D])
            y = y + rg * a
            da = (dy * rg).astype(BF16)
            da_ref[...] = da
            dr = dy * a * rg * (1.0 - rg)
            dr_ref[:, j * D:(j + 1) * D] = dr.astype(BF16)
            dbg_ref[:, j * D:(j + 1) * D] += jnp.sum(dr, axis=0, keepdims=True)
            do_ref[...] = _dot(da, wt_ref[...])
        y_ref[...] = y.astype(BF16)

    row = lambda w: pl.BlockSpec((t, w), lambda i: (i, 0))
    return _call(
        body, name=name, grid=(s // t,),
        in_specs=[row(D), pl.BlockSpec((t, 3 * D), lambda i: (i, P_R // (3 * D))), row(D), row(D), row(D),
                  _full((1, 3 * D)), _full((D, D)), _full((D, 512)), _full((D, 512)), _full((D, 512))],
        out_specs=[row(D), row(D), row(D), row(D), row(3 * D), row(512), row(512), row(512), _full((1, 3 * D))],
        out_shape=[jax.ShapeDtypeStruct((s, D), BF16)] * 4 + [jax.ShapeDtypeStruct((s, 3 * D), BF16)]
        + [jax.ShapeDtypeStruct((s, 512), F32)] * 3 + [jax.ShapeDtypeStruct((1, 3 * D), F32)],
        compiler_params=_cp(("arbitrary",)),
    )(g, proj, aa, ac, am, bg, wot, wat, wct, wmt)


def _branches_bwd(proj, o, mk, mv, cw, cb, mqg, doa, doc, dom, t, name):
    s = proj.shape[0]
    t = min(t, s)
    nt = s // t
    m = mk.shape[1]

    def body(cv_ref, pv_ref, nx_ref, mm_ref, ga_ref, o_ref, mk_ref, mv_ref, cw_ref, cb_ref, mqg_ref,
             doa_ref, doc_ref, dcp_ref, dcn_ref, dom_ref,
             dcv_ref, dmm_ref, dga_ref, do_ref, dl_ref, dcw_ref, dcb_ref, dmk_ref, dmv_ref, dmqg_ref):
        i = pl.program_id(0)

        @pl.when(i == 0)
        def _():
            for r in (dcw_ref, dcb_ref, dmk_ref, dmv_ref, dmqg_ref):
                r[...] = jnp.zeros_like(r)

        first, last = i == 0, i == nt - 1
        cw_, cb_ = cw_ref[...], cb_ref[...]
        pv, nx = pv_ref[...], nx_ref[...]
        c_b, c_c, c_u, g_c, z, z_m1, z_p1, conv = _conv_parts(cv_ref[...], pv, nx, first, last, cw_, cb_)
        sg = _sigmoid(g_c)
        silu = g_c * sg
        dsilu = sg * (1.0 + g_c * (1.0 - sg))
        doc_ = doc_ref[...]
        dconv = doc_ * c_b * silu
        gp = pv[15:16, 3 * CW:4 * CW].astype(F32)
        gn = nx[0:1, 3 * CW:4 * CW].astype(F32)
        dconv_p = jnp.where(first, 0.0, dcp_ref[15:16, :] * pv[15:16, 0:CW].astype(F32) * (gp * _sigmoid(gp)))
        dconv_n = jnp.where(last, 0.0, dcn_ref[0:1, :] * nx[0:1, 0:CW].astype(F32) * (gn * _sigmoid(gn)))
        row = lax.broadcasted_iota(jnp.int32, (t, CW), 0)
        d_m1 = jnp.where(row == 0, dconv_p, pltpu.roll(dconv, 1, 0))
        d_p1 = jnp.where(row == t - 1, dconv_n, pltpu.roll(dconv, t - 1, 0))
        dz = cw_[0:1] * d_p1 + cw_[1:2] * dconv + cw_[2:3] * d_m1
        dcv_ref[:, 0:CW] = (doc_ * conv * silu).astype(BF16)
        dcv_ref[:, CW:2 * CW] = (dz * c_u).astype(BF16)
        dcv_ref[:, 2 * CW:3 * CW] = (dz * c_c).astype(BF16)
        dcv_ref[:, 3 * CW:4 * CW] = (doc_ * c_b * conv * dsilu).astype(BF16)
        dcw_ref[0:1, :] += jnp.sum(dconv * z_m1, axis=0, keepdims=True)
        dcw_ref[1:2, :] += jnp.sum(dconv * z, axis=0, keepdims=True)
        dcw_ref[2:3, :] += jnp.sum(dconv * z_p1, axis=0, keepdims=True)
        dcb_ref[...] += jnp.sum(dconv, axis=0, keepdims=True)
        ga = ga_ref[...].astype(F32)
        sg = _sigmoid(ga)
        doa_ = doa_ref[...]
        ocat = jnp.concatenate([o_ref[h].astype(F32) for h in range(H)], axis=-1)
        dga_ref[...] = (doa_ * ocat * (sg * (1.0 + ga * (1.0 - sg)))).astype(BF16)
        dog = doa_ * (ga * sg)
        zeros = jnp.zeros((t, HP - VD), F32)
        lane = lax.broadcasted_iota(jnp.int32, (t, LANES), 1)
        dmat = jnp.zeros((t, LANES), F32)
        for h in range(H):
            dh = dog[:, h * VD:(h + 1) * VD]
            do_ref[h] = jnp.concatenate([dh, zeros], axis=-1).astype(BF16)
            dmat = jnp.where(lane == h, jnp.sum(dh * ocat[:, h * VD:(h + 1) * VD], axis=-1, keepdims=True), dmat)
        dlt = dmat.T
        for h in range(H):
            dl_ref[h] = dlt[h:h + 1, :]
        mblk = mm_ref[...].astype(F32)
        gm = mblk[:, MW:]
        sg = _sigmoid(gm)
        dom_ = dom_ref[...]
        heads = _mem_attn(mblk[:, :MW], mqg_ref[...], mk_ref, mv_ref)
        om = jnp.concatenate([hh[4] for hh in heads], axis=-1)
        dmm_ref[:, MW:] = (dom_ * om * (sg * (1.0 + gm * (1.0 - sg)))).astype(BF16)
        dmo = dom_ * (gm * sg)
        dmqg = jnp.zeros((1, MHD), F32)
        for h in range(HM):
            mq16, mqxh, mqrs, p, _ = heads[h]
            dmo_h = dmo[:, h * MHD:(h + 1) * MHD].astype(BF16)
            dp = _dot_nt(dmo_h, mv_ref[h])
            ds = (p * (dp - jnp.sum(dp * p, axis=-1, keepdims=True)) * MEM_SCALE).astype(BF16)
            dmq = _dot(ds, mk_ref[h])
            dmk_ref[h] += _dot_tn(ds, mq16)
            dmv_ref[h] += _dot_tn(p.astype(BF16), dmo_h)
            dq, dg = _rms_bwd(dmq, mqxh, mqrs, mqg_ref[...], MHD)
            dmm_ref[:, h * MHD:(h + 1) * MHD] = dq.astype(BF16)
            dmqg = dmqg + dg
        dmqg_ref[...] += dmqg

    pv, nx = _halo_specs(t, s, 4 * CW)
    dpv, dnx = _halo_specs(t, s, CW)
    row = lambda w: pl.BlockSpec((t, w), lambda i: (i, 0))
    hs = lambda w: pl.BlockSpec((H, t, w), lambda i: (0, i, 0))
    return _call(
        body, name=name, grid=(nt,),
        in_specs=[pl.BlockSpec((t, 4 * CW), lambda i: (i, 0)), pv, nx,
                  pl.BlockSpec((t, 2 * MW), lambda i: (i, P_MEM // (2 * MW))),
                  pl.BlockSpec((t, AW), lambda i: (i, P_GA // AW)),
                  hs(VD), _full(mk.shape), _full(mv.shape), _full((3, CW)), _full((1, CW)), _full((1, MHD)),
                  row(512), row(512), dpv, dnx, row(512)],
        out_specs=[row(4 * CW), row(2 * MW), row(AW), hs(HP), pl.BlockSpec((H, 1, t), lambda i: (0, 0, i)),
                   _full((3, CW)), _full((1, CW)),
                   _full((HM, m, MHD)), _full((HM, m, MHD)), _full((1, MHD))],
        out_shape=[jax.ShapeDtypeStruct((s, 4 * CW), BF16), jax.ShapeDtypeStruct((s, 2 * MW), BF16),
                   jax.ShapeDtypeStruct((s, AW), BF16), jax.ShapeDtypeStruct((H, s, HP), BF16),
                   jax.ShapeDtypeStruct((H, 1, s), F32), jax.ShapeDtypeStruct((3, CW), F32),
                   jax.ShapeDtypeStruct((1, CW), F32), jax.ShapeDtypeStruct((HM, m, MHD), F32),
                   jax.ShapeDtypeStruct((HM, m, MHD), F32), jax.ShapeDtypeStruct((1, MHD), F32)],
        compiler_params=_cp(("arbitrary",)),
    )(proj, proj, proj, proj, proj, o, mk, mv, cw, cb, mqg, doa, doc, doc, doc, dom)


def _flash_bwd(q, k, v, do, lse, dl, tq, tk, name):
    _, s, _ = q.shape
    tq, tk = min(tq, s), min(tk, s)
    nq, nkt, nc = s // tq, s // tk, tk // LANES
    unroll = 8 if nq % 8 == 0 else 1

    def body(q_ref, do_ref, lse_ref, dl_ref, k_ref, v_ref, dq_ref, dk_ref, dv_ref, dq_acc):
        j = pl.program_id(1)

        @pl.when(j == 0)
        def _():
            dq_acc[...] = jnp.zeros_like(dq_acc)

        kb, vb = k_ref[0], v_ref[0]

        def step(ii, carry):
            dkt, dvt = carry
            for u in range(unroll):
                i = ii * unroll + u
                off = pl.multiple_of(i * tq, tq)
                qb = q_ref[0, pl.ds(off, tq), :]
                dob = do_ref[0, pl.ds(off, tq), :]
                lse_b = jnp.broadcast_to(lse_ref[0, i], (LANES, tq)).T
                dl_b = jnp.broadcast_to(dl_ref[0, i], (LANES, tq)).T
                sc = _dot_nt(qb, kb)
                dp = _dot_nt(dob, vb)
                ps, dss = [], []
                for cc in range(nc):
                    p = jnp.exp2(sc[:, cc * LANES:(cc + 1) * LANES] - lse_b)
                    ps.append(p.astype(BF16))
                    dss.append((p * (dp[:, cc * LANES:(cc + 1) * LANES] - dl_b)).astype(BF16))
                p16, ds16 = jnp.concatenate(ps, axis=-1), jnp.concatenate(dss, axis=-1)
                dvt = dvt + _dot_tn(dob, p16)
                dkt = dkt + _dot_tn(qb, ds16)
                dq_acc[pl.ds(off, tq), :] += _dot(ds16, kb)
            return dkt, dvt

        dkt, dvt = lax.fori_loop(0, nq // unroll, step, (jnp.zeros((HP, tk), F32), jnp.zeros((HP, tk), F32)))
        dk_ref[0] = (dkt.T * (1.0 / LOG2E)).astype(BF16)
        dv_ref[0] = dvt.T.astype(BF16)

        @pl.when(j == nkt - 1)
        def _():
            dq_ref[0] = (dq_acc[...] * ATT_SCALE).astype(BF16)

    whole = pl.BlockSpec((1, s, HP), lambda h, j: (h, 0, 0))
    stat = pl.BlockSpec((1, nq, 1, tq), lambda h, j: (h, 0, 0, 0))
    tile = pl.BlockSpec((1, tk, HP), lambda h, j: (h, j, 0))
    return _call(
        body, name=name, grid=(H, nkt),
        in_specs=[whole, whole, stat, stat, tile, tile],
        out_specs=[whole, tile, tile],
        out_shape=[jax.ShapeDtypeStruct((H, s, HP), BF16)] * 3,
        scratch_shapes=[pltpu.VMEM((s, HP), F32)],
        compiler_params=_cp(("arbitrary", "arbitrary")),
    )(q, do, lse, dl, k, v)


def _mla_prep_bwd(proj_l, c, sn, qng, kvng, qhg, khg, wuq, wukv, wuqt, wukvt, dq, dk, dv, t, name):
    s = proj_l.shape[0]
    t = min(t, s)

    def body(l_ref, c_ref, sn_ref, qng_ref, kvng_ref, qhg_ref, khg_ref, wuq_ref, wukv_ref, wuqt_ref, wukvt_ref,
             dq_ref, dk_ref, dv_ref, dl_ref, dwuq_ref, dwukv_ref, dqng_ref, dkvng_ref, dqhg_ref, dkhg_ref):
        @pl.when(pl.program_id(0) == 0)
        def _():
            for r in (dwuq_ref, dwukv_ref, dqng_ref, dkvng_ref, dqhg_ref, dkhg_ref):
                r[...] = jnp.zeros_like(r)

        cc, ss = c_ref[...], sn_ref[...]
        qhg, khg = qhg_ref[...], khg_ref[...]
        (_, _, kpe, qxh, qrs, kvxh, kvrs, qn16, kvn16, qp, kvp) = _mla_heads(
            l_ref[...], cc, ss, qng_ref[...], kvng_ref[...], qhg, khg, wuq_ref[...], wukv_ref[...])
        lane = lax.broadcasted_iota(jnp.int32, (t, HP), 1)
        dqp, dkp, dvp = [], [], []
        dkpe = jnp.zeros((t, HP), F32)
        dqhg = jnp.zeros((1, HP), F32)
        dkhg = jnp.zeros((1, HP), F32)
        for h in range(H):
            _, xh, rs = _rms(qp[:, h * HP:(h + 1) * HP], qhg, QKD)
            du, dg = _rms_bwd(_rope_adj(dq_ref[h].astype(F32), cc, ss), xh, rs, qhg, QKD)
            dqp.append(du)
            dqhg = dqhg + dg
            _, xh, rs = _rms(kvp[:, h * HP:(h + 1) * HP] + kpe, khg, QKD)
            du, dg = _rms_bwd(_rope_adj(dk_ref[h].astype(F32), cc, ss), xh, rs, khg, QKD)
            dkp.append(du)
            dkhg = dkhg + dg
            dkpe = dkpe + jnp.where((lane >= NOPE) & (lane < QKD), du, 0.0)
            dvp.append(dv_ref[h].astype(F32))
        dqhg_ref[...] += dqhg
        dkhg_ref[...] += dkhg
        dqp16 = jnp.concatenate(dqp, axis=-1).astype(BF16)
        dkvp16 = jnp.concatenate(dkp + dvp, axis=-1).astype(BF16)
        dwuq_ref[...] += _dot_tn(qn16, dqp16)
        dwukv_ref[...] += _dot_tn(kvn16, dkvp16)
        dql, dg = _rms_bwd(_dot(dqp16, wuqt_ref[...]), qxh, qrs, qng_ref[...], QL)
        dqng_ref[...] += dg
        dkvl, dg = _rms_bwd(_dot(dkvp16, wukvt_ref[...]), kvxh, kvrs, kvng_ref[...], KVL)
        dkvng_ref[...] += dg
        dl_ref[:, 0:QL] = dql.astype(BF16)
        dl_ref[:, QL:QL + KVL] = dkvl.astype(BF16)
        dl_ref[:, QL + KVL:P_LW] = dkpe.astype(BF16)

    hs = pl.BlockSpec((H, t, HP), lambda i: (0, i, 0))
    row = lambda w: pl.BlockSpec((t, w), lambda i: (i, 0))
    return _call(
        body, name=name, grid=(s // t,),
        in_specs=[row(P_LW), row(HP), row(HP), _full((1, QL)), _full((1, KVL)), _full((1, HP)), _full((1, HP)),
                  _full((QL, H * HP)), _full((KVL, 2 * H * HP)), _full((H * HP, QL)), _full((2 * H * HP, KVL)),
                  hs, hs, hs],
        out_specs=[row(P_LW), _full((QL, H * HP)), _full((KVL, 2 * H * HP)), _full((1, QL)), _full((1, KVL)),
                   _full((1, HP)), _full((1, HP))],
        out_shape=[jax.ShapeDtypeStruct((s, P_LW), BF16), jax.ShapeDtypeStruct((QL, H * HP), F32),
                   jax.ShapeDtypeStruct((KVL, 2 * H * HP), F32), jax.ShapeDtypeStruct((1, QL), F32),
                   jax.ShapeDtypeStruct((1, KVL), F32), jax.ShapeDtypeStruct((1, HP), F32),
                   jax.ShapeDtypeStruct((1, HP), F32)],
        compiler_params=_cp(("arbitrary",)),
    )(proj_l, c, sn, qng, kvng, qhg, khg, wuq, wukv, wuqt, wukvt, dq, dk, dv)


def _mem_prep_bwd(mem, mng, wmkv, wmkvt, mkg, dmk, dmv, name):
    m = mem.shape[0]

    def body(mem_ref, mng_ref, w_ref, wt_ref, mkg_ref, dmk_ref, dmv_ref, dw_ref, dmng_ref, dmkg_ref):
        mn, xh, _ = _rms(mem_ref[...], mng_ref[...], D)
        mn16 = mn.astype(BF16)
        mkv = _dot(mn16, w_ref[...])
        parts = []
        dmkg = jnp.zeros((1, MHD), F32)
        for h in range(HM):
            _, kxh, krs = _rms(mkv[:, 2 * h * MHD:(2 * h + 1) * MHD], mkg_ref[...], MHD)
            du, dg = _rms_bwd(dmk_ref[h], kxh, krs, mkg_ref[...], MHD)
            dmkg = dmkg + dg
            parts += [du, dmv_ref[h]]
        dmkv = jnp.concatenate(parts, axis=-1).astype(BF16)
        dw_ref[...] = _dot_tn(mn16, dmkv)
        dmn = _dot(dmkv, wt_ref[...])
        dmng_ref[...] = jnp.sum(dmn * xh, axis=0, keepdims=True)
        dmkg_ref[...] = dmkg

    return _call(
        body, name=name,
        in_specs=[_full((m, D)), _full((1, D)), _full((D, 2 * MW)), _full((2 * MW, D)), _full((1, MHD)),
                  _full((HM, m, MHD)), _full((HM, m, MHD))],
        out_specs=[_full((D, 2 * MW)), _full((1, D)), _full((1, MHD))],
        out_shape=[jax.ShapeDtypeStruct((D, 2 * MW), F32), jax.ShapeDtypeStruct((1, D), F32),
                   jax.ShapeDtypeStruct((1, MHD), F32)],
        compiler_params=_cp(),
    )(mem, mng, wmkv, wmkvt, mkg, dmk, dmv)


def _rms_in_bwd(x, g_out, dh, ng, t, name):
    s = x.shape[0]
    t = min(t, s)

    def body(x_ref, go_ref, dh_ref, ng_ref, dx_ref, dng_ref):
        @pl.when(pl.program_id(0) == 0)
        def _():
            dng_ref[...] = jnp.zeros_like(dng_ref)

        _, xh, rs = _rms(x_ref[...], ng_ref[...], D)
        dx, dg = _rms_bwd(dh_ref[...], xh, rs, ng_ref[...], D)
        dx_ref[...] = go_ref[...] + dx
        dng_ref[...] += dg

    row = pl.BlockSpec((t, D), lambda i: (i, 0))
    return _call(
        body, name=name, grid=(s // t,),
        in_specs=[row, row, row, _full((1, D))], out_specs=[row, _full((1, D))],
        out_shape=[jax.ShapeDtypeStruct((s, D), F32), jax.ShapeDtypeStruct((1, D), F32)],
        compiler_params=_cp(("arbitrary",)),
    )(x, g_out, dh, ng)


def _allgather(arrs, name):
    n = len(arrs)

    def body(*refs):
        x_refs, out_refs = refs[:n], refs[n:2 * n]
        send_sems, recv_sems, local_sems = refs[2 * n:]
        x, y, c = lax.axis_index("x"), lax.axis_index("y"), lax.axis_index("c")
        me, sibling = (x, y, c), (x, y, 1 - c)
        chips = [(1 - x, y), (x, 1 - y), (1 - x, 1 - y)]

        def slot(a, px, py, pc):
            return out_refs[a].at[4 * px + 2 * py + pc]

        def copy(a, k, block, to, src=None):
            return pltpu.make_async_remote_copy(
                src_ref=slot(a, *block) if src is None else src, dst_ref=slot(a, *block),
                send_sem=send_sems.at[a, k], recv_sem=recv_sems.at[a, k],
                device_id=to, device_id_type=pl.DeviceIdType.MESH)

        mine = [pltpu.make_async_copy(x_refs[a], slot(a, *me), local_sems.at[a]) for a in range(n)]
        first, passed = [], []
        for a in range(n):
            mine[a].start()
            first.append(copy(a, 0, me, sibling, src=x_refs[a]))
            first += [copy(a, 1 + j, me, (*chip, c), src=x_refs[a]) for j, chip in enumerate(chips)]
        for cp in first:
            cp.start()
        for j, chip in enumerate(chips):
            for a in range(n):
                copy(a, 1 + j, (*chip, c), me).wait_recv()
                passed.append(copy(a, 4 + j, (*chip, c), sibling))
                passed[-1].start()
        for a in range(n):
            copy(a, 0, sibling, me).wait_recv()
        for j, chip in enumerate(chips):
            for a in range(n):
                copy(a, 4 + j, (*chip, 1 - c), me).wait_recv()
        for cp in first + passed:
            cp.wait_send()
        for cp in mine:
            cp.wait()

    any_spec = pl.BlockSpec(memory_space=pl.ANY)
    return _call(
        body, name=name,
        in_specs=[any_spec] * n, out_specs=[any_spec] * n,
        out_shape=[jax.ShapeDtypeStruct((N_DEV,) + a.shape, a.dtype) for a in arrs],
        scratch_shapes=[pltpu.SemaphoreType.DMA((n, 7)), pltpu.SemaphoreType.DMA((n, 7)),
                        pltpu.SemaphoreType.DMA((n,))],
    )(*arrs)


def _pair_exchange(arrs, name):
    n = len(arrs)

    def body(*refs):
        x_refs, out_refs = refs[:n], refs[n:2 * n]
        send_sems, recv_sems = refs[2 * n:]
        x, y, c = lax.axis_index("x"), lax.axis_index("y"), lax.axis_index("c")
        copies = [pltpu.make_async_remote_copy(
            src_ref=x_refs[a].at[1 - c], dst_ref=out_refs[a],
            send_sem=send_sems.at[a], recv_sem=recv_sems.at[a],
            device_id=(x, y, 1 - c), device_id_type=pl.DeviceIdType.MESH) for a in range(n)]
        for cp in copies:
            cp.start()
        for cp in copies:
            cp.wait_recv()
        for cp in copies:
            cp.wait_send()

    any_spec = pl.BlockSpec(memory_space=pl.ANY)
    return _call(
        body, name=name,
        in_specs=[any_spec] * n, out_specs=[any_spec] * n,
        out_shape=[jax.ShapeDtypeStruct(a.shape[1:], a.dtype) for a in arrs],
        scratch_shapes=[pltpu.SemaphoreType.DMA((n,)), pltpu.SemaphoreType.DMA((n,))],
    )(*arrs)


def _chip_exchange(arrs, name):
    n = len(arrs)

    def body(*refs):
        x_refs, out_refs = refs[:n], refs[n:2 * n]
        send_sems, recv_sems, local_sems = refs[2 * n:]
        x, y, c = lax.axis_index("x"), lax.axis_index("y"), lax.axis_index("c")
        me = 2 * x + y
        mine = [pltpu.make_async_copy(x_refs[a].at[me], out_refs[a].at[me], local_sems.at[a]) for a in range(n)]
        for cp in mine:
            cp.start()
        copies = []
        for k, (px, py) in enumerate([(1 - x, y), (x, 1 - y), (1 - x, 1 - y)]):
            for a in range(n):
                copies.append(pltpu.make_async_remote_copy(
                    src_ref=x_refs[a].at[2 * px + py], dst_ref=out_refs[a].at[me],
                    send_sem=send_sems.at[a, k], recv_sem=recv_sems.at[a, k],
                    device_id=(px, py, c), device_id_type=pl.DeviceIdType.MESH))
        for cp in copies:
            cp.start()
        for cp in copies:
            cp.wait_recv()
        for cp in copies:
            cp.wait_send()
        for cp in mine:
            cp.wait()

    any_spec = pl.BlockSpec(memory_space=pl.ANY)
    return _call(
        body, name=name,
        in_specs=[any_spec] * n, out_specs=[any_spec] * n,
        out_shape=[jax.ShapeDtypeStruct(a.shape, a.dtype) for a in arrs],
        scratch_shapes=[pltpu.SemaphoreType.DMA((n, 3)), pltpu.SemaphoreType.DMA((n, 3)),
                        pltpu.SemaphoreType.DMA((n,))],
    )(*arrs)


def _pair_add(a, b, name):
    r, c_ = a.shape
    cpad = -(-c_ // LANES) * LANES
    tr = r
    while tr * cpad * 4 > ADAMW_BLOCK_BYTES and tr % 32 == 0:
        tr //= 2

    def body(a_ref, b_ref, o_ref):
        o_ref[...] = (a_ref[...].astype(F32) + b_ref[...].astype(F32)).astype(o_ref.dtype)

    row = pl.BlockSpec((tr, c_), lambda i: (i, 0))
    return _call(
        body, name=name, grid=(r // tr,), in_specs=[row, row], out_specs=row,
        out_shape=jax.ShapeDtypeStruct((r, c_), a.dtype), compiler_params=_cp(("parallel",)),
    )(a, b)


ADAMW_BLOCK_BYTES = 4 * 1024 * 1024


def _adamw(parts, w, m, v, name):
    r, c_ = w.shape
    n_parts = parts.shape[0]
    cpad = -(-c_ // LANES) * LANES
    tr = r
    while N_DEV * tr * cpad * 4 > ADAMW_BLOCK_BYTES and tr % 16 == 0:
        tr //= 2
    c1 = 1.0 / (1.0 - ADAM_B1 ** ADAM_STEP)
    c2 = 1.0 / (1.0 - ADAM_B2 ** ADAM_STEP)

    def body(p_ref, w_ref, m_ref, v_ref, g_ref, d_ref, nm_ref, nv_ref):
        g = p_ref[0].astype(F32)
        for j in range(1, n_parts):
            g = g + p_ref[j].astype(F32)
        nm = ADAM_B1 * m_ref[...] + (1.0 - ADAM_B1) * g
        nv = ADAM_B2 * v_ref[...] + (1.0 - ADAM_B2) * (g * g)
        g_ref[...] = g
        nm_ref[...] = nm
        nv_ref[...] = nv
        d_ref[...] = -ADAM_LR * ((nm * c1) / (jnp.sqrt(nv * c2) + ADAM_EPS) + ADAM_WD * w_ref[...])

    row = pl.BlockSpec((tr, c_), lambda i: (i, 0))
    return _call(
        body, name=name, grid=(r // tr,),
        in_specs=[pl.BlockSpec((n_parts, tr, c_), lambda i: (0, i, 0)), row, row, row],
        out_specs=[row] * 4, out_shape=[jax.ShapeDtypeStruct((r, c_), F32)] * 4,
        compiler_params=_cp(("parallel",)),
    )(parts, w, m, v)


def _to_rows(flat, align=8):
    n = flat.shape[-1]
    rows = -(-n // (LANES * align)) * align
    return jnp.pad(flat, (0, rows * LANES - n)).reshape(rows, LANES)


def _shard_blocks(full, axis):
    r, c_ = full.shape
    if axis == 0:
        return full.reshape(N_DEV, r // N_DEV, c_)
    return full.reshape(r, N_DEV, c_ // N_DEV).transpose(1, 0, 2)


def _unshard_blocks(blocks, axis):
    _, r, c_ = blocks.shape
    if axis == 0:
        return blocks.reshape(N_DEV * r, c_)
    return blocks.transpose(1, 0, 2).reshape(r, N_DEV * c_)


def _pad_heads(w, width):
    k = w.shape[0]
    return jnp.pad(w.reshape(k, H, width), ((0, 0), (0, 0), (0, HP - width))).reshape(k, H * HP)


def _unpad_heads(w, width):
    k = w.shape[0]
    return w.reshape(k, H, HP)[:, :, :width].reshape(k, H * width)


IN_SPLIT = {'q_lat': (0, 384), 'kv_lat': (384, 640), 'k_pe': (640, 672), 'c_b': (672, 1184), 'c_c': (1184, 1696),
            'c_u': (1696, 2208), 'q_mem': (2208, 2720), 'g_attn': (2720, 3232), 'g_conv': (3232, 3744),
            'g_mem': (3744, 4256), 'r': (4256, 7328)}
P_ORDER = ['c_b', 'c_c', 'c_u', 'g_conv', 'q_mem', 'g_mem', 'r', 'g_attn', 'q_lat', 'kv_lat']


def _permute_w_in(w):
    k = w.shape[0]
    cols = [w[:, IN_SPLIT[n][0]:IN_SPLIT[n][1]] for n in P_ORDER]
    kpe = w[:, IN_SPLIT['k_pe'][0]:IN_SPLIT['k_pe'][1]]
    cols += [jnp.zeros((k, NOPE), w.dtype), kpe, jnp.zeros((k, HP - QKD), w.dtype)]
    return jnp.concatenate(cols, axis=1)


def _unpermute_w_in(pieces):
    bounds, off = [], 0
    for p in pieces:
        bounds.append((off, off + p.shape[1]))
        off += p.shape[1]

    def cols(lo, hi):
        for p, (b0, b1) in zip(pieces, bounds):
            if b0 <= lo and hi <= b1:
                return p[:, lo - b0:hi - b0]
        raise ValueError("a column range straddles two pieces")

    off, pos = 0, {}
    for n in P_ORDER:
        wd = IN_SPLIT[n][1] - IN_SPLIT[n][0]
        pos[n] = (off, off + wd)
        off += wd
    pos['k_pe'] = (off + NOPE, off + QKD)
    order = sorted(IN_SPLIT, key=lambda n: IN_SPLIT[n][0])
    return jnp.concatenate([cols(*pos[n]) for n in order], axis=1)


def _layer_weights(full, l):
    w = {}
    w_in_p = _permute_w_in(full['w_in'][l])
    w['w_main'] = w_in_p[:, :P_MAIN]
    w['w_l'] = w_in_p[:, P_MAIN:]
    w['w_in_t'] = w_in_p.T
    w['w_uq'] = _pad_heads(full['w_uq'][l], QKD)
    wukv = full['w_ukv'][l].reshape(KVL, H, 2, NOPE)
    kpart = jnp.pad(wukv[:, :, 0, :], ((0, 0), (0, 0), (0, HP - NOPE))).reshape(KVL, H * HP)
    vpart = jnp.pad(wukv[:, :, 1, :], ((0, 0), (0, 0), (0, HP - VD))).reshape(KVL, H * HP)
    w['w_ukv'] = jnp.concatenate([kpart, vpart], axis=1)
    w['w_uq_t'] = w['w_uq'].T
    w['w_ukv_t'] = w['w_ukv'].T
    w['w_mkv'] = full['w_mkv'][l]
    w['w_mkv_t'] = w['w_mkv'].T
    for n in ('w_br_attn', 'w_br_conv', 'w_br_mem', 'w_out'):
        w[n] = full[n][l]
        w[n + '_t'] = w[n].T
    return w


def kernel(x, mem, positions, norm_g, w_in, b_gate, q_norm_g, w_uq, kv_norm_g, w_ukv, q_head_g, k_head_g, conv_w, conv_b, mem_norm_g, w_mkv, mem_q_g, mem_k_g, w_br_attn, w_br_conv, w_br_mem, w_out, loss_target, m_norm_g, m_w_in, m_b_gate, m_q_norm_g, m_w_uq, m_kv_norm_g, m_w_ukv, m_q_head_g, m_k_head_g, m_conv_w, m_conv_b, m_mem_norm_g, m_w_mkv, m_mem_q_g, m_mem_k_g, m_w_br_attn, m_w_br_conv, m_w_br_mem, m_w_out, v_norm_g, v_w_in, v_b_gate, v_q_norm_g, v_w_uq, v_kv_norm_g, v_w_ukv, v_q_head_g, v_k_head_g, v_conv_w, v_conv_b, v_mem_norm_g, v_w_mkv, v_mem_q_g, v_mem_k_g, v_w_br_attn, v_w_br_conv, v_w_br_mem, v_w_out):
    a = dict(zip(INPUTS, (x, mem, positions, norm_g, w_in, b_gate, q_norm_g, w_uq, kv_norm_g, w_ukv, q_head_g, k_head_g, conv_w, conv_b, mem_norm_g, w_mkv, mem_q_g, mem_k_g, w_br_attn, w_br_conv, w_br_mem, w_out, loss_target, m_norm_g, m_w_in, m_b_gate, m_q_norm_g, m_w_uq, m_kv_norm_g, m_w_ukv, m_q_head_g, m_k_head_g, m_conv_w, m_conv_b, m_mem_norm_g, m_w_mkv, m_mem_q_g, m_mem_k_g, m_w_br_attn, m_w_br_conv, m_w_br_mem, m_w_out, v_norm_g, v_w_in, v_b_gate, v_q_norm_g, v_w_uq, v_kv_norm_g, v_w_ukv, v_q_head_g, v_k_head_g, v_conv_w, v_conv_b, v_mem_norm_g, v_w_mkv, v_mem_q_g, v_mem_k_g, v_w_br_attn, v_w_br_conv, v_w_br_mem, v_w_out)))
    x = a['x'][0]
    mem = a['mem'][0]
    tgt = a['loss_target'][0]
    s = x.shape[0]
    t_el = 256
    t_br = 512
    tq_f, tk_f, tq_b, tk_b = 512, 1024, 512, 512

    gathered = _allgather([a[n].astype(BF16) for n in BIG_ORDER] + [a['conv_w']], "ag_weights")
    full = {n: [_unshard_blocks(g8[:, l], BIG[n][1]) for l in range(DEPTH)] for n, g8 in zip(BIG_ORDER, gathered)}
    conv_w = gathered[-1].transpose(1, 2, 0, 3).reshape(DEPTH, 3, CW)

    inv_freq = ROPE_BASE ** (-jnp.arange(0, RP, 2, dtype=F32) / RP)
    ang = a['positions'][0].astype(F32)[:, None] * inv_freq
    cos, sin = jnp.cos(ang), jnp.sin(ang)
    rc = jnp.concatenate([jnp.ones((s, NOPE), F32), cos, cos, jnp.ones((s, HP - QKD), F32)], axis=1)
    rs = jnp.concatenate([jnp.zeros((s, NOPE), F32), -sin, sin, jnp.zeros((s, HP - QKD), F32)], axis=1)

    def small(n, l, width=None):
        v = a[n][l][None, :]
        return v if width is None else jnp.pad(v, ((0, 0), (0, width - v.shape[1])))

    saved = []
    layer_w = [_layer_weights(full, l) for l in range(DEPTH)]
    for l in range(DEPTH):
        w = layer_w[l]
        tag = ""
        h, ht = _rms_h(x, small('norm_g', l), 512, "rms_h" + tag)
        proj = _mm(h, w['w_main'], BF16, 2048, 512, "in_proj" + tag)
        proj_l = _mm(h, w['w_l'], BF16, 512, P_LW, "in_proj_lat" + tag)
        qng, kvng = small('q_norm_g', l), small('kv_norm_g', l)
        qhg, khg = small('q_head_g', l, HP), small('k_head_g', l, HP)
        q, k, v = _mla_prep(proj_l, rc, rs, qng, kvng, qhg, khg, w['w_uq'], w['w_ukv'], t_br, "mla_prep" + tag)
        o, lse = _flash_fwd(q, k, v, tq_f, tk_f, "flash_fwd" + tag)
        mk, mv = _mem_prep(mem, small('mem_norm_g', l), w['w_mkv'], small('mem_k_g', l), "mem_prep" + tag)
        cb, mqg = small('conv_b', l), small('mem_q_g', l)
        oa, oc, om = _branches(proj, o, mk, mv, conv_w[l], cb, mqg, t_br, "branches" + tag)
        x_out, aa, ac, am = _merge_fwd(x, proj, oa, oc, om, small('b_gate', l), w['w_br_attn'], w['w_br_conv'],
                                       w['w_br_mem'], w['w_out'], t_el, "merge" + tag)
        saved.append(dict(x=x, ht=ht, proj=proj, proj_l=proj_l, q=q, k=k, v=v, o=o, lse=lse, mk=mk, mv=mv,
                          oa=oa, oc=oc, om=om, aa=aa, ac=ac, am=am))
        x = x_out

    g, loss_parts = _loss_grad(x, tgt, 512, "loss")
    loss = lax.psum(jnp.sum(loss_parts), ("x", "y", "c"))

    gw = {n: [None] * DEPTH for n in WEIGHTS}
    for l in reversed(range(DEPTH)):
        w = layer_w[l]
        sv = saved[l]
        tag = ""
        tqb = min(tq_b, s)
        y, daa, dac, dam, dr, doa, doc, dom, dbg = _merge_bwd(
            g, sv['proj'], sv['aa'], sv['ac'], sv['am'], small('b_gate', l), w['w_out_t'], w['w_br_attn_t'],
            w['w_br_conv_t'], w['w_br_mem_t'], t_el, "merge_bwd" + tag)
        gw['b_gate'][l] = dbg[0]
        gw['w_out'][l] = _tn(y, g, 512, D, "dw_out" + tag)
        gw['w_br_attn'][l] = _tn(sv['oa'], daa, 512, D, "dw_attn" + tag)
        gw['w_br_conv'][l] = _tn(sv['oc'], dac, 512, D, "dw_conv" + tag)
        gw['w_br_mem'][l] = _tn(sv['om'], dam, 512, D, "dw_mem" + tag)
        cb, mqg = small('conv_b', l), small('mem_q_g', l)
        dcv, dmm, dga, do, dl, dcw, dcb, dmk, dmv, dmqg = _branches_bwd(
            sv['proj'], sv['o'], sv['mk'], sv['mv'], conv_w[l], cb, mqg, doa, doc, dom, t_br, "branches_bwd" + tag)
        gw['conv_w'][l], gw['conv_b'][l], gw['mem_q_g'][l] = dcw, dcb[0], dmqg[0]
        dwm, dmng, dmkg = _mem_prep_bwd(mem, small('mem_norm_g', l), w['w_mkv'], w['w_mkv_t'], small('mem_k_g', l),
                                        dmk, dmv, "mem_prep_bwd" + tag)
        gw['w_mkv'][l], gw['mem_norm_g'][l], gw['mem_k_g'][l] = dwm, dmng[0], dmkg[0]
        lse_r = sv['lse'].reshape(H, s // tqb, 1, tqb)
        dl_r = dl.reshape(H, s // tqb, 1, tqb)
        dq, dk, dv = _flash_bwd(sv['q'], sv['k'], sv['v'], do, lse_r, dl_r, tq_b, tk_b, "flash_bwd" + tag)
        qng, kvng = small('q_norm_g', l), small('kv_norm_g', l)
        qhg, khg = small('q_head_g', l, HP), small('k_head_g', l, HP)
        dlat, dwuq, dwukv, dqng, dkvng, dqhg, dkhg = _mla_prep_bwd(
            sv['proj_l'], rc, rs, qng, kvng, qhg, khg, w['w_uq'], w['w_ukv'], w['w_uq_t'], w['w_ukv_t'],
            dq, dk, dv, t_br, "mla_prep_bwd" + tag)
        gw['w_uq'][l] = _unpad_heads(dwuq, QKD)
        dwukv = dwukv.reshape(KVL, 2, H, HP)[:, :, :, :NOPE]
        gw['w_ukv'][l] = dwukv.transpose(0, 2, 1, 3).reshape(KVL, H * 2 * NOPE)
        gw['q_norm_g'][l], gw['kv_norm_g'][l] = dqng[0], dkvng[0]
        gw['q_head_g'][l], gw['k_head_g'][l] = dqhg[0, :QKD], dkhg[0, :QKD]
        dpieces = [dcv, dmm, dr, dga, dlat]
        dh = _mm_pieces(dpieces, w['w_in_t'], 512, 512, "d_h" + tag)
        gw['w_in'][l] = _unpermute_w_in([
            _mm_acc(sv['ht'], p, 1024, tn, f"dw_in_{i}" + tag)
            for i, (p, tn) in enumerate(zip(dpieces, (1024, 1024, 1536, 512, P_LW)))])
        g, dng = _rms_in_bwd(sv['x'], g, dh, small('norm_g', l), 512, "rms_bwd" + tag)
        gw['norm_g'][l] = dng[0]
    grad_x = g[None]

    sharded = BIG_ORDER + ['conv_w']
    axis_of = lambda n: 1 if n == 'conv_w' else BIG[n][1]
    send = [jnp.stack([_shard_blocks(gw[n][l], axis_of(n)) for l in range(DEPTH)], axis=1)
            .astype(F32 if n == 'conv_w' else BF16) for n in sharded]
    send = [t.reshape((4, 2) + t.shape[1:]).swapaxes(0, 1) for t in send]
    from_sibling = _pair_exchange(send, "rs_pair")
    my_c = lax.axis_index("c")
    chip_sums = []
    for n, t, got in zip(sharded, send, from_sibling):
        own = lax.dynamic_index_in_dim(t, my_c, axis=0, keepdims=False)
        flat = lambda u: u.reshape(-1, u.shape[-1])
        chip_sums.append(_pair_add(flat(own), flat(got), "rs_add_" + n).reshape(got.shape))
    parts_big = _chip_exchange(chip_sums, "rs_chips")
    small_flat = jnp.concatenate([jnp.stack(gw[n]).reshape(-1) for n in SMALL_ORDER])
    n_small = small_flat.shape[0]
    parts_small = _allgather([_to_rows(small_flat)], "ag_small_grads")[0]

    outs = [{} for _ in range(4)]
    for n, parts in zip(sharded, parts_big):
        loc = a[n].shape
        two_d = lambda t: t.reshape(loc[0] * loc[1], loc[2])
        res = _adamw(parts.reshape(4, loc[0] * loc[1], loc[2]), two_d(a[n]), two_d(a['m_' + n]),
                     two_d(a['v_' + n]), "adamw_" + n)
        for d, r in zip(outs, res):
            d[n] = r.reshape(loc)
    pks = lambda pre: _to_rows(jnp.concatenate([a[pre + n].reshape(-1) for n in SMALL_ORDER]))
    res_small = _adamw(parts_small, pks(''), pks('m_'), pks('v_'), "adamw_small")
    for d, rsm in zip(outs, res_small):
        flat = rsm.reshape(-1)[:n_small]
        off = 0
        for n in SMALL_ORDER:
            d[n] = flat[off:off + DEPTH * SMALL[n]].reshape(DEPTH, SMALL[n])
            off += DEPTH * SMALL[n]
    result = [loss, grad_x]
    for d in outs:
        result += [d[n] for n in WEIGHTS]
    return tuple(result)
```

```python
import functools

import jax
import jax.numpy as jnp
from jax import lax
from jax.experimental import pallas as pl
from jax.experimental.pallas import tpu as pltpu

F32, BF16 = jnp.float32, jnp.bfloat16

N_DEV = 8
DEPTH = 4
D = 1024
QL, KVL, RP = 384, 256, 32
H, NOPE, QKD, VD = 8, 64, 96, 64
HP = 128
CW, MW, AW = 512, 512, 512
HM, MHD = 4, 128
IN_WIDTH = 7328
EPS = 1e-6
ROPE_BASE = 10000.0
ATT_SCALE = QKD ** -0.5
MEM_SCALE = MHD ** -0.5
LOG2E = 1.4426950408889634
QSCALE = ATT_SCALE * LOG2E

ADAM_LR, ADAM_B1, ADAM_B2, ADAM_EPS, ADAM_WD, ADAM_STEP = 0.001, 0.9, 0.999, 1e-08, 0.01, 10

LANES = 128
VMEM_LIMIT = 56 * 1024 * 1024

P_CONV, P_MEM, P_R, P_GA, P_L = 0, 2048, 3072, 6144, 6656
P_MAIN = 6656
P_LW = 768
P_W = P_MAIN + P_LW
P_WPAD = 7680

WEIGHTS = ['norm_g', 'w_in', 'b_gate', 'q_norm_g', 'w_uq', 'kv_norm_g', 'w_ukv', 'q_head_g', 'k_head_g',
           'conv_w', 'conv_b', 'mem_norm_g', 'w_mkv', 'mem_q_g', 'mem_k_g', 'w_br_attn', 'w_br_conv',
           'w_br_mem', 'w_out']
INPUTS = ['x', 'mem', 'positions'] + WEIGHTS + ['loss_target'] + ['m_' + n for n in WEIGHTS] + ['v_' + n for n in WEIGHTS]

BIG = {'w_in': ((D, IN_WIDTH), 1), 'w_uq': ((QL, H * QKD), 1), 'w_ukv': ((KVL, H * 128), 1),
       'w_mkv': ((D, 2 * MW), 0), 'w_br_attn': ((AW, D), 1), 'w_br_conv': ((CW, D), 1),
       'w_br_mem': ((MW, D), 1), 'w_out': ((D, D), 0)}
BIG_ORDER = ['w_in', 'w_uq', 'w_ukv', 'w_mkv', 'w_br_attn', 'w_br_conv', 'w_br_mem', 'w_out']
CONVW_PAD = 256
SMALL = {'norm_g': D, 'b_gate': 3 * D, 'q_norm_g': QL, 'kv_norm_g': KVL, 'q_head_g': QKD, 'k_head_g': QKD,
         'conv_b': CW, 'mem_norm_g': D, 'mem_q_g': MHD, 'mem_k_g': MHD}
SMALL_ORDER = list(SMALL)
ROW_ALIGN = 1024


def _call(body, **kw):
    return pl.pallas_call(body, **kw)


def _cp(sem=None):
    return pltpu.CompilerParams(dimension_semantics=sem, vmem_limit_bytes=VMEM_LIMIT)


def _full(shape):
    n = len(shape)
    return pl.BlockSpec(shape, lambda *_: (0,) * n)


def _rms(x, g, n):
    rs = lax.rsqrt(jnp.sum(x * x, axis=-1, keepdims=True) * (1.0 / n) + EPS)
    xh = x * rs
    return xh * g, xh, rs


def _rms_bwd(dy, xh, rs, g, n):
    dxh = dy * g
    dx = rs * (dxh - xh * (jnp.sum(dxh * xh, axis=-1, keepdims=True) * (1.0 / n)))
    return dx, jnp.sum(dy * xh, axis=0, keepdims=True)


def _sigmoid(x):
    return 1.0 / (1.0 + jnp.exp(-x))


def _swap_rope(u):
    lane = lax.broadcasted_iota(jnp.int32, u.shape, 1)
    up = pltpu.roll(u, 16, 1)
    dn = pltpu.roll(u, 112, 1)
    return jnp.where((lane >= 64) & (lane < 80), dn, jnp.where((lane >= 80) & (lane < 96), up, 0.0))


def _rope(u, c, sn):
    return u * c + _swap_rope(u) * sn


def _rope_adj(d, c, sn):
    return d * c + _swap_rope(d * sn)


def _dot(a, b):
    return jnp.dot(a, b, preferred_element_type=F32)


def _dot_nt(a, b):
    return lax.dot_general(a, b, (((1,), (1,)), ((), ())), preferred_element_type=F32)


def _dot_tn(a, b):
    return lax.dot_general(a, b, (((0,), (0,)), ((), ())), preferred_element_type=F32)


def _mm(a, b, out_dtype, tm, tn, name):
    m, k = a.shape
    _, n = b.shape
    tm, tn = min(tm, m), min(tn, n)

    def body(a_ref, b_ref, o_ref):
        o_ref[...] = _dot(a_ref[...].astype(BF16), b_ref[...]).astype(o_ref.dtype)

    return _call(
        body, name=name, grid=(m // tm, n // tn),
        in_specs=[pl.BlockSpec((tm, k), lambda i, j: (i, 0)), pl.BlockSpec((k, tn), lambda i, j: (0, j))],
        out_specs=pl.BlockSpec((tm, tn), lambda i, j: (i, j)),
        out_shape=jax.ShapeDtypeStruct((m, n), out_dtype),
        compiler_params=_cp(("parallel", "arbitrary")),
    )(a, b)


def _tn(a, b, ts, tn, name):
    s, ka = a.shape
    _, n = b.shape
    ts, tn = min(ts, s), min(tn, n)

    def body(a_ref, b_ref, o_ref):
        @pl.when(pl.program_id(1) == 0)
        def _():
            o_ref[...] = jnp.zeros_like(o_ref)

        o_ref[...] += _dot_tn(a_ref[...].astype(BF16), b_ref[...].astype(BF16))

    return _call(
        body, name=name, grid=(n // tn, s // ts),
        in_specs=[pl.BlockSpec((ts, ka), lambda j, i: (i, 0)), pl.BlockSpec((ts, tn), lambda j, i: (i, j))],
        out_specs=pl.BlockSpec((ka, tn), lambda j, i: (0, j)),
        out_shape=jax.ShapeDtypeStruct((ka, n), F32),
        compiler_params=_cp(("parallel", "arbitrary")),
    )(a, b)


def _mm_pieces(pieces, b, tm, tn, name):
    n_p = len(pieces)
    m = pieces[0].shape[0]
    k, n = b.shape
    tm, tn = min(tm, m), min(tn, n)
    offs = [sum(p.shape[1] for p in pieces[:i]) for i in range(n_p)]

    def body(*refs):
        a_refs, b_ref, o_ref = refs[:n_p], refs[n_p], refs[n_p + 1]
        acc = None
        for a_ref, off in zip(a_refs, offs):
            d = _dot(a_ref[...], b_ref[off:off + a_ref.shape[1], :])
            acc = d if acc is None else acc + d
        o_ref[...] = acc

    return _call(
        body, name=name, grid=(m // tm, n // tn),
        in_specs=[pl.BlockSpec((tm, p.shape[1]), lambda i, j: (i, 0)) for p in pieces]
        + [pl.BlockSpec((k, tn), lambda i, j: (0, j))],
        out_specs=pl.BlockSpec((tm, tn), lambda i, j: (i, j)),
        out_shape=jax.ShapeDtypeStruct((m, n), F32),
        compiler_params=_cp(("parallel", "arbitrary")),
    )(*pieces, b)


def _mm_acc(a, b, tk, tn, name):
    m, k = a.shape
    _, n = b.shape
    tk, tn = min(tk, k), min(tn, n)

    def body(a_ref, b_ref, o_ref):
        @pl.when(pl.program_id(1) == 0)
        def _():
            o_ref[...] = jnp.zeros_like(o_ref)

        o_ref[...] += _dot(a_ref[...], b_ref[...])

    return _call(
        body, name=name, grid=(n // tn, k // tk),
        in_specs=[pl.BlockSpec((m, tk), lambda j, i: (0, i)), pl.BlockSpec((tk, tn), lambda j, i: (i, j))],
        out_specs=pl.BlockSpec((m, tn), lambda j, i: (0, j)),
        out_shape=jax.ShapeDtypeStruct((m, n), F32),
        compiler_params=_cp(("parallel", "arbitrary")),
    )(a, b)


def _rms_h(x, g, t, name):
    s = x.shape[0]
    t = min(t, s)

    def body(x_ref, g_ref, h_ref, ht_ref):
        h = _rms(x_ref[...], g_ref[...], D)[0]
        h_ref[...] = h.astype(BF16)
        ht_ref[...] = h.T.astype(BF16)

    return _call(
        body, name=name, grid=(s // t,),
        in_specs=[pl.BlockSpec((t, D), lambda i: (i, 0)), _full((1, D))],
        out_specs=[pl.BlockSpec((t, D), lambda i: (i, 0)), pl.BlockSpec((D, t), lambda i: (0, i))],
        out_shape=[jax.ShapeDtypeStruct((s, D), BF16), jax.ShapeDtypeStruct((D, s), BF16)],
        compiler_params=_cp(("parallel",)),
    )(x, g)


def _mla_heads(pl_blk, c, sn, qng, kvng, qhg, khg, wuq, wukv):
    ql = pl_blk[:, 0:QL].astype(F32)
    kvl = pl_blk[:, QL:QL + KVL].astype(F32)
    kpe = pl_blk[:, QL + KVL:P_LW].astype(F32)
    qn, qxh, qrs = _rms(ql, qng, QL)
    kvn, kvxh, kvrs = _rms(kvl, kvng, KVL)
    qn16, kvn16 = qn.astype(BF16), kvn.astype(BF16)
    qp = _dot(qn16, wuq)
    kvp = _dot(kvn16, wukv)
    return ql, kvl, kpe, qxh, qrs, kvxh, kvrs, qn16, kvn16, qp, kvp


def _mla_prep(proj_l, c, sn, qng, kvng, qhg, khg, wuq, wukv, t, name):
    s = proj_l.shape[0]
    t = min(t, s)

    def body(l_ref, c_ref, sn_ref, qng_ref, kvng_ref, qhg_ref, khg_ref, wuq_ref, wukv_ref, q_ref, k_ref, v_ref):
        cc, ss = c_ref[...], sn_ref[...]
        (_, _, kpe, _, _, _, _, _, _, qp, kvp) = _mla_heads(
            l_ref[...], cc, ss, qng_ref[...], kvng_ref[...], qhg_ref[...], khg_ref[...], wuq_ref[...], wukv_ref[...])
        lane = lax.broadcasted_iota(jnp.int32, cc.shape, 1)
        for h in range(H):
            u = qp[:, h * HP:(h + 1) * HP]
            q_ref[h] = (_rope(_rms(u, qhg_ref[...], QKD)[0], cc, ss) * QSCALE).astype(BF16)
            u = kvp[:, h * HP:(h + 1) * HP] + kpe
            k_ref[h] = _rope(_rms(u, khg_ref[...], QKD)[0], cc, ss).astype(BF16)
            v_ref[h] = jnp.where(lane == VD, 1.0, kvp[:, (H + h) * HP:(H + h + 1) * HP]).astype(BF16)

    hs = pl.BlockSpec((H, t, HP), lambda i: (0, i, 0))
    row = lambda w: pl.BlockSpec((t, w), lambda i: (i, 0))
    return _call(
        body, name=name, grid=(s // t,),
        in_specs=[row(P_LW), row(HP), row(HP), _full((1, QL)), _full((1, KVL)), _full((1, HP)), _full((1, HP)),
                  _full((QL, H * HP)), _full((KVL, 2 * H * HP))],
        out_specs=[hs, hs, hs],
        out_shape=[jax.ShapeDtypeStruct((H, s, HP), BF16)] * 3,
        compiler_params=_cp(("parallel",)),
    )(proj_l, c, sn, qng, kvng, qhg, khg, wuq, wukv)


def _flash_fwd(q, k, v, tq, tk, name):
    _, s, _ = q.shape
    tq, tk = min(tq, s), min(tk, s)
    nk, nc = s // tk, tk // LANES
    un = 8 if nk % 8 == 0 else 1

    def body(q_ref, k_ref, v_ref, o_ref, lse_ref, s_scr):
        qb = q_ref[0]

        def scores(jj, mx):
            for u in range(un):
                j = jj * un + u
                off = pl.multiple_of(j * tk, tk)
                sc = _dot_nt(qb, k_ref[0, pl.ds(off, tk), :])
                s_scr[j] = sc
                for cc in range(nc):
                    mx = jnp.maximum(mx, sc[:, cc * LANES:(cc + 1) * LANES])
            return mx

        mx = lax.fori_loop(0, nk // un, scores, jnp.full((tq, LANES), -jnp.inf, F32))
        m = jnp.max(mx, axis=-1, keepdims=True)
        mb = jnp.broadcast_to(m, (tq, LANES))

        def probs(jj, acc):
            for u in range(un):
                j = jj * un + u
                off = pl.multiple_of(j * tk, tk)
                sc = s_scr[j]
                ps = [jnp.exp2(sc[:, cc * LANES:(cc + 1) * LANES] - mb).astype(BF16) for cc in range(nc)]
                acc = acc + _dot(jnp.concatenate(ps, axis=-1), v_ref[0, pl.ds(off, tk), :])
            return acc

        acc = lax.fori_loop(0, nk // un, probs, jnp.zeros((tq, HP), F32))
        l = acc[:, VD:VD + 1]
        o_ref[0] = (acc[:, :VD] / l).astype(BF16)
        lse = jnp.broadcast_to(m + jnp.log(l) * LOG2E, (tq, LANES))
        lse_ref[0] = lse.T[0:1, :]

    return _call(
        body, name=name, grid=(H, s // tq),
        in_specs=[pl.BlockSpec((1, tq, HP), lambda h, i: (h, i, 0)),
                  pl.BlockSpec((1, s, HP), lambda h, i: (h, 0, 0), pipeline_mode=pl.Buffered(1)),
                  pl.BlockSpec((1, s, HP), lambda h, i: (h, 0, 0), pipeline_mode=pl.Buffered(1))],
        out_specs=[pl.BlockSpec((1, tq, VD), lambda h, i: (h, i, 0)), pl.BlockSpec((1, 1, tq), lambda h, i: (h, 0, i))],
        out_shape=[jax.ShapeDtypeStruct((H, s, VD), BF16), jax.ShapeDtypeStruct((H, 1, s), F32)],
        scratch_shapes=[pltpu.VMEM((nk, tq, tk), F32)],
        compiler_params=_cp(("parallel", "arbitrary")),
    )(q, k, v)


def _mem_prep(mem, mng, wmkv, mkg, name):
    m = mem.shape[0]

    def body(mem_ref, mng_ref, w_ref, mkg_ref, mk_ref, mv_ref):
        mn = _rms(mem_ref[...], mng_ref[...], D)[0].astype(BF16)
        mkv = _dot(mn, w_ref[...])
        for h in range(HM):
            mk_ref[h] = _rms(mkv[:, 2 * h * MHD:(2 * h + 1) * MHD], mkg_ref[...], MHD)[0].astype(BF16)
            mv_ref[h] = mkv[:, (2 * h + 1) * MHD:(2 * h + 2) * MHD].astype(BF16)

    return _call(
        body, name=name,
        in_specs=[_full((m, D)), _full((1, D)), _full((D, 2 * MW)), _full((1, MHD))],
        out_specs=[_full((HM, m, MHD))] * 2,
        out_shape=[jax.ShapeDtypeStruct((HM, m, MHD), BF16)] * 2,
        compiler_params=_cp(),
    )(mem, mng, wmkv, mkg)


def _conv_parts(cv, prev, nxt, first, last, cw, cb):
    t = cv.shape[0]
    c_b, c_c, c_u, g_c = (cv[:, i * CW:(i + 1) * CW].astype(F32) for i in range(4))
    z = c_c * c_u
    zp = jnp.where(first, 0.0, prev[15:16, CW:2 * CW].astype(F32) * prev[15:16, 2 * CW:3 * CW].astype(F32))
    zn = jnp.where(last, 0.0, nxt[0:1, CW:2 * CW].astype(F32) * nxt[0:1, 2 * CW:3 * CW].astype(F32))
    row = lax.broadcasted_iota(jnp.int32, (t, CW), 0)
    z_m1 = jnp.where(row == 0, zp, pltpu.roll(z, 1, 0))
    z_p1 = jnp.where(row == t - 1, zn, pltpu.roll(z, t - 1, 0))
    conv = cw[0:1] * z_m1 + cw[1:2] * z + cw[2:3] * z_p1 + cb
    return c_b, c_c, c_u, g_c, z, z_m1, z_p1, conv


def _mem_attn(qm, mqg, mk_ref, mv_ref):
    outs = []
    for h in range(HM):
        mq, mqxh, mqrs = _rms(qm[:, h * MHD:(h + 1) * MHD], mqg, MHD)
        mq16 = mq.astype(BF16)
        sc = _dot_nt(mq16, mk_ref[h]) * MEM_SCALE
        e = jnp.exp(sc - jnp.max(sc, axis=-1, keepdims=True))
        p = e / jnp.sum(e, axis=-1, keepdims=True)
        o = _dot(p.astype(BF16), mv_ref[h])
        outs.append((mq16, mqxh, mqrs, p, o))
    return outs


def _halo_specs(t, s, width):
    nb = s // 16
    prev = pl.BlockSpec((16, width), lambda i: (jnp.maximum(i * (t // 16) - 1, 0), 0))
    nxt = pl.BlockSpec((16, width), lambda i: (jnp.minimum((i + 1) * (t // 16), nb - 1), 0))
    return prev, nxt


def _branches(proj, o, mk, mv, cw, cb, mqg, t, name):
    s = proj.shape[0]
    t = min(t, s)
    nt = s // t

    def body(cv_ref, pv_ref, nx_ref, mm_ref, ga_ref, o_ref, mk_ref, mv_ref, cw_ref, cb_ref, mqg_ref,
             oa_ref, oc_ref, om_ref):
        i = pl.program_id(0)
        c_b, _, _, g_c, _, _, _, conv = _conv_parts(
            cv_ref[...], pv_ref[...], nx_ref[...], i == 0, i == nt - 1, cw_ref[...], cb_ref[...])
        oc_ref[...] = (c_b * conv * (g_c * _sigmoid(g_c))).astype(BF16)
        ga = ga_ref[...].astype(F32)
        ocat = jnp.concatenate([o_ref[h].astype(F32) for h in range(H)], axis=-1)
        oa_ref[...] = (ocat * (ga * _sigmoid(ga))).astype(BF16)
        mblk = mm_ref[...].astype(F32)
        gm = mblk[:, MW:]
        heads = _mem_attn(mblk[:, :MW], mqg_ref[...], mk_ref, mv_ref)
        om = jnp.concatenate([hh[4] for hh in heads], axis=-1)
        om_ref[...] = (om * (gm * _sigmoid(gm))).astype(BF16)

    pv, nx = _halo_specs(t, s, 4 * CW)
    out = pl.BlockSpec((t, 512), lambda i: (i, 0))
    return _call(
        body, name=name, grid=(nt,),
        in_specs=[pl.BlockSpec((t, 4 * CW), lambda i: (i, 0)), pv, nx,
                  pl.BlockSpec((t, 2 * MW), lambda i: (i, P_MEM // (2 * MW))),
                  pl.BlockSpec((t, AW), lambda i: (i, P_GA // AW)),
                  pl.BlockSpec((H, t, VD), lambda i: (0, i, 0)),
                  _full(mk.shape), _full(mv.shape), _full((3, CW)), _full((1, CW)), _full((1, MHD))],
        out_specs=[out, out, out],
        out_shape=[jax.ShapeDtypeStruct((s, 512), BF16)] * 3,
        compiler_params=_cp(("parallel",)),
    )(proj, proj, proj, proj, proj, o, mk, mv, cw, cb, mqg)


def _merge_fwd(x, proj, oa, oc, om, bg, wa, wc, wm, wo, t, name):
    s = x.shape[0]
    t = min(t, s)

    def body(x_ref, r_ref, oa_ref, oc_ref, om_ref, bg_ref, wa_ref, wc_ref, wm_ref, wo_ref,
             xo_ref, aa_ref, ac_ref, am_ref):
        y = jnp.zeros((t, D), F32)
        for j, (o_ref, w_ref, a_ref) in enumerate(((oa_ref, wa_ref, aa_ref), (oc_ref, wc_ref, ac_ref),
                                                   (om_ref, wm_ref, am_ref))):
            a = _dot(o_ref[...], w_ref[...])
            a_ref[...] = a.astype(BF16)
            rg = _sigmoid(r_ref[:, j * D:(j + 1) * D].astype(F32) + bg_ref[:, j * D:(j + 1) * D])
            y = y + rg * a
        xo_ref[...] = x_ref[...] + _dot(y.astype(BF16), wo_ref[...])

    row = lambda w: pl.BlockSpec((t, w), lambda i: (i, 0))
    return _call(
        body, name=name, grid=(s // t,),
        in_specs=[row(D), pl.BlockSpec((t, 3 * D), lambda i: (i, P_R // (3 * D))), row(512), row(512), row(512),
                  _full((1, 3 * D)), _full((512, D)), _full((512, D)), _full((512, D)), _full((D, D))],
        out_specs=[row(D), row(D), row(D), row(D)],
        out_shape=[jax.ShapeDtypeStruct((s, D), F32)] + [jax.ShapeDtypeStruct((s, D), BF16)] * 3,
        compiler_params=_cp(("parallel",)),
    )(x, proj, oa, oc, om, bg, wa, wc, wm, wo)


def _loss_grad(x, tgt, t, name):
    s = x.shape[0]
    t = min(t, s)

    def body(x_ref, t_ref, g_ref, l_ref):
        @pl.when(pl.program_id(0) == 0)
        def _():
            l_ref[...] = jnp.zeros_like(l_ref)

        e = x_ref[...] - t_ref[...]
        g_ref[...] = e * (1.0 / D)
        sq = e * e
        part = sq[:, 0:LANES]
        for j in range(1, D // LANES):
            part = part + sq[:, j * LANES:(j + 1) * LANES]
        acc = part[0:8]
        for j in range(1, t // 8):
            acc = acc + part[j * 8:(j + 1) * 8]
        l_ref[...] += acc * (0.5 / D)

    row = pl.BlockSpec((t, D), lambda i: (i, 0))
    return _call(
        body, name=name, grid=(s // t,),
        in_specs=[row, row], out_specs=[row, _full((8, LANES))],
        out_shape=[jax.ShapeDtypeStruct((s, D), F32), jax.ShapeDtypeStruct((8, LANES), F32)],
        compiler_params=_cp(("arbitrary",)),
    )(x, tgt)


def _merge_bwd(g, proj, aa, ac, am, bg, wot, wat, wct, wmt, t, name):
    s = g.shape[0]
    t = min(t, s)

    def body(g_ref, r_ref, aa_ref, ac_ref, am_ref, bg_ref, wot_ref, wat_ref, wct_ref, wmt_ref,
             y_ref, daa_ref, dac_ref, dam_ref, dr_ref, doa_ref, doc_ref, dom_ref, dbg_ref):
        @pl.when(pl.program_id(0) == 0)
        def _():
            dbg_ref[...] = jnp.zeros_like(dbg_ref)

        dy = _dot(g_ref[...].astype(BF16), wot_ref[...])
        y = jnp.zeros((t, D), F32)
        for j, (a_ref, wt_ref, da_ref, do_ref) in enumerate(((aa_ref, wat_ref, daa_ref, doa_ref),
                                                             (ac_ref, wct_ref, dac_ref, doc_ref),
                                                             (am_ref, wmt_ref, dam_ref, dom_ref))):
            a = a_ref[...].astype(F32)
            rg = _sigmoid(r_ref[:, j * D:(j + 1) * D].astype(F32) + bg_ref[:, j * D:(j + 1) * D])
            y = y + rg * a
            da = (dy * rg).astype(BF16)
            da_ref[...] = da
            dr = dy * a * rg * (1.0 - rg)
            dr_ref[:, j * D:(j + 1) * D] = dr.astype(BF16)
            dbg_ref[:, j * D:(j + 1) * D] += jnp.sum(dr, axis=0, keepdims=True)
            do_ref[...] = _dot(da, wt_ref[...])
        y_ref[...] = y.astype(BF16)

    row = lambda w: pl.BlockSpec((t, w), lambda i: (i, 0))
    return _call(
        body, name=name, grid=(s // t,),
        in_specs=[row(D), pl.BlockSpec((t, 3 * D), lambda i: (i, P_R // (3 * D))), row(D), row(D), row(D),
                  _full((1, 3 * D)), _full((D, D)), _full((D, 512)), _full((D, 512)), _full((D, 512))],
        out_specs=[row(D), row(D), row(D), row(D), row(3 * D), row(512), row(512), row(512), _full((1, 3 * D))],
        out_shape=[jax.ShapeDtypeStruct((s, D), BF16)] * 4 + [jax.ShapeDtypeStruct((s, 3 * D), BF16)]
        + [jax.ShapeDtypeStruct((s, 512), F32)] * 3 + [jax.ShapeDtypeStruct((1, 3 * D), F32)],
        compiler_params=_cp(("arbitrary",)),
    )(g, proj, aa, ac, am, bg, wot, wat, wct, wmt)


def _branches_bwd(proj, o, mk, mv, cw, cb, mqg, doa, doc, dom, t, name):
    s = proj.shape[0]
    t = min(t, s)
    nt = s // t
    m = mk.shape[1]

    def body(cv_ref, pv_ref, nx_ref, mm_ref, ga_ref, o_ref, mk_ref, mv_ref, cw_ref, cb_ref, mqg_ref,
             doa_ref, doc_ref, dcp_ref, dcn_ref, dom_ref,
             dcv_ref, dmm_ref, dga_ref, do_ref, dl_ref, dcw_ref, dcb_ref, dmk_ref, dmv_ref, dmqg_ref):
        i = pl.program_id(0)

        @pl.when(i == 0)
        def _():
            for r in (dcw_ref, dcb_ref, dmk_ref, dmv_ref, dmqg_ref):
                r[...] = jnp.zeros_like(r)

        first, last = i == 0, i == nt - 1
        cw_, cb_ = cw_ref[...], cb_ref[...]
        pv, nx = pv_ref[...], nx_ref[...]
        c_b, c_c, c_u, g_c, z, z_m1, z_p1, conv = _conv_parts(cv_ref[...], pv, nx, first, last, cw_, cb_)
        sg = _sigmoid(g_c)
        silu = g_c * sg
        dsilu = sg * (1.0 + g_c * (1.0 - sg))
        doc_ = doc_ref[...]
        dconv = doc_ * c_b * silu
        gp = pv[15:16, 3 * CW:4 * CW].astype(F32)
        gn = nx[0:1, 3 * CW:4 * CW].astype(F32)
        dconv_p = jnp.where(first, 0.0, dcp_ref[15:16, :] * pv[15:16, 0:CW].astype(F32) * (gp * _sigmoid(gp)))
        dconv_n = jnp.where(last, 0.0, dcn_ref[0:1, :] * nx[0:1, 0:CW].astype(F32) * (gn * _sigmoid(gn)))
        row = lax.broadcasted_iota(jnp.int32, (t, CW), 0)
        d_m1 = jnp.where(row == 0, dconv_p, pltpu.roll(dconv, 1, 0))
        d_p1 = jnp.where(row == t - 1, dconv_n, pltpu.roll(dconv, t - 1, 0))
        dz = cw_[0:1] * d_p1 + cw_[1:2] * dconv + cw_[2:3] * d_m1
        dcv_ref[:, 0:CW] = (doc_ * conv * silu).astype(BF16)
        dcv_ref[:, CW:2 * CW] = (dz * c_u).astype(BF16)
        dcv_ref[:, 2 * CW:3 * CW] = (dz * c_c).astype(BF16)
        dcv_ref[:, 3 * CW:4 * CW] = (doc_ * c_b * conv * dsilu).astype(BF16)
        dcw_ref[0:1, :] += jnp.sum(dconv * z_m1, axis=0, keepdims=True)
        dcw_ref[1:2, :] += jnp.sum(dconv * z, axis=0, keepdims=True)
        dcw_ref[2:3, :] += jnp.sum(dconv * z_p1, axis=0, keepdims=True)
        dcb_ref[...] += jnp.sum(dconv, axis=0, keepdims=True)
        ga = ga_ref[...].astype(F32)
        sg = _sigmoid(ga)
        doa_ = doa_ref[...]
        ocat = jnp.concatenate([o_ref[h].astype(F32) for h in range(H)], axis=-1)
        dga_ref[...] = (doa_ * ocat * (sg * (1.0 + ga * (1.0 - sg)))).astype(BF16)
        dog = doa_ * (ga * sg)
        zeros = jnp.zeros((t, HP - VD), F32)
        lane = lax.broadcasted_iota(jnp.int32, (t, LANES), 1)
        dmat = jnp.zeros((t, LANES), F32)
        for h in range(H):
            dh = dog[:, h * VD:(h + 1) * VD]
            do_ref[h] = jnp.concatenate([dh, zeros], axis=-1).astype(BF16)
            dmat = jnp.where(lane == h, jnp.sum(dh * ocat[:, h * VD:(h + 1) * VD], axis=-1, keepdims=True), dmat)
        dlt = dmat.T
        for h in range(H):
            dl_ref[h] = dlt[h:h + 1, :]
        mblk = mm_ref[...].astype(F32)
        gm = mblk[:, MW:]
        sg = _sigmoid(gm)
        dom_ = dom_ref[...]
        heads = _mem_attn(mblk[:, :MW], mqg_ref[...], mk_ref, mv_ref)
        om = jnp.concatenate([hh[4] for hh in heads], axis=-1)
        dmm_ref[:, MW:] = (dom_ * om * (sg * (1.0 + gm * (1.0 - sg)))).astype(BF16)
        dmo = dom_ * (gm * sg)
        dmqg = jnp.zeros((1, MHD), F32)
        for h in range(HM):
            mq16, mqxh, mqrs, p, _ = heads[h]
            dmo_h = dmo[:, h * MHD:(h + 1) * MHD].astype(BF16)
            dp = _dot_nt(dmo_h, mv_ref[h])
            ds = (p * (dp - jnp.sum(dp * p, axis=-1, keepdims=True)) * MEM_SCALE).astype(BF16)
            dmq = _dot(ds, mk_ref[h])
            dmk_ref[h] += _dot_tn(ds, mq16)
            dmv_ref[h] += _dot_tn(p.astype(BF16), dmo_h)
            dq, dg = _rms_bwd(dmq, mqxh, mqrs, mqg_ref[...], MHD)
            dmm_ref[:, h * MHD:(h + 1) * MHD] = dq.astype(BF16)
            dmqg = dmqg + dg
        dmqg_ref[...] += dmqg

    pv, nx = _halo_specs(t, s, 4 * CW)
    dpv, dnx = _halo_specs(t, s, CW)
    row = lambda w: pl.BlockSpec((t, w), lambda i: (i, 0))
    hs = lambda w: pl.BlockSpec((H, t, w), lambda i: (0, i, 0))
    return _call(
        body, name=name, grid=(nt,),
        in_specs=[pl.BlockSpec((t, 4 * CW), lambda i: (i, 0)), pv, nx,
                  pl.BlockSpec((t, 2 * MW), lambda i: (i, P_MEM // (2 * MW))),
                  pl.BlockSpec((t, AW), lambda i: (i, P_GA // AW)),
                  hs(VD), _full(mk.shape), _full(mv.shape), _full((3, CW)), _full((1, CW)), _full((1, MHD)),
                  row(512), row(512), dpv, dnx, row(512)],
        out_specs=[row(4 * CW), row(2 * MW), row(AW), hs(HP), pl.BlockSpec((H, 1, t), lambda i: (0, 0, i)),
                   _full((3, CW)), _full((1, CW)),
                   _full((HM, m, MHD)), _full((HM, m, MHD)), _full((1, MHD))],
        out_shape=[jax.ShapeDtypeStruct((s, 4 * CW), BF16), jax.ShapeDtypeStruct((s, 2 * MW), BF16),
                   jax.ShapeDtypeStruct((s, AW), BF16), jax.ShapeDtypeStruct((H, s, HP), BF16),
                   jax.ShapeDtypeStruct((H, 1, s), F32), jax.ShapeDtypeStruct((3, CW), F32),
                   jax.ShapeDtypeStruct((1, CW), F32), jax.ShapeDtypeStruct((HM, m, MHD), F32),
                   jax.ShapeDtypeStruct((HM, m, MHD), F32), jax.ShapeDtypeStruct((1, MHD), F32)],
        compiler_params=_cp(("arbitrary",)),
    )(proj, proj, proj, proj, proj, o, mk, mv, cw, cb, mqg, doa, doc, doc, doc, dom)


def _flash_bwd(q, k, v, do, lse, dl, tq, tk, name):
    _, s, _ = q.shape
    tq, tk = min(tq, s), min(tk, s)
    nq, nkt, nc = s // tq, s // tk, tk // LANES
    unroll = 16 if nq % 16 == 0 else 1

    def body(q_ref, do_ref, lse_ref, dl_ref, k_ref, v_ref, dq_ref, dk_ref, dv_ref, dq_acc):
        j = pl.program_id(1)

        @pl.when(j == 0)
        def _():
            dq_acc[...] = jnp.zeros_like(dq_acc)

        kb, vb = k_ref[0], v_ref[0]

        def step(ii, carry):
            dkt, dvt = carry
            for u in range(unroll):
                i = ii * unroll + u
                off = pl.multiple_of(i * tq, tq)
                qb = q_ref[0, pl.ds(off, tq), :]
                dob = do_ref[0, pl.ds(off, tq), :]
                lse_b = jnp.broadcast_to(lse_ref[0, i], (LANES, tq)).T
                dl_b = jnp.broadcast_to(dl_ref[0, i], (LANES, tq)).T
                sc = _dot_nt(qb, kb)
                dp = _dot_nt(dob, vb)
                ps, dss = [], []
                for cc in range(nc):
                    p = jnp.exp2(sc[:, cc * LANES:(cc + 1) * LANES] - lse_b)
                    ps.append(p.astype(BF16))
                    dss.append((p * (dp[:, cc * LANES:(cc + 1) * LANES] - dl_b)).astype(BF16))
                p16, ds16 = jnp.concatenate(ps, axis=-1), jnp.concatenate(dss, axis=-1)
                dvt = dvt + _dot_tn(dob, p16)
                dkt = dkt + _dot_tn(qb, ds16)
                dq_acc[pl.ds(off, tq), :] += _dot(ds16, kb)
            return dkt, dvt

        dkt, dvt = lax.fori_loop(0, nq // unroll, step, (jnp.zeros((HP, tk), F32), jnp.zeros((HP, tk), F32)))
        dk_ref[0] = (dkt.T * (1.0 / LOG2E)).astype(BF16)
        dv_ref[0] = dvt.T.astype(BF16)

        @pl.when(j == nkt - 1)
        def _():
            dq_ref[0] = (dq_acc[...] * ATT_SCALE).astype(BF16)

    whole = pl.BlockSpec((1, s, HP), lambda h, j: (h, 0, 0))
    stat = pl.BlockSpec((1, nq, 1, tq), lambda h, j: (h, 0, 0, 0))
    tile = pl.BlockSpec((1, tk, HP), lambda h, j: (h, j, 0))
    return _call(
        body, name=name, grid=(H, nkt),
        in_specs=[whole, whole, stat, stat, tile, tile],
        out_specs=[whole, tile, tile],
        out_shape=[jax.ShapeDtypeStruct((H, s, HP), BF16)] * 3,
        scratch_shapes=[pltpu.VMEM((s, HP), F32)],
        compiler_params=_cp(("arbitrary", "arbitrary")),
    )(q, do, lse, dl, k, v)


def _mla_prep_bwd(proj_l, c, sn, qng, kvng, qhg, khg, wuq, wukv, wuqt, wukvt, dq, dk, dv, t, name):
    s = proj_l.shape[0]
    t = min(t, s)

    def body(l_ref, c_ref, sn_ref, qng_ref, kvng_ref, qhg_ref, khg_ref, wuq_ref, wukv_ref, wuqt_ref, wukvt_ref,
             dq_ref, dk_ref, dv_ref, dl_ref, dwuq_ref, dwukv_ref, dqng_ref, dkvng_ref, dqhg_ref, dkhg_ref):
        @pl.when(pl.program_id(0) == 0)
        def _():
            for r in (dwuq_ref, dwukv_ref, dqng_ref, dkvng_ref, dqhg_ref, dkhg_ref):
                r[...] = jnp.zeros_like(r)

        cc, ss = c_ref[...], sn_ref[...]
        qhg, khg = qhg_ref[...], khg_ref[...]
        (_, _, kpe, qxh, qrs, kvxh, kvrs, qn16, kvn16, qp, kvp) = _mla_heads(
            l_ref[...], cc, ss, qng_ref[...], kvng_ref[...], qhg, khg, wuq_ref[...], wukv_ref[...])
        lane = lax.broadcasted_iota(jnp.int32, (t, HP), 1)
        dqp, dkp, dvp = [], [], []
        dkpe = jnp.zeros((t, HP), F32)
        dqhg = jnp.zeros((1, HP), F32)
        dkhg = jnp.zeros((1, HP), F32)
        for h in range(H):
            _, xh, rs = _rms(qp[:, h * HP:(h + 1) * HP], qhg, QKD)
            du, dg = _rms_bwd(_rope_adj(dq_ref[h].astype(F32), cc, ss), xh, rs, qhg, QKD)
            dqp.append(du)
            dqhg = dqhg + dg
            _, xh, rs = _rms(kvp[:, h * HP:(h + 1) * HP] + kpe, khg, QKD)
            du, dg = _rms_bwd(_rope_adj(dk_ref[h].astype(F32), cc, ss), xh, rs, khg, QKD)
            dkp.append(du)
            dkhg = dkhg + dg
            dkpe = dkpe + jnp.where((lane >= NOPE) & (lane < QKD), du, 0.0)
            dvp.append(dv_ref[h].astype(F32))
        dqhg_ref[...] += dqhg
        dkhg_ref[...] += dkhg
        dqp16 = jnp.concatenate(dqp, axis=-1).astype(BF16)
        dkvp16 = jnp.concatenate(dkp + dvp, axis=-1).astype(BF16)
        dwuq_ref[...] += _dot_tn(qn16, dqp16)
        dwukv_ref[...] += _dot_tn(kvn16, dkvp16)
        dql, dg = _rms_bwd(_dot(dqp16, wuqt_ref[...]), qxh, qrs, qng_ref[...], QL)
        dqng_ref[...] += dg
        dkvl, dg = _rms_bwd(_dot(dkvp16, wukvt_ref[...]), kvxh, kvrs, kvng_ref[...], KVL)
        dkvng_ref[...] += dg
        dl_ref[:, 0:QL] = dql.astype(BF16)
        dl_ref[:, QL:QL + KVL] = dkvl.astype(BF16)
        dl_ref[:, QL + KVL:P_LW] = dkpe.astype(BF16)

    hs = pl.BlockSpec((H, t, HP), lambda i: (0, i, 0))
    row = lambda w: pl.BlockSpec((t, w), lambda i: (i, 0))
    return _call(
        body, name=name, grid=(s // t,),
        in_specs=[row(P_LW), row(HP), row(HP), _full((1, QL)), _full((1, KVL)), _full((1, HP)), _full((1, HP)),
                  _full((QL, H * HP)), _full((KVL, 2 * H * HP)), _full((H * HP, QL)), _full((2 * H * HP, KVL)),
                  hs, hs, hs],
        out_specs=[row(P_LW), _full((QL, H * HP)), _full((KVL, 2 * H * HP)), _full((1, QL)), _full((1, KVL)),
                   _full((1, HP)), _full((1, HP))],
        out_shape=[jax.ShapeDtypeStruct((s, P_LW), BF16), jax.ShapeDtypeStruct((QL, H * HP), F32),
                   jax.ShapeDtypeStruct((KVL, 2 * H * HP), F32), jax.ShapeDtypeStruct((1, QL), F32),
                   jax.ShapeDtypeStruct((1, KVL), F32), jax.ShapeDtypeStruct((1, HP), F32),
                   jax.ShapeDtypeStruct((1, HP), F32)],
        compiler_params=_cp(("arbitrary",)),
    )(proj_l, c, sn, qng, kvng, qhg, khg, wuq, wukv, wuqt, wukvt, dq, dk, dv)


def _mem_prep_bwd(mem, mng, wmkv, wmkvt, mkg, dmk, dmv, name):
    m = mem.shape[0]

    def body(mem_ref, mng_ref, w_ref, wt_ref, mkg_ref, dmk_ref, dmv_ref, dw_ref, dmng_ref, dmkg_ref):
        mn, xh, _ = _rms(mem_ref[...], mng_ref[...], D)
        mn16 = mn.astype(BF16)
        mkv = _dot(mn16, w_ref[...])
        parts = []
        dmkg = jnp.zeros((1, MHD), F32)
        for h in range(HM):
            _, kxh, krs = _rms(mkv[:, 2 * h * MHD:(2 * h + 1) * MHD], mkg_ref[...], MHD)
            du, dg = _rms_bwd(dmk_ref[h], kxh, krs, mkg_ref[...], MHD)
            dmkg = dmkg + dg
            parts += [du, dmv_ref[h]]
        dmkv = jnp.concatenate(parts, axis=-1).astype(BF16)
        dw_ref[...] = _dot_tn(mn16, dmkv)
        dmn = _dot(dmkv, wt_ref[...])
        dmng_ref[...] = jnp.sum(dmn * xh, axis=0, keepdims=True)
        dmkg_ref[...] = dmkg

    return _call(
        body, name=name,
        in_specs=[_full((m, D)), _full((1, D)), _full((D, 2 * MW)), _full((2 * MW, D)), _full((1, MHD)),
                  _full((HM, m, MHD)), _full((HM, m, MHD))],
        out_specs=[_full((D, 2 * MW)), _full((1, D)), _full((1, MHD))],
        out_shape=[jax.ShapeDtypeStruct((D, 2 * MW), F32), jax.ShapeDtypeStruct((1, D), F32),
                   jax.ShapeDtypeStruct((1, MHD), F32)],
        compiler_params=_cp(),
    )(mem, mng, wmkv, wmkvt, mkg, dmk, dmv)


def _rms_in_bwd(x, g_out, dh, ng, t, name):
    s = x.shape[0]
    t = min(t, s)

    def body(x_ref, go_ref, dh_ref, ng_ref, dx_ref, dng_ref):
        @pl.when(pl.program_id(0) == 0)
        def _():
            dng_ref[...] = jnp.zeros_like(dng_ref)

        _, xh, rs = _rms(x_ref[...], ng_ref[...], D)
        dx, dg = _rms_bwd(dh_ref[...], xh, rs, ng_ref[...], D)
        dx_ref[...] = go_ref[...] + dx
        dng_ref[...] += dg

    row = pl.BlockSpec((t, D), lambda i: (i, 0))
    return _call(
        body, name=name, grid=(s // t,),
        in_specs=[row, row, row, _full((1, D))], out_specs=[row, _full((1, D))],
        out_shape=[jax.ShapeDtypeStruct((s, D), F32), jax.ShapeDtypeStruct((1, D), F32)],
        compiler_params=_cp(("arbitrary",)),
    )(x, g_out, dh, ng)


def _allgather(arrs, name):
    n = len(arrs)

    def body(*refs):
        x_refs, out_refs = refs[:n], refs[n:2 * n]
        send_sems, recv_sems, local_sems = refs[2 * n:]
        x, y, c = lax.axis_index("x"), lax.axis_index("y"), lax.axis_index("c")
        me, sibling = (x, y, c), (x, y, 1 - c)
        chips = [(1 - x, y), (x, 1 - y), (1 - x, 1 - y)]

        def slot(a, px, py, pc):
            return out_refs[a].at[4 * px + 2 * py + pc]

        def copy(a, k, block, to, src=None):
            return pltpu.make_async_remote_copy(
                src_ref=slot(a, *block) if src is None else src, dst_ref=slot(a, *block),
                send_sem=send_sems.at[a, k], recv_sem=recv_sems.at[a, k],
                device_id=to, device_id_type=pl.DeviceIdType.MESH)

        mine = [pltpu.make_async_copy(x_refs[a], slot(a, *me), local_sems.at[a]) for a in range(n)]
        first, passed = [], []
        for a in range(n):
            mine[a].start()
            first.append(copy(a, 0, me, sibling, src=x_refs[a]))
            first += [copy(a, 1 + j, me, (*chip, c), src=x_refs[a]) for j, chip in enumerate(chips)]
        for cp in first:
            cp.start()
        for j, chip in enumerate(chips):
            for a in range(n):
                copy(a, 1 + j, (*chip, c), me).wait_recv()
                passed.append(copy(a, 4 + j, (*chip, c), sibling))
                passed[-1].start()
        for a in range(n):
            copy(a, 0, sibling, me).wait_recv()
        for j, chip in enumerate(chips):
            for a in range(n):
                copy(a, 4 + j, (*chip, 1 - c), me).wait_recv()
        for cp in first + passed:
            cp.wait_send()
        for cp in mine:
            cp.wait()

    any_spec = pl.BlockSpec(memory_space=pl.ANY)
    return _call(
        body, name=name,
        in_specs=[any_spec] * n, out_specs=[any_spec] * n,
        out_shape=[jax.ShapeDtypeStruct((N_DEV,) + a.shape, a.dtype) for a in arrs],
        scratch_shapes=[pltpu.SemaphoreType.DMA((n, 7)), pltpu.SemaphoreType.DMA((n, 7)),
                        pltpu.SemaphoreType.DMA((n,))],
    )(*arrs)


def _pair_exchange(arrs, name):
    n = len(arrs)

    def body(*refs):
        x_refs, out_refs = refs[:n], refs[n:2 * n]
        send_sems, recv_sems = refs[2 * n:]
        x, y, c = lax.axis_index("x"), lax.axis_index("y"), lax.axis_index("c")
        copies = [pltpu.make_async_remote_copy(
            src_ref=x_refs[a].at[1 - c], dst_ref=out_refs[a],
            send_sem=send_sems.at[a], recv_sem=recv_sems.at[a],
            device_id=(x, y, 1 - c), device_id_type=pl.DeviceIdType.MESH) for a in range(n)]
        for cp in copies:
            cp.start()
        for cp in copies:
            cp.wait_recv()
        for cp in copies:
            cp.wait_send()

    any_spec = pl.BlockSpec(memory_space=pl.ANY)
    return _call(
        body, name=name,
        in_specs=[any_spec] * n, out_specs=[any_spec] * n,
        out_shape=[jax.ShapeDtypeStruct(a.shape[1:], a.dtype) for a in arrs],
        scratch_shapes=[pltpu.SemaphoreType.DMA((n,)), pltpu.SemaphoreType.DMA((n,))],
    )(*arrs)


def _chip_exchange(arrs, name):
    n = len(arrs)

    def body(*refs):
        x_refs, out_refs = refs[:n], refs[n:2 * n]
        send_sems, recv_sems, local_sems = refs[2 * n:]
        x, y, c = lax.axis_index("x"), lax.axis_index("y"), lax.axis_index("c")
        me = 2 * x + y
        mine = [pltpu.make_async_copy(x_refs[a].at[me], out_refs[a].at[me], local_sems.at[a]) for a in range(n)]
        for cp in mine:
            cp.start()
        copies = []
        for k, (px, py) in enumerate([(1 - x, y), (x, 1 - y), (1 - x, 1 - y)]):
            for a in range(n):
                copies.append(pltpu.make_async_remote_copy(
                    src_ref=x_refs[a].at[2 * px + py], dst_ref=out_refs[a].at[me],
                    send_sem=send_sems.at[a, k], recv_sem=recv_sems.at[a, k],
                    device_id=(px, py, c), device_id_type=pl.DeviceIdType.MESH))
        for cp in copies:
            cp.start()
        for cp in copies:
            cp.wait_recv()
        for cp in copies:
            cp.wait_send()
        for cp in mine:
            cp.wait()

    any_spec = pl.BlockSpec(memory_space=pl.ANY)
    return _call(
        body, name=name,
        in_specs=[any_spec] * n, out_specs=[any_spec] * n,
        out_shape=[jax.ShapeDtypeStruct(a.shape, a.dtype) for a in arrs],
        scratch_shapes=[pltpu.SemaphoreType.DMA((n, 3)), pltpu.SemaphoreType.DMA((n, 3)),
                        pltpu.SemaphoreType.DMA((n,))],
    )(*arrs)


def _pair_add(a, b, name):
    r, c_ = a.shape
    cpad = -(-c_ // LANES) * LANES
    tr = r
    while tr * cpad * 4 > ADAMW_BLOCK_BYTES and tr % 32 == 0:
        tr //= 2

    def body(a_ref, b_ref, o_ref):
        o_ref[...] = (a_ref[...].astype(F32) + b_ref[...].astype(F32)).astype(o_ref.dtype)

    row = pl.BlockSpec((tr, c_), lambda i: (i, 0))
    return _call(
        body, name=name, grid=(r // tr,), in_specs=[row, row], out_specs=row,
        out_shape=jax.ShapeDtypeStruct((r, c_), a.dtype), compiler_params=_cp(("parallel",)),
    )(a, b)


ADAMW_BLOCK_BYTES = 4 * 1024 * 1024


def _adamw(parts, w, m, v, name):
    r, c_ = w.shape
    n_parts = parts.shape[0]
    cpad = -(-c_ // LANES) * LANES
    tr = r
    while N_DEV * tr * cpad * 4 > ADAMW_BLOCK_BYTES and tr % 16 == 0:
        tr //= 2
    c1 = 1.0 / (1.0 - ADAM_B1 ** ADAM_STEP)
    c2 = 1.0 / (1.0 - ADAM_B2 ** ADAM_STEP)

    def body(p_ref, w_ref, m_ref, v_ref, g_ref, d_ref, nm_ref, nv_ref):
        g = p_ref[0].astype(F32)
        for j in range(1, n_parts):
            g = g + p_ref[j].astype(F32)
        nm = ADAM_B1 * m_ref[...] + (1.0 - ADAM_B1) * g
        nv = ADAM_B2 * v_ref[...] + (1.0 - ADAM_B2) * (g * g)
        g_ref[...] = g
        nm_ref[...] = nm
        nv_ref[...] = nv
        d_ref[...] = -ADAM_LR * ((nm * c1) / (jnp.sqrt(nv * c2) + ADAM_EPS) + ADAM_WD * w_ref[...])

    row = pl.BlockSpec((tr, c_), lambda i: (i, 0))
    return _call(
        body, name=name, grid=(r // tr,),
        in_specs=[pl.BlockSpec((n_parts, tr, c_), lambda i: (0, i, 0)), row, row, row],
        out_specs=[row] * 4, out_shape=[jax.ShapeDtypeStruct((r, c_), F32)] * 4,
        compiler_params=_cp(("parallel",)),
    )(parts, w, m, v)


def _to_rows(flat, align=8):
    n = flat.shape[-1]
    rows = -(-n // (LANES * align)) * align
    return jnp.pad(flat, (0, rows * LANES - n)).reshape(rows, LANES)


def _shard_blocks(full, axis):
    r, c_ = full.shape
    if axis == 0:
        return full.reshape(4, 2, r // N_DEV, c_).transpose(1, 0, 2, 3)
    return full.reshape(r, 4, 2, c_ // N_DEV).transpose(2, 1, 0, 3)


def _unshard_blocks(blocks, axis):
    _, r, c_ = blocks.shape
    if axis == 0:
        return blocks.reshape(N_DEV * r, c_)
    return blocks.transpose(1, 0, 2).reshape(r, N_DEV * c_)


def _pad_heads(w, width):
    k = w.shape[0]
    return jnp.pad(w.reshape(k, H, width), ((0, 0), (0, 0), (0, HP - width))).reshape(k, H * HP)


def _unpad_heads(w, width):
    k = w.shape[0]
    return w.reshape(k, H, HP)[:, :, :width].reshape(k, H * width)


IN_SPLIT = {'q_lat': (0, 384), 'kv_lat': (384, 640), 'k_pe': (640, 672), 'c_b': (672, 1184), 'c_c': (1184, 1696),
            'c_u': (1696, 2208), 'q_mem': (2208, 2720), 'g_attn': (2720, 3232), 'g_conv': (3232, 3744),
            'g_mem': (3744, 4256), 'r': (4256, 7328)}
P_ORDER = ['c_b', 'c_c', 'c_u', 'g_conv', 'q_mem', 'g_mem', 'r', 'g_attn', 'q_lat', 'kv_lat']


def _permute_w_in(w):
    k = w.shape[0]
    cols = [w[:, IN_SPLIT[n][0]:IN_SPLIT[n][1]] for n in P_ORDER]
    kpe = w[:, IN_SPLIT['k_pe'][0]:IN_SPLIT['k_pe'][1]]
    cols += [jnp.zeros((k, NOPE), w.dtype), kpe, jnp.zeros((k, HP - QKD), w.dtype)]
    return jnp.concatenate(cols, axis=1)


def _unpermute_w_in(pieces):
    bounds, off = [], 0
    for p in pieces:
        bounds.append((off, off + p.shape[1]))
        off += p.shape[1]

    def cols(lo, hi):
        for p, (b0, b1) in zip(pieces, bounds):
            if b0 <= lo and hi <= b1:
                return p[:, lo - b0:hi - b0]
        raise ValueError("a column range straddles two pieces")

    off, pos = 0, {}
    for n in P_ORDER:
        wd = IN_SPLIT[n][1] - IN_SPLIT[n][0]
        pos[n] = (off, off + wd)
        off += wd
    pos['k_pe'] = (off + NOPE, off + QKD)
    order = sorted(IN_SPLIT, key=lambda n: IN_SPLIT[n][0])
    return jnp.concatenate([cols(*pos[n]) for n in order], axis=1)


def _layer_weights(full, l):
    w = {}
    w_in_p = _permute_w_in(full['w_in'][l])
    w['w_main'] = w_in_p[:, :P_MAIN]
    w['w_l'] = w_in_p[:, P_MAIN:]
    w['w_in_t'] = w_in_p.T
    w['w_uq'] = _pad_heads(full['w_uq'][l], QKD)
    wukv = full['w_ukv'][l].reshape(KVL, H, 2, NOPE)
    kpart = jnp.pad(wukv[:, :, 0, :], ((0, 0), (0, 0), (0, HP - NOPE))).reshape(KVL, H * HP)
    vpart = jnp.pad(wukv[:, :, 1, :], ((0, 0), (0, 0), (0, HP - VD))).reshape(KVL, H * HP)
    w['w_ukv'] = jnp.concatenate([kpart, vpart], axis=1)
    w['w_uq_t'] = w['w_uq'].T
    w['w_ukv_t'] = w['w_ukv'].T
    w['w_mkv'] = full['w_mkv'][l]
    w['w_mkv_t'] = w['w_mkv'].T
    for n in ('w_br_attn', 'w_br_conv', 'w_br_mem', 'w_out'):
        w[n] = full[n][l]
        w[n + '_t'] = w[n].T
    return w


def kernel(x, mem, positions, norm_g, w_in, b_gate, q_norm_g, w_uq, kv_norm_g, w_ukv, q_head_g, k_head_g, conv_w, conv_b, mem_norm_g, w_mkv, mem_q_g, mem_k_g, w_br_attn, w_br_conv, w_br_mem, w_out, loss_target, m_norm_g, m_w_in, m_b_gate, m_q_norm_g, m_w_uq, m_kv_norm_g, m_w_ukv, m_q_head_g, m_k_head_g, m_conv_w, m_conv_b, m_mem_norm_g, m_w_mkv, m_mem_q_g, m_mem_k_g, m_w_br_attn, m_w_br_conv, m_w_br_mem, m_w_out, v_norm_g, v_w_in, v_b_gate, v_q_norm_g, v_w_uq, v_kv_norm_g, v_w_ukv, v_q_head_g, v_k_head_g, v_conv_w, v_conv_b, v_mem_norm_g, v_w_mkv, v_mem_q_g, v_mem_k_g, v_w_br_attn, v_w_br_conv, v_w_br_mem, v_w_out):
    a = dict(zip(INPUTS, (x, mem, positions, norm_g, w_in, b_gate, q_norm_g, w_uq, kv_norm_g, w_ukv, q_head_g, k_head_g, conv_w, conv_b, mem_norm_g, w_mkv, mem_q_g, mem_k_g, w_br_attn, w_br_conv, w_br_mem, w_out, loss_target, m_norm_g, m_w_in, m_b_gate, m_q_norm_g, m_w_uq, m_kv_norm_g, m_w_ukv, m_q_head_g, m_k_head_g, m_conv_w, m_conv_b, m_mem_norm_g, m_w_mkv, m_mem_q_g, m_mem_k_g, m_w_br_attn, m_w_br_conv, m_w_br_mem, m_w_out, v_norm_g, v_w_in, v_b_gate, v_q_norm_g, v_w_uq, v_kv_norm_g, v_w_ukv, v_q_head_g, v_k_head_g, v_conv_w, v_conv_b, v_mem_norm_g, v_w_mkv, v_mem_q_g, v_mem_k_g, v_w_br_attn, v_w_br_conv, v_w_br_mem, v_w_out)))
    x = a['x'][0]
    mem = a['mem'][0]
    tgt = a['loss_target'][0]
    s = x.shape[0]
    t_el = 512
    t_br = 512
    tq_f, tk_f, tq_b, tk_b = 512, 1024, 512, 512

    gathered = _allgather([a[n].astype(BF16) for n in BIG_ORDER] + [a['conv_w']], "ag_weights")
    full = {n: [_unshard_blocks(g8[:, l], BIG[n][1]) for l in range(DEPTH)] for n, g8 in zip(BIG_ORDER, gathered)}
    conv_w = gathered[-1].transpose(1, 2, 0, 3).reshape(DEPTH, 3, CW)

    inv_freq = ROPE_BASE ** (-jnp.arange(0, RP, 2, dtype=F32) / RP)
    ang = a['positions'][0].astype(F32)[:, None] * inv_freq
    cos, sin = jnp.cos(ang), jnp.sin(ang)
    rc = jnp.concatenate([jnp.ones((s, NOPE), F32), cos, cos, jnp.ones((s, HP - QKD), F32)], axis=1)
    rs = jnp.concatenate([jnp.zeros((s, NOPE), F32), -sin, sin, jnp.zeros((s, HP - QKD), F32)], axis=1)

    def small(n, l, width=None):
        v = a[n][l][None, :]
        return v if width is None else jnp.pad(v, ((0, 0), (0, width - v.shape[1])))

    saved = []
    layer_w = [_layer_weights(full, l) for l in range(DEPTH)]
    for l in range(DEPTH):
        w = layer_w[l]
        tag = ""
        h, ht = _rms_h(x, small('norm_g', l), 512, "rms_h" + tag)
        proj = _mm(h, w['w_main'], BF16, 2048, 512, "in_proj" + tag)
        proj_l = _mm(h, w['w_l'], BF16, 512, P_LW, "in_proj_lat" + tag)
        qng, kvng = small('q_norm_g', l), small('kv_norm_g', l)
        qhg, khg = small('q_head_g', l, HP), small('k_head_g', l, HP)
        q, k, v = _mla_prep(proj_l, rc, rs, qng, kvng, qhg, khg, w['w_uq'], w['w_ukv'], t_br, "mla_prep" + tag)
        o, lse = _flash_fwd(q, k, v, tq_f, tk_f, "flash_fwd" + tag)
        mk, mv = _mem_prep(mem, small('mem_norm_g', l), w['w_mkv'], small('mem_k_g', l), "mem_prep" + tag)
        cb, mqg = small('conv_b', l), small('mem_q_g', l)
        oa, oc, om = _branches(proj, o, mk, mv, conv_w[l], cb, mqg, t_br, "branches" + tag)
        x_out, aa, ac, am = _merge_fwd(x, proj, oa, oc, om, small('b_gate', l), w['w_br_attn'], w['w_br_conv'],
                                       w['w_br_mem'], w['w_out'], t_el, "merge" + tag)
        saved.append(dict(x=x, ht=ht, proj=proj, proj_l=proj_l, q=q, k=k, v=v, o=o, lse=lse, mk=mk, mv=mv,
                          oa=oa, oc=oc, om=om, aa=aa, ac=ac, am=am))
        x = x_out

    g, loss_parts = _loss_grad(x, tgt, 512, "loss")
    loss = lax.psum(jnp.sum(loss_parts), ("x", "y", "c"))

    gw = {n: [None] * DEPTH for n in WEIGHTS}
    for l in reversed(range(DEPTH)):
        w = layer_w[l]
        sv = saved[l]
        tag = ""
        tqb = min(tq_b, s)
        y, daa, dac, dam, dr, doa, doc, dom, dbg = _merge_bwd(
            g, sv['proj'], sv['aa'], sv['ac'], sv['am'], small('b_gate', l), w['w_out_t'], w['w_br_attn_t'],
            w['w_br_conv_t'], w['w_br_mem_t'], t_el, "merge_bwd" + tag)
        gw['b_gate'][l] = dbg[0]
        gw['w_out'][l] = _tn(y, g, 512, D, "dw_out" + tag)
        gw['w_br_attn'][l] = _tn(sv['oa'], daa, 512, D, "dw_attn" + tag)
        gw['w_br_conv'][l] = _tn(sv['oc'], dac, 512, D, "dw_conv" + tag)
        gw['w_br_mem'][l] = _tn(sv['om'], dam, 512, D, "dw_mem" + tag)
        cb, mqg = small('conv_b', l), small('mem_q_g', l)
        dcv, dmm, dga, do, dl, dcw, dcb, dmk, dmv, dmqg = _branches_bwd(
            sv['proj'], sv['o'], sv['mk'], sv['mv'], conv_w[l], cb, mqg, doa, doc, dom, t_br, "branches_bwd" + tag)
        gw['conv_w'][l], gw['conv_b'][l], gw['mem_q_g'][l] = dcw, dcb[0], dmqg[0]
        dwm, dmng, dmkg = _mem_prep_bwd(mem, small('mem_norm_g', l), w['w_mkv'], w['w_mkv_t'], small('mem_k_g', l),
                                        dmk, dmv, "mem_prep_bwd" + tag)
        gw['w_mkv'][l], gw['mem_norm_g'][l], gw['mem_k_g'][l] = dwm, dmng[0], dmkg[0]
        lse_r = sv['lse'].reshape(H, s // tqb, 1, tqb)
        dl_r = dl.reshape(H, s // tqb, 1, tqb)
        dq, dk, dv = _flash_bwd(sv['q'], sv['k'], sv['v'], do, lse_r, dl_r, tq_b, tk_b, "flash_bwd" + tag)
        qng, kvng = small('q_norm_g', l), small('kv_norm_g', l)
        qhg, khg = small('q_head_g', l, HP), small('k_head_g', l, HP)
        dlat, dwuq, dwukv, dqng, dkvng, dqhg, dkhg = _mla_prep_bwd(
            sv['proj_l'], rc, rs, qng, kvng, qhg, khg, w['w_uq'], w['w_ukv'], w['w_uq_t'], w['w_ukv_t'],
            dq, dk, dv, t_br, "mla_prep_bwd" + tag)
        gw['w_uq'][l] = _unpad_heads(dwuq, QKD)
        dwukv = dwukv.reshape(KVL, 2, H, HP)[:, :, :, :NOPE]
        gw['w_ukv'][l] = dwukv.transpose(0, 2, 1, 3).reshape(KVL, H * 2 * NOPE)
        gw['q_norm_g'][l], gw['kv_norm_g'][l] = dqng[0], dkvng[0]
        gw['q_head_g'][l], gw['k_head_g'][l] = dqhg[0, :QKD], dkhg[0, :QKD]
        dpieces = [dcv, dmm, dr, dga, dlat]
        dh = _mm_pieces(dpieces, w['w_in_t'], 512, 512, "d_h" + tag)
        gw['w_in'][l] = _unpermute_w_in([
            _mm_acc(sv['ht'], p, 1024, tn, f"dw_in_{i}" + tag)
            for i, (p, tn) in enumerate(zip(dpieces, (1024, 1024, 1536, 512, P_LW)))])
        g, dng = _rms_in_bwd(sv['x'], g, dh, small('norm_g', l), 512, "rms_bwd" + tag)
        gw['norm_g'][l] = dng[0]
    grad_x = g[None]

    sharded = BIG_ORDER + ['conv_w']
    axis_of = lambda n: 1 if n == 'conv_w' else BIG[n][1]
    send = [jnp.stack([_shard_blocks(gw[n][l], axis_of(n)) for l in range(DEPTH)], axis=2)
            .astype(F32 if n == 'conv_w' else BF16) for n in sharded]
    from_sibling = _pair_exchange(send, "rs_pair")
    my_c = lax.axis_index("c")
    chip_sums = []
    for n, t, got in zip(sharded, send, from_sibling):
        own = lax.dynamic_index_in_dim(t, my_c, axis=0, keepdims=False)
        flat = lambda u: u.reshape(-1, u.shape[-1])
        chip_sums.append(_pair_add(flat(own), flat(got), "rs_add_" + n).reshape(got.shape))
    parts_big = _chip_exchange(chip_sums, "rs_chips")
    small_flat = jnp.concatenate([jnp.stack(gw[n]).reshape(-1) for n in SMALL_ORDER])
    n_small = small_flat.shape[0]
    parts_small = _allgather([_to_rows(small_flat)], "ag_small_grads")[0]

    outs = [{} for _ in range(4)]
    for n, parts in zip(sharded, parts_big):
        loc = a[n].shape
        two_d = lambda t: t.reshape(loc[0] * loc[1], loc[2])
        res = _adamw(parts.reshape(4, loc[0] * loc[1], loc[2]), two_d(a[n]), two_d(a['m_' + n]),
                     two_d(a['v_' + n]), "adamw_" + n)
        for d, r in zip(outs, res):
            d[n] = r.reshape(loc)
    pks = lambda pre: _to_rows(jnp.concatenate([a[pre + n].reshape(-1) for n in SMALL_ORDER]))
    res_small = _adamw(parts_small, pks(''), pks('m_'), pks('v_'), "adamw_small")
    for d, rsm in zip(outs, res_small):
        flat = rsm.reshape(-1)[:n_small]
        off = 0
        for n in SMALL_ORDER:
            d[n] = flat[off:off + DEPTH * SMALL[n]].reshape(DEPTH, SMALL[n])
            off += DEPTH * SMALL[n]
    result = [loss, grad_x]
    for d in outs:
        result += [d[n] for n in WEIGHTS]
    return tuple(result)
```

```python
import functools

import jax
import jax.numpy as jnp
from jax import lax
from jax.experimental import pallas as pl
from jax.experimental.pallas import tpu as pltpu

F32, BF16 = jnp.float32, jnp.bfloat16

N_DEV = 8
DEPTH = 4
D = 1024
QL, KVL, RP = 384, 256, 32
H, NOPE, QKD, VD = 8, 64, 96, 64
HP = 128
CW, MW, AW = 512, 512, 512
HM, MHD = 4, 128
IN_WIDTH = 7328
EPS = 1e-6
ROPE_BASE = 10000.0
ATT_SCALE = QKD ** -0.5
MEM_SCALE = MHD ** -0.5
LOG2E = 1.4426950408889634
QSCALE = ATT_SCALE * LOG2E

ADAM_LR, ADAM_B1, ADAM_B2, ADAM_EPS, ADAM_WD, ADAM_STEP = 0.001, 0.9, 0.999, 1e-08, 0.01, 10

LANES = 128
VMEM_LIMIT = 56 * 1024 * 1024

P_CONV, P_MEM, P_R, P_GA, P_L = 0, 2048, 3072, 6144, 6656
P_MAIN = 6656
P_LW = 768
P_W = P_MAIN + P_LW
P_WPAD = 7680

WEIGHTS = ['norm_g', 'w_in', 'b_gate', 'q_norm_g', 'w_uq', 'kv_norm_g', 'w_ukv', 'q_head_g', 'k_head_g',
           'conv_w', 'conv_b', 'mem_norm_g', 'w_mkv', 'mem_q_g', 'mem_k_g', 'w_br_attn', 'w_br_conv',
           'w_br_mem', 'w_out']
INPUTS = ['x', 'mem', 'positions'] + WEIGHTS + ['loss_target'] + ['m_' + n for n in WEIGHTS] + ['v_' + n for n in WEIGHTS]

BIG = {'w_in': ((D, IN_WIDTH), 1), 'w_uq': ((QL, H * QKD), 1), 'w_ukv': ((KVL, H * 128), 1),
       'w_mkv': ((D, 2 * MW), 0), 'w_br_attn': ((AW, D), 1), 'w_br_conv': ((CW, D), 1),
       'w_br_mem': ((MW, D), 1), 'w_out': ((D, D), 0)}
BIG_ORDER = ['w_in', 'w_uq', 'w_ukv', 'w_mkv', 'w_br_attn', 'w_br_conv', 'w_br_mem', 'w_out']
CONVW_PAD = 256
SMALL = {'norm_g': D, 'b_gate': 3 * D, 'q_norm_g': QL, 'kv_norm_g': KVL, 'q_head_g': QKD, 'k_head_g': QKD,
         'conv_b': CW, 'mem_norm_g': D, 'mem_q_g': MHD, 'mem_k_g': MHD}
SMALL_ORDER = list(SMALL)
ROW_ALIGN = 1024


def _call(body, **kw):
    return pl.pallas_call(body, **kw)


def _cp(sem=None):
    return pltpu.CompilerParams(dimension_semantics=sem, vmem_limit_bytes=VMEM_LIMIT)


def _full(shape):
    n = len(shape)
    return pl.BlockSpec(shape, lambda *_: (0,) * n)


def _rms(x, g, n):
    rs = lax.rsqrt(jnp.sum(x * x, axis=-1, keepdims=True) * (1.0 / n) + EPS)
    xh = x * rs
    return xh * g, xh, rs


def _rms_bwd(dy, xh, rs, g, n):
    dxh = dy * g
    dx = rs * (dxh - xh * (jnp.sum(dxh * xh, axis=-1, keepdims=True) * (1.0 / n)))
    return dx, jnp.sum(dy * xh, axis=0, keepdims=True)


def _sigmoid(x):
    return 1.0 / (1.0 + jnp.exp(-x))


def _swap_rope(u):
    return pltpu.roll(u, 64, 1)


def _rope(u, c, sn):
    return u * c + _swap_rope(u) * sn


def _rope_adj(d, c, sn):
    return d * c + _swap_rope(d * sn)


def _dot(a, b):
    return jnp.dot(a, b, preferred_element_type=F32)


def _dot_nt(a, b):
    return lax.dot_general(a, b, (((1,), (1,)), ((), ())), preferred_element_type=F32)


def _dot_tn(a, b):
    return lax.dot_general(a, b, (((0,), (0,)), ((), ())), preferred_element_type=F32)


def _mm(a, b, out_dtype, tm, tn, name):
    m, k = a.shape
    _, n = b.shape
    tm, tn = min(tm, m), min(tn, n)

    def body(a_ref, b_ref, o_ref):
        o_ref[...] = _dot(a_ref[...].astype(BF16), b_ref[...]).astype(o_ref.dtype)

    return _call(
        body, name=name, grid=(m // tm, n // tn),
        in_specs=[pl.BlockSpec((tm, k), lambda i, j: (i, 0)), pl.BlockSpec((k, tn), lambda i, j: (0, j))],
        out_specs=pl.BlockSpec((tm, tn), lambda i, j: (i, j)),
        out_shape=jax.ShapeDtypeStruct((m, n), out_dtype),
        compiler_params=_cp(("parallel", "arbitrary")),
    )(a, b)


def _tn(a, b, ts, tn, name):
    s, ka = a.shape
    _, n = b.shape
    ts, tn = min(ts, s), min(tn, n)

    def body(a_ref, b_ref, o_ref):
        @pl.when(pl.program_id(1) == 0)
        def _():
            o_ref[...] = jnp.zeros_like(o_ref)

        o_ref[...] += _dot_tn(a_ref[...].astype(BF16), b_ref[...].astype(BF16))

    return _call(
        body, name=name, grid=(n // tn, s // ts),
        in_specs=[pl.BlockSpec((ts, ka), lambda j, i: (i, 0)), pl.BlockSpec((ts, tn), lambda j, i: (i, j))],
        out_specs=pl.BlockSpec((ka, tn), lambda j, i: (0, j)),
        out_shape=jax.ShapeDtypeStruct((ka, n), F32),
        compiler_params=_cp(("parallel", "arbitrary")),
    )(a, b)


def _mm_pieces(pieces, b, tm, tn, name):
    n_p = len(pieces)
    m = pieces[0].shape[0]
    k, n = b.shape
    tm, tn = min(tm, m), min(tn, n)
    offs = [sum(p.shape[1] for p in pieces[:i]) for i in range(n_p)]

    def body(*refs):
        a_refs, b_ref, o_ref = refs[:n_p], refs[n_p], refs[n_p + 1]
        acc = None
        for a_ref, off in zip(a_refs, offs):
            d = _dot(a_ref[...], b_ref[off:off + a_ref.shape[1], :])
            acc = d if acc is None else acc + d
        o_ref[...] = acc

    return _call(
        body, name=name, grid=(m // tm, n // tn),
        in_specs=[pl.BlockSpec((tm, p.shape[1]), lambda i, j: (i, 0)) for p in pieces]
        + [pl.BlockSpec((k, tn), lambda i, j: (0, j))],
        out_specs=pl.BlockSpec((tm, tn), lambda i, j: (i, j)),
        out_shape=jax.ShapeDtypeStruct((m, n), F32),
        compiler_params=_cp(("parallel", "arbitrary")),
    )(*pieces, b)


def _mm_acc(a, b, tk, tn, name):
    m, k = a.shape
    _, n = b.shape
    tk, tn = min(tk, k), min(tn, n)

    def body(a_ref, b_ref, o_ref):
        @pl.when(pl.program_id(1) == 0)
        def _():
            o_ref[...] = jnp.zeros_like(o_ref)

        o_ref[...] += _dot(a_ref[...], b_ref[...])

    return _call(
        body, name=name, grid=(n // tn, k // tk),
        in_specs=[pl.BlockSpec((m, tk), lambda j, i: (0, i)), pl.BlockSpec((tk, tn), lambda j, i: (i, j))],
        out_specs=pl.BlockSpec((m, tn), lambda j, i: (0, j)),
        out_shape=jax.ShapeDtypeStruct((m, n), F32),
        compiler_params=_cp(("parallel", "arbitrary")),
    )(a, b)


def _rms_h(x, g, t, name):
    s = x.shape[0]
    t = min(t, s)

    def body(x_ref, g_ref, h_ref, ht_ref):
        h = _rms(x_ref[...], g_ref[...], D)[0]
        h_ref[...] = h.astype(BF16)
        ht_ref[...] = h.T.astype(BF16)

    return _call(
        body, name=name, grid=(s // t,),
        in_specs=[pl.BlockSpec((t, D), lambda i: (i, 0)), _full((1, D))],
        out_specs=[pl.BlockSpec((t, D), lambda i: (i, 0)), pl.BlockSpec((D, t), lambda i: (0, i))],
        out_shape=[jax.ShapeDtypeStruct((s, D), BF16), jax.ShapeDtypeStruct((D, s), BF16)],
        compiler_params=_cp(("parallel",)),
    )(x, g)


def _mla_heads(pl_blk, c, sn, qng, kvng, qhg, khg, wuq, wukv):
    ql = pl_blk[:, 0:QL].astype(F32)
    kvl = pl_blk[:, QL:QL + KVL].astype(F32)
    kpe = pl_blk[:, QL + KVL:P_LW].astype(F32)
    qn, qxh, qrs = _rms(ql, qng, QL)
    kvn, kvxh, kvrs = _rms(kvl, kvng, KVL)
    qn16, kvn16 = qn.astype(BF16), kvn.astype(BF16)
    qp = _dot(qn16, wuq)
    kvp = _dot(kvn16, wukv)
    return ql, kvl, kpe, qxh, qrs, kvxh, kvrs, qn16, kvn16, qp, kvp


def _mla_prep(proj_l, c, sn, qng, kvng, qhg, khg, wuq, wukv, t, name):
    s = proj_l.shape[0]
    t = min(t, s)

    def body(l_ref, c_ref, sn_ref, qng_ref, kvng_ref, qhg_ref, khg_ref, wuq_ref, wukv_ref, q_ref, k_ref, v_ref):
        cc, ss = c_ref[...], sn_ref[...]
        (_, _, kpe, _, _, _, _, _, _, qp, kvp) = _mla_heads(
            l_ref[...], cc, ss, qng_ref[...], kvng_ref[...], qhg_ref[...], khg_ref[...], wuq_ref[...], wukv_ref[...])
        lane = lax.broadcasted_iota(jnp.int32, cc.shape, 1)
        for h in range(H):
            u = qp[:, h * HP:(h + 1) * HP]
            q_ref[h] = (_rope(_rms(u, qhg_ref[...], QKD)[0], cc, ss) * QSCALE).astype(BF16)
            u = kvp[:, h * HP:(h + 1) * HP] + kpe
            k_ref[h] = _rope(_rms(u, khg_ref[...], QKD)[0], cc, ss).astype(BF16)
            v_ref[h] = jnp.where(lane == VD, 1.0, kvp[:, (H + h) * HP:(H + h + 1) * HP]).astype(BF16)

    hs = pl.BlockSpec((H, t, HP), lambda i: (0, i, 0))
    row = lambda w: pl.BlockSpec((t, w), lambda i: (i, 0))
    return _call(
        body, name=name, grid=(s // t,),
        in_specs=[row(P_LW), row(HP), row(HP), _full((1, QL)), _full((1, KVL)), _full((1, HP)), _full((1, HP)),
                  _full((QL, H * HP)), _full((KVL, 2 * H * HP))],
        out_specs=[hs, hs, hs],
        out_shape=[jax.ShapeDtypeStruct((H, s, HP), BF16)] * 3,
        compiler_params=_cp(("parallel",)),
    )(proj_l, c, sn, qng, kvng, qhg, khg, wuq, wukv)


def _flash_fwd(q, k, v, tq, tk, name):
    _, s, _ = q.shape
    tq, tk = min(tq, s), min(tk, s)
    nk, nc = s // tk, tk // LANES
    un = 8 if nk % 8 == 0 else 1

    def body(q_ref, k_ref, v_ref, o_ref, lse_ref, s_scr):
        qb = q_ref[0]

        def scores(jj, mx):
            for u in range(un):
                j = jj * un + u
                off = pl.multiple_of(j * tk, tk)
                sc = _dot_nt(qb, k_ref[0, pl.ds(off, tk), :])
                s_scr[j] = sc
                for cc in range(nc):
                    mx = jnp.maximum(mx, sc[:, cc * LANES:(cc + 1) * LANES])
            return mx

        mx = lax.fori_loop(0, nk // un, scores, jnp.full((tq, LANES), -jnp.inf, F32))
        m = jnp.max(mx, axis=-1, keepdims=True)
        mb = jnp.broadcast_to(m, (tq, LANES))

        def probs(jj, acc):
            for u in range(un):
                j = jj * un + u
                off = pl.multiple_of(j * tk, tk)
                sc = s_scr[j]
                ps = [jnp.exp2(sc[:, cc * LANES:(cc + 1) * LANES] - mb).astype(BF16) for cc in range(nc)]
                acc = acc + _dot(jnp.concatenate(ps, axis=-1), v_ref[0, pl.ds(off, tk), :])
            return acc

        acc = lax.fori_loop(0, nk // un, probs, jnp.zeros((tq, HP), F32))
        l = acc[:, VD:VD + 1]
        o_ref[0] = (acc[:, :VD] / l).astype(BF16)
        lse = jnp.broadcast_to(m + jnp.log(l) * LOG2E, (tq, LANES))
        lse_ref[0] = lse.T[0:1, :]

    return _call(
        body, name=name, grid=(H, s // tq),
        in_specs=[pl.BlockSpec((1, tq, HP), lambda h, i: (h, i, 0)),
                  pl.BlockSpec((1, s, HP), lambda h, i: (h, 0, 0), pipeline_mode=pl.Buffered(1)),
                  pl.BlockSpec((1, s, HP), lambda h, i: (h, 0, 0), pipeline_mode=pl.Buffered(1))],
        out_specs=[pl.BlockSpec((1, tq, VD), lambda h, i: (h, i, 0)), pl.BlockSpec((1, 1, tq), lambda h, i: (h, 0, i))],
        out_shape=[jax.ShapeDtypeStruct((H, s, VD), BF16), jax.ShapeDtypeStruct((H, 1, s), F32)],
        scratch_shapes=[pltpu.VMEM((nk, tq, tk), F32)],
        compiler_params=_cp(("parallel", "arbitrary")),
    )(q, k, v)


def _mem_prep(mem, mng, wmkv, mkg, name):
    m = mem.shape[0]

    def body(mem_ref, mng_ref, w_ref, mkg_ref, mk_ref, mv_ref):
        mn = _rms(mem_ref[...], mng_ref[...], D)[0].astype(BF16)
        mkv = _dot(mn, w_ref[...])
        for h in range(HM):
            mk_ref[h] = _rms(mkv[:, 2 * h * MHD:(2 * h + 1) * MHD], mkg_ref[...], MHD)[0].astype(BF16)
            mv_ref[h] = mkv[:, (2 * h + 1) * MHD:(2 * h + 2) * MHD].astype(BF16)

    return _call(
        body, name=name,
        in_specs=[_full((m, D)), _full((1, D)), _full((D, 2 * MW)), _full((1, MHD))],
        out_specs=[_full((HM, m, MHD))] * 2,
        out_shape=[jax.ShapeDtypeStruct((HM, m, MHD), BF16)] * 2,
        compiler_params=_cp(),
    )(mem, mng, wmkv, mkg)


def _conv_parts(cv, prev, nxt, first, last, cw, cb):
    t = cv.shape[0]
    c_b, c_c, c_u, g_c = (cv[:, i * CW:(i + 1) * CW].astype(F32) for i in range(4))
    z = c_c * c_u
    zp = jnp.where(first, 0.0, prev[15:16, CW:2 * CW].astype(F32) * prev[15:16, 2 * CW:3 * CW].astype(F32))
    zn = jnp.where(last, 0.0, nxt[0:1, CW:2 * CW].astype(F32) * nxt[0:1, 2 * CW:3 * CW].astype(F32))
    row = lax.broadcasted_iota(jnp.int32, (t, CW), 0)
    z_m1 = jnp.where(row == 0, zp, pltpu.roll(z, 1, 0))
    z_p1 = jnp.where(row == t - 1, zn, pltpu.roll(z, t - 1, 0))
    conv = cw[0:1] * z_m1 + cw[1:2] * z + cw[2:3] * z_p1 + cb
    return c_b, c_c, c_u, g_c, z, z_m1, z_p1, conv


def _mem_attn(qm, mqg, mk_ref, mv_ref):
    outs = []
    for h in range(HM):
        mq, mqxh, mqrs = _rms(qm[:, h * MHD:(h + 1) * MHD], mqg, MHD)
        mq16 = mq.astype(BF16)
        sc = _dot_nt(mq16, mk_ref[h]) * MEM_SCALE
        e = jnp.exp(sc - jnp.max(sc, axis=-1, keepdims=True))
        p = e / jnp.sum(e, axis=-1, keepdims=True)
        o = _dot(p.astype(BF16), mv_ref[h])
        outs.append((mq16, mqxh, mqrs, p, o))
    return outs


def _halo_specs(t, s, width):
    nb = s // 16
    prev = pl.BlockSpec((16, width), lambda i: (jnp.maximum(i * (t // 16) - 1, 0), 0))
    nxt = pl.BlockSpec((16, width), lambda i: (jnp.minimum((i + 1) * (t // 16), nb - 1), 0))
    return prev, nxt


def _branches(proj, o, mk, mv, cw, cb, mqg, t, name):
    s = proj.shape[0]
    t = min(t, s)
    nt = s // t

    def body(cv_ref, pv_ref, nx_ref, mm_ref, ga_ref, o_ref, mk_ref, mv_ref, cw_ref, cb_ref, mqg_ref,
             oa_ref, oc_ref, om_ref):
        i = pl.program_id(0)
        c_b, _, _, g_c, _, _, _, conv = _conv_parts(
            cv_ref[...], pv_ref[...], nx_ref[...], i == 0, i == nt - 1, cw_ref[...], cb_ref[...])
        oc_ref[...] = (c_b * conv * (g_c * _sigmoid(g_c))).astype(BF16)
        ga = ga_ref[...].astype(F32)
        ocat = jnp.concatenate([o_ref[h].astype(F32) for h in range(H)], axis=-1)
        oa_ref[...] = (ocat * (ga * _sigmoid(ga))).astype(BF16)
        mblk = mm_ref[...].astype(F32)
        gm = mblk[:, MW:]
        heads = _mem_attn(mblk[:, :MW], mqg_ref[...], mk_ref, mv_ref)
        om = jnp.concatenate([hh[4] for hh in heads], axis=-1)
        om_ref[...] = (om * (gm * _sigmoid(gm))).astype(BF16)

    pv, nx = _halo_specs(t, s, 4 * CW)
    out = pl.BlockSpec((t, 512), lambda i: (i, 0))
    return _call(
        body, name=name, grid=(nt,),
        in_specs=[pl.BlockSpec((t, 4 * CW), lambda i: (i, 0)), pv, nx,
                  pl.BlockSpec((t, 2 * MW), lambda i: (i, P_MEM // (2 * MW))),
                  pl.BlockSpec((t, AW), lambda i: (i, P_GA // AW)),
                  pl.BlockSpec((H, t, VD), lambda i: (0, i, 0)),
                  _full(mk.shape), _full(mv.shape), _full((3, CW)), _full((1, CW)), _full((1, MHD))],
        out_specs=[out, out, out],
        out_shape=[jax.ShapeDtypeStruct((s, 512), BF16)] * 3,
        compiler_params=_cp(("parallel",)),
    )(proj, proj, proj, proj, proj, o, mk, mv, cw, cb, mqg)


def _merge_fwd(x, proj, oa, oc, om, bg, wa, wc, wm, wo, t, name):
    s = x.shape[0]
    t = min(t, s)

    def body(x_ref, r_ref, oa_ref, oc_ref, om_ref, bg_ref, wa_ref, wc_ref, wm_ref, wo_ref,
             xo_ref, aa_ref, ac_ref, am_ref):
        y = jnp.zeros((t, D), F32)
        for j, (o_ref, w_ref, a_ref) in enumerate(((oa_ref, wa_ref, aa_ref), (oc_ref, wc_ref, ac_ref),
                                                   (om_ref, wm_ref, am_ref))):
            a = _dot(o_ref[...], w_ref[...])
            a_ref[...] = a.astype(BF16)
            rg = _sigmoid(r_ref[:, j * D:(j + 1) * D].astype(F32) + bg_ref[:, j * D:(j + 1) * D])
            y = y + rg * a
        xo_ref[...] = x_ref[...] + _dot(y.astype(BF16), wo_ref[...])

    row = lambda w: pl.BlockSpec((t, w), lambda i: (i, 0))
    return _call(
        body, name=name, grid=(s // t,),
        in_specs=[row(D), pl.BlockSpec((t, 3 * D), lambda i: (i, P_R // (3 * D))), row(512), row(512), row(512),
                  _full((1, 3 * D)), _full((512, D)), _full((512, D)), _full((512, D)), _full((D, D))],
        out_specs=[row(D), row(D), row(D), row(D)],
        out_shape=[jax.ShapeDtypeStruct((s, D), F32)] + [jax.ShapeDtypeStruct((s, D), BF16)] * 3,
        compiler_params=_cp(("parallel",)),
    )(x, proj, oa, oc, om, bg, wa, wc, wm, wo)


def _loss_grad(x, tgt, t, name):
    s = x.shape[0]
    t = min(t, s)

    def body(x_ref, t_ref, g_ref, l_ref):
        @pl.when(pl.program_id(0) == 0)
        def _():
            l_ref[...] = jnp.zeros_like(l_ref)

        e = x_ref[...] - t_ref[...]
        g_ref[...] = e * (1.0 / D)
        sq = e * e
        part = sq[:, 0:LANES]
        for j in range(1, D // LANES):
            part = part + sq[:, j * LANES:(j + 1) * LANES]
        acc = part[0:8]
        for j in range(1, t // 8):
            acc = acc + part[j * 8:(j + 1) * 8]
        l_ref[...] += acc * (0.5 / D)

    row = pl.BlockSpec((t, D), lambda i: (i, 0))
    return _call(
        body, name=name, grid=(s // t,),
        in_specs=[row, row], out_specs=[row, _full((8, LANES))],
        out_shape=[jax.ShapeDtypeStruct((s, D), F32), jax.ShapeDtypeStruct((8, LANES), F32)],
        compiler_params=_cp(("arbitrary",)),
    )(x, tgt)


def _merge_bwd(g, proj, aa, ac, am, bg, wot, wat, wct, wmt, t, name):
    s = g.shape[0]
    t = min(t, s)

    def body(g_ref, r_ref, aa_ref, ac_ref, am_ref, bg_ref, wot_ref, wat_ref, wct_ref, wmt_ref,
             y_ref, daa_ref, dac_ref, dam_ref, dr_ref, doa_ref, doc_ref, dom_ref, dbg_ref):
        @pl.when(pl.program_id(0) == 0)
        def _():
            dbg_ref[...] = jnp.zeros_like(dbg_ref)

        dy = _dot(g_ref[...].astype(BF16), wot_ref[...])
        y = jnp.zeros((t, D), F32)
        for j, (a_ref, wt_ref, da_ref, do_ref) in enumerate(((aa_ref, wat_ref, daa_ref, doa_ref),
                                                             (ac_ref, wct_ref, dac_ref, doc_ref),
                                                             (am_ref, wmt_ref, dam_ref, dom_ref))):
            a = a_ref[...].astype(F32)
            rg = _sigmoid(r_ref[:, j * D:(j + 1) * D].astype(F32) + bg_ref[:, j * D:(j + 1) * D])
            y = y + rg * a
            da = (dy * rg).astype(BF16)
            da_ref[...] = da
            dr = dy * a * rg * (1.0 - rg)
            dr_ref[:, j * D:(j + 1) * D] = dr.astype(BF16)
            dbg_ref[:, j * D:(j + 1) * D] += jnp.sum(dr, axis=0, keepdims=True)
            do_ref[...] = _dot(da, wt_ref[...])
        y_ref[...] = y.astype(BF16)

    row = lambda w: pl.BlockSpec((t, w), lambda i: (i, 0))
    return _call(
        body, name=name, grid=(s // t,),
        in_specs=[row(D), pl.BlockSpec((t, 3 * D), lambda i: (i, P_R // (3 * D))), row(D), row(D), row(D),
                  _full((1, 3 * D)), _full((D, D)), _full((D, 512)), _full((D, 512)), _full((D, 512))],
        out_specs=[row(D), row(D), row(D), row(D), row(3 * D), row(512), row(512), row(512), _full((1, 3 * D))],
        out_shape=[jax.ShapeDtypeStruct((s, D), BF16)] * 4 + [jax.ShapeDtypeStruct((s, 3 * D), BF16)]
        + [jax.ShapeDtypeStruct((s, 512), F32)] * 3 + [jax.ShapeDtypeStruct((1, 3 * D), F32)],
        compiler_params=_cp(("arbitrary",)),
    )(g, proj, aa, ac, am, bg, wot, wat, wct, wmt)


def _branches_bwd(proj, o, mk, mv, cw, cb, mqg, doa, doc, dom, t, name):
    s = proj.shape[0]
    t = min(t, s)
    nt = s // t
    m = mk.shape[1]

    def body(cv_ref, pv_ref, nx_ref, mm_ref, ga_ref, o_ref, mk_ref, mv_ref, cw_ref, cb_ref, mqg_ref,
             doa_ref, doc_ref, dcp_ref, dcn_ref, dom_ref,
             dcv_ref, dmm_ref, dga_ref, do_ref, dl_ref, dcw_ref, dcb_ref, dmk_ref, dmv_ref, dmqg_ref):
        i = pl.program_id(0)

        @pl.when(i == 0)
        def _():
            for r in (dcw_ref, dcb_ref, dmk_ref, dmv_ref, dmqg_ref):
                r[...] = jnp.zeros_like(r)

        first, last = i == 0, i == nt - 1
        cw_, cb_ = cw_ref[...], cb_ref[...]
        pv, nx = pv_ref[...], nx_ref[...]
        c_b, c_c, c_u, g_c, z, z_m1, z_p1, conv = _conv_parts(cv_ref[...], pv, nx, first, last, cw_, cb_)
        sg = _sigmoid(g_c)
        silu = g_c * sg
        dsilu = sg * (1.0 + g_c * (1.0 - sg))
        doc_ = doc_ref[...]
        dconv = doc_ * c_b * silu
        gp = pv[15:16, 3 * CW:4 * CW].astype(F32)
        gn = nx[0:1, 3 * CW:4 * CW].astype(F32)
        dconv_p = jnp.where(first, 0.0, dcp_ref[15:16, :] * pv[15:16, 0:CW].astype(F32) * (gp * _sigmoid(gp)))
        dconv_n = jnp.where(last, 0.0, dcn_ref[0:1, :] * nx[0:1, 0:CW].astype(F32) * (gn * _sigmoid(gn)))
        row = lax.broadcasted_iota(jnp.int32, (t, CW), 0)
        d_m1 = jnp.where(row == 0, dconv_p, pltpu.roll(dconv, 1, 0))
        d_p1 = jnp.where(row == t - 1, dconv_n, pltpu.roll(dconv, t - 1, 0))
        dz = cw_[0:1] * d_p1 + cw_[1:2] * dconv + cw_[2:3] * d_m1
        dcv_ref[:, 0:CW] = (doc_ * conv * silu).astype(BF16)
        dcv_ref[:, CW:2 * CW] = (dz * c_u).astype(BF16)
        dcv_ref[:, 2 * CW:3 * CW] = (dz * c_c).astype(BF16)
        dcv_ref[:, 3 * CW:4 * CW] = (doc_ * c_b * conv * dsilu).astype(BF16)
        dcw_ref[0:1, :] += jnp.sum(dconv * z_m1, axis=0, keepdims=True)
        dcw_ref[1:2, :] += jnp.sum(dconv * z, axis=0, keepdims=True)
        dcw_ref[2:3, :] += jnp.sum(dconv * z_p1, axis=0, keepdims=True)
        dcb_ref[...] += jnp.sum(dconv, axis=0, keepdims=True)
        ga = ga_ref[...].astype(F32)
        sg = _sigmoid(ga)
        doa_ = doa_ref[...]
        ocat = jnp.concatenate([o_ref[h].astype(F32) for h in range(H)], axis=-1)
        dga_ref[...] = (doa_ * ocat * (sg * (1.0 + ga * (1.0 - sg)))).astype(BF16)
        dog = doa_ * (ga * sg)
        zeros = jnp.zeros((t, HP - VD), F32)
        lane = lax.broadcasted_iota(jnp.int32, (t, LANES), 1)
        dmat = jnp.zeros((t, LANES), F32)
        for h in range(H):
            dh = dog[:, h * VD:(h + 1) * VD]
            do_ref[h] = jnp.concatenate([dh, zeros], axis=-1).astype(BF16)
            dmat = jnp.where(lane == h, jnp.sum(dh * ocat[:, h * VD:(h + 1) * VD], axis=-1, keepdims=True), dmat)
        dlt = dmat.T
        for h in range(H):
            dl_ref[h] = dlt[h:h + 1, :]
        mblk = mm_ref[...].astype(F32)
        gm = mblk[:, MW:]
        sg = _sigmoid(gm)
        dom_ = dom_ref[...]
        heads = _mem_attn(mblk[:, :MW], mqg_ref[...], mk_ref, mv_ref)
        om = jnp.concatenate([hh[4] for hh in heads], axis=-1)
        dmm_ref[:, MW:] = (dom_ * om * (sg * (1.0 + gm * (1.0 - sg)))).astype(BF16)
        dmo = dom_ * (gm * sg)
        dmqg = jnp.zeros((1, MHD), F32)
        for h in range(HM):
            mq16, mqxh, mqrs, p, _ = heads[h]
            dmo_h = dmo[:, h * MHD:(h + 1) * MHD].astype(BF16)
            dp = _dot_nt(dmo_h, mv_ref[h])
            ds = (p * (dp - jnp.sum(dp * p, axis=-1, keepdims=True)) * MEM_SCALE).astype(BF16)
            dmq = _dot(ds, mk_ref[h])
            dmk_ref[h] += _dot_tn(ds, mq16)
            dmv_ref[h] += _dot_tn(p.astype(BF16), dmo_h)
            dq, dg = _rms_bwd(dmq, mqxh, mqrs, mqg_ref[...], MHD)
            dmm_ref[:, h * MHD:(h + 1) * MHD] = dq.astype(BF16)
            dmqg = dmqg + dg
        dmqg_ref[...] += dmqg

    pv, nx = _halo_specs(t, s, 4 * CW)
    dpv, dnx = _halo_specs(t, s, CW)
    row = lambda w: pl.BlockSpec((t, w), lambda i: (i, 0))
    hs = lambda w: pl.BlockSpec((H, t, w), lambda i: (0, i, 0))
    return _call(
        body, name=name, grid=(nt,),
        in_specs=[pl.BlockSpec((t, 4 * CW), lambda i: (i, 0)), pv, nx,
                  pl.BlockSpec((t, 2 * MW), lambda i: (i, P_MEM // (2 * MW))),
                  pl.BlockSpec((t, AW), lambda i: (i, P_GA // AW)),
                  hs(VD), _full(mk.shape), _full(mv.shape), _full((3, CW)), _full((1, CW)), _full((1, MHD)),
                  row(512), row(512), dpv, dnx, row(512)],
        out_specs=[row(4 * CW), row(2 * MW), row(AW), hs(HP), pl.BlockSpec((H, 1, t), lambda i: (0, 0, i)),
                   _full((3, CW)), _full((1, CW)),
                   _full((HM, m, MHD)), _full((HM, m, MHD)), _full((1, MHD))],
        out_shape=[jax.ShapeDtypeStruct((s, 4 * CW), BF16), jax.ShapeDtypeStruct((s, 2 * MW), BF16),
                   jax.ShapeDtypeStruct((s, AW), BF16), jax.ShapeDtypeStruct((H, s, HP), BF16),
                   jax.ShapeDtypeStruct((H, 1, s), F32), jax.ShapeDtypeStruct((3, CW), F32),
                   jax.ShapeDtypeStruct((1, CW), F32), jax.ShapeDtypeStruct((HM, m, MHD), F32),
                   jax.ShapeDtypeStruct((HM, m, MHD), F32), jax.ShapeDtypeStruct((1, MHD), F32)],
        compiler_params=_cp(("arbitrary",)),
    )(proj, proj, proj, proj, proj, o, mk, mv, cw, cb, mqg, doa, doc, doc, doc, dom)


def _flash_bwd(q, k, v, do, lse, dl, tq, tk, name):
    _, s, _ = q.shape
    tq, tk = min(tq, s), min(tk, s)
    nq, nkt, nc = s // tq, s // tk, tk // LANES
    unroll = 16 if nq % 16 == 0 else 1

    def body(q_ref, do_ref, lse_ref, dl_ref, k_ref, v_ref, dq_ref, dk_ref, dv_ref, dq_acc):
        j = pl.program_id(1)

        @pl.when(j == 0)
        def _():
            dq_acc[...] = jnp.zeros_like(dq_acc)

        kb, vb = k_ref[0], v_ref[0]

        def step(ii, carry):
            dkt, dvt = carry
            for u in range(unroll):
                i = ii * unroll + u
                off = pl.multiple_of(i * tq, tq)
                qb = q_ref[0, pl.ds(off, tq), :]
                dob = do_ref[0, pl.ds(off, tq), :]
                lse_b = jnp.broadcast_to(lse_ref[0, i], (LANES, tq)).T
                dl_b = jnp.broadcast_to(dl_ref[0, i], (LANES, tq)).T
                sc = _dot_nt(qb, kb)
                dp = _dot_nt(dob, vb)
                ps, dss = [], []
                for cc in range(nc):
                    p = jnp.exp2(sc[:, cc * LANES:(cc + 1) * LANES] - lse_b)
                    ps.append(p.astype(BF16))
                    dss.append((p * (dp[:, cc * LANES:(cc + 1) * LANES] - dl_b)).astype(BF16))
                p16, ds16 = jnp.concatenate(ps, axis=-1), jnp.concatenate(dss, axis=-1)
                dvt = dvt + _dot_tn(dob, p16)
                dkt = dkt + _dot_tn(qb, ds16)
                dq_acc[pl.ds(off, tq), :] += _dot(ds16, kb)
            return dkt, dvt

        dkt, dvt = lax.fori_loop(0, nq // unroll, step, (jnp.zeros((HP, tk), F32), jnp.zeros((HP, tk), F32)))
        dk_ref[0] = (dkt.T * (1.0 / LOG2E)).astype(BF16)
        dv_ref[0] = dvt.T.astype(BF16)

        @pl.when(j == nkt - 1)
        def _():
            dq_ref[0] = (dq_acc[...] * ATT_SCALE).astype(BF16)

    whole = pl.BlockSpec((1, s, HP), lambda h, j: (h, 0, 0))
    stat = pl.BlockSpec((1, nq, 1, tq), lambda h, j: (h, 0, 0, 0))
    tile = pl.BlockSpec((1, tk, HP), lambda h, j: (h, j, 0))
    return _call(
        body, name=name, grid=(H, nkt),
        in_specs=[whole, whole, stat, stat, tile, tile],
        out_specs=[whole, tile, tile],
        out_shape=[jax.ShapeDtypeStruct((H, s, HP), BF16)] * 3,
        scratch_shapes=[pltpu.VMEM((s, HP), F32)],
        compiler_params=_cp(("arbitrary", "arbitrary")),
    )(q, do, lse, dl, k, v)


def _mla_prep_bwd(proj_l, c, sn, qng, kvng, qhg, khg, wuq, wukv, wuqt, wukvt, dq, dk, dv, t, name):
    s = proj_l.shape[0]
    t = min(t, s)

    def body(l_ref, c_ref, sn_ref, qng_ref, kvng_ref, qhg_ref, khg_ref, wuq_ref, wukv_ref, wuqt_ref, wukvt_ref,
             dq_ref, dk_ref, dv_ref, dl_ref, dwuq_ref, dwukv_ref, dqng_ref, dkvng_ref, dqhg_ref, dkhg_ref):
        @pl.when(pl.program_id(0) == 0)
        def _():
            for r in (dwuq_ref, dwukv_ref, dqng_ref, dkvng_ref, dqhg_ref, dkhg_ref):
                r[...] = jnp.zeros_like(r)

        cc, ss = c_ref[...], sn_ref[...]
        qhg, khg = qhg_ref[...], khg_ref[...]
        (_, _, kpe, qxh, qrs, kvxh, kvrs, qn16, kvn16, qp, kvp) = _mla_heads(
            l_ref[...], cc, ss, qng_ref[...], kvng_ref[...], qhg, khg, wuq_ref[...], wukv_ref[...])
        lane = lax.broadcasted_iota(jnp.int32, (t, HP), 1)
        dqp, dkp, dvp = [], [], []
        dkpe = jnp.zeros((t, HP), F32)
        dqhg = jnp.zeros((1, HP), F32)
        dkhg = jnp.zeros((1, HP), F32)
        for h in range(H):
            _, xh, rs = _rms(qp[:, h * HP:(h + 1) * HP], qhg, QKD)
            du, dg = _rms_bwd(_rope_adj(dq_ref[h].astype(F32), cc, ss), xh, rs, qhg, QKD)
            dqp.append(du)
            dqhg = dqhg + dg
            _, xh, rs = _rms(kvp[:, h * HP:(h + 1) * HP] + kpe, khg, QKD)
            du, dg = _rms_bwd(_rope_adj(dk_ref[h].astype(F32), cc, ss), xh, rs, khg, QKD)
            dkp.append(du)
            dkhg = dkhg + dg
            dkpe = dkpe + jnp.where((lane < RP // 2) | ((lane >= 64) & (lane < 64 + RP // 2)), du, 0.0)
            dvp.append(dv_ref[h].astype(F32))
        dqhg_ref[...] += dqhg
        dkhg_ref[...] += dkhg
        dqp16 = jnp.concatenate(dqp, axis=-1).astype(BF16)
        dkvp16 = jnp.concatenate(dkp + dvp, axis=-1).astype(BF16)
        dwuq_ref[...] += _dot_tn(qn16, dqp16)
        dwukv_ref[...] += _dot_tn(kvn16, dkvp16)
        dql, dg = _rms_bwd(_dot(dqp16, wuqt_ref[...]), qxh, qrs, qng_ref[...], QL)
        dqng_ref[...] += dg
        dkvl, dg = _rms_bwd(_dot(dkvp16, wukvt_ref[...]), kvxh, kvrs, kvng_ref[...], KVL)
        dkvng_ref[...] += dg
        dl_ref[:, 0:QL] = dql.astype(BF16)
        dl_ref[:, QL:QL + KVL] = dkvl.astype(BF16)
        dl_ref[:, QL + KVL:P_LW] = dkpe.astype(BF16)

    hs = pl.BlockSpec((H, t, HP), lambda i: (0, i, 0))
    row = lambda w: pl.BlockSpec((t, w), lambda i: (i, 0))
    return _call(
        body, name=name, grid=(s // t,),
        in_specs=[row(P_LW), row(HP), row(HP), _full((1, QL)), _full((1, KVL)), _full((1, HP)), _full((1, HP)),
                  _full((QL, H * HP)), _full((KVL, 2 * H * HP)), _full((H * HP, QL)), _full((2 * H * HP, KVL)),
                  hs, hs, hs],
        out_specs=[row(P_LW), _full((QL, H * HP)), _full((KVL, 2 * H * HP)), _full((1, QL)), _full((1, KVL)),
                   _full((1, HP)), _full((1, HP))],
        out_shape=[jax.ShapeDtypeStruct((s, P_LW), BF16), jax.ShapeDtypeStruct((QL, H * HP), F32),
                   jax.ShapeDtypeStruct((KVL, 2 * H * HP), F32), jax.ShapeDtypeStruct((1, QL), F32),
                   jax.ShapeDtypeStruct((1, KVL), F32), jax.ShapeDtypeStruct((1, HP), F32),
                   jax.ShapeDtypeStruct((1, HP), F32)],
        compiler_params=_cp(("arbitrary",)),
    )(proj_l, c, sn, qng, kvng, qhg, khg, wuq, wukv, wuqt, wukvt, dq, dk, dv)


def _mem_prep_bwd(mem, mng, wmkv, wmkvt, mkg, dmk, dmv, name):
    m = mem.shape[0]

    def body(mem_ref, mng_ref, w_ref, wt_ref, mkg_ref, dmk_ref, dmv_ref, dw_ref, dmng_ref, dmkg_ref):
        mn, xh, _ = _rms(mem_ref[...], mng_ref[...], D)
        mn16 = mn.astype(BF16)
        mkv = _dot(mn16, w_ref[...])
        parts = []
        dmkg = jnp.zeros((1, MHD), F32)
        for h in range(HM):
            _, kxh, krs = _rms(mkv[:, 2 * h * MHD:(2 * h + 1) * MHD], mkg_ref[...], MHD)
            du, dg = _rms_bwd(dmk_ref[h], kxh, krs, mkg_ref[...], MHD)
            dmkg = dmkg + dg
            parts += [du, dmv_ref[h]]
        dmkv = jnp.concatenate(parts, axis=-1).astype(BF16)
        dw_ref[...] = _dot_tn(mn16, dmkv)
        dmn = _dot(dmkv, wt_ref[...])
        dmng_ref[...] = jnp.sum(dmn * xh, axis=0, keepdims=True)
        dmkg_ref[...] = dmkg

    return _call(
        body, name=name,
        in_specs=[_full((m, D)), _full((1, D)), _full((D, 2 * MW)), _full((2 * MW, D)), _full((1, MHD)),
                  _full((HM, m, MHD)), _full((HM, m, MHD))],
        out_specs=[_full((D, 2 * MW)), _full((1, D)), _full((1, MHD))],
        out_shape=[jax.ShapeDtypeStruct((D, 2 * MW), F32), jax.ShapeDtypeStruct((1, D), F32),
                   jax.ShapeDtypeStruct((1, MHD), F32)],
        compiler_params=_cp(),
    )(mem, mng, wmkv, wmkvt, mkg, dmk, dmv)


def _rms_in_bwd(x, g_out, dh, ng, t, name):
    s = x.shape[0]
    t = min(t, s)

    def body(x_ref, go_ref, dh_ref, ng_ref, dx_ref, dng_ref):
        @pl.when(pl.program_id(0) == 0)
        def _():
            dng_ref[...] = jnp.zeros_like(dng_ref)

        _, xh, rs = _rms(x_ref[...], ng_ref[...], D)
        dx, dg = _rms_bwd(dh_ref[...], xh, rs, ng_ref[...], D)
        dx_ref[...] = go_ref[...] + dx
        dng_ref[...] += dg

    row = pl.BlockSpec((t, D), lambda i: (i, 0))
    return _call(
        body, name=name, grid=(s // t,),
        in_specs=[row, row, row, _full((1, D))], out_specs=[row, _full((1, D))],
        out_shape=[jax.ShapeDtypeStruct((s, D), F32), jax.ShapeDtypeStruct((1, D), F32)],
        compiler_params=_cp(("arbitrary",)),
    )(x, g_out, dh, ng)


def _allgather(arrs, name):
    n = len(arrs)

    def body(*refs):
        x_refs, out_refs = refs[:n], refs[n:2 * n]
        send_sems, recv_sems, local_sems = refs[2 * n:]
        x, y, c = lax.axis_index("x"), lax.axis_index("y"), lax.axis_index("c")
        me, sibling = (x, y, c), (x, y, 1 - c)
        chips = [(1 - x, y), (x, 1 - y), (1 - x, 1 - y)]

        def slot(a, px, py, pc):
            return out_refs[a].at[4 * px + 2 * py + pc]

        def copy(a, k, block, to, src=None):
            return pltpu.make_async_remote_copy(
                src_ref=slot(a, *block) if src is None else src, dst_ref=slot(a, *block),
                send_sem=send_sems.at[a, k], recv_sem=recv_sems.at[a, k],
                device_id=to, device_id_type=pl.DeviceIdType.MESH)

        mine = [pltpu.make_async_copy(x_refs[a], slot(a, *me), local_sems.at[a]) for a in range(n)]
        first, passed = [], []
        for a in range(n):
            mine[a].start()
            first.append(copy(a, 0, me, sibling, src=x_refs[a]))
            first += [copy(a, 1 + j, me, (*chip, c), src=x_refs[a]) for j, chip in enumerate(chips)]
        for cp in first:
            cp.start()
        for j, chip in enumerate(chips):
            for a in range(n):
                copy(a, 1 + j, (*chip, c), me).wait_recv()
                passed.append(copy(a, 4 + j, (*chip, c), sibling))
                passed[-1].start()
        for a in range(n):
            copy(a, 0, sibling, me).wait_recv()
        for j, chip in enumerate(chips):
            for a in range(n):
                copy(a, 4 + j, (*chip, 1 - c), me).wait_recv()
        for cp in first + passed:
            cp.wait_send()
        for cp in mine:
            cp.wait()

    any_spec = pl.BlockSpec(memory_space=pl.ANY)
    return _call(
        body, name=name,
        in_specs=[any_spec] * n, out_specs=[any_spec] * n,
        out_shape=[jax.ShapeDtypeStruct((N_DEV,) + a.shape, a.dtype) for a in arrs],
        scratch_shapes=[pltpu.SemaphoreType.DMA((n, 7)), pltpu.SemaphoreType.DMA((n, 7)),
                        pltpu.SemaphoreType.DMA((n,))],
    )(*arrs)


def _pair_exchange(arrs, name):
    n = len(arrs)

    def body(*refs):
        x_refs, out_refs = refs[:n], refs[n:2 * n]
        send_sems, recv_sems = refs[2 * n:]
        x, y, c = lax.axis_index("x"), lax.axis_index("y"), lax.axis_index("c")
        copies = [pltpu.make_async_remote_copy(
            src_ref=x_refs[a].at[1 - c], dst_ref=out_refs[a],
            send_sem=send_sems.at[a], recv_sem=recv_sems.at[a],
            device_id=(x, y, 1 - c), device_id_type=pl.DeviceIdType.MESH) for a in range(n)]
        for cp in copies:
            cp.start()
        for cp in copies:
            cp.wait_recv()
        for cp in copies:
            cp.wait_send()

    any_spec = pl.BlockSpec(memory_space=pl.ANY)
    return _call(
        body, name=name,
        in_specs=[any_spec] * n, out_specs=[any_spec] * n,
        out_shape=[jax.ShapeDtypeStruct(a.shape[1:], a.dtype) for a in arrs],
        scratch_shapes=[pltpu.SemaphoreType.DMA((n,)), pltpu.SemaphoreType.DMA((n,))],
    )(*arrs)


def _chip_exchange(arrs, name):
    n = len(arrs)

    def body(*refs):
        x_refs, out_refs = refs[:n], refs[n:2 * n]
        send_sems, recv_sems, local_sems = refs[2 * n:]
        x, y, c = lax.axis_index("x"), lax.axis_index("y"), lax.axis_index("c")
        me = 2 * x + y
        mine = [pltpu.make_async_copy(x_refs[a].at[me], out_refs[a].at[me], local_sems.at[a]) for a in range(n)]
        for cp in mine:
            cp.start()
        copies = []
        for k, (px, py) in enumerate([(1 - x, y), (x, 1 - y), (1 - x, 1 - y)]):
            for a in range(n):
                copies.append(pltpu.make_async_remote_copy(
                    src_ref=x_refs[a].at[2 * px + py], dst_ref=out_refs[a].at[me],
                    send_sem=send_sems.at[a, k], recv_sem=recv_sems.at[a, k],
                    device_id=(px, py, c), device_id_type=pl.DeviceIdType.MESH))
        for cp in copies:
            cp.start()
        for cp in copies:
            cp.wait_recv()
        for cp in copies:
            cp.wait_send()
        for cp in mine:
            cp.wait()

    any_spec = pl.BlockSpec(memory_space=pl.ANY)
    return _call(
        body, name=name,
        in_specs=[any_spec] * n, out_specs=[any_spec] * n,
        out_shape=[jax.ShapeDtypeStruct(a.shape, a.dtype) for a in arrs],
        scratch_shapes=[pltpu.SemaphoreType.DMA((n, 3)), pltpu.SemaphoreType.DMA((n, 3)),
                        pltpu.SemaphoreType.DMA((n,))],
    )(*arrs)


def _pair_add(a, b, name):
    r, c_ = a.shape
    cpad = -(-c_ // LANES) * LANES
    tr = r
    while tr * cpad * 4 > ADAMW_BLOCK_BYTES and tr % 32 == 0:
        tr //= 2

    def body(a_ref, b_ref, o_ref):
        o_ref[...] = (a_ref[...].astype(F32) + b_ref[...].astype(F32)).astype(o_ref.dtype)

    row = pl.BlockSpec((tr, c_), lambda i: (i, 0))
    return _call(
        body, name=name, grid=(r // tr,), in_specs=[row, row], out_specs=row,
        out_shape=jax.ShapeDtypeStruct((r, c_), a.dtype), compiler_params=_cp(("parallel",)),
    )(a, b)


ADAMW_BLOCK_BYTES = 4 * 1024 * 1024


def _adamw(parts, w, m, v, name):
    r, c_ = w.shape
    n_parts = parts.shape[0]
    cpad = -(-c_ // LANES) * LANES
    tr = r
    while N_DEV * tr * cpad * 4 > ADAMW_BLOCK_BYTES and tr % 16 == 0:
        tr //= 2
    c1 = 1.0 / (1.0 - ADAM_B1 ** ADAM_STEP)
    c2 = 1.0 / (1.0 - ADAM_B2 ** ADAM_STEP)

    def body(p_ref, w_ref, m_ref, v_ref, g_ref, d_ref, nm_ref, nv_ref):
        g = p_ref[0].astype(F32)
        for j in range(1, n_parts):
            g = g + p_ref[j].astype(F32)
        nm = ADAM_B1 * m_ref[...] + (1.0 - ADAM_B1) * g
        nv = ADAM_B2 * v_ref[...] + (1.0 - ADAM_B2) * (g * g)
        g_ref[...] = g
        nm_ref[...] = nm
        nv_ref[...] = nv
        d_ref[...] = -ADAM_LR * ((nm * c1) / (jnp.sqrt(nv * c2) + ADAM_EPS) + ADAM_WD * w_ref[...])

    row = pl.BlockSpec((tr, c_), lambda i: (i, 0))
    return _call(
        body, name=name, grid=(r // tr,),
        in_specs=[pl.BlockSpec((n_parts, tr, c_), lambda i: (0, i, 0)), row, row, row],
        out_specs=[row] * 4, out_shape=[jax.ShapeDtypeStruct((r, c_), F32)] * 4,
        compiler_params=_cp(("parallel",)),
    )(parts, w, m, v)


def _to_rows(flat, align=8):
    n = flat.shape[-1]
    rows = -(-n // (LANES * align)) * align
    return jnp.pad(flat, (0, rows * LANES - n)).reshape(rows, LANES)


def _shard_blocks(full, axis):
    r, c_ = full.shape
    if axis == 0:
        return full.reshape(4, 2, r // N_DEV, c_).transpose(1, 0, 2, 3)
    return full.reshape(r, 4, 2, c_ // N_DEV).transpose(2, 1, 0, 3)


def _unshard_blocks(blocks, axis):
    _, r, c_ = blocks.shape
    if axis == 0:
        return blocks.reshape(N_DEV * r, c_)
    return blocks.transpose(1, 0, 2).reshape(r, N_DEV * c_)


def _head_lanes(v):
    z = jnp.zeros(v.shape[:-1] + (HP - QKD,), v.dtype)
    return jnp.concatenate([v[..., 64:80], v[..., 0:48], v[..., 80:96], v[..., 48:64], z], axis=-1)


def _head_dims(v):
    return jnp.concatenate([v[..., 16:64], v[..., 80:96], v[..., 0:16], v[..., 64:80]], axis=-1)


def _pad_heads(w):
    k = w.shape[0]
    return _head_lanes(w.reshape(k, H, QKD)).reshape(k, H * HP)


def _unpad_heads(w):
    k = w.shape[0]
    return _head_dims(w.reshape(k, H, HP)).reshape(k, H * QKD)


IN_SPLIT = {'q_lat': (0, 384), 'kv_lat': (384, 640), 'k_pe': (640, 672), 'c_b': (672, 1184), 'c_c': (1184, 1696),
            'c_u': (1696, 2208), 'q_mem': (2208, 2720), 'g_attn': (2720, 3232), 'g_conv': (3232, 3744),
            'g_mem': (3744, 4256), 'r': (4256, 7328)}
P_ORDER = ['c_b', 'c_c', 'c_u', 'g_conv', 'q_mem', 'g_mem', 'r', 'g_attn', 'q_lat', 'kv_lat']


def _permute_w_in(w):
    k = w.shape[0]
    cols = [w[:, IN_SPLIT[n][0]:IN_SPLIT[n][1]] for n in P_ORDER]
    kpe = w[:, IN_SPLIT['k_pe'][0]:IN_SPLIT['k_pe'][1]]
    cols.append(_head_lanes(jnp.concatenate([jnp.zeros((k, NOPE), w.dtype), kpe], axis=1)))
    return jnp.concatenate(cols, axis=1)


def _unpermute_w_in(pieces):
    bounds, off = [], 0
    for p in pieces:
        bounds.append((off, off + p.shape[1]))
        off += p.shape[1]

    def cols(lo, hi):
        for p, (b0, b1) in zip(pieces, bounds):
            if b0 <= lo and hi <= b1:
                return p[:, lo - b0:hi - b0]
        raise ValueError("a column range straddles two pieces")

    off, pos = 0, {}
    for n in P_ORDER:
        wd = IN_SPLIT[n][1] - IN_SPLIT[n][0]
        pos[n] = (off, off + wd)
        off += wd
    kpe = _head_dims(cols(off, off + HP))[:, NOPE:QKD]
    order = sorted(IN_SPLIT, key=lambda n: IN_SPLIT[n][0])
    return jnp.concatenate([kpe if n == 'k_pe' else cols(*pos[n]) for n in order], axis=1)


def _layer_weights(full, l):
    w = {}
    w_in_p = _permute_w_in(full['w_in'][l])
    w['w_main'] = w_in_p[:, :P_MAIN]
    w['w_l'] = w_in_p[:, P_MAIN:]
    w['w_in_t'] = w_in_p.T
    w['w_uq'] = _pad_heads(full['w_uq'][l])
    wukv = full['w_ukv'][l].reshape(KVL, H, 2, NOPE)
    kpart = _head_lanes(jnp.pad(wukv[:, :, 0, :], ((0, 0), (0, 0), (0, QKD - NOPE)))).reshape(KVL, H * HP)
    vpart = jnp.pad(wukv[:, :, 1, :], ((0, 0), (0, 0), (0, HP - VD))).reshape(KVL, H * HP)
    w['w_ukv'] = jnp.concatenate([kpart, vpart], axis=1)
    w['w_uq_t'] = w['w_uq'].T
    w['w_ukv_t'] = w['w_ukv'].T
    w['w_mkv'] = full['w_mkv'][l]
    w['w_mkv_t'] = w['w_mkv'].T
    for n in ('w_br_attn', 'w_br_conv', 'w_br_mem', 'w_out'):
        w[n] = full[n][l]
        w[n + '_t'] = w[n].T
    return w


def kernel(x, mem, positions, norm_g, w_in, b_gate, q_norm_g, w_uq, kv_norm_g, w_ukv, q_head_g, k_head_g, conv_w, conv_b, mem_norm_g, w_mkv, mem_q_g, mem_k_g, w_br_attn, w_br_conv, w_br_mem, w_out, loss_target, m_norm_g, m_w_in, m_b_gate, m_q_norm_g, m_w_uq, m_kv_norm_g, m_w_ukv, m_q_head_g, m_k_head_g, m_conv_w, m_conv_b, m_mem_norm_g, m_w_mkv, m_mem_q_g, m_mem_k_g, m_w_br_attn, m_w_br_conv, m_w_br_mem, m_w_out, v_norm_g, v_w_in, v_b_gate, v_q_norm_g, v_w_uq, v_kv_norm_g, v_w_ukv, v_q_head_g, v_k_head_g, v_conv_w, v_conv_b, v_mem_norm_g, v_w_mkv, v_mem_q_g, v_mem_k_g, v_w_br_attn, v_w_br_conv, v_w_br_mem, v_w_out):
    a = dict(zip(INPUTS, (x, mem, positions, norm_g, w_in, b_gate, q_norm_g, w_uq, kv_norm_g, w_ukv, q_head_g, k_head_g, conv_w, conv_b, mem_norm_g, w_mkv, mem_q_g, mem_k_g, w_br_attn, w_br_conv, w_br_mem, w_out, loss_target, m_norm_g, m_w_in, m_b_gate, m_q_norm_g, m_w_uq, m_kv_norm_g, m_w_ukv, m_q_head_g, m_k_head_g, m_conv_w, m_conv_b, m_mem_norm_g, m_w_mkv, m_mem_q_g, m_mem_k_g, m_w_br_attn, m_w_br_conv, m_w_br_mem, m_w_out, v_norm_g, v_w_in, v_b_gate, v_q_norm_g, v_w_uq, v_kv_norm_g, v_w_ukv, v_q_head_g, v_k_head_g, v_conv_w, v_conv_b, v_mem_norm_g, v_w_mkv, v_mem_q_g, v_mem_k_g, v_w_br_attn, v_w_br_conv, v_w_br_mem, v_w_out)))
    x = a['x'][0]
    mem = a['mem'][0]
    tgt = a['loss_target'][0]
    s = x.shape[0]
    t_el = 512
    t_br = 512
    tq_f, tk_f, tq_b, tk_b = 512, 1024, 512, 512

    gathered = _allgather([a[n].astype(BF16) for n in BIG_ORDER] + [a['conv_w']], "ag_weights")
    full = {n: [_unshard_blocks(g8[:, l], BIG[n][1]) for l in range(DEPTH)] for n, g8 in zip(BIG_ORDER, gathered)}
    conv_w = gathered[-1].transpose(1, 2, 0, 3).reshape(DEPTH, 3, CW)

    inv_freq = ROPE_BASE ** (-jnp.arange(0, RP, 2, dtype=F32) / RP)
    ang = a['positions'][0].astype(F32)[:, None] * inv_freq
    cos, sin = jnp.cos(ang), jnp.sin(ang)
    rc = _head_lanes(jnp.concatenate([jnp.ones((s, NOPE), F32), cos, cos], axis=1))
    rs = _head_lanes(jnp.concatenate([jnp.zeros((s, NOPE), F32), -sin, sin], axis=1))

    def small(n, l, width=None):
        v = a[n][l][None, :]
        return v if width is None else _head_lanes(v)

    saved = []
    layer_w = [_layer_weights(full, l) for l in range(DEPTH)]
    for l in range(DEPTH):
        w = layer_w[l]
        tag = ""
        h, ht = _rms_h(x, small('norm_g', l), 512, "rms_h" + tag)
        proj = _mm(h, w['w_main'], BF16, 2048, 512, "in_proj" + tag)
        proj_l = _mm(h, w['w_l'], BF16, 512, P_LW, "in_proj_lat" + tag)
        qng, kvng = small('q_norm_g', l), small('kv_norm_g', l)
        qhg, khg = small('q_head_g', l, HP), small('k_head_g', l, HP)
        q, k, v = _mla_prep(proj_l, rc, rs, qng, kvng, qhg, khg, w['w_uq'], w['w_ukv'], t_br, "mla_prep" + tag)
        o, lse = _flash_fwd(q, k, v, tq_f, tk_f, "flash_fwd" + tag)
        mk, mv = _mem_prep(mem, small('mem_norm_g', l), w['w_mkv'], small('mem_k_g', l), "mem_prep" + tag)
        cb, mqg = small('conv_b', l), small('mem_q_g', l)
        oa, oc, om = _branches(proj, o, mk, mv, conv_w[l], cb, mqg, t_br, "branches" + tag)
        x_out, aa, ac, am = _merge_fwd(x, proj, oa, oc, om, small('b_gate', l), w['w_br_attn'], w['w_br_conv'],
                                       w['w_br_mem'], w['w_out'], t_el, "merge" + tag)
        saved.append(dict(x=x, ht=ht, proj=proj, proj_l=proj_l, q=q, k=k, v=v, o=o, lse=lse, mk=mk, mv=mv,
                          oa=oa, oc=oc, om=om, aa=aa, ac=ac, am=am))
        x = x_out

    g, loss_parts = _loss_grad(x, tgt, 512, "loss")
    loss = lax.psum(jnp.sum(loss_parts), ("x", "y", "c"))

    gw = {n: [None] * DEPTH for n in WEIGHTS}
    for l in reversed(range(DEPTH)):
        w = layer_w[l]
        sv = saved[l]
        tag = ""
        tqb = min(tq_b, s)
        y, daa, dac, dam, dr, doa, doc, dom, dbg = _merge_bwd(
            g, sv['proj'], sv['aa'], sv['ac'], sv['am'], small('b_gate', l), w['w_out_t'], w['w_br_attn_t'],
            w['w_br_conv_t'], w['w_br_mem_t'], t_el, "merge_bwd" + tag)
        gw['b_gate'][l] = dbg[0]
        gw['w_out'][l] = _tn(y, g, 512, D, "dw_out" + tag)
        gw['w_br_attn'][l] = _tn(sv['oa'], daa, 512, D, "dw_attn" + tag)
        gw['w_br_conv'][l] = _tn(sv['oc'], dac, 512, D, "dw_conv" + tag)
        gw['w_br_mem'][l] = _tn(sv['om'], dam, 512, D, "dw_mem" + tag)
        cb, mqg = small('conv_b', l), small('mem_q_g', l)
        dcv, dmm, dga, do, dl, dcw, dcb, dmk, dmv, dmqg = _branches_bwd(
            sv['proj'], sv['o'], sv['mk'], sv['mv'], conv_w[l], cb, mqg, doa, doc, dom, t_br, "branches_bwd" + tag)
        gw['conv_w'][l], gw['conv_b'][l], gw['mem_q_g'][l] = dcw, dcb[0], dmqg[0]
        dwm, dmng, dmkg = _mem_prep_bwd(mem, small('mem_norm_g', l), w['w_mkv'], w['w_mkv_t'], small('mem_k_g', l),
                                        dmk, dmv, "mem_prep_bwd" + tag)
        gw['w_mkv'][l], gw['mem_norm_g'][l], gw['mem_k_g'][l] = dwm, dmng[0], dmkg[0]
        lse_r = sv['lse'].reshape(H, s // tqb, 1, tqb)
        dl_r = dl.reshape(H, s // tqb, 1, tqb)
        dq, dk, dv = _flash_bwd(sv['q'], sv['k'], sv['v'], do, lse_r, dl_r, tq_b, tk_b, "flash_bwd" + tag)
        qng, kvng = small('q_norm_g', l), small('kv_norm_g', l)
        qhg, khg = small('q_head_g', l, HP), small('k_head_g', l, HP)
        dlat, dwuq, dwukv, dqng, dkvng, dqhg, dkhg = _mla_prep_bwd(
            sv['proj_l'], rc, rs, qng, kvng, qhg, khg, w['w_uq'], w['w_ukv'], w['w_uq_t'], w['w_ukv_t'],
            dq, dk, dv, t_br, "mla_prep_bwd" + tag)
        gw['w_uq'][l] = _unpad_heads(dwuq)
        dwukv = dwukv.reshape(KVL, 2, H, HP)
        dwukv = jnp.stack([_head_dims(dwukv[:, 0])[..., :NOPE], dwukv[:, 1, :, :VD]], axis=2)
        gw['w_ukv'][l] = dwukv.reshape(KVL, H * 2 * NOPE)
        gw['q_norm_g'][l], gw['kv_norm_g'][l] = dqng[0], dkvng[0]
        gw['q_head_g'][l], gw['k_head_g'][l] = _head_dims(dqhg[0]), _head_dims(dkhg[0])
        dpieces = [dcv, dmm, dr, dga, dlat]
        dh = _mm_pieces(dpieces, w['w_in_t'], 512, 512, "d_h" + tag)
        gw['w_in'][l] = _unpermute_w_in([
            _mm_acc(sv['ht'], p, 1024, tn, f"dw_in_{i}" + tag)
            for i, (p, tn) in enumerate(zip(dpieces, (1024, 1024, 1536, 512, P_LW)))])
        g, dng = _rms_in_bwd(sv['x'], g, dh, small('norm_g', l), 512, "rms_bwd" + tag)
        gw['norm_g'][l] = dng[0]
    grad_x = g[None]

    sharded = BIG_ORDER + ['conv_w']
    axis_of = lambda n: 1 if n == 'conv_w' else BIG[n][1]
    send = [jnp.stack([_shard_blocks(gw[n][l], axis_of(n)) for l in range(DEPTH)], axis=2)
            .astype(F32 if n == 'conv_w' else BF16) for n in sharded]
    from_sibling = _pair_exchange(send, "rs_pair")
    my_c = lax.axis_index("c")
    chip_sums = []
    for n, t, got in zip(sharded, send, from_sibling):
        own = lax.dynamic_index_in_dim(t, my_c, axis=0, keepdims=False)
        flat = lambda u: u.reshape(-1, u.shape[-1])
        chip_sums.append(_pair_add(flat(own), flat(got), "rs_add_" + n).reshape(got.shape))
    parts_big = _chip_exchange(chip_sums, "rs_chips")
    small_flat = jnp.concatenate([jnp.stack(gw[n]).reshape(-1) for n in SMALL_ORDER])
    n_small = small_flat.shape[0]
    parts_small = _allgather([_to_rows(small_flat)], "ag_small_grads")[0]

    outs = [{} for _ in range(4)]
    for n, parts in zip(sharded, parts_big):
        loc = a[n].shape
        two_d = lambda t: t.reshape(loc[0] * loc[1], loc[2])
        res = _adamw(parts.reshape(4, loc[0] * loc[1], loc[2]), two_d(a[n]), two_d(a['m_' + n]),
                     two_d(a['v_' + n]), "adamw_" + n)
        for d, r in zip(outs, res):
            d[n] = r.reshape(loc)
    pks = lambda pre: _to_rows(jnp.concatenate([a[pre + n].reshape(-1) for n in SMALL_ORDER]))
    res_small = _adamw(parts_small, pks(''), pks('m_'), pks('v_'), "adamw_small")
    for d, rsm in zip(outs, res_small):
        flat = rsm.reshape(-1)[:n_small]
        off = 0
        for n in SMALL_ORDER:
            d[n] = flat[off:off + DEPTH * SMALL[n]].reshape(DEPTH, SMALL[n])
            off += DEPTH * SMALL[n]
    result = [loss, grad_x]
    for d in outs:
        result += [d[n] for n in WEIGHTS]
    return tuple(result)
```

```python
import functools

import jax
import jax.numpy as jnp
from jax import lax
from jax.experimental import pallas as pl
from jax.experimental.pallas import tpu as pltpu

F32, BF16 = jnp.float32, jnp.bfloat16

N_DEV = 8
DEPTH = 4
D = 1024
QL, KVL, RP = 384, 256, 32
H, NOPE, QKD, VD = 8, 64, 96, 64
HP = 128
CW, MW, AW = 512, 512, 512
HM, MHD = 4, 128
IN_WIDTH = 7328
EPS = 1e-6
ROPE_BASE = 10000.0
ATT_SCALE = QKD ** -0.5
MEM_SCALE = MHD ** -0.5
LOG2E = 1.4426950408889634
QSCALE = ATT_SCALE * LOG2E

ADAM_LR, ADAM_B1, ADAM_B2, ADAM_EPS, ADAM_WD, ADAM_STEP = 0.001, 0.9, 0.999, 1e-08, 0.01, 10

LANES = 128
VMEM_LIMIT = 56 * 1024 * 1024

P_CONV, P_MEM, P_R, P_GA, P_L = 0, 2048, 3072, 6144, 6656
P_MAIN = 6656
P_LW = 768
P_W = P_MAIN + P_LW
P_WPAD = 7680

WEIGHTS = ['norm_g', 'w_in', 'b_gate', 'q_norm_g', 'w_uq', 'kv_norm_g', 'w_ukv', 'q_head_g', 'k_head_g',
           'conv_w', 'conv_b', 'mem_norm_g', 'w_mkv', 'mem_q_g', 'mem_k_g', 'w_br_attn', 'w_br_conv',
           'w_br_mem', 'w_out']
INPUTS = ['x', 'mem', 'positions'] + WEIGHTS + ['loss_target'] + ['m_' + n for n in WEIGHTS] + ['v_' + n for n in WEIGHTS]

BIG = {'w_in': ((D, IN_WIDTH), 1), 'w_uq': ((QL, H * QKD), 1), 'w_ukv': ((KVL, H * 128), 1),
       'w_mkv': ((D, 2 * MW), 0), 'w_br_attn': ((AW, D), 1), 'w_br_conv': ((CW, D), 1),
       'w_br_mem': ((MW, D), 1), 'w_out': ((D, D), 0)}
BIG_ORDER = ['w_in', 'w_uq', 'w_ukv', 'w_mkv', 'w_br_attn', 'w_br_conv', 'w_br_mem', 'w_out']
CONVW_PAD = 256
SMALL = {'norm_g': D, 'b_gate': 3 * D, 'q_norm_g': QL, 'kv_norm_g': KVL, 'q_head_g': QKD, 'k_head_g': QKD,
         'conv_b': CW, 'mem_norm_g': D, 'mem_q_g': MHD, 'mem_k_g': MHD}
SMALL_ORDER = list(SMALL)
ROW_ALIGN = 1024


def _call(body, **kw):
    return pl.pallas_call(body, **kw)


def _cp(sem=None):
    return pltpu.CompilerParams(dimension_semantics=sem, vmem_limit_bytes=VMEM_LIMIT)


def _full(shape):
    n = len(shape)
    return pl.BlockSpec(shape, lambda *_: (0,) * n)


def _rms(x, g, n):
    rs = lax.rsqrt(jnp.sum(x * x, axis=-1, keepdims=True) * (1.0 / n) + EPS)
    xh = x * rs
    return xh * g, xh, rs


def _rms_bwd(dy, xh, rs, g, n):
    dxh = dy * g
    dx = rs * (dxh - xh * (jnp.sum(dxh * xh, axis=-1, keepdims=True) * (1.0 / n)))
    return dx, jnp.sum(dy * xh, axis=0, keepdims=True)


def _sigmoid(x):
    return 1.0 / (1.0 + jnp.exp(-x))


def _swap_rope(u):
    lane = lax.broadcasted_iota(jnp.int32, u.shape, 1)
    up = pltpu.roll(u, 16, 1)
    dn = pltpu.roll(u, 112, 1)
    return jnp.where((lane >= 64) & (lane < 80), dn, jnp.where((lane >= 80) & (lane < 96), up, 0.0))


def _rope(u, c, sn):
    return u * c + _swap_rope(u) * sn


def _rope_adj(d, c, sn):
    return d * c + _swap_rope(d * sn)


def _dot(a, b):
    return jnp.dot(a, b, preferred_element_type=F32)


def _dot_nt(a, b):
    return lax.dot_general(a, b, (((1,), (1,)), ((), ())), preferred_element_type=F32)


def _dot_tn(a, b):
    return lax.dot_general(a, b, (((0,), (0,)), ((), ())), preferred_element_type=F32)


def _mm(a, b, out_dtype, tm, tn, name):
    m, k = a.shape
    _, n = b.shape
    tm, tn = min(tm, m), min(tn, n)

    def body(a_ref, b_ref, o_ref):
        o_ref[...] = _dot(a_ref[...].astype(BF16), b_ref[...]).astype(o_ref.dtype)

    return _call(
        body, name=name, grid=(m // tm, n // tn),
        in_specs=[pl.BlockSpec((tm, k), lambda i, j: (i, 0)), pl.BlockSpec((k, tn), lambda i, j: (0, j))],
        out_specs=pl.BlockSpec((tm, tn), lambda i, j: (i, j)),
        out_shape=jax.ShapeDtypeStruct((m, n), out_dtype),
        compiler_params=_cp(("parallel", "arbitrary")),
    )(a, b)


def _tn(a, b, ts, tn, name):
    s, ka = a.shape
    _, n = b.shape
    ts, tn = min(ts, s), min(tn, n)

    def body(a_ref, b_ref, o_ref):
        @pl.when(pl.program_id(1) == 0)
        def _():
            o_ref[...] = jnp.zeros_like(o_ref)

        o_ref[...] += _dot_tn(a_ref[...].astype(BF16), b_ref[...].astype(BF16))

    return _call(
        body, name=name, grid=(n // tn, s // ts),
        in_specs=[pl.BlockSpec((ts, ka), lambda j, i: (i, 0)), pl.BlockSpec((ts, tn), lambda j, i: (i, j))],
        out_specs=pl.BlockSpec((ka, tn), lambda j, i: (0, j)),
        out_shape=jax.ShapeDtypeStruct((ka, n), F32),
        compiler_params=_cp(("parallel", "arbitrary")),
    )(a, b)


def _mm_pieces(pieces, b, tm, tn, name):
    n_p = len(pieces)
    m = pieces[0].shape[0]
    k, n = b.shape
    tm, tn = min(tm, m), min(tn, n)
    offs = [sum(p.shape[1] for p in pieces[:i]) for i in range(n_p)]

    def body(*refs):
        a_refs, b_ref, o_ref = refs[:n_p], refs[n_p], refs[n_p + 1]
        acc = None
        for a_ref, off in zip(a_refs, offs):
            d = _dot(a_ref[...], b_ref[off:off + a_ref.shape[1], :])
            acc = d if acc is None else acc + d
        o_ref[...] = acc

    return _call(
        body, name=name, grid=(m // tm, n // tn),
        in_specs=[pl.BlockSpec((tm, p.shape[1]), lambda i, j: (i, 0)) for p in pieces]
        + [pl.BlockSpec((k, tn), lambda i, j: (0, j))],
        out_specs=pl.BlockSpec((tm, tn), lambda i, j: (i, j)),
        out_shape=jax.ShapeDtypeStruct((m, n), F32),
        compiler_params=_cp(("parallel", "arbitrary")),
    )(*pieces, b)


def _mm_acc(a, b, tk, tn, name):
    m, k = a.shape
    _, n = b.shape
    tk, tn = min(tk, k), min(tn, n)

    def body(a_ref, b_ref, o_ref):
        @pl.when(pl.program_id(1) == 0)
        def _():
            o_ref[...] = jnp.zeros_like(o_ref)

        o_ref[...] += _dot(a_ref[...], b_ref[...])

    return _call(
        body, name=name, grid=(n // tn, k // tk),
        in_specs=[pl.BlockSpec((m, tk), lambda j, i: (0, i)), pl.BlockSpec((tk, tn), lambda j, i: (i, j))],
        out_specs=pl.BlockSpec((m, tn), lambda j, i: (0, j)),
        out_shape=jax.ShapeDtypeStruct((m, n), F32),
        compiler_params=_cp(("parallel", "arbitrary")),
    )(a, b)


def _rms_h(x, g, t, name):
    s = x.shape[0]
    t = min(t, s)

    def body(x_ref, g_ref, h_ref, ht_ref):
        h = _rms(x_ref[...], g_ref[...], D)[0]
        h_ref[...] = h.astype(BF16)
        ht_ref[...] = h.T.astype(BF16)

    return _call(
        body, name=name, grid=(s // t,),
        in_specs=[pl.BlockSpec((t, D), lambda i: (i, 0)), _full((1, D))],
        out_specs=[pl.BlockSpec((t, D), lambda i: (i, 0)), pl.BlockSpec((D, t), lambda i: (0, i))],
        out_shape=[jax.ShapeDtypeStruct((s, D), BF16), jax.ShapeDtypeStruct((D, s), BF16)],
        compiler_params=_cp(("parallel",)),
    )(x, g)


def _mla_heads(pl_blk, c, sn, qng, kvng, qhg, khg, wuq, wukv):
    ql = pl_blk[:, 0:QL].astype(F32)
    kvl = pl_blk[:, QL:QL + KVL].astype(F32)
    kpe = pl_blk[:, QL + KVL:P_LW].astype(F32)
    qn, qxh, qrs = _rms(ql, qng, QL)
    kvn, kvxh, kvrs = _rms(kvl, kvng, KVL)
    qn16, kvn16 = qn.astype(BF16), kvn.astype(BF16)
    qp = _dot(qn16, wuq)
    kvp = _dot(kvn16, wukv)
    return ql, kvl, kpe, qxh, qrs, kvxh, kvrs, qn16, kvn16, qp, kvp


def _mla_prep(proj_l, c, sn, qng, kvng, qhg, khg, wuq, wukv, t, name):
    s = proj_l.shape[0]
    t = min(t, s)

    def body(l_ref, c_ref, sn_ref, qng_ref, kvng_ref, qhg_ref, khg_ref, wuq_ref, wukv_ref, q_ref, k_ref, v_ref):
        cc, ss = c_ref[...], sn_ref[...]
        (_, _, kpe, _, _, _, _, _, _, qp, kvp) = _mla_heads(
            l_ref[...], cc, ss, qng_ref[...], kvng_ref[...], qhg_ref[...], khg_ref[...], wuq_ref[...], wukv_ref[...])
        lane = lax.broadcasted_iota(jnp.int32, cc.shape, 1)
        for h in range(H):
            u = qp[:, h * HP:(h + 1) * HP]
            q_ref[h] = (_rope(_rms(u, qhg_ref[...], QKD)[0], cc, ss) * QSCALE).astype(BF16)
            u = kvp[:, h * HP:(h + 1) * HP] + kpe
            k_ref[h] = _rope(_rms(u, khg_ref[...], QKD)[0], cc, ss).astype(BF16)
            v_ref[h] = jnp.where(lane == VD, 1.0, kvp[:, (H + h) * HP:(H + h + 1) * HP]).astype(BF16)

    hs = pl.BlockSpec((H, t, HP), lambda i: (0, i, 0))
    row = lambda w: pl.BlockSpec((t, w), lambda i: (i, 0))
    return _call(
        body, name=name, grid=(s // t,),
        in_specs=[row(P_LW), row(HP), row(HP), _full((1, QL)), _full((1, KVL)), _full((1, HP)), _full((1, HP)),
                  _full((QL, H * HP)), _full((KVL, 2 * H * HP))],
        out_specs=[hs, hs, hs],
        out_shape=[jax.ShapeDtypeStruct((H, s, HP), BF16)] * 3,
        compiler_params=_cp(("parallel",)),
    )(proj_l, c, sn, qng, kvng, qhg, khg, wuq, wukv)


def _flash_fwd(q, k, v, tq, tk, name):
    _, s, _ = q.shape
    tq, tk = min(tq, s), min(tk, s)
    nk, nc = s // tk, tk // LANES
    un = 8 if nk % 8 == 0 else 1

    def body(q_ref, k_ref, v_ref, o_ref, lse_ref, s_scr):
        qb = q_ref[0]

        def scores(jj, mx):
            for u in range(un):
                j = jj * un + u
                off = pl.multiple_of(j * tk, tk)
                sc = _dot_nt(qb, k_ref[0, pl.ds(off, tk), :])
                s_scr[j] = sc
                for cc in range(nc):
                    mx = jnp.maximum(mx, sc[:, cc * LANES:(cc + 1) * LANES])
            return mx

        mx = lax.fori_loop(0, nk // un, scores, jnp.full((tq, LANES), -jnp.inf, F32))
        m = jnp.max(mx, axis=-1, keepdims=True)
        mb = jnp.broadcast_to(m, (tq, LANES))

        def probs(jj, acc):
            for u in range(un):
                j = jj * un + u
                off = pl.multiple_of(j * tk, tk)
                sc = s_scr[j]
                ps = [jnp.exp2(sc[:, cc * LANES:(cc + 1) * LANES] - mb).astype(BF16) for cc in range(nc)]
                acc = acc + _dot(jnp.concatenate(ps, axis=-1), v_ref[0, pl.ds(off, tk), :])
            return acc

        acc = lax.fori_loop(0, nk // un, probs, jnp.zeros((tq, HP), F32))
        l = acc[:, VD:VD + 1]
        o_ref[0] = (acc[:, :VD] / l).astype(BF16)
        lse = jnp.broadcast_to(m + jnp.log(l) * LOG2E, (tq, LANES))
        lse_ref[0] = lse.T[0:1, :]

    return _call(
        body, name=name, grid=(H, s // tq),
        in_specs=[pl.BlockSpec((1, tq, HP), lambda h, i: (h, i, 0)),
                  pl.BlockSpec((1, s, HP), lambda h, i: (h, 0, 0), pipeline_mode=pl.Buffered(1)),
                  pl.BlockSpec((1, s, HP), lambda h, i: (h, 0, 0), pipeline_mode=pl.Buffered(1))],
        out_specs=[pl.BlockSpec((1, tq, VD), lambda h, i: (h, i, 0)), pl.BlockSpec((1, 1, tq), lambda h, i: (h, 0, i))],
        out_shape=[jax.ShapeDtypeStruct((H, s, VD), BF16), jax.ShapeDtypeStruct((H, 1, s), F32)],
        scratch_shapes=[pltpu.VMEM((nk, tq, tk), F32)],
        compiler_params=_cp(("parallel", "arbitrary")),
    )(q, k, v)


def _mem_prep(mem, mng, wmkv, mkg, name):
    m = mem.shape[0]

    def body(mem_ref, mng_ref, w_ref, mkg_ref, mk_ref, mv_ref):
        mn = _rms(mem_ref[...], mng_ref[...], D)[0].astype(BF16)
        mkv = _dot(mn, w_ref[...])
        for h in range(HM):
            mk_ref[h] = _rms(mkv[:, 2 * h * MHD:(2 * h + 1) * MHD], mkg_ref[...], MHD)[0].astype(BF16)
            mv_ref[h] = mkv[:, (2 * h + 1) * MHD:(2 * h + 2) * MHD].astype(BF16)

    return _call(
        body, name=name,
        in_specs=[_full((m, D)), _full((1, D)), _full((D, 2 * MW)), _full((1, MHD))],
        out_specs=[_full((HM, m, MHD))] * 2,
        out_shape=[jax.ShapeDtypeStruct((HM, m, MHD), BF16)] * 2,
        compiler_params=_cp(),
    )(mem, mng, wmkv, mkg)


def _conv_parts(cv, prev, nxt, first, last, cw, cb):
    t = cv.shape[0]
    c_b, c_c, c_u, g_c = (cv[:, i * CW:(i + 1) * CW].astype(F32) for i in range(4))
    z = c_c * c_u
    zp = jnp.where(first, 0.0, prev[15:16, CW:2 * CW].astype(F32) * prev[15:16, 2 * CW:3 * CW].astype(F32))
    zn = jnp.where(last, 0.0, nxt[0:1, CW:2 * CW].astype(F32) * nxt[0:1, 2 * CW:3 * CW].astype(F32))
    row = lax.broadcasted_iota(jnp.int32, (t, CW), 0)
    z_m1 = jnp.where(row == 0, zp, pltpu.roll(z, 1, 0))
    z_p1 = jnp.where(row == t - 1, zn, pltpu.roll(z, t - 1, 0))
    conv = cw[0:1] * z_m1 + cw[1:2] * z + cw[2:3] * z_p1 + cb
    return c_b, c_c, c_u, g_c, z, z_m1, z_p1, conv


def _mem_attn(qm, mqg, mk_ref, mv_ref):
    outs = []
    for h in range(HM):
        mq, mqxh, mqrs = _rms(qm[:, h * MHD:(h + 1) * MHD], mqg, MHD)
        mq16 = mq.astype(BF16)
        sc = _dot_nt(mq16, mk_ref[h]) * MEM_SCALE
        e = jnp.exp(sc - jnp.max(sc, axis=-1, keepdims=True))
        p = e / jnp.sum(e, axis=-1, keepdims=True)
        o = _dot(p.astype(BF16), mv_ref[h])
        outs.append((mq16, mqxh, mqrs, p, o))
    return outs


def _halo_specs(t, s, width):
    nb = s // 16
    prev = pl.BlockSpec((16, width), lambda i: (jnp.maximum(i * (t // 16) - 1, 0), 0))
    nxt = pl.BlockSpec((16, width), lambda i: (jnp.minimum((i + 1) * (t // 16), nb - 1), 0))
    return prev, nxt


def _branches(proj, o, mk, mv, cw, cb, mqg, t, name):
    s = proj.shape[0]
    t = min(t, s)
    nt = s // t

    def body(cv_ref, pv_ref, nx_ref, mm_ref, ga_ref, o_ref, mk_ref, mv_ref, cw_ref, cb_ref, mqg_ref,
             oa_ref, oc_ref, om_ref):
        i = pl.program_id(0)
        c_b, _, _, g_c, _, _, _, conv = _conv_parts(
            cv_ref[...], pv_ref[...], nx_ref[...], i == 0, i == nt - 1, cw_ref[...], cb_ref[...])
        oc_ref[...] = (c_b * conv * (g_c * _sigmoid(g_c))).astype(BF16)
        ga = ga_ref[...].astype(F32)
        ocat = jnp.concatenate([o_ref[h].astype(F32) for h in range(H)], axis=-1)
        oa_ref[...] = (ocat * (ga * _sigmoid(ga))).astype(BF16)
        mblk = mm_ref[...].astype(F32)
        gm = mblk[:, MW:]
        heads = _mem_attn(mblk[:, :MW], mqg_ref[...], mk_ref, mv_ref)
        om = jnp.concatenate([hh[4] for hh in heads], axis=-1)
        om_ref[...] = (om * (gm * _sigmoid(gm))).astype(BF16)

    pv, nx = _halo_specs(t, s, 4 * CW)
    out = pl.BlockSpec((t, 512), lambda i: (i, 0))
    return _call(
        body, name=name, grid=(nt,),
        in_specs=[pl.BlockSpec((t, 4 * CW), lambda i: (i, 0)), pv, nx,
                  pl.BlockSpec((t, 2 * MW), lambda i: (i, P_MEM // (2 * MW))),
                  pl.BlockSpec((t, AW), lambda i: (i, P_GA // AW)),
                  pl.BlockSpec((H, t, VD), lambda i: (0, i, 0)),
                  _full(mk.shape), _full(mv.shape), _full((3, CW)), _full((1, CW)), _full((1, MHD))],
        out_specs=[out, out, out],
        out_shape=[jax.ShapeDtypeStruct((s, 512), BF16)] * 3,
        compiler_params=_cp(("parallel",)),
    )(proj, proj, proj, proj, proj, o, mk, mv, cw, cb, mqg)


def _merge_fwd(x, proj, oa, oc, om, bg, wa, wc, wm, wo, t, name):
    s = x.shape[0]
    t = min(t, s)

    def body(x_ref, r_ref, oa_ref, oc_ref, om_ref, bg_ref, wa_ref, wc_ref, wm_ref, wo_ref,
             xo_ref, aa_ref, ac_ref, am_ref):
        y = jnp.zeros((t, D), F32)
        for j, (o_ref, w_ref, a_ref) in enumerate(((oa_ref, wa_ref, aa_ref), (oc_ref, wc_ref, ac_ref),
                                                   (om_ref, wm_ref, am_ref))):
            a = _dot(o_ref[...], w_ref[...])
            a_ref[...] = a.astype(BF16)
            rg = _sigmoid(r_ref[:, j * D:(j + 1) * D].astype(F32) + bg_ref[:, j * D:(j + 1) * D])
            y = y + rg * a
        xo_ref[...] = x_ref[...] + _dot(y.astype(BF16), wo_ref[...])

    row = lambda w: pl.BlockSpec((t, w), lambda i: (i, 0))
    return _call(
        body, name=name, grid=(s // t,),
        in_specs=[row(D), pl.BlockSpec((t, 3 * D), lambda i: (i, P_R // (3 * D))), row(512), row(512), row(512),
                  _full((1, 3 * D)), _full((512, D)), _full((512, D)), _full((512, D)), _full((D, D))],
        out_specs=[row(D), row(D), row(D), row(D)],
        out_shape=[jax.ShapeDtypeStruct((s, D), F32)] + [jax.ShapeDtypeStruct((s, D), BF16)] * 3,
        compiler_params=_cp(("parallel",)),
    )(x, proj, oa, oc, om, bg, wa, wc, wm, wo)


def _loss_grad(x, tgt, t, name):
    s = x.shape[0]
    t = min(t, s)

    def body(x_ref, t_ref, g_ref, l_ref):
        @pl.when(pl.program_id(0) == 0)
        def _():
            l_ref[...] = jnp.zeros_like(l_ref)

        e = x_ref[...] - t_ref[...]
        g_ref[...] = e * (1.0 / D)
        sq = e * e
        part = sq[:, 0:LANES]
        for j in range(1, D // LANES):
            part = part + sq[:, j * LANES:(j + 1) * LANES]
        acc = part[0:8]
        for j in range(1, t // 8):
            acc = acc + part[j * 8:(j + 1) * 8]
        l_ref[...] += acc * (0.5 / D)

    row = pl.BlockSpec((t, D), lambda i: (i, 0))
    return _call(
        body, name=name, grid=(s // t,),
        in_specs=[row, row], out_specs=[row, _full((8, LANES))],
        out_shape=[jax.ShapeDtypeStruct((s, D), F32), jax.ShapeDtypeStruct((8, LANES), F32)],
        compiler_params=_cp(("arbitrary",)),
    )(x, tgt)


def _merge_bwd(g, proj, aa, ac, am, bg, wot, wat, wct, wmt, t, name):
    s = g.shape[0]
    t = min(t, s)

    def body(g_ref, r_ref, aa_ref, ac_ref, am_ref, bg_ref, wot_ref, wat_ref, wct_ref, wmt_ref,
             y_ref, daa_ref, dac_ref, dam_ref, dr_ref, doa_ref, doc_ref, dom_ref, dbg_ref):
        @pl.when(pl.program_id(0) == 0)
        def _():
            dbg_ref[...] = jnp.zeros_like(dbg_ref)

        dy = _dot(g_ref[...].astype(BF16), wot_ref[...])
        y = jnp.zeros((t, D), F32)
        for j, (a_ref, wt_ref, da_ref, do_ref) in enumerate(((aa_ref, wat_ref, daa_ref, doa_ref),
                                                             (ac_ref, wct_ref, dac_ref, doc_ref),
                                                             (am_ref, wmt_ref, dam_ref, dom_ref))):
            a = a_ref[...].astype(F32)
            rg = _sigmoid(r_ref[:, j * D:(j + 1) * D].astype(F32) + bg_ref[:, j * D:(j + 1) * D])
            y = y + rg * a
            da = (dy * rg).astype(BF16)
            da_ref[...] = da
            dr = dy * a * rg * (1.0 - rg)
            dr_ref[:, j * D:(j + 1) * D] = dr.astype(BF16)
            dbg_ref[:, j * D:(j + 1) * D] += jnp.sum(dr, axis=0, keepdims=True)
            do_ref[...] = _dot(da, wt_ref[...]).astype(BF16)
        y_ref[...] = y.astype(BF16)

    row = lambda w: pl.BlockSpec((t, w), lambda i: (i, 0))
    return _call(
        body, name=name, grid=(s // t,),
        in_specs=[row(D), pl.BlockSpec((t, 3 * D), lambda i: (i, P_R // (3 * D))), row(D), row(D), row(D),
                  _full((1, 3 * D)), _full((D, D)), _full((D, 512)), _full((D, 512)), _full((D, 512))],
        out_specs=[row(D), row(D), row(D), row(D), row(3 * D), row(512), row(512), row(512), _full((1, 3 * D))],
        out_shape=[jax.ShapeDtypeStruct((s, D), BF16)] * 4 + [jax.ShapeDtypeStruct((s, 3 * D), BF16)]
        + [jax.ShapeDtypeStruct((s, 512), BF16)] * 3 + [jax.ShapeDtypeStruct((1, 3 * D), F32)],
        compiler_params=_cp(("arbitrary",)),
    )(g, proj, aa, ac, am, bg, wot, wat, wct, wmt)


def _branches_bwd(proj, o, mk, mv, cw, cb, mqg, doa, doc, dom, t, name):
    s = proj.shape[0]
    t = min(t, s)
    nt = s // t
    m = mk.shape[1]

    def body(cv_ref, pv_ref, nx_ref, mm_ref, ga_ref, o_ref, mk_ref, mv_ref, cw_ref, cb_ref, mqg_ref,
             doa_ref, doc_ref, dcp_ref, dcn_ref, dom_ref,
             dcv_ref, dmm_ref, dga_ref, do_ref, dl_ref, dcw_ref, dcb_ref, dmk_ref, dmv_ref, dmqg_ref):
        i = pl.program_id(0)

        @pl.when(i == 0)
        def _():
            for r in (dcw_ref, dcb_ref, dmk_ref, dmv_ref, dmqg_ref):
                r[...] = jnp.zeros_like(r)

        first, last = i == 0, i == nt - 1
        cw_, cb_ = cw_ref[...], cb_ref[...]
        pv, nx = pv_ref[...], nx_ref[...]
        c_b, c_c, c_u, g_c, z, z_m1, z_p1, conv = _conv_parts(cv_ref[...], pv, nx, first, last, cw_, cb_)
        sg = _sigmoid(g_c)
        silu = g_c * sg
        dsilu = sg * (1.0 + g_c * (1.0 - sg))
        doc_ = doc_ref[...].astype(F32)
        dconv = doc_ * c_b * silu
        gp = pv[15:16, 3 * CW:4 * CW].astype(F32)
        gn = nx[0:1, 3 * CW:4 * CW].astype(F32)
        dconv_p = jnp.where(first, 0.0, dcp_ref[15:16, :].astype(F32) * pv[15:16, 0:CW].astype(F32) * (gp * _sigmoid(gp)))
        dconv_n = jnp.where(last, 0.0, dcn_ref[0:1, :].astype(F32) * nx[0:1, 0:CW].astype(F32) * (gn * _sigmoid(gn)))
        row = lax.broadcasted_iota(jnp.int32, (t, CW), 0)
        d_m1 = jnp.where(row == 0, dconv_p, pltpu.roll(dconv, 1, 0))
        d_p1 = jnp.where(row == t - 1, dconv_n, pltpu.roll(dconv, t - 1, 0))
        dz = cw_[0:1] * d_p1 + cw_[1:2] * dconv + cw_[2:3] * d_m1
        dcv_ref[:, 0:CW] = (doc_ * conv * silu).astype(BF16)
        dcv_ref[:, CW:2 * CW] = (dz * c_u).astype(BF16)
        dcv_ref[:, 2 * CW:3 * CW] = (dz * c_c).astype(BF16)
        dcv_ref[:, 3 * CW:4 * CW] = (doc_ * c_b * conv * dsilu).astype(BF16)
        dcw_ref[0:1, :] += jnp.sum(dconv * z_m1, axis=0, keepdims=True)
        dcw_ref[1:2, :] += jnp.sum(dconv * z, axis=0, keepdims=True)
        dcw_ref[2:3, :] += jnp.sum(dconv * z_p1, axis=0, keepdims=True)
        dcb_ref[...] += jnp.sum(dconv, axis=0, keepdims=True)
        ga = ga_ref[...].astype(F32)
        sg = _sigmoid(ga)
        doa_ = doa_ref[...].astype(F32)
        ocat = jnp.concatenate([o_ref[h].astype(F32) for h in range(H)], axis=-1)
        dga_ref[...] = (doa_ * ocat * (sg * (1.0 + ga * (1.0 - sg)))).astype(BF16)
        dog = doa_ * (ga * sg)
        zeros = jnp.zeros((t, HP - VD), F32)
        lane = lax.broadcasted_iota(jnp.int32, (t, LANES), 1)
        dmat = jnp.zeros((t, LANES), F32)
        for h in range(H):
            dh = dog[:, h * VD:(h + 1) * VD]
            do_ref[h] = jnp.concatenate([dh, zeros], axis=-1).astype(BF16)
            dmat = jnp.where(lane == h, jnp.sum(dh * ocat[:, h * VD:(h + 1) * VD], axis=-1, keepdims=True), dmat)
        dlt = dmat.T
        for h in range(H):
            dl_ref[h] = dlt[h:h + 1, :]
        mblk = mm_ref[...].astype(F32)
        gm = mblk[:, MW:]
        sg = _sigmoid(gm)
        dom_ = dom_ref[...].astype(F32)
        heads = _mem_attn(mblk[:, :MW], mqg_ref[...], mk_ref, mv_ref)
        om = jnp.concatenate([hh[4] for hh in heads], axis=-1)
        dmm_ref[:, MW:] = (dom_ * om * (sg * (1.0 + gm * (1.0 - sg)))).astype(BF16)
        dmo = dom_ * (gm * sg)
        dmqg = jnp.zeros((1, MHD), F32)
        for h in range(HM):
            mq16, mqxh, mqrs, p, _ = heads[h]
            dmo_h = dmo[:, h * MHD:(h + 1) * MHD].astype(BF16)
            dp = _dot_nt(dmo_h, mv_ref[h])
            ds = (p * (dp - jnp.sum(dp * p, axis=-1, keepdims=True)) * MEM_SCALE).astype(BF16)
            dmq = _dot(ds, mk_ref[h])
            dmk_ref[h] += _dot_tn(ds, mq16)
            dmv_ref[h] += _dot_tn(p.astype(BF16), dmo_h)
            dq, dg = _rms_bwd(dmq, mqxh, mqrs, mqg_ref[...], MHD)
            dmm_ref[:, h * MHD:(h + 1) * MHD] = dq.astype(BF16)
            dmqg = dmqg + dg
        dmqg_ref[...] += dmqg

    pv, nx = _halo_specs(t, s, 4 * CW)
    dpv, dnx = _halo_specs(t, s, CW)
    row = lambda w: pl.BlockSpec((t, w), lambda i: (i, 0))
    hs = lambda w: pl.BlockSpec((H, t, w), lambda i: (0, i, 0))
    return _call(
        body, name=name, grid=(nt,),
        in_specs=[pl.BlockSpec((t, 4 * CW), lambda i: (i, 0)), pv, nx,
                  pl.BlockSpec((t, 2 * MW), lambda i: (i, P_MEM // (2 * MW))),
                  pl.BlockSpec((t, AW), lambda i: (i, P_GA // AW)),
                  hs(VD), _full(mk.shape), _full(mv.shape), _full((3, CW)), _full((1, CW)), _full((1, MHD)),
                  row(512), row(512), dpv, dnx, row(512)],
        out_specs=[row(4 * CW), row(2 * MW), row(AW), hs(HP), pl.BlockSpec((H, 1, t), lambda i: (0, 0, i)),
                   _full((3, CW)), _full((1, CW)),
                   _full((HM, m, MHD)), _full((HM, m, MHD)), _full((1, MHD))],
        out_shape=[jax.ShapeDtypeStruct((s, 4 * CW), BF16), jax.ShapeDtypeStruct((s, 2 * MW), BF16),
                   jax.ShapeDtypeStruct((s, AW), BF16), jax.ShapeDtypeStruct((H, s, HP), BF16),
                   jax.ShapeDtypeStruct((H, 1, s), F32), jax.ShapeDtypeStruct((3, CW), F32),
                   jax.ShapeDtypeStruct((1, CW), F32), jax.ShapeDtypeStruct((HM, m, MHD), F32),
                   jax.ShapeDtypeStruct((HM, m, MHD), F32), jax.ShapeDtypeStruct((1, MHD), F32)],
        compiler_params=_cp(("arbitrary",)),
    )(proj, proj, proj, proj, proj, o, mk, mv, cw, cb, mqg, doa, doc, doc, doc, dom)


def _flash_bwd(q, k, v, do, lse, dl, tq, tk, name):
    _, s, _ = q.shape
    tq, tk = min(tq, s), min(tk, s)
    nq, nkt, nc = s // tq, s // tk, tk // LANES
    unroll = 16 if nq % 16 == 0 else 1

    def body(q_ref, do_ref, lse_ref, dl_ref, k_ref, v_ref, dq_ref, dk_ref, dv_ref, dq_acc):
        j = pl.program_id(1)

        @pl.when(j == 0)
        def _():
            dq_acc[...] = jnp.zeros_like(dq_acc)

        kb, vb = k_ref[0], v_ref[0]

        def step(ii, carry):
            dkt, dvt = carry
            for u in range(unroll):
                i = ii * unroll + u
                off = pl.multiple_of(i * tq, tq)
                qb = q_ref[0, pl.ds(off, tq), :]
                dob = do_ref[0, pl.ds(off, tq), :]
                lse_b = jnp.broadcast_to(lse_ref[0, i], (LANES, tq)).T
                dl_b = jnp.broadcast_to(dl_ref[0, i], (LANES, tq)).T
                sc = _dot_nt(qb, kb)
                dp = _dot_nt(dob, vb)
                ps, dss = [], []
                for cc in range(nc):
                    p = jnp.exp2(sc[:, cc * LANES:(cc + 1) * LANES] - lse_b)
                    ps.append(p.astype(BF16))
                    dss.append((p * (dp[:, cc * LANES:(cc + 1) * LANES] - dl_b)).astype(BF16))
                p16, ds16 = jnp.concatenate(ps, axis=-1), jnp.concatenate(dss, axis=-1)
                dvt = dvt + _dot_tn(dob, p16)
                dkt = dkt + _dot_tn(qb, ds16)
                dq_acc[pl.ds(off, tq), :] += _dot(ds16, kb)
            return dkt, dvt

        dkt, dvt = lax.fori_loop(0, nq // unroll, step, (jnp.zeros((HP, tk), F32), jnp.zeros((HP, tk), F32)))
        dk_ref[0] = (dkt.T * (1.0 / LOG2E)).astype(BF16)
        dv_ref[0] = dvt.T.astype(BF16)

        @pl.when(j == nkt - 1)
        def _():
            dq_ref[0] = (dq_acc[...] * ATT_SCALE).astype(BF16)

    whole = pl.BlockSpec((1, s, HP), lambda h, j: (h, 0, 0))
    stat = pl.BlockSpec((1, nq, 1, tq), lambda h, j: (h, 0, 0, 0))
    tile = pl.BlockSpec((1, tk, HP), lambda h, j: (h, j, 0))
    return _call(
        body, name=name, grid=(H, nkt),
        in_specs=[whole, whole, stat, stat, tile, tile],
        out_specs=[whole, tile, tile],
        out_shape=[jax.ShapeDtypeStruct((H, s, HP), BF16)] * 3,
        scratch_shapes=[pltpu.VMEM((s, HP), F32)],
        compiler_params=_cp(("arbitrary", "arbitrary")),
    )(q, do, lse, dl, k, v)


def _mla_prep_bwd(proj_l, c, sn, qng, kvng, qhg, khg, wuq, wukv, wuqt, wukvt, dq, dk, dv, t, name):
    s = proj_l.shape[0]
    t = min(t, s)

    def body(l_ref, c_ref, sn_ref, qng_ref, kvng_ref, qhg_ref, khg_ref, wuq_ref, wukv_ref, wuqt_ref, wukvt_ref,
             dq_ref, dk_ref, dv_ref, dl_ref, dwuq_ref, dwukv_ref, dqng_ref, dkvng_ref, dqhg_ref, dkhg_ref):
        @pl.when(pl.program_id(0) == 0)
        def _():
            for r in (dwuq_ref, dwukv_ref, dqng_ref, dkvng_ref, dqhg_ref, dkhg_ref):
                r[...] = jnp.zeros_like(r)

        cc, ss = c_ref[...], sn_ref[...]
        qhg, khg = qhg_ref[...], khg_ref[...]
        (_, _, kpe, qxh, qrs, kvxh, kvrs, qn16, kvn16, qp, kvp) = _mla_heads(
            l_ref[...], cc, ss, qng_ref[...], kvng_ref[...], qhg, khg, wuq_ref[...], wukv_ref[...])
        lane = lax.broadcasted_iota(jnp.int32, (t, HP), 1)
        dqp, dkp, dvp = [], [], []
        dkpe = jnp.zeros((t, HP), F32)
        dqhg = jnp.zeros((1, HP), F32)
        dkhg = jnp.zeros((1, HP), F32)
        for h in range(H):
            _, xh, rs = _rms(qp[:, h * HP:(h + 1) * HP], qhg, QKD)
            du, dg = _rms_bwd(_rope_adj(dq_ref[h].astype(F32), cc, ss), xh, rs, qhg, QKD)
            dqp.append(du)
            dqhg = dqhg + dg
            _, xh, rs = _rms(kvp[:, h * HP:(h + 1) * HP] + kpe, khg, QKD)
            du, dg = _rms_bwd(_rope_adj(dk_ref[h].astype(F32), cc, ss), xh, rs, khg, QKD)
            dkp.append(du)
            dkhg = dkhg + dg
            dkpe = dkpe + jnp.where((lane >= NOPE) & (lane < QKD), du, 0.0)
            dvp.append(dv_ref[h].astype(F32))
        dqhg_ref[...] += dqhg
        dkhg_ref[...] += dkhg
        dqp16 = jnp.concatenate(dqp, axis=-1).astype(BF16)
        dkvp16 = jnp.concatenate(dkp + dvp, axis=-1).astype(BF16)
        dwuq_ref[...] += _dot_tn(qn16, dqp16)
        dwukv_ref[...] += _dot_tn(kvn16, dkvp16)
        dql, dg = _rms_bwd(_dot(dqp16, wuqt_ref[...]), qxh, qrs, qng_ref[...], QL)
        dqng_ref[...] += dg
        dkvl, dg = _rms_bwd(_dot(dkvp16, wukvt_ref[...]), kvxh, kvrs, kvng_ref[...], KVL)
        dkvng_ref[...] += dg
        dl_ref[:, 0:QL] = dql.astype(BF16)
        dl_ref[:, QL:QL + KVL] = dkvl.astype(BF16)
        dl_ref[:, QL + KVL:P_LW] = dkpe.astype(BF16)

    hs = pl.BlockSpec((H, t, HP), lambda i: (0, i, 0))
    row = lambda w: pl.BlockSpec((t, w), lambda i: (i, 0))
    return _call(
        body, name=name, grid=(s // t,),
        in_specs=[row(P_LW), row(HP), row(HP), _full((1, QL)), _full((1, KVL)), _full((1, HP)), _full((1, HP)),
                  _full((QL, H * HP)), _full((KVL, 2 * H * HP)), _full((H * HP, QL)), _full((2 * H * HP, KVL)),
                  hs, hs, hs],
        out_specs=[row(P_LW), _full((QL, H * HP)), _full((KVL, 2 * H * HP)), _full((1, QL)), _full((1, KVL)),
                   _full((1, HP)), _full((1, HP))],
        out_shape=[jax.ShapeDtypeStruct((s, P_LW), BF16), jax.ShapeDtypeStruct((QL, H * HP), F32),
                   jax.ShapeDtypeStruct((KVL, 2 * H * HP), F32), jax.ShapeDtypeStruct((1, QL), F32),
                   jax.ShapeDtypeStruct((1, KVL), F32), jax.ShapeDtypeStruct((1, HP), F32),
                   jax.ShapeDtypeStruct((1, HP), F32)],
        compiler_params=_cp(("arbitrary",)),
    )(proj_l, c, sn, qng, kvng, qhg, khg, wuq, wukv, wuqt, wukvt, dq, dk, dv)


def _mem_prep_bwd(mem, mng, wmkv, wmkvt, mkg, dmk, dmv, name):
    m = mem.shape[0]

    def body(mem_ref, mng_ref, w_ref, wt_ref, mkg_ref, dmk_ref, dmv_ref, dw_ref, dmng_ref, dmkg_ref):
        mn, xh, _ = _rms(mem_ref[...], mng_ref[...], D)
        mn16 = mn.astype(BF16)
        mkv = _dot(mn16, w_ref[...])
        parts = []
        dmkg = jnp.zeros((1, MHD), F32)
        for h in range(HM):
            _, kxh, krs = _rms(mkv[:, 2 * h * MHD:(2 * h + 1) * MHD], mkg_ref[...], MHD)
            du, dg = _rms_bwd(dmk_ref[h], kxh, krs, mkg_ref[...], MHD)
            dmkg = dmkg + dg
            parts += [du, dmv_ref[h]]
        dmkv = jnp.concatenate(parts, axis=-1).astype(BF16)
        dw_ref[...] = _dot_tn(mn16, dmkv)
        dmn = _dot(dmkv, wt_ref[...])
        dmng_ref[...] = jnp.sum(dmn * xh, axis=0, keepdims=True)
        dmkg_ref[...] = dmkg

    return _call(
        body, name=name,
        in_specs=[_full((m, D)), _full((1, D)), _full((D, 2 * MW)), _full((2 * MW, D)), _full((1, MHD)),
                  _full((HM, m, MHD)), _full((HM, m, MHD))],
        out_specs=[_full((D, 2 * MW)), _full((1, D)), _full((1, MHD))],
        out_shape=[jax.ShapeDtypeStruct((D, 2 * MW), F32), jax.ShapeDtypeStruct((1, D), F32),
                   jax.ShapeDtypeStruct((1, MHD), F32)],
        compiler_params=_cp(),
    )(mem, mng, wmkv, wmkvt, mkg, dmk, dmv)


def _rms_in_bwd(x, g_out, dh, ng, t, name):
    s = x.shape[0]
    t = min(t, s)

    def body(x_ref, go_ref, dh_ref, ng_ref, dx_ref, dng_ref):
        @pl.when(pl.program_id(0) == 0)
        def _():
            dng_ref[...] = jnp.zeros_like(dng_ref)

        _, xh, rs = _rms(x_ref[...], ng_ref[...], D)
        dx, dg = _rms_bwd(dh_ref[...], xh, rs, ng_ref[...], D)
        dx_ref[...] = go_ref[...] + dx
        dng_ref[...] += dg

    row = pl.BlockSpec((t, D), lambda i: (i, 0))
    return _call(
        body, name=name, grid=(s // t,),
        in_specs=[row, row, row, _full((1, D))], out_specs=[row, _full((1, D))],
        out_shape=[jax.ShapeDtypeStruct((s, D), F32), jax.ShapeDtypeStruct((1, D), F32)],
        compiler_params=_cp(("arbitrary",)),
    )(x, g_out, dh, ng)


def _allgather(arrs, name):
    n = len(arrs)

    def body(*refs):
        x_refs, out_refs = refs[:n], refs[n:2 * n]
        send_sems, recv_sems, local_sems = refs[2 * n:]
        x, y, c = lax.axis_index("x"), lax.axis_index("y"), lax.axis_index("c")
        me, sibling = (x, y, c), (x, y, 1 - c)
        chips = [(1 - x, y), (x, 1 - y), (1 - x, 1 - y)]

        def slot(a, px, py, pc):
            return out_refs[a].at[4 * px + 2 * py + pc]

        def copy(a, k, block, to, src=None):
            return pltpu.make_async_remote_copy(
                src_ref=slot(a, *block) if src is None else src, dst_ref=slot(a, *block),
                send_sem=send_sems.at[a, k], recv_sem=recv_sems.at[a, k],
                device_id=to, device_id_type=pl.DeviceIdType.MESH)

        mine = [pltpu.make_async_copy(x_refs[a], slot(a, *me), local_sems.at[a]) for a in range(n)]
        first, passed = [], []
        for a in range(n):
            mine[a].start()
            first.append(copy(a, 0, me, sibling, src=x_refs[a]))
            first += [copy(a, 1 + j, me, (*chip, c), src=x_refs[a]) for j, chip in enumerate(chips)]
        for cp in first:
            cp.start()
        for j, chip in enumerate(chips):
            for a in range(n):
                copy(a, 1 + j, (*chip, c), me).wait_recv()
                passed.append(copy(a, 4 + j, (*chip, c), sibling))
                passed[-1].start()
        for a in range(n):
            copy(a, 0, sibling, me).wait_recv()
        for j, chip in enumerate(chips):
            for a in range(n):
                copy(a, 4 + j, (*chip, 1 - c), me).wait_recv()
        for cp in first + passed:
            cp.wait_send()
        for cp in mine:
            cp.wait()

    any_spec = pl.BlockSpec(memory_space=pl.ANY)
    return _call(
        body, name=name,
        in_specs=[any_spec] * n, out_specs=[any_spec] * n,
        out_shape=[jax.ShapeDtypeStruct((N_DEV,) + a.shape, a.dtype) for a in arrs],
        scratch_shapes=[pltpu.SemaphoreType.DMA((n, 7)), pltpu.SemaphoreType.DMA((n, 7)),
                        pltpu.SemaphoreType.DMA((n,))],
    )(*arrs)


def _pair_exchange(arrs, name):
    n = len(arrs)

    def body(*refs):
        x_refs, out_refs = refs[:n], refs[n:2 * n]
        send_sems, recv_sems = refs[2 * n:]
        x, y, c = lax.axis_index("x"), lax.axis_index("y"), lax.axis_index("c")
        copies = [pltpu.make_async_remote_copy(
            src_ref=x_refs[a].at[1 - c], dst_ref=out_refs[a],
            send_sem=send_sems.at[a], recv_sem=recv_sems.at[a],
            device_id=(x, y, 1 - c), device_id_type=pl.DeviceIdType.MESH) for a in range(n)]
        for cp in copies:
            cp.start()
        for cp in copies:
            cp.wait_recv()
        for cp in copies:
            cp.wait_send()

    any_spec = pl.BlockSpec(memory_space=pl.ANY)
    return _call(
        body, name=name,
        in_specs=[any_spec] * n, out_specs=[any_spec] * n,
        out_shape=[jax.ShapeDtypeStruct(a.shape[1:], a.dtype) for a in arrs],
        scratch_shapes=[pltpu.SemaphoreType.DMA((n,)), pltpu.SemaphoreType.DMA((n,))],
    )(*arrs)


def _chip_exchange(arrs, name):
    n = len(arrs)

    def body(*refs):
        x_refs, out_refs = refs[:n], refs[n:2 * n]
        send_sems, recv_sems, local_sems = refs[2 * n:]
        x, y, c = lax.axis_index("x"), lax.axis_index("y"), lax.axis_index("c")
        me = 2 * x + y
        mine = [pltpu.make_async_copy(x_refs[a].at[me], out_refs[a].at[me], local_sems.at[a]) for a in range(n)]
        for cp in mine:
            cp.start()
        copies = []
        for k, (px, py) in enumerate([(1 - x, y), (x, 1 - y), (1 - x, 1 - y)]):
            for a in range(n):
                copies.append(pltpu.make_async_remote_copy(
                    src_ref=x_refs[a].at[2 * px + py], dst_ref=out_refs[a].at[me],
                    send_sem=send_sems.at[a, k], recv_sem=recv_sems.at[a, k],
                    device_id=(px, py, c), device_id_type=pl.DeviceIdType.MESH))
        for cp in copies:
            cp.start()
        for cp in copies:
            cp.wait_recv()
        for cp in copies:
            cp.wait_send()
        for cp in mine:
            cp.wait()

    any_spec = pl.BlockSpec(memory_space=pl.ANY)
    return _call(
        body, name=name,
        in_specs=[any_spec] * n, out_specs=[any_spec] * n,
        out_shape=[jax.ShapeDtypeStruct(a.shape, a.dtype) for a in arrs],
        scratch_shapes=[pltpu.SemaphoreType.DMA((n, 3)), pltpu.SemaphoreType.DMA((n, 3)),
                        pltpu.SemaphoreType.DMA((n,))],
    )(*arrs)


def _pair_add(a, b, name):
    r, c_ = a.shape
    cpad = -(-c_ // LANES) * LANES
    tr = r
    while tr * cpad * 4 > ADAMW_BLOCK_BYTES and tr % 32 == 0:
        tr //= 2

    def body(a_ref, b_ref, o_ref):
        o_ref[...] = (a_ref[...].astype(F32) + b_ref[...].astype(F32)).astype(o_ref.dtype)

    row = pl.BlockSpec((tr, c_), lambda i: (i, 0))
    return _call(
        body, name=name, grid=(r // tr,), in_specs=[row, row], out_specs=row,
        out_shape=jax.ShapeDtypeStruct((r, c_), a.dtype), compiler_params=_cp(("parallel",)),
    )(a, b)


ADAMW_BLOCK_BYTES = 4 * 1024 * 1024


def _adamw(parts, w, m, v, name):
    r, c_ = w.shape
    n_parts = parts.shape[0]
    cpad = -(-c_ // LANES) * LANES
    tr = r
    while N_DEV * tr * cpad * 4 > ADAMW_BLOCK_BYTES and tr % 16 == 0:
        tr //= 2
    c1 = 1.0 / (1.0 - ADAM_B1 ** ADAM_STEP)
    c2 = 1.0 / (1.0 - ADAM_B2 ** ADAM_STEP)

    def body(p_ref, w_ref, m_ref, v_ref, g_ref, d_ref, nm_ref, nv_ref):
        g = p_ref[0].astype(F32)
        for j in range(1, n_parts):
            g = g + p_ref[j].astype(F32)
        nm = ADAM_B1 * m_ref[...] + (1.0 - ADAM_B1) * g
        nv = ADAM_B2 * v_ref[...] + (1.0 - ADAM_B2) * (g * g)
        g_ref[...] = g
        nm_ref[...] = nm
        nv_ref[...] = nv
        d_ref[...] = -ADAM_LR * ((nm * c1) / (jnp.sqrt(nv * c2) + ADAM_EPS) + ADAM_WD * w_ref[...])

    row = pl.BlockSpec((tr, c_), lambda i: (i, 0))
    return _call(
        body, name=name, grid=(r // tr,),
        in_specs=[pl.BlockSpec((n_parts, tr, c_), lambda i: (0, i, 0)), row, row, row],
        out_specs=[row] * 4, out_shape=[jax.ShapeDtypeStruct((r, c_), F32)] * 4,
        compiler_params=_cp(("parallel",)),
    )(parts, w, m, v)


def _to_rows(flat, align=8):
    n = flat.shape[-1]
    rows = -(-n // (LANES * align)) * align
    return jnp.pad(flat, (0, rows * LANES - n)).reshape(rows, LANES)


def _shard_blocks(full, axis):
    r, c_ = full.shape
    if axis == 0:
        return full.reshape(4, 2, r // N_DEV, c_).transpose(1, 0, 2, 3)
    return full.reshape(r, 4, 2, c_ // N_DEV).transpose(2, 1, 0, 3)


def _unshard_blocks(blocks, axis):
    _, r, c_ = blocks.shape
    if axis == 0:
        return blocks.reshape(N_DEV * r, c_)
    return blocks.transpose(1, 0, 2).reshape(r, N_DEV * c_)


def _pad_heads(w, width):
    k = w.shape[0]
    return jnp.pad(w.reshape(k, H, width), ((0, 0), (0, 0), (0, HP - width))).reshape(k, H * HP)


def _unpad_heads(w, width):
    k = w.shape[0]
    return w.reshape(k, H, HP)[:, :, :width].reshape(k, H * width)


IN_SPLIT = {'q_lat': (0, 384), 'kv_lat': (384, 640), 'k_pe': (640, 672), 'c_b': (672, 1184), 'c_c': (1184, 1696),
            'c_u': (1696, 2208), 'q_mem': (2208, 2720), 'g_attn': (2720, 3232), 'g_conv': (3232, 3744),
            'g_mem': (3744, 4256), 'r': (4256, 7328)}
P_ORDER = ['c_b', 'c_c', 'c_u', 'g_conv', 'q_mem', 'g_mem', 'r', 'g_attn', 'q_lat', 'kv_lat']


def _permute_w_in(w):
    k = w.shape[0]
    cols = [w[:, IN_SPLIT[n][0]:IN_SPLIT[n][1]] for n in P_ORDER]
    kpe = w[:, IN_SPLIT['k_pe'][0]:IN_SPLIT['k_pe'][1]]
    cols += [jnp.zeros((k, NOPE), w.dtype), kpe, jnp.zeros((k, HP - QKD), w.dtype)]
    return jnp.concatenate(cols, axis=1)


def _unpermute_w_in(pieces):
    bounds, off = [], 0
    for p in pieces:
        bounds.append((off, off + p.shape[1]))
        off += p.shape[1]

    def cols(lo, hi):
        for p, (b0, b1) in zip(pieces, bounds):
            if b0 <= lo and hi <= b1:
                return p[:, lo - b0:hi - b0]
        raise ValueError("a column range straddles two pieces")

    off, pos = 0, {}
    for n in P_ORDER:
        wd = IN_SPLIT[n][1] - IN_SPLIT[n][0]
        pos[n] = (off, off + wd)
        off += wd
    pos['k_pe'] = (off + NOPE, off + QKD)
    order = sorted(IN_SPLIT, key=lambda n: IN_SPLIT[n][0])
    return jnp.concatenate([cols(*pos[n]) for n in order], axis=1)


def _layer_weights(full, l):
    w = {}
    w_in_p = _permute_w_in(full['w_in'][l])
    w['w_main'] = w_in_p[:, :P_MAIN]
    w['w_l'] = w_in_p[:, P_MAIN:]
    w['w_in_t'] = w_in_p.T
    w['w_uq'] = _pad_heads(full['w_uq'][l], QKD)
    wukv = full['w_ukv'][l].reshape(KVL, H, 2, NOPE)
    kpart = jnp.pad(wukv[:, :, 0, :], ((0, 0), (0, 0), (0, HP - NOPE))).reshape(KVL, H * HP)
    vpart = jnp.pad(wukv[:, :, 1, :], ((0, 0), (0, 0), (0, HP - VD))).reshape(KVL, H * HP)
    w['w_ukv'] = jnp.concatenate([kpart, vpart], axis=1)
    w['w_uq_t'] = w['w_uq'].T
    w['w_ukv_t'] = w['w_ukv'].T
    w['w_mkv'] = full['w_mkv'][l]
    w['w_mkv_t'] = w['w_mkv'].T
    for n in ('w_br_attn', 'w_br_conv', 'w_br_mem', 'w_out'):
        w[n] = full[n][l]
        w[n + '_t'] = w[n].T
    return w


def kernel(x, mem, positions, norm_g, w_in, b_gate, q_norm_g, w_uq, kv_norm_g, w_ukv, q_head_g, k_head_g, conv_w, conv_b, mem_norm_g, w_mkv, mem_q_g, mem_k_g, w_br_attn, w_br_conv, w_br_mem, w_out, loss_target, m_norm_g, m_w_in, m_b_gate, m_q_norm_g, m_w_uq, m_kv_norm_g, m_w_ukv, m_q_head_g, m_k_head_g, m_conv_w, m_conv_b, m_mem_norm_g, m_w_mkv, m_mem_q_g, m_mem_k_g, m_w_br_attn, m_w_br_conv, m_w_br_mem, m_w_out, v_norm_g, v_w_in, v_b_gate, v_q_norm_g, v_w_uq, v_kv_norm_g, v_w_ukv, v_q_head_g, v_k_head_g, v_conv_w, v_conv_b, v_mem_norm_g, v_w_mkv, v_mem_q_g, v_mem_k_g, v_w_br_attn, v_w_br_conv, v_w_br_mem, v_w_out):
    a = dict(zip(INPUTS, (x, mem, positions, norm_g, w_in, b_gate, q_norm_g, w_uq, kv_norm_g, w_ukv, q_head_g, k_head_g, conv_w, conv_b, mem_norm_g, w_mkv, mem_q_g, mem_k_g, w_br_attn, w_br_conv, w_br_mem, w_out, loss_target, m_norm_g, m_w_in, m_b_gate, m_q_norm_g, m_w_uq, m_kv_norm_g, m_w_ukv, m_q_head_g, m_k_head_g, m_conv_w, m_conv_b, m_mem_norm_g, m_w_mkv, m_mem_q_g, m_mem_k_g, m_w_br_attn, m_w_br_conv, m_w_br_mem, m_w_out, v_norm_g, v_w_in, v_b_gate, v_q_norm_g, v_w_uq, v_kv_norm_g, v_w_ukv, v_q_head_g, v_k_head_g, v_conv_w, v_conv_b, v_mem_norm_g, v_w_mkv, v_mem_q_g, v_mem_k_g, v_w_br_attn, v_w_br_conv, v_w_br_mem, v_w_out)))
    x = a['x'][0]
    mem = a['mem'][0]
    tgt = a['loss_target'][0]
    s = x.shape[0]
    t_el = 512
    t_br = 512
    tq_f, tk_f, tq_b, tk_b = 512, 1024, 512, 512

    gathered = _allgather([a[n].astype(BF16) for n in BIG_ORDER] + [a['conv_w']], "ag_weights")
    full = {n: [_unshard_blocks(g8[:, l], BIG[n][1]) for l in range(DEPTH)] for n, g8 in zip(BIG_ORDER, gathered)}
    conv_w = gathered[-1].transpose(1, 2, 0, 3).reshape(DEPTH, 3, CW)

    inv_freq = ROPE_BASE ** (-jnp.arange(0, RP, 2, dtype=F32) / RP)
    ang = a['positions'][0].astype(F32)[:, None] * inv_freq
    cos, sin = jnp.cos(ang), jnp.sin(ang)
    rc = jnp.concatenate([jnp.ones((s, NOPE), F32), cos, cos, jnp.ones((s, HP - QKD), F32)], axis=1)
    rs = jnp.concatenate([jnp.zeros((s, NOPE), F32), -sin, sin, jnp.zeros((s, HP - QKD), F32)], axis=1)

    def small(n, l, width=None):
        v = a[n][l][None, :]
        return v if width is None else jnp.pad(v, ((0, 0), (0, width - v.shape[1])))

    saved = []
    layer_w = [_layer_weights(full, l) for l in range(DEPTH)]
    for l in range(DEPTH):
        w = layer_w[l]
        tag = ""
        h, ht = _rms_h(x, small('norm_g', l), 512, "rms_h" + tag)
        proj = _mm(h, w['w_main'], BF16, 2048, 512, "in_proj" + tag)
        proj_l = _mm(h, w['w_l'], BF16, 512, P_LW, "in_proj_lat" + tag)
        qng, kvng = small('q_norm_g', l), small('kv_norm_g', l)
        qhg, khg = small('q_head_g', l, HP), small('k_head_g', l, HP)
        q, k, v = _mla_prep(proj_l, rc, rs, qng, kvng, qhg, khg, w['w_uq'], w['w_ukv'], t_br, "mla_prep" + tag)
        o, lse = _flash_fwd(q, k, v, tq_f, tk_f, "flash_fwd" + tag)
        mk, mv = _mem_prep(mem, small('mem_norm_g', l), w['w_mkv'], small('mem_k_g', l), "mem_prep" + tag)
        cb, mqg = small('conv_b', l), small('mem_q_g', l)
        oa, oc, om = _branches(proj, o, mk, mv, conv_w[l], cb, mqg, t_br, "branches" + tag)
        x_out, aa, ac, am = _merge_fwd(x, proj, oa, oc, om, small('b_gate', l), w['w_br_attn'], w['w_br_conv'],
                                       w['w_br_mem'], w['w_out'], t_el, "merge" + tag)
        saved.append(dict(x=x, ht=ht, proj=proj, proj_l=proj_l, q=q, k=k, v=v, o=o, lse=lse, mk=mk, mv=mv,
                          oa=oa, oc=oc, om=om, aa=aa, ac=ac, am=am))
        x = x_out

    g, loss_parts = _loss_grad(x, tgt, 512, "loss")
    loss = lax.psum(jnp.sum(loss_parts), ("x", "y", "c"))

    gw = {n: [None] * DEPTH for n in WEIGHTS}
    for l in reversed(range(DEPTH)):
        w = layer_w[l]
        sv = saved[l]
        tag = ""
        tqb = min(tq_b, s)
        y, daa, dac, dam, dr, doa, doc, dom, dbg = _merge_bwd(
            g, sv['proj'], sv['aa'], sv['ac'], sv['am'], small('b_gate', l), w['w_out_t'], w['w_br_attn_t'],
            w['w_br_conv_t'], w['w_br_mem_t'], t_el, "merge_bwd" + tag)
        gw['b_gate'][l] = dbg[0]
        gw['w_out'][l] = _tn(y, g, 512, D, "dw_out" + tag)
        gw['w_br_attn'][l] = _tn(sv['oa'], daa, 512, D, "dw_attn" + tag)
        gw['w_br_conv'][l] = _tn(sv['oc'], dac, 512, D, "dw_conv" + tag)
        gw['w_br_mem'][l] = _tn(sv['om'], dam, 512, D, "dw_mem" + tag)
        cb, mqg = small('conv_b', l), small('mem_q_g', l)
        dcv, dmm, dga, do, dl, dcw, dcb, dmk, dmv, dmqg = _branches_bwd(
            sv['proj'], sv['o'], sv['mk'], sv['mv'], conv_w[l], cb, mqg, doa, doc, dom, t_br, "branches_bwd" + tag)
        gw['conv_w'][l], gw['conv_b'][l], gw['mem_q_g'][l] = dcw, dcb[0], dmqg[0]
        dwm, dmng, dmkg = _mem_prep_bwd(mem, small('mem_norm_g', l), w['w_mkv'], w['w_mkv_t'], small('mem_k_g', l),
                                        dmk, dmv, "mem_prep_bwd" + tag)
        gw['w_mkv'][l], gw['mem_norm_g'][l], gw['mem_k_g'][l] = dwm, dmng[0], dmkg[0]
        lse_r = sv['lse'].reshape(H, s // tqb, 1, tqb)
        dl_r = dl.reshape(H, s // tqb, 1, tqb)
        dq, dk, dv = _flash_bwd(sv['q'], sv['k'], sv['v'], do, lse_r, dl_r, tq_b, tk_b, "flash_bwd" + tag)
        qng, kvng = small('q_norm_g', l), small('kv_norm_g', l)
        qhg, khg = small('q_head_g', l, HP), small('k_head_g', l, HP)
        dlat, dwuq, dwukv, dqng, dkvng, dqhg, dkhg = _mla_prep_bwd(
            sv['proj_l'], rc, rs, qng, kvng, qhg, khg, w['w_uq'], w['w_ukv'], w['w_uq_t'], w['w_ukv_t'],
            dq, dk, dv, t_br, "mla_prep_bwd" + tag)
        gw['w_uq'][l] = _unpad_heads(dwuq, QKD)
        dwukv = dwukv.reshape(KVL, 2, H, HP)[:, :, :, :NOPE]
        gw['w_ukv'][l] = dwukv.transpose(0, 2, 1, 3).reshape(KVL, H * 2 * NOPE)
        gw['q_norm_g'][l], gw['kv_norm_g'][l] = dqng[0], dkvng[0]
        gw['q_head_g'][l], gw['k_head_g'][l] = dqhg[0, :QKD], dkhg[0, :QKD]
        dpieces = [dcv, dmm, dr, dga, dlat]
        dh = _mm_pieces(dpieces, w['w_in_t'], 512, 512, "d_h" + tag)
        gw['w_in'][l] = _unpermute_w_in([
            _mm_acc(sv['ht'], p, 1024, tn, f"dw_in_{i}" + tag)
            for i, (p, tn) in enumerate(zip(dpieces, (1024, 1024, 1536, 512, P_LW)))])
        g, dng = _rms_in_bwd(sv['x'], g, dh, small('norm_g', l), 512, "rms_bwd" + tag)
        gw['norm_g'][l] = dng[0]
    grad_x = g[None]

    sharded = BIG_ORDER + ['conv_w']
    axis_of = lambda n: 1 if n == 'conv_w' else BIG[n][1]
    send = [jnp.stack([_shard_blocks(gw[n][l], axis_of(n)) for l in range(DEPTH)], axis=2)
            .astype(F32 if n == 'conv_w' else BF16) for n in sharded]
    from_sibling = _pair_exchange(send, "rs_pair")
    my_c = lax.axis_index("c")
    chip_sums = []
    for n, t, got in zip(sharded, send, from_sibling):
        own = lax.dynamic_index_in_dim(t, my_c, axis=0, keepdims=False)
        flat = lambda u: u.reshape(-1, u.shape[-1])
        chip_sums.append(_pair_add(flat(own), flat(got), "rs_add_" + n).reshape(got.shape))
    parts_big = _chip_exchange(chip_sums, "rs_chips")
    small_flat = jnp.concatenate([jnp.stack(gw[n]).reshape(-1) for n in SMALL_ORDER])
    n_small = small_flat.shape[0]
    parts_small = _allgather([_to_rows(small_flat)], "ag_small_grads")[0]

    outs = [{} for _ in range(4)]
    for n, parts in zip(sharded, parts_big):
        loc = a[n].shape
        two_d = lambda t: t.reshape(loc[0] * loc[1], loc[2])
        res = _adamw(parts.reshape(4, loc[0] * loc[1], loc[2]), two_d(a[n]), two_d(a['m_' + n]),
                     two_d(a['v_' + n]), "adamw_" + n)
        for d, r in zip(outs, res):
            d[n] = r.reshape(loc)
    pks = lambda pre: _to_rows(jnp.concatenate([a[pre + n].reshape(-1) for n in SMALL_ORDER]))
    res_small = _adamw(parts_small, pks(''), pks('m_'), pks('v_'), "adamw_small")
    for d, rsm in zip(outs, res_small):
        flat = rsm.reshape(-1)[:n_small]
        off = 0
        for n in SMALL_ORDER:
            d[n] = flat[off:off + DEPTH * SMALL[n]].reshape(DEPTH, SMALL[n])
            off += DEPTH * SMALL[n]
    result = [loss, grad_x]
    for d in outs:
        result += [d[n] for n in WEIGHTS]
    return tuple(result)
```

```python
import functools

import jax
import jax.numpy as jnp
from jax import lax
from jax.experimental import pallas as pl
from jax.experimental.pallas import tpu as pltpu

F32, BF16 = jnp.float32, jnp.bfloat16

N_DEV = 8
DEPTH = 4
D = 1024
QL, KVL, RP = 384, 256, 32
H, NOPE, QKD, VD = 8, 64, 96, 64
HP = 128
CW, MW, AW = 512, 512, 512
HM, MHD = 4, 128
IN_WIDTH = 7328
EPS = 1e-6
ROPE_BASE = 10000.0
ATT_SCALE = QKD ** -0.5
MEM_SCALE = MHD ** -0.5
LOG2E = 1.4426950408889634
QSCALE = ATT_SCALE * LOG2E

ADAM_LR, ADAM_B1, ADAM_B2, ADAM_EPS, ADAM_WD, ADAM_STEP = 0.001, 0.9, 0.999, 1e-08, 0.01, 10

LANES = 128
VMEM_LIMIT = 56 * 1024 * 1024

P_CONV, P_MEM, P_R, P_GA, P_L = 0, 2048, 3072, 6144, 6656
P_MAIN = 6656
P_LW = 768
P_W = P_MAIN + P_LW
P_WPAD = 7680

WEIGHTS = ['norm_g', 'w_in', 'b_gate', 'q_norm_g', 'w_uq', 'kv_norm_g', 'w_ukv', 'q_head_g', 'k_head_g',
           'conv_w', 'conv_b', 'mem_norm_g', 'w_mkv', 'mem_q_g', 'mem_k_g', 'w_br_attn', 'w_br_conv',
           'w_br_mem', 'w_out']
INPUTS = ['x', 'mem', 'positions'] + WEIGHTS + ['loss_target'] + ['m_' + n for n in WEIGHTS] + ['v_' + n for n in WEIGHTS]

BIG = {'w_in': ((D, IN_WIDTH), 1), 'w_uq': ((QL, H * QKD), 1), 'w_ukv': ((KVL, H * 128), 1),
       'w_mkv': ((D, 2 * MW), 0), 'w_br_attn': ((AW, D), 1), 'w_br_conv': ((CW, D), 1),
       'w_br_mem': ((MW, D), 1), 'w_out': ((D, D), 0)}
BIG_ORDER = ['w_in', 'w_uq', 'w_ukv', 'w_mkv', 'w_br_attn', 'w_br_conv', 'w_br_mem', 'w_out']
CONVW_PAD = 256
SMALL = {'norm_g': D, 'b_gate': 3 * D, 'q_norm_g': QL, 'kv_norm_g': KVL, 'q_head_g': QKD, 'k_head_g': QKD,
         'conv_b': CW, 'mem_norm_g': D, 'mem_q_g': MHD, 'mem_k_g': MHD}
SMALL_ORDER = list(SMALL)
ROW_ALIGN = 1024


def _call(body, **kw):
    return pl.pallas_call(body, **kw)


def _cp(sem=None):
    return pltpu.CompilerParams(dimension_semantics=sem, vmem_limit_bytes=VMEM_LIMIT)


def _full(shape):
    n = len(shape)
    return pl.BlockSpec(shape, lambda *_: (0,) * n)


def _rms(x, g, n):
    rs = lax.rsqrt(jnp.sum(x * x, axis=-1, keepdims=True) * (1.0 / n) + EPS)
    xh = x * rs
    return xh * g, xh, rs


def _rms_bwd(dy, xh, rs, g, n):
    dxh = dy * g
    dx = rs * (dxh - xh * (jnp.sum(dxh * xh, axis=-1, keepdims=True) * (1.0 / n)))
    return dx, jnp.sum(dy * xh, axis=0, keepdims=True)


def _sigmoid(x):
    return 1.0 / (1.0 + jnp.exp(-x))


def _swap_rope(u):
    lane = lax.broadcasted_iota(jnp.int32, u.shape, 1)
    up = pltpu.roll(u, 16, 1)
    dn = pltpu.roll(u, 112, 1)
    return jnp.where((lane >= 64) & (lane < 80), dn, jnp.where((lane >= 80) & (lane < 96), up, 0.0))


def _rope(u, c, sn):
    return u * c + _swap_rope(u) * sn


def _rope_adj(d, c, sn):
    return d * c + _swap_rope(d * sn)


def _dot(a, b):
    return jnp.dot(a, b, preferred_element_type=F32)


def _dot_nt(a, b):
    return lax.dot_general(a, b, (((1,), (1,)), ((), ())), preferred_element_type=F32)


def _dot_tn(a, b):
    return lax.dot_general(a, b, (((0,), (0,)), ((), ())), preferred_element_type=F32)


def _mm(a, b, out_dtype, tm, tn, name):
    m, k = a.shape
    _, n = b.shape
    tm, tn = min(tm, m), min(tn, n)

    def body(a_ref, b_ref, o_ref):
        o_ref[...] = _dot(a_ref[...].astype(BF16), b_ref[...]).astype(o_ref.dtype)

    return _call(
        body, name=name, grid=(m // tm, n // tn),
        in_specs=[pl.BlockSpec((tm, k), lambda i, j: (i, 0)), pl.BlockSpec((k, tn), lambda i, j: (0, j))],
        out_specs=pl.BlockSpec((tm, tn), lambda i, j: (i, j)),
        out_shape=jax.ShapeDtypeStruct((m, n), out_dtype),
        compiler_params=_cp(("parallel", "arbitrary")),
    )(a, b)


def _tn(a, b, ts, tn, name):
    s, ka = a.shape
    _, n = b.shape
    ts, tn = min(ts, s), min(tn, n)

    def body(a_ref, b_ref, o_ref):
        @pl.when(pl.program_id(1) == 0)
        def _():
            o_ref[...] = jnp.zeros_like(o_ref)

        o_ref[...] += _dot_tn(a_ref[...].astype(BF16), b_ref[...].astype(BF16))

    return _call(
        body, name=name, grid=(n // tn, s // ts),
        in_specs=[pl.BlockSpec((ts, ka), lambda j, i: (i, 0)), pl.BlockSpec((ts, tn), lambda j, i: (i, j))],
        out_specs=pl.BlockSpec((ka, tn), lambda j, i: (0, j)),
        out_shape=jax.ShapeDtypeStruct((ka, n), F32),
        compiler_params=_cp(("parallel", "arbitrary")),
    )(a, b)


def _mm_pieces(pieces, b, tm, tn, name):
    n_p = len(pieces)
    m = pieces[0].shape[0]
    k, n = b.shape
    tm, tn = min(tm, m), min(tn, n)
    offs = [sum(p.shape[1] for p in pieces[:i]) for i in range(n_p)]

    def body(*refs):
        a_refs, b_ref, o_ref = refs[:n_p], refs[n_p], refs[n_p + 1]
        acc = None
        for a_ref, off in zip(a_refs, offs):
            d = _dot(a_ref[...], b_ref[off:off + a_ref.shape[1], :])
            acc = d if acc is None else acc + d
        o_ref[...] = acc

    return _call(
        body, name=name, grid=(m // tm, n // tn),
        in_specs=[pl.BlockSpec((tm, p.shape[1]), lambda i, j: (i, 0)) for p in pieces]
        + [pl.BlockSpec((k, tn), lambda i, j: (0, j))],
        out_specs=pl.BlockSpec((tm, tn), lambda i, j: (i, j)),
        out_shape=jax.ShapeDtypeStruct((m, n), F32),
        compiler_params=_cp(("parallel", "arbitrary")),
    )(*pieces, b)


def _mm_acc(a, b, tk, tn, name):
    m, k = a.shape
    _, n = b.shape
    tk, tn = min(tk, k), min(tn, n)

    def body(a_ref, b_ref, o_ref):
        @pl.when(pl.program_id(1) == 0)
        def _():
            o_ref[...] = jnp.zeros_like(o_ref)

        o_ref[...] += _dot(a_ref[...], b_ref[...])

    return _call(
        body, name=name, grid=(n // tn, k // tk),
        in_specs=[pl.BlockSpec((m, tk), lambda j, i: (0, i)), pl.BlockSpec((tk, tn), lambda j, i: (i, j))],
        out_specs=pl.BlockSpec((m, tn), lambda j, i: (0, j)),
        out_shape=jax.ShapeDtypeStruct((m, n), F32),
        compiler_params=_cp(("parallel", "arbitrary")),
    )(a, b)


def _rms_h(x, g, t, name):
    s = x.shape[0]
    t = min(t, s)

    def body(x_ref, g_ref, h_ref, ht_ref):
        h = _rms(x_ref[...], g_ref[...], D)[0]
        h_ref[...] = h.astype(BF16)
        ht_ref[...] = h.T.astype(BF16)

    return _call(
        body, name=name, grid=(s // t,),
        in_specs=[pl.BlockSpec((t, D), lambda i: (i, 0)), _full((1, D))],
        out_specs=[pl.BlockSpec((t, D), lambda i: (i, 0)), pl.BlockSpec((D, t), lambda i: (0, i))],
        out_shape=[jax.ShapeDtypeStruct((s, D), BF16), jax.ShapeDtypeStruct((D, s), BF16)],
        compiler_params=_cp(("parallel",)),
    )(x, g)


def _mla_heads(pl_blk, c, sn, qng, kvng, qhg, khg, wuq, wukv):
    ql = pl_blk[:, 0:QL].astype(F32)
    kvl = pl_blk[:, QL:QL + KVL].astype(F32)
    kpe = pl_blk[:, QL + KVL:P_LW].astype(F32)
    qn, qxh, qrs = _rms(ql, qng, QL)
    kvn, kvxh, kvrs = _rms(kvl, kvng, KVL)
    qn16, kvn16 = qn.astype(BF16), kvn.astype(BF16)
    qp = _dot(qn16, wuq)
    kvp = _dot(kvn16, wukv)
    return ql, kvl, kpe, qxh, qrs, kvxh, kvrs, qn16, kvn16, qp, kvp


def _mla_prep(proj_l, c, sn, qng, kvng, qhg, khg, wuq, wukv, t, name):
    s = proj_l.shape[0]
    t = min(t, s)

    def body(l_ref, c_ref, sn_ref, qng_ref, kvng_ref, qhg_ref, khg_ref, wuq_ref, wukv_ref, q_ref, k_ref, v_ref):
        cc, ss = c_ref[...], sn_ref[...]
        (_, _, kpe, _, _, _, _, _, _, qp, kvp) = _mla_heads(
            l_ref[...], cc, ss, qng_ref[...], kvng_ref[...], qhg_ref[...], khg_ref[...], wuq_ref[...], wukv_ref[...])
        lane = lax.broadcasted_iota(jnp.int32, cc.shape, 1)
        for h in range(H):
            u = qp[:, h * HP:(h + 1) * HP]
            q_ref[h] = (_rope(_rms(u, qhg_ref[...], QKD)[0], cc, ss) * QSCALE).astype(BF16)
            u = kvp[:, h * HP:(h + 1) * HP] + kpe
            k_ref[h] = _rope(_rms(u, khg_ref[...], QKD)[0], cc, ss).astype(BF16)
            v_ref[h] = jnp.where(lane == VD, 1.0, kvp[:, (H + h) * HP:(H + h + 1) * HP]).astype(BF16)

    hs = pl.BlockSpec((H, t, HP), lambda i: (0, i, 0))
    row = lambda w: pl.BlockSpec((t, w), lambda i: (i, 0))
    return _call(
        body, name=name, grid=(s // t,),
        in_specs=[row(P_LW), row(HP), row(HP), _full((1, QL)), _full((1, KVL)), _full((1, HP)), _full((1, HP)),
                  _full((QL, H * HP)), _full((KVL, 2 * H * HP))],
        out_specs=[hs, hs, hs],
        out_shape=[jax.ShapeDtypeStruct((H, s, HP), BF16)] * 3,
        compiler_params=_cp(("parallel",)),
    )(proj_l, c, sn, qng, kvng, qhg, khg, wuq, wukv)


def _flash_fwd(q, k, v, tq, tk, name):
    _, s, _ = q.shape
    tq, tk = min(tq, s), min(tk, s)
    nk, nc = s // tk, tk // LANES
    un = 8 if nk % 8 == 0 else 1

    def body(q_ref, k_ref, v_ref, o_ref, lse_ref, s_scr):
        qb = q_ref[0]

        def scores(jj, mx):
            for u in range(un):
                j = jj * un + u
                off = pl.multiple_of(j * tk, tk)
                sc = _dot_nt(qb, k_ref[0, pl.ds(off, tk), :])
                s_scr[j] = sc
                for cc in range(nc):
                    mx = jnp.maximum(mx, sc[:, cc * LANES:(cc + 1) * LANES])
            return mx

        mx = lax.fori_loop(0, nk // un, scores, jnp.full((tq, LANES), -jnp.inf, F32))
        m = jnp.max(mx, axis=-1, keepdims=True)
        mb = jnp.broadcast_to(m, (tq, LANES))

        def probs(jj, acc):
            for u in range(un):
                j = jj * un + u
                off = pl.multiple_of(j * tk, tk)
                sc = s_scr[j]
                ps = [jnp.exp2(sc[:, cc * LANES:(cc + 1) * LANES] - mb).astype(BF16) for cc in range(nc)]
                acc = acc + _dot(jnp.concatenate(ps, axis=-1), v_ref[0, pl.ds(off, tk), :])
            return acc

        acc = lax.fori_loop(0, nk // un, probs, jnp.zeros((tq, HP), F32))
        l = acc[:, VD:VD + 1]
        o_ref[0] = (acc[:, :VD] / l).astype(BF16)
        lse = jnp.broadcast_to(m + jnp.log(l) * LOG2E, (tq, LANES))
        lse_ref[0] = lse.T[0:1, :]

    return _call(
        body, name=name, grid=(H, s // tq),
        in_specs=[pl.BlockSpec((1, tq, HP), lambda h, i: (h, i, 0)),
                  pl.BlockSpec((1, s, HP), lambda h, i: (h, 0, 0), pipeline_mode=pl.Buffered(1)),
                  pl.BlockSpec((1, s, HP), lambda h, i: (h, 0, 0), pipeline_mode=pl.Buffered(1))],
        out_specs=[pl.BlockSpec((1, tq, VD), lambda h, i: (h, i, 0)), pl.BlockSpec((1, 1, tq), lambda h, i: (h, 0, i))],
        out_shape=[jax.ShapeDtypeStruct((H, s, VD), BF16), jax.ShapeDtypeStruct((H, 1, s), F32)],
        scratch_shapes=[pltpu.VMEM((nk, tq, tk), F32)],
        compiler_params=_cp(("parallel", "arbitrary")),
    )(q, k, v)


def _mem_prep(mem, mng, wmkv, mkg, name):
    m = mem.shape[0]

    def body(mem_ref, mng_ref, w_ref, mkg_ref, mk_ref, mv_ref):
        mn = _rms(mem_ref[...], mng_ref[...], D)[0].astype(BF16)
        mkv = _dot(mn, w_ref[...])
        for h in range(HM):
            mk_ref[h] = _rms(mkv[:, 2 * h * MHD:(2 * h + 1) * MHD], mkg_ref[...], MHD)[0].astype(BF16)
            mv_ref[h] = mkv[:, (2 * h + 1) * MHD:(2 * h + 2) * MHD].astype(BF16)

    return _call(
        body, name=name,
        in_specs=[_full((m, D)), _full((1, D)), _full((D, 2 * MW)), _full((1, MHD))],
        out_specs=[_full((HM, m, MHD))] * 2,
        out_shape=[jax.ShapeDtypeStruct((HM, m, MHD), BF16)] * 2,
        compiler_params=_cp(),
    )(mem, mng, wmkv, mkg)


def _conv_parts(cv, prev, nxt, first, last, cw, cb):
    t = cv.shape[0]
    c_b, c_c, c_u, g_c = (cv[:, i * CW:(i + 1) * CW].astype(F32) for i in range(4))
    z = c_c * c_u
    zp = jnp.where(first, 0.0, prev[15:16, CW:2 * CW].astype(F32) * prev[15:16, 2 * CW:3 * CW].astype(F32))
    zn = jnp.where(last, 0.0, nxt[0:1, CW:2 * CW].astype(F32) * nxt[0:1, 2 * CW:3 * CW].astype(F32))
    row = lax.broadcasted_iota(jnp.int32, (t, CW), 0)
    z_m1 = jnp.where(row == 0, zp, pltpu.roll(z, 1, 0))
    z_p1 = jnp.where(row == t - 1, zn, pltpu.roll(z, t - 1, 0))
    conv = cw[0:1] * z_m1 + cw[1:2] * z + cw[2:3] * z_p1 + cb
    return c_b, c_c, c_u, g_c, z, z_m1, z_p1, conv


def _mem_attn(qm, mqg, mk_ref, mv_ref):
    outs = []
    for h in range(HM):
        mq, mqxh, mqrs = _rms(qm[:, h * MHD:(h + 1) * MHD], mqg, MHD)
        mq16 = mq.astype(BF16)
        sc = _dot_nt(mq16, mk_ref[h]) * MEM_SCALE
        e = jnp.exp(sc - jnp.max(sc, axis=-1, keepdims=True))
        p = e / jnp.sum(e, axis=-1, keepdims=True)
        o = _dot(p.astype(BF16), mv_ref[h])
        outs.append((mq16, mqxh, mqrs, p, o))
    return outs


def _halo_specs(t, s, width):
    nb = s // 16
    prev = pl.BlockSpec((16, width), lambda i: (jnp.maximum(i * (t // 16) - 1, 0), 0))
    nxt = pl.BlockSpec((16, width), lambda i: (jnp.minimum((i + 1) * (t // 16), nb - 1), 0))
    return prev, nxt


def _branches(proj, o, mk, mv, cw, cb, mqg, t, name):
    s = proj.shape[0]
    t = min(t, s)
    nt = s // t

    def body(cv_ref, pv_ref, nx_ref, mm_ref, ga_ref, o_ref, mk_ref, mv_ref, cw_ref, cb_ref, mqg_ref,
             oa_ref, oc_ref, om_ref):
        i = pl.program_id(0)
        c_b, _, _, g_c, _, _, _, conv = _conv_parts(
            cv_ref[...], pv_ref[...], nx_ref[...], i == 0, i == nt - 1, cw_ref[...], cb_ref[...])
        oc_ref[...] = (c_b * conv * (g_c * _sigmoid(g_c))).astype(BF16)
        ga = ga_ref[...].astype(F32)
        ocat = jnp.concatenate([o_ref[h].astype(F32) for h in range(H)], axis=-1)
        oa_ref[...] = (ocat * (ga * _sigmoid(ga))).astype(BF16)
        mblk = mm_ref[...].astype(F32)
        gm = mblk[:, MW:]
        heads = _mem_attn(mblk[:, :MW], mqg_ref[...], mk_ref, mv_ref)
        om = jnp.concatenate([hh[4] for hh in heads], axis=-1)
        om_ref[...] = (om * (gm * _sigmoid(gm))).astype(BF16)

    pv, nx = _halo_specs(t, s, 4 * CW)
    out = pl.BlockSpec((t, 512), lambda i: (i, 0))
    return _call(
        body, name=name, grid=(nt,),
        in_specs=[pl.BlockSpec((t, 4 * CW), lambda i: (i, 0)), pv, nx,
                  pl.BlockSpec((t, 2 * MW), lambda i: (i, P_MEM // (2 * MW))),
                  pl.BlockSpec((t, AW), lambda i: (i, P_GA // AW)),
                  pl.BlockSpec((H, t, VD), lambda i: (0, i, 0)),
                  _full(mk.shape), _full(mv.shape), _full((3, CW)), _full((1, CW)), _full((1, MHD))],
        out_specs=[out, out, out],
        out_shape=[jax.ShapeDtypeStruct((s, 512), BF16)] * 3,
        compiler_params=_cp(("parallel",)),
    )(proj, proj, proj, proj, proj, o, mk, mv, cw, cb, mqg)


def _merge_fwd(x, proj, oa, oc, om, bg, wa, wc, wm, wo, t, name):
    s = x.shape[0]
    t = min(t, s)

    def body(x_ref, r_ref, oa_ref, oc_ref, om_ref, bg_ref, wa_ref, wc_ref, wm_ref, wo_ref,
             xo_ref, aa_ref, ac_ref, am_ref):
        y = jnp.zeros((t, D), F32)
        for j, (o_ref, w_ref, a_ref) in enumerate(((oa_ref, wa_ref, aa_ref), (oc_ref, wc_ref, ac_ref),
                                                   (om_ref, wm_ref, am_ref))):
            a = _dot(o_ref[...], w_ref[...])
            a_ref[...] = a.astype(BF16)
            rg = _sigmoid(r_ref[:, j * D:(j + 1) * D].astype(F32) + bg_ref[:, j * D:(j + 1) * D])
            y = y + rg * a
        xo_ref[...] = x_ref[...] + _dot(y.astype(BF16), wo_ref[...])

    row = lambda w: pl.BlockSpec((t, w), lambda i: (i, 0))
    return _call(
        body, name=name, grid=(s // t,),
        in_specs=[row(D), pl.BlockSpec((t, 3 * D), lambda i: (i, P_R // (3 * D))), row(512), row(512), row(512),
                  _full((1, 3 * D)), _full((512, D)), _full((512, D)), _full((512, D)), _full((D, D))],
        out_specs=[row(D), row(D), row(D), row(D)],
        out_shape=[jax.ShapeDtypeStruct((s, D), F32)] + [jax.ShapeDtypeStruct((s, D), BF16)] * 3,
        compiler_params=_cp(("parallel",)),
    )(x, proj, oa, oc, om, bg, wa, wc, wm, wo)


def _loss_grad(x, tgt, t, name):
    s = x.shape[0]
    t = min(t, s)

    def body(x_ref, t_ref, g_ref, l_ref):
        @pl.when(pl.program_id(0) == 0)
        def _():
            l_ref[...] = jnp.zeros_like(l_ref)

        e = x_ref[...] - t_ref[...]
        g_ref[...] = e * (1.0 / D)
        sq = e * e
        part = sq[:, 0:LANES]
        for j in range(1, D // LANES):
            part = part + sq[:, j * LANES:(j + 1) * LANES]
        acc = part[0:8]
        for j in range(1, t // 8):
            acc = acc + part[j * 8:(j + 1) * 8]
        l_ref[...] += acc * (0.5 / D)

    row = pl.BlockSpec((t, D), lambda i: (i, 0))
    return _call(
        body, name=name, grid=(s // t,),
        in_specs=[row, row], out_specs=[row, _full((8, LANES))],
        out_shape=[jax.ShapeDtypeStruct((s, D), F32), jax.ShapeDtypeStruct((8, LANES), F32)],
        compiler_params=_cp(("arbitrary",)),
    )(x, tgt)


def _merge_bwd(g, proj, aa, ac, am, bg, wot, wat, wct, wmt, t, name):
    s = g.shape[0]
    t = min(t, s)

    def body(g_ref, r_ref, aa_ref, ac_ref, am_ref, bg_ref, wot_ref, wat_ref, wct_ref, wmt_ref,
             y_ref, daa_ref, dac_ref, dam_ref, dr_ref, doa_ref, doc_ref, dom_ref, dbg_ref):
        @pl.when(pl.program_id(0) == 0)
        def _():
            dbg_ref[...] = jnp.zeros_like(dbg_ref)

        dy = _dot(g_ref[...].astype(BF16), wot_ref[...])
        y = jnp.zeros((t, D), F32)
        for j, (a_ref, wt_ref, da_ref, do_ref) in enumerate(((aa_ref, wat_ref, daa_ref, doa_ref),
                                                             (ac_ref, wct_ref, dac_ref, doc_ref),
                                                             (am_ref, wmt_ref, dam_ref, dom_ref))):
            a = a_ref[...].astype(F32)
            rg = _sigmoid(r_ref[:, j * D:(j + 1) * D].astype(F32) + bg_ref[:, j * D:(j + 1) * D])
            y = y + rg * a
            da = (dy * rg).astype(BF16)
            da_ref[...] = da
            dr = dy * a * rg * (1.0 - rg)
            dr_ref[:, j * D:(j + 1) * D] = dr.astype(BF16)
            dbg_ref[:, j * D:(j + 1) * D] += jnp.sum(dr, axis=0, keepdims=True)
            do_ref[...] = _dot(da, wt_ref[...])
        y_ref[...] = y.astype(BF16)

    row = lambda w: pl.BlockSpec((t, w), lambda i: (i, 0))
    return _call(
        body, name=name, grid=(s // t,),
        in_specs=[row(D), pl.BlockSpec((t, 3 * D), lambda i: (i, P_R // (3 * D))), row(D), row(D), row(D),
                  _full((1, 3 * D)), _full((D, D)), _full((D, 512)), _full((D, 512)), _full((D, 512))],
        out_specs=[row(D), row(D), row(D), row(D), row(3 * D), row(512), row(512), row(512), _full((1, 3 * D))],
        out_shape=[jax.ShapeDtypeStruct((s, D), BF16)] * 4 + [jax.ShapeDtypeStruct((s, 3 * D), BF16)]
        + [jax.ShapeDtypeStruct((s, 512), F32)] * 3 + [jax.ShapeDtypeStruct((1, 3 * D), F32)],
        compiler_params=_cp(("arbitrary",)),
    )(g, proj, aa, ac, am, bg, wot, wat, wct, wmt)


def _branches_bwd(proj, o, mk, mv, cw, cb, mqg, doa, doc, dom, t, name):
    s = proj.shape[0]
    t = min(t, s)
    nt = s // t
    m = mk.shape[1]

    def body(cv_ref, pv_ref, nx_ref, mm_ref, ga_ref, o_ref, mk_ref, mv_ref, cw_ref, cb_ref, mqg_ref,
             doa_ref, doc_ref, dcp_ref, dcn_ref, dom_ref,
             dcv_ref, dmm_ref, dga_ref, do_ref, dl_ref, dcw_ref, dcb_ref, dmk_ref, dmv_ref, dmqg_ref):
        i = pl.program_id(0)

        @pl.when(i == 0)
        def _():
            for r in (dcw_ref, dcb_ref, dmk_ref, dmv_ref, dmqg_ref):
                r[...] = jnp.zeros_like(r)

        first, last = i == 0, i == nt - 1
        cw_, cb_ = cw_ref[...], cb_ref[...]
        pv, nx = pv_ref[...], nx_ref[...]
        c_b, c_c, c_u, g_c, z, z_m1, z_p1, conv = _conv_parts(cv_ref[...], pv, nx, first, last, cw_, cb_)
        sg = _sigmoid(g_c)
        silu = g_c * sg
        dsilu = sg * (1.0 + g_c * (1.0 - sg))
        doc_ = doc_ref[...]
        dconv = doc_ * c_b * silu
        gp = pv[15:16, 3 * CW:4 * CW].astype(F32)
        gn = nx[0:1, 3 * CW:4 * CW].astype(F32)
        dconv_p = jnp.where(first, 0.0, dcp_ref[15:16, :] * pv[15:16, 0:CW].astype(F32) * (gp * _sigmoid(gp)))
        dconv_n = jnp.where(last, 0.0, dcn_ref[0:1, :] * nx[0:1, 0:CW].astype(F32) * (gn * _sigmoid(gn)))
        row = lax.broadcasted_iota(jnp.int32, (t, CW), 0)
        d_m1 = jnp.where(row == 0, dconv_p, pltpu.roll(dconv, 1, 0))
        d_p1 = jnp.where(row == t - 1, dconv_n, pltpu.roll(dconv, t - 1, 0))
        dz = cw_[0:1] * d_p1 + cw_[1:2] * dconv + cw_[2:3] * d_m1
        dcv_ref[:, 0:CW] = (doc_ * conv * silu).astype(BF16)
        dcv_ref[:, CW:2 * CW] = (dz * c_u).astype(BF16)
        dcv_ref[:, 2 * CW:3 * CW] = (dz * c_c).astype(BF16)
        dcv_ref[:, 3 * CW:4 * CW] = (doc_ * c_b * conv * dsilu).astype(BF16)
        dcw_ref[0:1, :] += jnp.sum(dconv * z_m1, axis=0, keepdims=True)
        dcw_ref[1:2, :] += jnp.sum(dconv * z, axis=0, keepdims=True)
        dcw_ref[2:3, :] += jnp.sum(dconv * z_p1, axis=0, keepdims=True)
        dcb_ref[...] += jnp.sum(dconv, axis=0, keepdims=True)
        ga = ga_ref[...].astype(F32)
        sg = _sigmoid(ga)
        doa_ = doa_ref[...]
        ocat = jnp.concatenate([o_ref[h].astype(F32) for h in range(H)], axis=-1)
        dga_ref[...] = (doa_ * ocat * (sg * (1.0 + ga * (1.0 - sg)))).astype(BF16)
        dog = doa_ * (ga * sg)
        zeros = jnp.zeros((t, HP - VD), F32)
        lane = lax.broadcasted_iota(jnp.int32, (t, LANES), 1)
        dmat = jnp.zeros((t, LANES), F32)
        for h in range(H):
            dh = dog[:, h * VD:(h + 1) * VD]
            do_ref[h] = jnp.concatenate([dh, zeros], axis=-1).astype(BF16)
            dmat = jnp.where(lane == h, jnp.sum(dh * ocat[:, h * VD:(h + 1) * VD], axis=-1, keepdims=True), dmat)
        dlt = dmat.T
        for h in range(H):
            dl_ref[h] = dlt[h:h + 1, :]
        mblk = mm_ref[...].astype(F32)
        gm = mblk[:, MW:]
        sg = _sigmoid(gm)
        dom_ = dom_ref[...]
        heads = _mem_attn(mblk[:, :MW], mqg_ref[...], mk_ref, mv_ref)
        om = jnp.concatenate([hh[4] for hh in heads], axis=-1)
        dmm_ref[:, MW:] = (dom_ * om * (sg * (1.0 + gm * (1.0 - sg)))).astype(BF16)
        dmo = dom_ * (gm * sg)
        dmqg = jnp.zeros((1, MHD), F32)
        for h in range(HM):
            mq16, mqxh, mqrs, p, _ = heads[h]
            dmo_h = dmo[:, h * MHD:(h + 1) * MHD].astype(BF16)
            dp = _dot_nt(dmo_h, mv_ref[h])
            ds = (p * (dp - jnp.sum(dp * p, axis=-1, keepdims=True)) * MEM_SCALE).astype(BF16)
            dmq = _dot(ds, mk_ref[h])
            dmk_ref[h] += _dot_tn(ds, mq16)
            dmv_ref[h] += _dot_tn(p.astype(BF16), dmo_h)
            dq, dg = _rms_bwd(dmq, mqxh, mqrs, mqg_ref[...], MHD)
            dmm_ref[:, h * MHD:(h + 1) * MHD] = dq.astype(BF16)
            dmqg = dmqg + dg
        dmqg_ref[...] += dmqg

    pv, nx = _halo_specs(t, s, 4 * CW)
    dpv, dnx = _halo_specs(t, s, CW)
    row = lambda w: pl.BlockSpec((t, w), lambda i: (i, 0))
    hs = lambda w: pl.BlockSpec((H, t, w), lambda i: (0, i, 0))
    return _call(
        body, name=name, grid=(nt,),
        in_specs=[pl.BlockSpec((t, 4 * CW), lambda i: (i, 0)), pv, nx,
                  pl.BlockSpec((t, 2 * MW), lambda i: (i, P_MEM // (2 * MW))),
                  pl.BlockSpec((t, AW), lambda i: (i, P_GA // AW)),
                  hs(VD), _full(mk.shape), _full(mv.shape), _full((3, CW)), _full((1, CW)), _full((1, MHD)),
                  row(512), row(512), dpv, dnx, row(512)],
        out_specs=[row(4 * CW), row(2 * MW), row(AW), hs(HP), pl.BlockSpec((H, 1, t), lambda i: (0, 0, i)),
                   _full((3, CW)), _full((1, CW)),
                   _full((HM, m, MHD)), _full((HM, m, MHD)), _full((1, MHD))],
        out_shape=[jax.ShapeDtypeStruct((s, 4 * CW), BF16), jax.ShapeDtypeStruct((s, 2 * MW), BF16),
                   jax.ShapeDtypeStruct((s, AW), BF16), jax.ShapeDtypeStruct((H, s, HP), BF16),
                   jax.ShapeDtypeStruct((H, 1, s), F32), jax.ShapeDtypeStruct((3, CW), F32),
                   jax.ShapeDtypeStruct((1, CW), F32), jax.ShapeDtypeStruct((HM, m, MHD), F32),
                   jax.ShapeDtypeStruct((HM, m, MHD), F32), jax.ShapeDtypeStruct((1, MHD), F32)],
        compiler_params=_cp(("arbitrary",)),
    )(proj, proj, proj, proj, proj, o, mk, mv, cw, cb, mqg, doa, doc, doc, doc, dom)


def _flash_bwd(q, k, v, do, lse, dl, tq, tk, name):
    _, s, _ = q.shape
    tq, tk = min(tq, s), min(tk, s)
    nq, nkt, nc = s // tq, s // tk, tk // LANES
    unroll = 8 if nq % 8 == 0 else 1

    def body(q_ref, do_ref, lse_ref, dl_ref, k_ref, v_ref, dq_ref, dk_ref, dv_ref, dq_acc):
        j = pl.program_id(1)

        @pl.when(j == 0)
        def _():
            dq_acc[...] = jnp.zeros_like(dq_acc)

        kb, vb = k_ref[0], v_ref[0]

        def step(ii, carry):
            dkt, dvt = carry
            for u in range(unroll):
                i = ii * unroll + u
                off = pl.multiple_of(i * tq, tq)
                qb = q_ref[0, pl.ds(off, tq), :]
                dob = do_ref[0, pl.ds(off, tq), :]
                lse_b = jnp.broadcast_to(lse_ref[0, i], (LANES, tq)).T
                dl_b = jnp.broadcast_to(dl_ref[0, i], (LANES, tq)).T
                sc = _dot_nt(qb, kb)
                dp = _dot_nt(dob, vb)
                ps, dss = [], []
                for cc in range(nc):
                    p = jnp.exp2(sc[:, cc * LANES:(cc + 1) * LANES] - lse_b)
                    ps.append(p.astype(BF16))
                    dss.append((p * (dp[:, cc * LANES:(cc + 1) * LANES] - dl_b)).astype(BF16))
                p16, ds16 = jnp.concatenate(ps, axis=-1), jnp.concatenate(dss, axis=-1)
                dvt = dvt + _dot_tn(dob, p16)
                dkt = dkt + _dot_tn(qb, ds16)
                dq_acc[pl.ds(off, tq), :] += _dot(ds16, kb)
            return dkt, dvt

        dkt, dvt = lax.fori_loop(0, nq // unroll, step, (jnp.zeros((HP, tk), F32), jnp.zeros((HP, tk), F32)))
        dk_ref[0] = (dkt.T * (1.0 / LOG2E)).astype(BF16)
        dv_ref[0] = dvt.T.astype(BF16)

        @pl.when(j == nkt - 1)
        def _():
            dq_ref[0] = (dq_acc[...] * ATT_SCALE).astype(BF16)

    whole = pl.BlockSpec((1, s, HP), lambda h, j: (h, 0, 0))
    stat = pl.BlockSpec((1, nq, 1, tq), lambda h, j: (h, 0, 0, 0))
    tile = pl.BlockSpec((1, tk, HP), lambda h, j: (h, j, 0))
    return _call(
        body, name=name, grid=(H, nkt),
        in_specs=[whole, whole, stat, stat, tile, tile],
        out_specs=[whole, tile, tile],
        out_shape=[jax.ShapeDtypeStruct((H, s, HP), BF16)] * 3,
        scratch_shapes=[pltpu.VMEM((s, HP), F32)],
        compiler_params=_cp(("arbitrary", "arbitrary")),
    )(q, do, lse, dl, k, v)


def _mla_prep_bwd(proj_l, c, sn, qng, kvng, qhg, khg, wuq, wukv, wuqt, wukvt, dq, dk, dv, t, name):
    s = proj_l.shape[0]
    t = min(t, s)

    def body(l_ref, c_ref, sn_ref, qng_ref, kvng_ref, qhg_ref, khg_ref, wuq_ref, wukv_ref, wuqt_ref, wukvt_ref,
             dq_ref, dk_ref, dv_ref, dl_ref, dwuq_ref, dwukv_ref, dqng_ref, dkvng_ref, dqhg_ref, dkhg_ref):
        @pl.when(pl.program_id(0) == 0)
        def _():
            for r in (dwuq_ref, dwukv_ref, dqng_ref, dkvng_ref, dqhg_ref, dkhg_ref):
                r[...] = jnp.zeros_like(r)

        cc, ss = c_ref[...], sn_ref[...]
        qhg, khg = qhg_ref[...], khg_ref[...]
        (_, _, kpe, qxh, qrs, kvxh, kvrs, qn16, kvn16, qp, kvp) = _mla_heads(
            l_ref[...], cc, ss, qng_ref[...], kvng_ref[...], qhg, khg, wuq_ref[...], wukv_ref[...])
        lane = lax.broadcasted_iota(jnp.int32, (t, HP), 1)
        dqp, dkp, dvp = [], [], []
        dkpe = jnp.zeros((t, HP), F32)
        dqhg = jnp.zeros((1, HP), F32)
        dkhg = jnp.zeros((1, HP), F32)
        for h in range(H):
            _, xh, rs = _rms(qp[:, h * HP:(h + 1) * HP], qhg, QKD)
            du, dg = _rms_bwd(_rope_adj(dq_ref[h].astype(F32), cc, ss), xh, rs, qhg, QKD)
            dqp.append(du)
            dqhg = dqhg + dg
            _, xh, rs = _rms(kvp[:, h * HP:(h + 1) * HP] + kpe, khg, QKD)
            du, dg = _rms_bwd(_rope_adj(dk_ref[h].astype(F32), cc, ss), xh, rs, khg, QKD)
            dkp.append(du)
            dkhg = dkhg + dg
            dkpe = dkpe + jnp.where((lane >= NOPE) & (lane < QKD), du, 0.0)
            dvp.append(dv_ref[h].astype(F32))
        dqhg_ref[...] += dqhg
        dkhg_ref[...] += dkhg
        dqp16 = jnp.concatenate(dqp, axis=-1).astype(BF16)
        dkvp16 = jnp.concatenate(dkp + dvp, axis=-1).astype(BF16)
        dwuq_ref[...] += _dot_tn(qn16, dqp16)
        dwukv_ref[...] += _dot_tn(kvn16, dkvp16)
        dql, dg = _rms_bwd(_dot(dqp16, wuqt_ref[...]), qxh, qrs, qng_ref[...], QL)
        dqng_ref[...] += dg
        dkvl, dg = _rms_bwd(_dot(dkvp16, wukvt_ref[...]), kvxh, kvrs, kvng_ref[...], KVL)
        dkvng_ref[...] += dg
        dl_ref[:, 0:QL] = dql.astype(BF16)
        dl_ref[:, QL:QL + KVL] = dkvl.astype(BF16)
        dl_ref[:, QL + KVL:P_LW] = dkpe.astype(BF16)

    hs = pl.BlockSpec((H, t, HP), lambda i: (0, i, 0))
    row = lambda w: pl.BlockSpec((t, w), lambda i: (i, 0))
    return _call(
        body, name=name, grid=(s // t,),
        in_specs=[row(P_LW), row(HP), row(HP), _full((1, QL)), _full((1, KVL)), _full((1, HP)), _full((1, HP)),
                  _full((QL, H * HP)), _full((KVL, 2 * H * HP)), _full((H * HP, QL)), _full((2 * H * HP, KVL)),
                  hs, hs, hs],
        out_specs=[row(P_LW), _full((QL, H * HP)), _full((KVL, 2 * H * HP)), _full((1, QL)), _full((1, KVL)),
                   _full((1, HP)), _full((1, HP))],
        out_shape=[jax.ShapeDtypeStruct((s, P_LW), BF16), jax.ShapeDtypeStruct((QL, H * HP), F32),
                   jax.ShapeDtypeStruct((KVL, 2 * H * HP), F32), jax.ShapeDtypeStruct((1, QL), F32),
                   jax.ShapeDtypeStruct((1, KVL), F32), jax.ShapeDtypeStruct((1, HP), F32),
                   jax.ShapeDtypeStruct((1, HP), F32)],
        compiler_params=_cp(("arbitrary",)),
    )(proj_l, c, sn, qng, kvng, qhg, khg, wuq, wukv, wuqt, wukvt, dq, dk, dv)


def _mem_prep_bwd(mem, mng, wmkv, wmkvt, mkg, dmk, dmv, name):
    m = mem.shape[0]

    def body(mem_ref, mng_ref, w_ref, wt_ref, mkg_ref, dmk_ref, dmv_ref, dw_ref, dmng_ref, dmkg_ref):
        mn, xh, _ = _rms(mem_ref[...], mng_ref[...], D)
        mn16 = mn.astype(BF16)
        mkv = _dot(mn16, w_ref[...])
        parts = []
        dmkg = jnp.zeros((1, MHD), F32)
        for h in range(HM):
            _, kxh, krs = _rms(mkv[:, 2 * h * MHD:(2 * h + 1) * MHD], mkg_ref[...], MHD)
            du, dg = _rms_bwd(dmk_ref[h], kxh, krs, mkg_ref[...], MHD)
            dmkg = dmkg + dg
            parts += [du, dmv_ref[h]]
        dmkv = jnp.concatenate(parts, axis=-1).astype(BF16)
        dw_ref[...] = _dot_tn(mn16, dmkv)
        dmn = _dot(dmkv, wt_ref[...])
        dmng_ref[...] = jnp.sum(dmn * xh, axis=0, keepdims=True)
        dmkg_ref[...] = dmkg

    return _call(
        body, name=name,
        in_specs=[_full((m, D)), _full((1, D)), _full((D, 2 * MW)), _full((2 * MW, D)), _full((1, MHD)),
                  _full((HM, m, MHD)), _full((HM, m, MHD))],
        out_specs=[_full((D, 2 * MW)), _full((1, D)), _full((1, MHD))],
        out_shape=[jax.ShapeDtypeStruct((D, 2 * MW), F32), jax.ShapeDtypeStruct((1, D), F32),
                   jax.ShapeDtypeStruct((1, MHD), F32)],
        compiler_params=_cp(),
    )(mem, mng, wmkv, wmkvt, mkg, dmk, dmv)


def _rms_in_bwd(x, g_out, dh, ng, t, name):
    s = x.shape[0]
    t = min(t, s)

    def body(x_ref, go_ref, dh_ref, ng_ref, dx_ref, dng_ref):
        @pl.when(pl.program_id(0) == 0)
        def _():
            dng_ref[...] = jnp.zeros_like(dng_ref)

        _, xh, rs = _rms(x_ref[...], ng_ref[...], D)
        dx, dg = _rms_bwd(dh_ref[...], xh, rs, ng_ref[...], D)
        dx_ref[...] = go_ref[...] + dx
        dng_ref[...] += dg

    row = pl.BlockSpec((t, D), lambda i: (i, 0))
    return _call(
        body, name=name, grid=(s // t,),
        in_specs=[row, row, row, _full((1, D))], out_specs=[row, _full((1, D))],
        out_shape=[jax.ShapeDtypeStruct((s, D), F32), jax.ShapeDtypeStruct((1, D), F32)],
        compiler_params=_cp(("arbitrary",)),
    )(x, g_out, dh, ng)


def _allgather(arrs, name):
    n = len(arrs)

    def body(*refs):
        x_refs, out_refs = refs[:n], refs[n:2 * n]
        send_sems, recv_sems, local_sems = refs[2 * n:]
        x, y, c = lax.axis_index("x"), lax.axis_index("y"), lax.axis_index("c")
        me, sibling = (x, y, c), (x, y, 1 - c)
        chips = [(1 - x, y), (x, 1 - y), (1 - x, 1 - y)]

        def slot(a, px, py, pc):
            return out_refs[a].at[4 * px + 2 * py + pc]

        def copy(a, k, block, to, src=None):
            return pltpu.make_async_remote_copy(
                src_ref=slot(a, *block) if src is None else src, dst_ref=slot(a, *block),
                send_sem=send_sems.at[a, k], recv_sem=recv_sems.at[a, k],
                device_id=to, device_id_type=pl.DeviceIdType.MESH)

        mine = [pltpu.make_async_copy(x_refs[a], slot(a, *me), local_sems.at[a]) for a in range(n)]
        first, passed = [], []
        for a in range(n):
            mine[a].start()
            first.append(copy(a, 0, me, sibling, src=x_refs[a]))
            first += [copy(a, 1 + j, me, (*chip, c), src=x_refs[a]) for j, chip in enumerate(chips)]
        for cp in first:
            cp.start()
        for j, chip in enumerate(chips):
            for a in range(n):
                copy(a, 1 + j, (*chip, c), me).wait_recv()
                passed.append(copy(a, 4 + j, (*chip, c), sibling))
                passed[-1].start()
        for a in range(n):
            copy(a, 0, sibling, me).wait_recv()
        for j, chip in enumerate(chips):
            for a in range(n):
                copy(a, 4 + j, (*chip, 1 - c), me).wait_recv()
        for cp in first + passed:
            cp.wait_send()
        for cp in mine:
            cp.wait()

    any_spec = pl.BlockSpec(memory_space=pl.ANY)
    return _call(
        body, name=name,
        in_specs=[any_spec] * n, out_specs=[any_spec] * n,
        out_shape=[jax.ShapeDtypeStruct((N_DEV,) + a.shape, a.dtype) for a in arrs],
        scratch_shapes=[pltpu.SemaphoreType.DMA((n, 7)), pltpu.SemaphoreType.DMA((n, 7)),
                        pltpu.SemaphoreType.DMA((n,))],
    )(*arrs)


def _pair_exchange(arrs, name):
    n = len(arrs)

    def body(*refs):
        x_refs, out_refs = refs[:n], refs[n:2 * n]
        send_sems, recv_sems = refs[2 * n:]
        x, y, c = lax.axis_index("x"), lax.axis_index("y"), lax.axis_index("c")
        copies = [pltpu.make_async_remote_copy(
            src_ref=x_refs[a].at[1 - c], dst_ref=out_refs[a],
            send_sem=send_sems.at[a], recv_sem=recv_sems.at[a],
            device_id=(x, y, 1 - c), device_id_type=pl.DeviceIdType.MESH) for a in range(n)]
        for cp in copies:
            cp.start()
        for cp in copies:
            cp.wait_recv()
        for cp in copies:
            cp.wait_send()

    any_spec = pl.BlockSpec(memory_space=pl.ANY)
    return _call(
        body, name=name,
        in_specs=[any_spec] * n, out_specs=[any_spec] * n,
        out_shape=[jax.ShapeDtypeStruct(a.shape[1:], a.dtype) for a in arrs],
        scratch_shapes=[pltpu.SemaphoreType.DMA((n,)), pltpu.SemaphoreType.DMA((n,))],
    )(*arrs)


def _chip_exchange(arrs, name):
    n = len(arrs)

    def body(*refs):
        x_refs, out_refs = refs[:n], refs[n:2 * n]
        send_sems, recv_sems, local_sems = refs[2 * n:]
        x, y, c = lax.axis_index("x"), lax.axis_index("y"), lax.axis_index("c")
        me = 2 * x + y
        mine = [pltpu.make_async_copy(x_refs[a].at[me], out_refs[a].at[me], local_sems.at[a]) for a in range(n)]
        for cp in mine:
            cp.start()
        copies = []
        for k, (px, py) in enumerate([(1 - x, y), (x, 1 - y), (1 - x, 1 - y)]):
            for a in range(n):
                copies.append(pltpu.make_async_remote_copy(
                    src_ref=x_refs[a].at[2 * px + py], dst_ref=out_refs[a].at[me],
                    send_sem=send_sems.at[a, k], recv_sem=recv_sems.at[a, k],
                    device_id=(px, py, c), device_id_type=pl.DeviceIdType.MESH))
        for cp in copies:
            cp.start()
        for cp in copies:
            cp.wait_recv()
        for cp in copies:
            cp.wait_send()
        for cp in mine:
            cp.wait()

    any_spec = pl.BlockSpec(memory_space=pl.ANY)
    return _call(
        body, name=name,
        in_specs=[any_spec] * n, out_specs=[any_spec] * n,
        out_shape=[jax.ShapeDtypeStruct(a.shape, a.dtype) for a in arrs],
        scratch_shapes=[pltpu.SemaphoreType.DMA((n, 3)), pltpu.SemaphoreType.DMA((n, 3)),
                        pltpu.SemaphoreType.DMA((n,))],
    )(*arrs)


def _pair_add(a, b, name):
    r, c_ = a.shape
    cpad = -(-c_ // LANES) * LANES
    tr = r
    while tr * cpad * 4 > ADAMW_BLOCK_BYTES and tr % 32 == 0:
        tr //= 2

    def body(a_ref, b_ref, o_ref):
        o_ref[...] = (a_ref[...].astype(F32) + b_ref[...].astype(F32)).astype(o_ref.dtype)

    row = pl.BlockSpec((tr, c_), lambda i: (i, 0))
    return _call(
        body, name=name, grid=(r // tr,), in_specs=[row, row], out_specs=row,
        out_shape=jax.ShapeDtypeStruct((r, c_), a.dtype), compiler_params=_cp(("parallel",)),
    )(a, b)


ADAMW_BLOCK_BYTES = 4 * 1024 * 1024


def _adamw(parts, w, m, v, name):
    r, c_ = w.shape
    n_parts = parts.shape[0]
    cpad = -(-c_ // LANES) * LANES
    tr = r
    while N_DEV * tr * cpad * 4 > ADAMW_BLOCK_BYTES and tr % 16 == 0:
        tr //= 2
    c1 = 1.0 / (1.0 - ADAM_B1 ** ADAM_STEP)
    c2 = 1.0 / (1.0 - ADAM_B2 ** ADAM_STEP)

    def body(p_ref, w_ref, m_ref, v_ref, g_ref, d_ref, nm_ref, nv_ref):
        g = p_ref[0].astype(F32)
        for j in range(1, n_parts):
            g = g + p_ref[j].astype(F32)
        nm = ADAM_B1 * m_ref[...] + (1.0 - ADAM_B1) * g
        nv = ADAM_B2 * v_ref[...] + (1.0 - ADAM_B2) * (g * g)
        g_ref[...] = g
        nm_ref[...] = nm
        nv_ref[...] = nv
        d_ref[...] = -ADAM_LR * ((nm * c1) / (jnp.sqrt(nv * c2) + ADAM_EPS) + ADAM_WD * w_ref[...])

    row = pl.BlockSpec((tr, c_), lambda i: (i, 0))
    return _call(
        body, name=name, grid=(r // tr,),
        in_specs=[pl.BlockSpec((n_parts, tr, c_), lambda i: (0, i, 0)), row, row, row],
        out_specs=[row] * 4, out_shape=[jax.ShapeDtypeStruct((r, c_), F32)] * 4,
        compiler_params=_cp(("parallel",)),
    )(parts, w, m, v)


def _to_rows(flat, align=8):
    n = flat.shape[-1]
    rows = -(-n // (LANES * align)) * align
    return jnp.pad(flat, (0, rows * LANES - n)).reshape(rows, LANES)


def _shard_blocks(full, axis):
    r, c_ = full.shape
    if axis == 0:
        return full.reshape(4, 2, r // N_DEV, c_).transpose(1, 0, 2, 3)
    return full.reshape(r, 4, 2, c_ // N_DEV).transpose(2, 1, 0, 3)


def _unshard_blocks(blocks, axis):
    _, r, c_ = blocks.shape
    if axis == 0:
        return blocks.reshape(N_DEV * r, c_)
    return blocks.transpose(1, 0, 2).reshape(r, N_DEV * c_)


def _pad_heads(w, width):
    k = w.shape[0]
    return jnp.pad(w.reshape(k, H, width), ((0, 0), (0, 0), (0, HP - width))).reshape(k, H * HP)


def _unpad_heads(w, width):
    k = w.shape[0]
    return w.reshape(k, H, HP)[:, :, :width].reshape(k, H * width)


IN_SPLIT = {'q_lat': (0, 384), 'kv_lat': (384, 640), 'k_pe': (640, 672), 'c_b': (672, 1184), 'c_c': (1184, 1696),
            'c_u': (1696, 2208), 'q_mem': (2208, 2720), 'g_attn': (2720, 3232), 'g_conv': (3232, 3744),
            'g_mem': (3744, 4256), 'r': (4256, 7328)}
P_ORDER = ['c_b', 'c_c', 'c_u', 'g_conv', 'q_mem', 'g_mem', 'r', 'g_attn', 'q_lat', 'kv_lat']


def _permute_w_in(w):
    k = w.shape[0]
    cols = [w[:, IN_SPLIT[n][0]:IN_SPLIT[n][1]] for n in P_ORDER]
    kpe = w[:, IN_SPLIT['k_pe'][0]:IN_SPLIT['k_pe'][1]]
    cols += [jnp.zeros((k, NOPE), w.dtype), kpe, jnp.zeros((k, HP - QKD), w.dtype)]
    return jnp.concatenate(cols, axis=1)


def _unpermute_w_in(pieces):
    bounds, off = [], 0
    for p in pieces:
        bounds.append((off, off + p.shape[1]))
        off += p.shape[1]

    def cols(lo, hi):
        for p, (b0, b1) in zip(pieces, bounds):
            if b0 <= lo and hi <= b1:
                return p[:, lo - b0:hi - b0]
        raise ValueError("a column range straddles two pieces")

    off, pos = 0, {}
    for n in P_ORDER:
        wd = IN_SPLIT[n][1] - IN_SPLIT[n][0]
        pos[n] = (off, off + wd)
        off += wd
    pos['k_pe'] = (off + NOPE, off + QKD)
    order = sorted(IN_SPLIT, key=lambda n: IN_SPLIT[n][0])
    return jnp.concatenate([cols(*pos[n]) for n in order], axis=1)


def _layer_weights(full, l):
    w = {}
    w_in_p = _permute_w_in(full['w_in'][l])
    w['w_main'] = w_in_p[:, :P_MAIN]
    w['w_l'] = w_in_p[:, P_MAIN:]
    w['w_in_t'] = w_in_p.T
    w['w_uq'] = _pad_heads(full['w_uq'][l], QKD)
    wukv = full['w_ukv'][l].reshape(KVL, H, 2, NOPE)
    kpart = jnp.pad(wukv[:, :, 0, :], ((0, 0), (0, 0), (0, HP - NOPE))).reshape(KVL, H * HP)
    vpart = jnp.pad(wukv[:, :, 1, :], ((0, 0), (0, 0), (0, HP - VD))).reshape(KVL, H * HP)
    w['w_ukv'] = jnp.concatenate([kpart, vpart], axis=1)
    w['w_uq_t'] = w['w_uq'].T
    w['w_ukv_t'] = w['w_ukv'].T
    w['w_mkv'] = full['w_mkv'][l]
    w['w_mkv_t'] = w['w_mkv'].T
    for n in ('w_br_attn', 'w_br_conv', 'w_br_mem', 'w_out'):
        w[n] = full[n][l]
        w[n + '_t'] = w[n].T
    return w


def kernel(x, mem, positions, norm_g, w_in, b_gate, q_norm_g, w_uq, kv_norm_g, w_ukv, q_head_g, k_head_g, conv_w, conv_b, mem_norm_g, w_mkv, mem_q_g, mem_k_g, w_br_attn, w_br_conv, w_br_mem, w_out, loss_target, m_norm_g, m_w_in, m_b_gate, m_q_norm_g, m_w_uq, m_kv_norm_g, m_w_ukv, m_q_head_g, m_k_head_g, m_conv_w, m_conv_b, m_mem_norm_g, m_w_mkv, m_mem_q_g, m_mem_k_g, m_w_br_attn, m_w_br_conv, m_w_br_mem, m_w_out, v_norm_g, v_w_in, v_b_gate, v_q_norm_g, v_w_uq, v_kv_norm_g, v_w_ukv, v_q_head_g, v_k_head_g, v_conv_w, v_conv_b, v_mem_norm_g, v_w_mkv, v_mem_q_g, v_mem_k_g, v_w_br_attn, v_w_br_conv, v_w_br_mem, v_w_out):
    a = dict(zip(INPUTS, (x, mem, positions, norm_g, w_in, b_gate, q_norm_g, w_uq, kv_norm_g, w_ukv, q_head_g, k_head_g, conv_w, conv_b, mem_norm_g, w_mkv, mem_q_g, mem_k_g, w_br_attn, w_br_conv, w_br_mem, w_out, loss_target, m_norm_g, m_w_in, m_b_gate, m_q_norm_g, m_w_uq, m_kv_norm_g, m_w_ukv, m_q_head_g, m_k_head_g, m_conv_w, m_conv_b, m_mem_norm_g, m_w_mkv, m_mem_q_g, m_mem_k_g, m_w_br_attn, m_w_br_conv, m_w_br_mem, m_w_out, v_norm_g, v_w_in, v_b_gate, v_q_norm_g, v_w_uq, v_kv_norm_g, v_w_ukv, v_q_head_g, v_k_head_g, v_conv_w, v_conv_b, v_mem_norm_g, v_w_mkv, v_mem_q_g, v_mem_k_g, v_w_br_attn, v_w_br_conv, v_w_br_mem, v_w_out)))
    x = a['x'][0]
    mem = a['mem'][0]
    tgt = a['loss_target'][0]
    s = x.shape[0]
    t_el = 512
    t_br = 512
    tq_f, tk_f, tq_b, tk_b = 512, 1024, 512, 1024

    gathered = _allgather([a[n].astype(BF16) for n in BIG_ORDER] + [a['conv_w']], "ag_weights")
    full = {n: [_unshard_blocks(g8[:, l], BIG[n][1]) for l in range(DEPTH)] for n, g8 in zip(BIG_ORDER, gathered)}
    conv_w = gathered[-1].transpose(1, 2, 0, 3).reshape(DEPTH, 3, CW)

    inv_freq = ROPE_BASE ** (-jnp.arange(0, RP, 2, dtype=F32) / RP)
    ang = a['positions'][0].astype(F32)[:, None] * inv_freq
    cos, sin = jnp.cos(ang), jnp.sin(ang)
    rc = jnp.concatenate([jnp.ones((s, NOPE), F32), cos, cos, jnp.ones((s, HP - QKD), F32)], axis=1)
    rs = jnp.concatenate([jnp.zeros((s, NOPE), F32), -sin, sin, jnp.zeros((s, HP - QKD), F32)], axis=1)

    def small(n, l, width=None):
        v = a[n][l][None, :]
        return v if width is None else jnp.pad(v, ((0, 0), (0, width - v.shape[1])))

    saved = []
    layer_w = [_layer_weights(full, l) for l in range(DEPTH)]
    for l in range(DEPTH):
        w = layer_w[l]
        tag = ""
        h, ht = _rms_h(x, small('norm_g', l), 512, "rms_h" + tag)
        proj = _mm(h, w['w_main'], BF16, 2048, 512, "in_proj" + tag)
        proj_l = _mm(h, w['w_l'], BF16, 512, P_LW, "in_proj_lat" + tag)
        qng, kvng = small('q_norm_g', l), small('kv_norm_g', l)
        qhg, khg = small('q_head_g', l, HP), small('k_head_g', l, HP)
        q, k, v = _mla_prep(proj_l, rc, rs, qng, kvng, qhg, khg, w['w_uq'], w['w_ukv'], t_br, "mla_prep" + tag)
        o, lse = _flash_fwd(q, k, v, tq_f, tk_f, "flash_fwd" + tag)
        mk, mv = _mem_prep(mem, small('mem_norm_g', l), w['w_mkv'], small('mem_k_g', l), "mem_prep" + tag)
        cb, mqg = small('conv_b', l), small('mem_q_g', l)
        oa, oc, om = _branches(proj, o, mk, mv, conv_w[l], cb, mqg, t_br, "branches" + tag)
        x_out, aa, ac, am = _merge_fwd(x, proj, oa, oc, om, small('b_gate', l), w['w_br_attn'], w['w_br_conv'],
                                       w['w_br_mem'], w['w_out'], t_el, "merge" + tag)
        saved.append(dict(x=x, ht=ht, proj=proj, proj_l=proj_l, q=q, k=k, v=v, o=o, lse=lse, mk=mk, mv=mv,
                          oa=oa, oc=oc, om=om, aa=aa, ac=ac, am=am))
        x = x_out

    g, loss_parts = _loss_grad(x, tgt, 512, "loss")
    loss = lax.psum(jnp.sum(loss_parts), ("x", "y", "c"))

    gw = {n: [None] * DEPTH for n in WEIGHTS}
    for l in reversed(range(DEPTH)):
        w = layer_w[l]
        sv = saved[l]
        tag = ""
        tqb = min(tq_b, s)
        y, daa, dac, dam, dr, doa, doc, dom, dbg = _merge_bwd(
            g, sv['proj'], sv['aa'], sv['ac'], sv['am'], small('b_gate', l), w['w_out_t'], w['w_br_attn_t'],
            w['w_br_conv_t'], w['w_br_mem_t'], t_el, "merge_bwd" + tag)
        gw['b_gate'][l] = dbg[0]
        gw['w_out'][l] = _tn(y, g, 512, D, "dw_out" + tag)
        gw['w_br_attn'][l] = _tn(sv['oa'], daa, 512, D, "dw_attn" + tag)
        gw['w_br_conv'][l] = _tn(sv['oc'], dac, 512, D, "dw_conv" + tag)
        gw['w_br_mem'][l] = _tn(sv['om'], dam, 512, D, "dw_mem" + tag)
        cb, mqg = small('conv_b', l), small('mem_q_g', l)
        dcv, dmm, dga, do, dl, dcw, dcb, dmk, dmv, dmqg = _branches_bwd(
            sv['proj'], sv['o'], sv['mk'], sv['mv'], conv_w[l], cb, mqg, doa, doc, dom, t_br, "branches_bwd" + tag)
        gw['conv_w'][l], gw['conv_b'][l], gw['mem_q_g'][l] = dcw, dcb[0], dmqg[0]
        dwm, dmng, dmkg = _mem_prep_bwd(mem, small('mem_norm_g', l), w['w_mkv'], w['w_mkv_t'], small('mem_k_g', l),
                                        dmk, dmv, "mem_prep_bwd" + tag)
        gw['w_mkv'][l], gw['mem_norm_g'][l], gw['mem_k_g'][l] = dwm, dmng[0], dmkg[0]
        lse_r = sv['lse'].reshape(H, s // tqb, 1, tqb)
        dl_r = dl.reshape(H, s // tqb, 1, tqb)
        dq, dk, dv = _flash_bwd(sv['q'], sv['k'], sv['v'], do, lse_r, dl_r, tq_b, tk_b, "flash_bwd" + tag)
        qng, kvng = small('q_norm_g', l), small('kv_norm_g', l)
        qhg, khg = small('q_head_g', l, HP), small('k_head_g', l, HP)
        dlat, dwuq, dwukv, dqng, dkvng, dqhg, dkhg = _mla_prep_bwd(
            sv['proj_l'], rc, rs, qng, kvng, qhg, khg, w['w_uq'], w['w_ukv'], w['w_uq_t'], w['w_ukv_t'],
            dq, dk, dv, t_br, "mla_prep_bwd" + tag)
        gw['w_uq'][l] = _unpad_heads(dwuq, QKD)
        dwukv = dwukv.reshape(KVL, 2, H, HP)[:, :, :, :NOPE]
        gw['w_ukv'][l] = dwukv.transpose(0, 2, 1, 3).reshape(KVL, H * 2 * NOPE)
        gw['q_norm_g'][l], gw['kv_norm_g'][l] = dqng[0], dkvng[0]
        gw['q_head_g'][l], gw['k_head_g'][l] = dqhg[0, :QKD], dkhg[0, :QKD]
        dpieces = [dcv, dmm, dr, dga, dlat]
        dh = _mm_pieces(dpieces, w['w_in_t'], 512, 512, "d_h" + tag)
        gw['w_in'][l] = _unpermute_w_in([
            _mm_acc(sv['ht'], p, 1024, tn, f"dw_in_{i}" + tag)
            for i, (p, tn) in enumerate(zip(dpieces, (1024, 1024, 1536, 512, P_LW)))])
        g, dng = _rms_in_bwd(sv['x'], g, dh, small('norm_g', l), 512, "rms_bwd" + tag)
        gw['norm_g'][l] = dng[0]
    grad_x = g[None]

    sharded = BIG_ORDER + ['conv_w']
    axis_of = lambda n: 1 if n == 'conv_w' else BIG[n][1]
    send = [jnp.stack([_shard_blocks(gw[n][l], axis_of(n)) for l in range(DEPTH)], axis=2)
            .astype(F32 if n == 'conv_w' else BF16) for n in sharded]
    from_sibling = _pair_exchange(send, "rs_pair")
    my_c = lax.axis_index("c")
    chip_sums = []
    for n, t, got in zip(sharded, send, from_sibling):
        own = lax.dynamic_index_in_dim(t, my_c, axis=0, keepdims=False)
        flat = lambda u: u.reshape(-1, u.shape[-1])
        chip_sums.append(_pair_add(flat(own), flat(got), "rs_add_" + n).reshape(got.shape))
    parts_big = _chip_exchange(chip_sums, "rs_chips")
    small_flat = jnp.concatenate([jnp.stack(gw[n]).reshape(-1) for n in SMALL_ORDER])
    n_small = small_flat.shape[0]
    parts_small = _allgather([_to_rows(small_flat)], "ag_small_grads")[0]

    outs = [{} for _ in range(4)]
    for n, parts in zip(sharded, parts_big):
        loc = a[n].shape
        two_d = lambda t: t.reshape(loc[0] * loc[1], loc[2])
        res = _adamw(parts.reshape(4, loc[0] * loc[1], loc[2]), two_d(a[n]), two_d(a['m_' + n]),
                     two_d(a['v_' + n]), "adamw_" + n)
        for d, r in zip(outs, res):
            d[n] = r.reshape(loc)
    pks = lambda pre: _to_rows(jnp.concatenate([a[pre + n].reshape(-1) for n in SMALL_ORDER]))
    res_small = _adamw(parts_small, pks(''), pks('m_'), pks('v_'), "adamw_small")
    for d, rsm in zip(outs, res_small):
        flat = rsm.reshape(-1)[:n_small]
        off = 0
        for n in SMALL_ORDER:
            d[n] = flat[off:off + DEPTH * SMALL[n]].reshape(DEPTH, SMALL[n])
            off += DEPTH * SMALL[n]
    result = [loss, grad_x]
    for d in outs:
        result += [d[n] for n in WEIGHTS]
    return tuple(result)
```

```python
import functools

import jax
import jax.numpy as jnp
from jax import lax
from jax.experimental import pallas as pl
from jax.experimental.pallas import tpu as pltpu

F32, BF16 = jnp.float32, jnp.bfloat16

N_DEV = 8
DEPTH = 4
D = 1024
QL, KVL, RP = 384, 256, 32
H, NOPE, QKD, VD = 8, 64, 96, 64
HP = 128
CW, MW, AW = 512, 512, 512
HM, MHD = 4, 128
IN_WIDTH = 7328
EPS = 1e-6
ROPE_BASE = 10000.0
ATT_SCALE = QKD ** -0.5
MEM_SCALE = MHD ** -0.5
LOG2E = 1.4426950408889634
QSCALE = ATT_SCALE * LOG2E

ADAM_LR, ADAM_B1, ADAM_B2, ADAM_EPS, ADAM_WD, ADAM_STEP = 0.001, 0.9, 0.999, 1e-08, 0.01, 10

LANES = 128
VMEM_LIMIT = 56 * 1024 * 1024

P_CONV, P_MEM, P_R, P_GA, P_L = 0, 2048, 3072, 6144, 6656
P_MAIN = 6656
P_LW = 768
P_W = P_MAIN + P_LW
P_WPAD = 7680

WEIGHTS = ['norm_g', 'w_in', 'b_gate', 'q_norm_g', 'w_uq', 'kv_norm_g', 'w_ukv', 'q_head_g', 'k_head_g',
           'conv_w', 'conv_b', 'mem_norm_g', 'w_mkv', 'mem_q_g', 'mem_k_g', 'w_br_attn', 'w_br_conv',
           'w_br_mem', 'w_out']
INPUTS = ['x', 'mem', 'positions'] + WEIGHTS + ['loss_target'] + ['m_' + n for n in WEIGHTS] + ['v_' + n for n in WEIGHTS]

BIG = {'w_in': ((D, IN_WIDTH), 1), 'w_uq': ((QL, H * QKD), 1), 'w_ukv': ((KVL, H * 128), 1),
       'w_mkv': ((D, 2 * MW), 0), 'w_br_attn': ((AW, D), 1), 'w_br_conv': ((CW, D), 1),
       'w_br_mem': ((MW, D), 1), 'w_out': ((D, D), 0)}
BIG_ORDER = ['w_in', 'w_uq', 'w_ukv', 'w_mkv', 'w_br_attn', 'w_br_conv', 'w_br_mem', 'w_out']
CONVW_PAD = 256
SMALL = {'norm_g': D, 'b_gate': 3 * D, 'q_norm_g': QL, 'kv_norm_g': KVL, 'q_head_g': QKD, 'k_head_g': QKD,
         'conv_b': CW, 'mem_norm_g': D, 'mem_q_g': MHD, 'mem_k_g': MHD}
SMALL_ORDER = list(SMALL)
ROW_ALIGN = 1024


def _call(body, **kw):
    return pl.pallas_call(body, **kw)


def _cp(sem=None):
    return pltpu.CompilerParams(dimension_semantics=sem, vmem_limit_bytes=VMEM_LIMIT)


def _full(shape):
    n = len(shape)
    return pl.BlockSpec(shape, lambda *_: (0,) * n)


def _rms(x, g, n):
    rs = lax.rsqrt(jnp.sum(x * x, axis=-1, keepdims=True) * (1.0 / n) + EPS)
    xh = x * rs
    return xh * g, xh, rs


def _rms_bwd(dy, xh, rs, g, n):
    dxh = dy * g
    dx = rs * (dxh - xh * (jnp.sum(dxh * xh, axis=-1, keepdims=True) * (1.0 / n)))
    return dx, jnp.sum(dy * xh, axis=0, keepdims=True)


def _sigmoid(x):
    return 1.0 / (1.0 + jnp.exp(-x))


def _swap_rope(u):
    lane = lax.broadcasted_iota(jnp.int32, u.shape, 1)
    up = pltpu.roll(u, 16, 1)
    dn = pltpu.roll(u, 112, 1)
    return jnp.where((lane >= 64) & (lane < 80), dn, jnp.where((lane >= 80) & (lane < 96), up, 0.0))


def _rope(u, c, sn):
    return u * c + _swap_rope(u) * sn


def _rope_adj(d, c, sn):
    return d * c + _swap_rope(d * sn)


def _dot(a, b):
    return jnp.dot(a, b, preferred_element_type=F32)


def _dot_nt(a, b):
    return lax.dot_general(a, b, (((1,), (1,)), ((), ())), preferred_element_type=F32)


def _dot_tn(a, b):
    return lax.dot_general(a, b, (((0,), (0,)), ((), ())), preferred_element_type=F32)


def _mm(a, b, out_dtype, tm, tn, name):
    m, k = a.shape
    _, n = b.shape
    tm, tn = min(tm, m), min(tn, n)

    def body(a_ref, b_ref, o_ref):
        o_ref[...] = _dot(a_ref[...].astype(BF16), b_ref[...]).astype(o_ref.dtype)

    return _call(
        body, name=name, grid=(m // tm, n // tn),
        in_specs=[pl.BlockSpec((tm, k), lambda i, j: (i, 0)), pl.BlockSpec((k, tn), lambda i, j: (0, j))],
        out_specs=pl.BlockSpec((tm, tn), lambda i, j: (i, j)),
        out_shape=jax.ShapeDtypeStruct((m, n), out_dtype),
        compiler_params=_cp(("parallel", "arbitrary")),
    )(a, b)


def _tn(a, b, ts, tn, name):
    s, ka = a.shape
    _, n = b.shape
    ts, tn = min(ts, s), min(tn, n)

    def body(a_ref, b_ref, o_ref):
        @pl.when(pl.program_id(1) == 0)
        def _():
            o_ref[...] = jnp.zeros_like(o_ref)

        o_ref[...] += _dot_tn(a_ref[...].astype(BF16), b_ref[...].astype(BF16))

    return _call(
        body, name=name, grid=(n // tn, s // ts),
        in_specs=[pl.BlockSpec((ts, ka), lambda j, i: (i, 0)), pl.BlockSpec((ts, tn), lambda j, i: (i, j))],
        out_specs=pl.BlockSpec((ka, tn), lambda j, i: (0, j)),
        out_shape=jax.ShapeDtypeStruct((ka, n), F32),
        compiler_params=_cp(("parallel", "arbitrary")),
    )(a, b)


def _mm_pieces(pieces, b, tm, tn, name):
    n_p = len(pieces)
    m = pieces[0].shape[0]
    k, n = b.shape
    tm, tn = min(tm, m), min(tn, n)
    offs = [sum(p.shape[1] for p in pieces[:i]) for i in range(n_p)]

    def body(*refs):
        a_refs, b_ref, o_ref = refs[:n_p], refs[n_p], refs[n_p + 1]
        acc = None
        for a_ref, off in zip(a_refs, offs):
            d = _dot(a_ref[...], b_ref[off:off + a_ref.shape[1], :])
            acc = d if acc is None else acc + d
        o_ref[...] = acc

    return _call(
        body, name=name, grid=(m // tm, n // tn),
        in_specs=[pl.BlockSpec((tm, p.shape[1]), lambda i, j: (i, 0)) for p in pieces]
        + [pl.BlockSpec((k, tn), lambda i, j: (0, j))],
        out_specs=pl.BlockSpec((tm, tn), lambda i, j: (i, j)),
        out_shape=jax.ShapeDtypeStruct((m, n), F32),
        compiler_params=_cp(("parallel", "arbitrary")),
    )(*pieces, b)


def _mm_acc(a, b, tk, tn, name):
    m, k = a.shape
    _, n = b.shape
    tk, tn = min(tk, k), min(tn, n)

    def body(a_ref, b_ref, o_ref):
        @pl.when(pl.program_id(1) == 0)
        def _():
            o_ref[...] = jnp.zeros_like(o_ref)

        o_ref[...] += _dot(a_ref[...], b_ref[...].astype(BF16))

    return _call(
        body, name=name, grid=(n // tn, k // tk),
        in_specs=[pl.BlockSpec((m, tk), lambda j, i: (0, i)), pl.BlockSpec((tk, tn), lambda j, i: (i, j))],
        out_specs=pl.BlockSpec((m, tn), lambda j, i: (0, j)),
        out_shape=jax.ShapeDtypeStruct((m, n), F32),
        compiler_params=_cp(("parallel", "arbitrary")),
    )(a, b)


def _rms_h(x, g, t, name):
    s = x.shape[0]
    t = min(t, s)

    def body(x_ref, g_ref, h_ref, ht_ref):
        h = _rms(x_ref[...], g_ref[...], D)[0]
        h_ref[...] = h.astype(BF16)
        ht_ref[...] = h.T.astype(BF16)

    return _call(
        body, name=name, grid=(s // t,),
        in_specs=[pl.BlockSpec((t, D), lambda i: (i, 0)), _full((1, D))],
        out_specs=[pl.BlockSpec((t, D), lambda i: (i, 0)), pl.BlockSpec((D, t), lambda i: (0, i))],
        out_shape=[jax.ShapeDtypeStruct((s, D), BF16), jax.ShapeDtypeStruct((D, s), BF16)],
        compiler_params=_cp(("parallel",)),
    )(x, g)


def _mla_heads(pl_blk, c, sn, qng, kvng, qhg, khg, wuq, wukv):
    ql = pl_blk[:, 0:QL].astype(F32)
    kvl = pl_blk[:, QL:QL + KVL].astype(F32)
    kpe = pl_blk[:, QL + KVL:P_LW].astype(F32)
    qn, qxh, qrs = _rms(ql, qng, QL)
    kvn, kvxh, kvrs = _rms(kvl, kvng, KVL)
    qn16, kvn16 = qn.astype(BF16), kvn.astype(BF16)
    qp = _dot(qn16, wuq)
    kvp = _dot(kvn16, wukv)
    return ql, kvl, kpe, qxh, qrs, kvxh, kvrs, qn16, kvn16, qp, kvp


def _mla_prep(proj_l, c, sn, qng, kvng, qhg, khg, wuq, wukv, t, name):
    s = proj_l.shape[0]
    t = min(t, s)

    def body(l_ref, c_ref, sn_ref, qng_ref, kvng_ref, qhg_ref, khg_ref, wuq_ref, wukv_ref, q_ref, k_ref, v_ref):
        cc, ss = c_ref[...], sn_ref[...]
        (_, _, kpe, _, _, _, _, _, _, qp, kvp) = _mla_heads(
            l_ref[...], cc, ss, qng_ref[...], kvng_ref[...], qhg_ref[...], khg_ref[...], wuq_ref[...], wukv_ref[...])
        lane = lax.broadcasted_iota(jnp.int32, cc.shape, 1)
        for h in range(H):
            u = qp[:, h * HP:(h + 1) * HP]
            q_ref[h] = (_rope(_rms(u, qhg_ref[...], QKD)[0], cc, ss) * QSCALE).astype(BF16)
            u = kvp[:, h * HP:(h + 1) * HP] + kpe
            k_ref[h] = _rope(_rms(u, khg_ref[...], QKD)[0], cc, ss).astype(BF16)
            v_ref[h] = jnp.where(lane == VD, 1.0, kvp[:, (H + h) * HP:(H + h + 1) * HP]).astype(BF16)

    hs = pl.BlockSpec((H, t, HP), lambda i: (0, i, 0))
    row = lambda w: pl.BlockSpec((t, w), lambda i: (i, 0))
    return _call(
        body, name=name, grid=(s // t,),
        in_specs=[row(P_LW), row(HP), row(HP), _full((1, QL)), _full((1, KVL)), _full((1, HP)), _full((1, HP)),
                  _full((QL, H * HP)), _full((KVL, 2 * H * HP))],
        out_specs=[hs, hs, hs],
        out_shape=[jax.ShapeDtypeStruct((H, s, HP), BF16)] * 3,
        compiler_params=_cp(("parallel",)),
    )(proj_l, c, sn, qng, kvng, qhg, khg, wuq, wukv)


def _flash_fwd(q, k, v, tq, tk, name):
    _, s, _ = q.shape
    tq, tk = min(tq, s), min(tk, s)
    nk, nc = s // tk, tk // LANES
    un = 8 if nk % 8 == 0 else 1

    def body(q_ref, k_ref, v_ref, o_ref, lse_ref, s_scr):
        qb = q_ref[0]

        def scores(jj, mx):
            for u in range(un):
                j = jj * un + u
                off = pl.multiple_of(j * tk, tk)
                sc = _dot_nt(qb, k_ref[0, pl.ds(off, tk), :])
                s_scr[j] = sc
                for cc in range(nc):
                    mx = jnp.maximum(mx, sc[:, cc * LANES:(cc + 1) * LANES])
            return mx

        mx = lax.fori_loop(0, nk // un, scores, jnp.full((tq, LANES), -jnp.inf, F32))
        m = jnp.max(mx, axis=-1, keepdims=True)
        mb = jnp.broadcast_to(m, (tq, LANES))

        def probs(jj, acc):
            for u in range(un):
                j = jj * un + u
                off = pl.multiple_of(j * tk, tk)
                sc = s_scr[j]
                ps = [jnp.exp2(sc[:, cc * LANES:(cc + 1) * LANES] - mb).astype(BF16) for cc in range(nc)]
                acc = acc + _dot(jnp.concatenate(ps, axis=-1), v_ref[0, pl.ds(off, tk), :])
            return acc

        acc = lax.fori_loop(0, nk // un, probs, jnp.zeros((tq, HP), F32))
        l = acc[:, VD:VD + 1]
        o_ref[0] = (acc[:, :VD] / l).astype(BF16)
        lse = jnp.broadcast_to(m + jnp.log(l) * LOG2E, (tq, LANES))
        lse_ref[0] = lse.T[0:1, :]

    return _call(
        body, name=name, grid=(H, s // tq),
        in_specs=[pl.BlockSpec((1, tq, HP), lambda h, i: (h, i, 0)),
                  pl.BlockSpec((1, s, HP), lambda h, i: (h, 0, 0), pipeline_mode=pl.Buffered(1)),
                  pl.BlockSpec((1, s, HP), lambda h, i: (h, 0, 0), pipeline_mode=pl.Buffered(1))],
        out_specs=[pl.BlockSpec((1, tq, VD), lambda h, i: (h, i, 0)), pl.BlockSpec((1, 1, tq), lambda h, i: (h, 0, i))],
        out_shape=[jax.ShapeDtypeStruct((H, s, VD), BF16), jax.ShapeDtypeStruct((H, 1, s), F32)],
        scratch_shapes=[pltpu.VMEM((nk, tq, tk), F32)],
        compiler_params=_cp(("parallel", "arbitrary")),
    )(q, k, v)


def _mem_prep(mem, mng, wmkv, mkg, name):
    m = mem.shape[0]

    def body(mem_ref, mng_ref, w_ref, mkg_ref, mk_ref, mv_ref):
        mn = _rms(mem_ref[...], mng_ref[...], D)[0].astype(BF16)
        mkv = _dot(mn, w_ref[...])
        for h in range(HM):
            mk_ref[h] = _rms(mkv[:, 2 * h * MHD:(2 * h + 1) * MHD], mkg_ref[...], MHD)[0].astype(BF16)
            mv_ref[h] = mkv[:, (2 * h + 1) * MHD:(2 * h + 2) * MHD].astype(BF16)

    return _call(
        body, name=name,
        in_specs=[_full((m, D)), _full((1, D)), _full((D, 2 * MW)), _full((1, MHD))],
        out_specs=[_full((HM, m, MHD))] * 2,
        out_shape=[jax.ShapeDtypeStruct((HM, m, MHD), BF16)] * 2,
        compiler_params=_cp(),
    )(mem, mng, wmkv, mkg)


def _conv_parts(cv, prev, nxt, first, last, cw, cb):
    t = cv.shape[0]
    c_b, c_c, c_u, g_c = (cv[:, i * CW:(i + 1) * CW].astype(F32) for i in range(4))
    z = c_c * c_u
    zp = jnp.where(first, 0.0, prev[15:16, CW:2 * CW].astype(F32) * prev[15:16, 2 * CW:3 * CW].astype(F32))
    zn = jnp.where(last, 0.0, nxt[0:1, CW:2 * CW].astype(F32) * nxt[0:1, 2 * CW:3 * CW].astype(F32))
    row = lax.broadcasted_iota(jnp.int32, (t, CW), 0)
    z_m1 = jnp.where(row == 0, zp, pltpu.roll(z, 1, 0))
    z_p1 = jnp.where(row == t - 1, zn, pltpu.roll(z, t - 1, 0))
    conv = cw[0:1] * z_m1 + cw[1:2] * z + cw[2:3] * z_p1 + cb
    return c_b, c_c, c_u, g_c, z, z_m1, z_p1, conv


def _mem_attn(qm, mqg, mk_ref, mv_ref):
    outs = []
    for h in range(HM):
        mq, mqxh, mqrs = _rms(qm[:, h * MHD:(h + 1) * MHD], mqg, MHD)
        mq16 = mq.astype(BF16)
        sc = _dot_nt(mq16, mk_ref[h]) * MEM_SCALE
        e = jnp.exp(sc - jnp.max(sc, axis=-1, keepdims=True))
        p = e / jnp.sum(e, axis=-1, keepdims=True)
        o = _dot(p.astype(BF16), mv_ref[h])
        outs.append((mq16, mqxh, mqrs, p, o))
    return outs


def _halo_specs(t, s, width):
    nb = s // 16
    prev = pl.BlockSpec((16, width), lambda i: (jnp.maximum(i * (t // 16) - 1, 0), 0))
    nxt = pl.BlockSpec((16, width), lambda i: (jnp.minimum((i + 1) * (t // 16), nb - 1), 0))
    return prev, nxt


def _branches(proj, o, mk, mv, cw, cb, mqg, t, name):
    s = proj.shape[0]
    t = min(t, s)
    nt = s // t

    def body(cv_ref, pv_ref, nx_ref, mm_ref, ga_ref, o_ref, mk_ref, mv_ref, cw_ref, cb_ref, mqg_ref,
             oa_ref, oc_ref, om_ref):
        i = pl.program_id(0)
        c_b, _, _, g_c, _, _, _, conv = _conv_parts(
            cv_ref[...], pv_ref[...], nx_ref[...], i == 0, i == nt - 1, cw_ref[...], cb_ref[...])
        oc_ref[...] = (c_b * conv * (g_c * _sigmoid(g_c))).astype(BF16)
        ga = ga_ref[...].astype(F32)
        ocat = jnp.concatenate([o_ref[h].astype(F32) for h in range(H)], axis=-1)
        oa_ref[...] = (ocat * (ga * _sigmoid(ga))).astype(BF16)
        mblk = mm_ref[...].astype(F32)
        gm = mblk[:, MW:]
        heads = _mem_attn(mblk[:, :MW], mqg_ref[...], mk_ref, mv_ref)
        om = jnp.concatenate([hh[4] for hh in heads], axis=-1)
        om_ref[...] = (om * (gm * _sigmoid(gm))).astype(BF16)

    pv, nx = _halo_specs(t, s, 4 * CW)
    out = pl.BlockSpec((t, 512), lambda i: (i, 0))
    return _call(
        body, name=name, grid=(nt,),
        in_specs=[pl.BlockSpec((t, 4 * CW), lambda i: (i, 0)), pv, nx,
                  pl.BlockSpec((t, 2 * MW), lambda i: (i, P_MEM // (2 * MW))),
                  pl.BlockSpec((t, AW), lambda i: (i, P_GA // AW)),
                  pl.BlockSpec((H, t, VD), lambda i: (0, i, 0)),
                  _full(mk.shape), _full(mv.shape), _full((3, CW)), _full((1, CW)), _full((1, MHD))],
        out_specs=[out, out, out],
        out_shape=[jax.ShapeDtypeStruct((s, 512), BF16)] * 3,
        compiler_params=_cp(("parallel",)),
    )(proj, proj, proj, proj, proj, o, mk, mv, cw, cb, mqg)


def _merge_fwd(x, proj, oa, oc, om, bg, wa, wc, wm, wo, t, name):
    s = x.shape[0]
    t = min(t, s)

    def body(x_ref, r_ref, oa_ref, oc_ref, om_ref, bg_ref, wa_ref, wc_ref, wm_ref, wo_ref,
             xo_ref, aa_ref, ac_ref, am_ref):
        y = jnp.zeros((t, D), F32)
        for j, (o_ref, w_ref, a_ref) in enumerate(((oa_ref, wa_ref, aa_ref), (oc_ref, wc_ref, ac_ref),
                                                   (om_ref, wm_ref, am_ref))):
            a = _dot(o_ref[...], w_ref[...])
            a_ref[...] = a.astype(BF16)
            rg = _sigmoid(r_ref[:, j * D:(j + 1) * D].astype(F32) + bg_ref[:, j * D:(j + 1) * D])
            y = y + rg * a
        xo_ref[...] = x_ref[...] + _dot(y.astype(BF16), wo_ref[...])

    row = lambda w: pl.BlockSpec((t, w), lambda i: (i, 0))
    return _call(
        body, name=name, grid=(s // t,),
        in_specs=[row(D), pl.BlockSpec((t, 3 * D), lambda i: (i, P_R // (3 * D))), row(512), row(512), row(512),
                  _full((1, 3 * D)), _full((512, D)), _full((512, D)), _full((512, D)), _full((D, D))],
        out_specs=[row(D), row(D), row(D), row(D)],
        out_shape=[jax.ShapeDtypeStruct((s, D), F32)] + [jax.ShapeDtypeStruct((s, D), BF16)] * 3,
        compiler_params=_cp(("parallel",)),
    )(x, proj, oa, oc, om, bg, wa, wc, wm, wo)


def _loss_grad(x, tgt, t, name):
    s = x.shape[0]
    t = min(t, s)

    def body(x_ref, t_ref, g_ref, l_ref):
        @pl.when(pl.program_id(0) == 0)
        def _():
            l_ref[...] = jnp.zeros_like(l_ref)

        e = x_ref[...] - t_ref[...]
        g_ref[...] = e * (1.0 / D)
        sq = e * e
        part = sq[:, 0:LANES]
        for j in range(1, D // LANES):
            part = part + sq[:, j * LANES:(j + 1) * LANES]
        acc = part[0:8]
        for j in range(1, t // 8):
            acc = acc + part[j * 8:(j + 1) * 8]
        l_ref[...] += acc * (0.5 / D)

    row = pl.BlockSpec((t, D), lambda i: (i, 0))
    return _call(
        body, name=name, grid=(s // t,),
        in_specs=[row, row], out_specs=[row, _full((8, LANES))],
        out_shape=[jax.ShapeDtypeStruct((s, D), F32), jax.ShapeDtypeStruct((8, LANES), F32)],
        compiler_params=_cp(("arbitrary",)),
    )(x, tgt)


def _merge_bwd(g, proj, aa, ac, am, bg, wot, wat, wct, wmt, t, name):
    s = g.shape[0]
    t = min(t, s)

    def body(g_ref, r_ref, aa_ref, ac_ref, am_ref, bg_ref, wot_ref, wat_ref, wct_ref, wmt_ref,
             y_ref, daa_ref, dac_ref, dam_ref, dr_ref, doa_ref, doc_ref, dom_ref, dbg_ref):
        @pl.when(pl.program_id(0) == 0)
        def _():
            dbg_ref[...] = jnp.zeros_like(dbg_ref)

        dy = _dot(g_ref[...].astype(BF16), wot_ref[...])
        y = jnp.zeros((t, D), F32)
        for j, (a_ref, wt_ref, da_ref, do_ref) in enumerate(((aa_ref, wat_ref, daa_ref, doa_ref),
                                                             (ac_ref, wct_ref, dac_ref, doc_ref),
                                                             (am_ref, wmt_ref, dam_ref, dom_ref))):
            a = a_ref[...].astype(F32)
            rg = _sigmoid(r_ref[:, j * D:(j + 1) * D].astype(F32) + bg_ref[:, j * D:(j + 1) * D])
            y = y + rg * a
            da = (dy * rg).astype(BF16)
            da_ref[...] = da
            dr = dy * a * rg * (1.0 - rg)
            dr_ref[:, j * D:(j + 1) * D] = dr.astype(BF16)
            dbg_ref[:, j * D:(j + 1) * D] += jnp.sum(dr, axis=0, keepdims=True)
            do_ref[...] = _dot(da, wt_ref[...])
        y_ref[...] = y.T.astype(BF16)

    row = lambda w: pl.BlockSpec((t, w), lambda i: (i, 0))
    return _call(
        body, name=name, grid=(s // t,),
        in_specs=[row(D), pl.BlockSpec((t, 3 * D), lambda i: (i, P_R // (3 * D))), row(D), row(D), row(D),
                  _full((1, 3 * D)), _full((D, D)), _full((D, 512)), _full((D, 512)), _full((D, 512))],
        out_specs=[pl.BlockSpec((D, t), lambda i: (0, i)), row(D), row(D), row(D), row(3 * D), row(512), row(512),
                   row(512), _full((1, 3 * D))],
        out_shape=[jax.ShapeDtypeStruct((D, s), BF16)] + [jax.ShapeDtypeStruct((s, D), BF16)] * 3
        + [jax.ShapeDtypeStruct((s, 3 * D), BF16)]
        + [jax.ShapeDtypeStruct((s, 512), F32)] * 3 + [jax.ShapeDtypeStruct((1, 3 * D), F32)],
        compiler_params=_cp(("arbitrary",)),
    )(g, proj, aa, ac, am, bg, wot, wat, wct, wmt)


def _branches_bwd(proj, o, mk, mv, cw, cb, mqg, doa, doc, dom, t, name):
    s = proj.shape[0]
    t = min(t, s)
    nt = s // t
    m = mk.shape[1]

    def body(cv_ref, pv_ref, nx_ref, mm_ref, ga_ref, o_ref, mk_ref, mv_ref, cw_ref, cb_ref, mqg_ref,
             doa_ref, doc_ref, dcp_ref, dcn_ref, dom_ref,
             dcv_ref, dmm_ref, dga_ref, do_ref, dl_ref, dcw_ref, dcb_ref, dmk_ref, dmv_ref, dmqg_ref):
        i = pl.program_id(0)

        @pl.when(i == 0)
        def _():
            for r in (dcw_ref, dcb_ref, dmk_ref, dmv_ref, dmqg_ref):
                r[...] = jnp.zeros_like(r)

        first, last = i == 0, i == nt - 1
        cw_, cb_ = cw_ref[...], cb_ref[...]
        pv, nx = pv_ref[...], nx_ref[...]
        c_b, c_c, c_u, g_c, z, z_m1, z_p1, conv = _conv_parts(cv_ref[...], pv, nx, first, last, cw_, cb_)
        sg = _sigmoid(g_c)
        silu = g_c * sg
        dsilu = sg * (1.0 + g_c * (1.0 - sg))
        doc_ = doc_ref[...]
        dconv = doc_ * c_b * silu
        gp = pv[15:16, 3 * CW:4 * CW].astype(F32)
        gn = nx[0:1, 3 * CW:4 * CW].astype(F32)
        dconv_p = jnp.where(first, 0.0, dcp_ref[15:16, :] * pv[15:16, 0:CW].astype(F32) * (gp * _sigmoid(gp)))
        dconv_n = jnp.where(last, 0.0, dcn_ref[0:1, :] * nx[0:1, 0:CW].astype(F32) * (gn * _sigmoid(gn)))
        row = lax.broadcasted_iota(jnp.int32, (t, CW), 0)
        d_m1 = jnp.where(row == 0, dconv_p, pltpu.roll(dconv, 1, 0))
        d_p1 = jnp.where(row == t - 1, dconv_n, pltpu.roll(dconv, t - 1, 0))
        dz = cw_[0:1] * d_p1 + cw_[1:2] * dconv + cw_[2:3] * d_m1
        dcv_ref[:, 0:CW] = (doc_ * conv * silu).astype(BF16)
        dcv_ref[:, CW:2 * CW] = (dz * c_u).astype(BF16)
        dcv_ref[:, 2 * CW:3 * CW] = (dz * c_c).astype(BF16)
        dcv_ref[:, 3 * CW:4 * CW] = (doc_ * c_b * conv * dsilu).astype(BF16)
        dcw_ref[0:1, :] += jnp.sum(dconv * z_m1, axis=0, keepdims=True)
        dcw_ref[1:2, :] += jnp.sum(dconv * z, axis=0, keepdims=True)
        dcw_ref[2:3, :] += jnp.sum(dconv * z_p1, axis=0, keepdims=True)
        dcb_ref[...] += jnp.sum(dconv, axis=0, keepdims=True)
        ga = ga_ref[...].astype(F32)
        sg = _sigmoid(ga)
        doa_ = doa_ref[...]
        ocat = jnp.concatenate([o_ref[h].astype(F32) for h in range(H)], axis=-1)
        dga_ref[...] = (doa_ * ocat * (sg * (1.0 + ga * (1.0 - sg)))).astype(BF16)
        dog = doa_ * (ga * sg)
        zeros = jnp.zeros((t, HP - VD), F32)
        lane = lax.broadcasted_iota(jnp.int32, (t, LANES), 1)
        dmat = jnp.zeros((t, LANES), F32)
        for h in range(H):
            dh = dog[:, h * VD:(h + 1) * VD]
            do_ref[h] = jnp.concatenate([dh, zeros], axis=-1).astype(BF16)
            dmat = jnp.where(lane == h, jnp.sum(dh * ocat[:, h * VD:(h + 1) * VD], axis=-1, keepdims=True), dmat)
        dlt = dmat.T
        for h in range(H):
            dl_ref[h] = dlt[h:h + 1, :]
        mblk = mm_ref[...].astype(F32)
        gm = mblk[:, MW:]
        sg = _sigmoid(gm)
        dom_ = dom_ref[...]
        heads = _mem_attn(mblk[:, :MW], mqg_ref[...], mk_ref, mv_ref)
        om = jnp.concatenate([hh[4] for hh in heads], axis=-1)
        dmm_ref[:, MW:] = (dom_ * om * (sg * (1.0 + gm * (1.0 - sg)))).astype(BF16)
        dmo = dom_ * (gm * sg)
        dmqg = jnp.zeros((1, MHD), F32)
        for h in range(HM):
            mq16, mqxh, mqrs, p, _ = heads[h]
            dmo_h = dmo[:, h * MHD:(h + 1) * MHD].astype(BF16)
            dp = _dot_nt(dmo_h, mv_ref[h])
            ds = (p * (dp - jnp.sum(dp * p, axis=-1, keepdims=True)) * MEM_SCALE).astype(BF16)
            dmq = _dot(ds, mk_ref[h])
            dmk_ref[h] += _dot_tn(ds, mq16)
            dmv_ref[h] += _dot_tn(p.astype(BF16), dmo_h)
            dq, dg = _rms_bwd(dmq, mqxh, mqrs, mqg_ref[...], MHD)
            dmm_ref[:, h * MHD:(h + 1) * MHD] = dq.astype(BF16)
            dmqg = dmqg + dg
        dmqg_ref[...] += dmqg

    pv, nx = _halo_specs(t, s, 4 * CW)
    dpv, dnx = _halo_specs(t, s, CW)
    row = lambda w: pl.BlockSpec((t, w), lambda i: (i, 0))
    hs = lambda w: pl.BlockSpec((H, t, w), lambda i: (0, i, 0))
    return _call(
        body, name=name, grid=(nt,),
        in_specs=[pl.BlockSpec((t, 4 * CW), lambda i: (i, 0)), pv, nx,
                  pl.BlockSpec((t, 2 * MW), lambda i: (i, P_MEM // (2 * MW))),
                  pl.BlockSpec((t, AW), lambda i: (i, P_GA // AW)),
                  hs(VD), _full(mk.shape), _full(mv.shape), _full((3, CW)), _full((1, CW)), _full((1, MHD)),
                  row(512), row(512), dpv, dnx, row(512)],
        out_specs=[row(4 * CW), row(2 * MW), row(AW), hs(HP), pl.BlockSpec((H, 1, t), lambda i: (0, 0, i)),
                   _full((3, CW)), _full((1, CW)),
                   _full((HM, m, MHD)), _full((HM, m, MHD)), _full((1, MHD))],
        out_shape=[jax.ShapeDtypeStruct((s, 4 * CW), BF16), jax.ShapeDtypeStruct((s, 2 * MW), BF16),
                   jax.ShapeDtypeStruct((s, AW), BF16), jax.ShapeDtypeStruct((H, s, HP), BF16),
                   jax.ShapeDtypeStruct((H, 1, s), F32), jax.ShapeDtypeStruct((3, CW), F32),
                   jax.ShapeDtypeStruct((1, CW), F32), jax.ShapeDtypeStruct((HM, m, MHD), F32),
                   jax.ShapeDtypeStruct((HM, m, MHD), F32), jax.ShapeDtypeStruct((1, MHD), F32)],
        compiler_params=_cp(("arbitrary",)),
    )(proj, proj, proj, proj, proj, o, mk, mv, cw, cb, mqg, doa, doc, doc, doc, dom)


def _flash_bwd(q, k, v, do, lse, dl, tq, tk, name):
    _, s, _ = q.shape
    tq, tk = min(tq, s), min(tk, s)
    nq, nkt, nc = s // tq, s // tk, tk // LANES
    unroll = 16 if nq % 16 == 0 else 1

    def body(q_ref, do_ref, lse_ref, dl_ref, k_ref, v_ref, dq_ref, dk_ref, dv_ref, dq_acc):
        j = pl.program_id(1)

        @pl.when(j == 0)
        def _():
            dq_acc[...] = jnp.zeros_like(dq_acc)

        kb, vb = k_ref[0], v_ref[0]

        def step(ii, carry):
            dkt, dvt = carry
            for u in range(unroll):
                i = ii * unroll + u
                off = pl.multiple_of(i * tq, tq)
                qb = q_ref[0, pl.ds(off, tq), :]
                dob = do_ref[0, pl.ds(off, tq), :]
                lse_b = jnp.broadcast_to(lse_ref[0, i], (LANES, tq)).T
                dl_b = jnp.broadcast_to(dl_ref[0, i], (LANES, tq)).T
                sc = _dot_nt(qb, kb)
                dp = _dot_nt(dob, vb)
                ps, dss = [], []
                for cc in range(nc):
                    p = jnp.exp2(sc[:, cc * LANES:(cc + 1) * LANES] - lse_b)
                    ps.append(p.astype(BF16))
                    dss.append((p * (dp[:, cc * LANES:(cc + 1) * LANES] - dl_b)).astype(BF16))
                p16, ds16 = jnp.concatenate(ps, axis=-1), jnp.concatenate(dss, axis=-1)
                dvt = dvt + _dot_tn(dob, p16)
                dkt = dkt + _dot_tn(qb, ds16)
                dq_acc[pl.ds(off, tq), :] += _dot(ds16, kb)
            return dkt, dvt

        dkt, dvt = lax.fori_loop(0, nq // unroll, step, (jnp.zeros((HP, tk), F32), jnp.zeros((HP, tk), F32)))
        dk_ref[0] = (dkt.T * (1.0 / LOG2E)).astype(BF16)
        dv_ref[0] = dvt.T.astype(BF16)

        @pl.when(j == nkt - 1)
        def _():
            dq_ref[0] = (dq_acc[...] * ATT_SCALE).astype(BF16)

    whole = pl.BlockSpec((1, s, HP), lambda h, j: (h, 0, 0))
    stat = pl.BlockSpec((1, nq, 1, tq), lambda h, j: (h, 0, 0, 0))
    tile = pl.BlockSpec((1, tk, HP), lambda h, j: (h, j, 0))
    return _call(
        body, name=name, grid=(H, nkt),
        in_specs=[whole, whole, stat, stat, tile, tile],
        out_specs=[whole, tile, tile],
        out_shape=[jax.ShapeDtypeStruct((H, s, HP), BF16)] * 3,
        scratch_shapes=[pltpu.VMEM((s, HP), F32)],
        compiler_params=_cp(("arbitrary", "arbitrary")),
    )(q, do, lse, dl, k, v)


def _mla_prep_bwd(proj_l, c, sn, qng, kvng, qhg, khg, wuq, wukv, wuqt, wukvt, dq, dk, dv, t, name):
    s = proj_l.shape[0]
    t = min(t, s)

    def body(l_ref, c_ref, sn_ref, qng_ref, kvng_ref, qhg_ref, khg_ref, wuq_ref, wukv_ref, wuqt_ref, wukvt_ref,
             dq_ref, dk_ref, dv_ref, dl_ref, dwuq_ref, dwukv_ref, dqng_ref, dkvng_ref, dqhg_ref, dkhg_ref):
        @pl.when(pl.program_id(0) == 0)
        def _():
            for r in (dwuq_ref, dwukv_ref, dqng_ref, dkvng_ref, dqhg_ref, dkhg_ref):
                r[...] = jnp.zeros_like(r)

        cc, ss = c_ref[...], sn_ref[...]
        qhg, khg = qhg_ref[...], khg_ref[...]
        (_, _, kpe, qxh, qrs, kvxh, kvrs, qn16, kvn16, qp, kvp) = _mla_heads(
            l_ref[...], cc, ss, qng_ref[...], kvng_ref[...], qhg, khg, wuq_ref[...], wukv_ref[...])
        lane = lax.broadcasted_iota(jnp.int32, (t, HP), 1)
        dqp, dkp, dvp = [], [], []
        dkpe = jnp.zeros((t, HP), F32)
        dqhg = jnp.zeros((1, HP), F32)
        dkhg = jnp.zeros((1, HP), F32)
        for h in range(H):
            _, xh, rs = _rms(qp[:, h * HP:(h + 1) * HP], qhg, QKD)
            du, dg = _rms_bwd(_rope_adj(dq_ref[h].astype(F32), cc, ss), xh, rs, qhg, QKD)
            dqp.append(du)
            dqhg = dqhg + dg
            _, xh, rs = _rms(kvp[:, h * HP:(h + 1) * HP] + kpe, khg, QKD)
            du, dg = _rms_bwd(_rope_adj(dk_ref[h].astype(F32), cc, ss), xh, rs, khg, QKD)
            dkp.append(du)
            dkhg = dkhg + dg
            dkpe = dkpe + jnp.where((lane >= NOPE) & (lane < QKD), du, 0.0)
            dvp.append(dv_ref[h].astype(F32))
        dqhg_ref[...] += dqhg
        dkhg_ref[...] += dkhg
        dqp16 = jnp.concatenate(dqp, axis=-1).astype(BF16)
        dkvp16 = jnp.concatenate(dkp + dvp, axis=-1).astype(BF16)
        dwuq_ref[...] += _dot_tn(qn16, dqp16)
        dwukv_ref[...] += _dot_tn(kvn16, dkvp16)
        dql, dg = _rms_bwd(_dot(dqp16, wuqt_ref[...]), qxh, qrs, qng_ref[...], QL)
        dqng_ref[...] += dg
        dkvl, dg = _rms_bwd(_dot(dkvp16, wukvt_ref[...]), kvxh, kvrs, kvng_ref[...], KVL)
        dkvng_ref[...] += dg
        dl_ref[:, 0:QL] = dql.astype(BF16)
        dl_ref[:, QL:QL + KVL] = dkvl.astype(BF16)
        dl_ref[:, QL + KVL:P_LW] = dkpe.astype(BF16)

    hs = pl.BlockSpec((H, t, HP), lambda i: (0, i, 0))
    row = lambda w: pl.BlockSpec((t, w), lambda i: (i, 0))
    return _call(
        body, name=name, grid=(s // t,),
        in_specs=[row(P_LW), row(HP), row(HP), _full((1, QL)), _full((1, KVL)), _full((1, HP)), _full((1, HP)),
                  _full((QL, H * HP)), _full((KVL, 2 * H * HP)), _full((H * HP, QL)), _full((2 * H * HP, KVL)),
                  hs, hs, hs],
        out_specs=[row(P_LW), _full((QL, H * HP)), _full((KVL, 2 * H * HP)), _full((1, QL)), _full((1, KVL)),
                   _full((1, HP)), _full((1, HP))],
        out_shape=[jax.ShapeDtypeStruct((s, P_LW), BF16), jax.ShapeDtypeStruct((QL, H * HP), F32),
                   jax.ShapeDtypeStruct((KVL, 2 * H * HP), F32), jax.ShapeDtypeStruct((1, QL), F32),
                   jax.ShapeDtypeStruct((1, KVL), F32), jax.ShapeDtypeStruct((1, HP), F32),
                   jax.ShapeDtypeStruct((1, HP), F32)],
        compiler_params=_cp(("arbitrary",)),
    )(proj_l, c, sn, qng, kvng, qhg, khg, wuq, wukv, wuqt, wukvt, dq, dk, dv)


def _mem_prep_bwd(mem, mng, wmkv, wmkvt, mkg, dmk, dmv, name):
    m = mem.shape[0]

    def body(mem_ref, mng_ref, w_ref, wt_ref, mkg_ref, dmk_ref, dmv_ref, dw_ref, dmng_ref, dmkg_ref):
        mn, xh, _ = _rms(mem_ref[...], mng_ref[...], D)
        mn16 = mn.astype(BF16)
        mkv = _dot(mn16, w_ref[...])
        parts = []
        dmkg = jnp.zeros((1, MHD), F32)
        for h in range(HM):
            _, kxh, krs = _rms(mkv[:, 2 * h * MHD:(2 * h + 1) * MHD], mkg_ref[...], MHD)
            du, dg = _rms_bwd(dmk_ref[h], kxh, krs, mkg_ref[...], MHD)
            dmkg = dmkg + dg
            parts += [du, dmv_ref[h]]
        dmkv = jnp.concatenate(parts, axis=-1).astype(BF16)
        dw_ref[...] = _dot_tn(mn16, dmkv)
        dmn = _dot(dmkv, wt_ref[...])
        dmng_ref[...] = jnp.sum(dmn * xh, axis=0, keepdims=True)
        dmkg_ref[...] = dmkg

    return _call(
        body, name=name,
        in_specs=[_full((m, D)), _full((1, D)), _full((D, 2 * MW)), _full((2 * MW, D)), _full((1, MHD)),
                  _full((HM, m, MHD)), _full((HM, m, MHD))],
        out_specs=[_full((D, 2 * MW)), _full((1, D)), _full((1, MHD))],
        out_shape=[jax.ShapeDtypeStruct((D, 2 * MW), F32), jax.ShapeDtypeStruct((1, D), F32),
                   jax.ShapeDtypeStruct((1, MHD), F32)],
        compiler_params=_cp(),
    )(mem, mng, wmkv, wmkvt, mkg, dmk, dmv)


def _rms_in_bwd(x, g_out, dh, ng, t, name):
    s = x.shape[0]
    t = min(t, s)

    def body(x_ref, go_ref, dh_ref, ng_ref, dx_ref, dng_ref):
        @pl.when(pl.program_id(0) == 0)
        def _():
            dng_ref[...] = jnp.zeros_like(dng_ref)

        _, xh, rs = _rms(x_ref[...], ng_ref[...], D)
        dx, dg = _rms_bwd(dh_ref[...], xh, rs, ng_ref[...], D)
        dx_ref[...] = go_ref[...] + dx
        dng_ref[...] += dg

    row = pl.BlockSpec((t, D), lambda i: (i, 0))
    return _call(
        body, name=name, grid=(s // t,),
        in_specs=[row, row, row, _full((1, D))], out_specs=[row, _full((1, D))],
        out_shape=[jax.ShapeDtypeStruct((s, D), F32), jax.ShapeDtypeStruct((1, D), F32)],
        compiler_params=_cp(("arbitrary",)),
    )(x, g_out, dh, ng)


def _allgather(arrs, name):
    n = len(arrs)

    def body(*refs):
        x_refs, out_refs = refs[:n], refs[n:2 * n]
        send_sems, recv_sems, local_sems = refs[2 * n:]
        x, y, c = lax.axis_index("x"), lax.axis_index("y"), lax.axis_index("c")
        me, sibling = (x, y, c), (x, y, 1 - c)
        chips = [(1 - x, y), (x, 1 - y), (1 - x, 1 - y)]

        def slot(a, px, py, pc):
            return out_refs[a].at[4 * px + 2 * py + pc]

        def copy(a, k, block, to, src=None):
            return pltpu.make_async_remote_copy(
                src_ref=slot(a, *block) if src is None else src, dst_ref=slot(a, *block),
                send_sem=send_sems.at[a, k], recv_sem=recv_sems.at[a, k],
                device_id=to, device_id_type=pl.DeviceIdType.MESH)

        mine = [pltpu.make_async_copy(x_refs[a], slot(a, *me), local_sems.at[a]) for a in range(n)]
        first, passed = [], []
        for a in range(n):
            mine[a].start()
            first.append(copy(a, 0, me, sibling, src=x_refs[a]))
            first += [copy(a, 1 + j, me, (*chip, c), src=x_refs[a]) for j, chip in enumerate(chips)]
        for cp in first:
            cp.start()
        for j, chip in enumerate(chips):
            for a in range(n):
                copy(a, 1 + j, (*chip, c), me).wait_recv()
                passed.append(copy(a, 4 + j, (*chip, c), sibling))
                passed[-1].start()
        for a in range(n):
            copy(a, 0, sibling, me).wait_recv()
        for j, chip in enumerate(chips):
            for a in range(n):
                copy(a, 4 + j, (*chip, 1 - c), me).wait_recv()
        for cp in first + passed:
            cp.wait_send()
        for cp in mine:
            cp.wait()

    any_spec = pl.BlockSpec(memory_space=pl.ANY)
    return _call(
        body, name=name,
        in_specs=[any_spec] * n, out_specs=[any_spec] * n,
        out_shape=[jax.ShapeDtypeStruct((N_DEV,) + a.shape, a.dtype) for a in arrs],
        scratch_shapes=[pltpu.SemaphoreType.DMA((n, 7)), pltpu.SemaphoreType.DMA((n, 7)),
                        pltpu.SemaphoreType.DMA((n,))],
    )(*arrs)


def _pair_exchange(arrs, name):
    n = len(arrs)

    def body(*refs):
        x_refs, out_refs = refs[:n], refs[n:2 * n]
        send_sems, recv_sems = refs[2 * n:]
        x, y, c = lax.axis_index("x"), lax.axis_index("y"), lax.axis_index("c")
        copies = [pltpu.make_async_remote_copy(
            src_ref=x_refs[a].at[1 - c], dst_ref=out_refs[a],
            send_sem=send_sems.at[a], recv_sem=recv_sems.at[a],
            device_id=(x, y, 1 - c), device_id_type=pl.DeviceIdType.MESH) for a in range(n)]
        for cp in copies:
            cp.start()
        for cp in copies:
            cp.wait_recv()
        for cp in copies:
            cp.wait_send()

    any_spec = pl.BlockSpec(memory_space=pl.ANY)
    return _call(
        body, name=name,
        in_specs=[any_spec] * n, out_specs=[any_spec] * n,
        out_shape=[jax.ShapeDtypeStruct(a.shape[1:], a.dtype) for a in arrs],
        scratch_shapes=[pltpu.SemaphoreType.DMA((n,)), pltpu.SemaphoreType.DMA((n,))],
    )(*arrs)


def _chip_exchange(arrs, name):
    n = len(arrs)

    def body(*refs):
        x_refs, out_refs = refs[:n], refs[n:2 * n]
        send_sems, recv_sems, local_sems = refs[2 * n:]
        x, y, c = lax.axis_index("x"), lax.axis_index("y"), lax.axis_index("c")
        me = 2 * x + y
        mine = [pltpu.make_async_copy(x_refs[a].at[me], out_refs[a].at[me], local_sems.at[a]) for a in range(n)]
        for cp in mine:
            cp.start()
        copies = []
        for k, (px, py) in enumerate([(1 - x, y), (x, 1 - y), (1 - x, 1 - y)]):
            for a in range(n):
                copies.append(pltpu.make_async_remote_copy(
                    src_ref=x_refs[a].at[2 * px + py], dst_ref=out_refs[a].at[me],
                    send_sem=send_sems.at[a, k], recv_sem=recv_sems.at[a, k],
                    device_id=(px, py, c), device_id_type=pl.DeviceIdType.MESH))
        for cp in copies:
            cp.start()
        for cp in copies:
            cp.wait_recv()
        for cp in copies:
            cp.wait_send()
        for cp in mine:
            cp.wait()

    any_spec = pl.BlockSpec(memory_space=pl.ANY)
    return _call(
        body, name=name,
        in_specs=[any_spec] * n, out_specs=[any_spec] * n,
        out_shape=[jax.ShapeDtypeStruct(a.shape, a.dtype) for a in arrs],
        scratch_shapes=[pltpu.SemaphoreType.DMA((n, 3)), pltpu.SemaphoreType.DMA((n, 3)),
                        pltpu.SemaphoreType.DMA((n,))],
    )(*arrs)


def _pair_add(a, b, name):
    r, c_ = a.shape
    cpad = -(-c_ // LANES) * LANES
    tr = r
    while tr * cpad * 4 > ADAMW_BLOCK_BYTES and tr % 32 == 0:
        tr //= 2

    def body(a_ref, b_ref, o_ref):
        o_ref[...] = (a_ref[...].astype(F32) + b_ref[...].astype(F32)).astype(o_ref.dtype)

    row = pl.BlockSpec((tr, c_), lambda i: (i, 0))
    return _call(
        body, name=name, grid=(r // tr,), in_specs=[row, row], out_specs=row,
        out_shape=jax.ShapeDtypeStruct((r, c_), a.dtype), compiler_params=_cp(("parallel",)),
    )(a, b)


ADAMW_BLOCK_BYTES = 4 * 1024 * 1024


def _adamw(parts, w, m, v, name):
    r, c_ = w.shape
    n_parts = parts.shape[0]
    cpad = -(-c_ // LANES) * LANES
    tr = r
    while N_DEV * tr * cpad * 4 > ADAMW_BLOCK_BYTES and tr % 16 == 0:
        tr //= 2
    c1 = 1.0 / (1.0 - ADAM_B1 ** ADAM_STEP)
    c2 = 1.0 / (1.0 - ADAM_B2 ** ADAM_STEP)

    def body(p_ref, w_ref, m_ref, v_ref, g_ref, d_ref, nm_ref, nv_ref):
        g = p_ref[0].astype(F32)
        for j in range(1, n_parts):
            g = g + p_ref[j].astype(F32)
        nm = ADAM_B1 * m_ref[...] + (1.0 - ADAM_B1) * g
        nv = ADAM_B2 * v_ref[...] + (1.0 - ADAM_B2) * (g * g)
        g_ref[...] = g
        nm_ref[...] = nm
        nv_ref[...] = nv
        d_ref[...] = -ADAM_LR * ((nm * c1) / (jnp.sqrt(nv * c2) + ADAM_EPS) + ADAM_WD * w_ref[...])

    row = pl.BlockSpec((tr, c_), lambda i: (i, 0))
    return _call(
        body, name=name, grid=(r // tr,),
        in_specs=[pl.BlockSpec((n_parts, tr, c_), lambda i: (0, i, 0)), row, row, row],
        out_specs=[row] * 4, out_shape=[jax.ShapeDtypeStruct((r, c_), F32)] * 4,
        compiler_params=_cp(("parallel",)),
    )(parts, w, m, v)


def _to_rows(flat, align=8):
    n = flat.shape[-1]
    rows = -(-n // (LANES * align)) * align
    return jnp.pad(flat, (0, rows * LANES - n)).reshape(rows, LANES)


def _shard_blocks(full, axis):
    r, c_ = full.shape
    if axis == 0:
        return full.reshape(4, 2, r // N_DEV, c_).transpose(1, 0, 2, 3)
    return full.reshape(r, 4, 2, c_ // N_DEV).transpose(2, 1, 0, 3)


def _unshard_blocks(blocks, axis):
    _, r, c_ = blocks.shape
    if axis == 0:
        return blocks.reshape(N_DEV * r, c_)
    return blocks.transpose(1, 0, 2).reshape(r, N_DEV * c_)


def _pad_heads(w, width):
    k = w.shape[0]
    return jnp.pad(w.reshape(k, H, width), ((0, 0), (0, 0), (0, HP - width))).reshape(k, H * HP)


def _unpad_heads(w, width):
    k = w.shape[0]
    return w.reshape(k, H, HP)[:, :, :width].reshape(k, H * width)


IN_SPLIT = {'q_lat': (0, 384), 'kv_lat': (384, 640), 'k_pe': (640, 672), 'c_b': (672, 1184), 'c_c': (1184, 1696),
            'c_u': (1696, 2208), 'q_mem': (2208, 2720), 'g_attn': (2720, 3232), 'g_conv': (3232, 3744),
            'g_mem': (3744, 4256), 'r': (4256, 7328)}
P_ORDER = ['c_b', 'c_c', 'c_u', 'g_conv', 'q_mem', 'g_mem', 'r', 'g_attn', 'q_lat', 'kv_lat']


def _permute_w_in(w):
    k = w.shape[0]
    cols = [w[:, IN_SPLIT[n][0]:IN_SPLIT[n][1]] for n in P_ORDER]
    kpe = w[:, IN_SPLIT['k_pe'][0]:IN_SPLIT['k_pe'][1]]
    cols += [jnp.zeros((k, NOPE), w.dtype), kpe, jnp.zeros((k, HP - QKD), w.dtype)]
    return jnp.concatenate(cols, axis=1)


def _unpermute_w_in(pieces):
    bounds, off = [], 0
    for p in pieces:
        bounds.append((off, off + p.shape[1]))
        off += p.shape[1]

    def cols(lo, hi):
        for p, (b0, b1) in zip(pieces, bounds):
            if b0 <= lo and hi <= b1:
                return p[:, lo - b0:hi - b0]
        raise ValueError("a column range straddles two pieces")

    off, pos = 0, {}
    for n in P_ORDER:
        wd = IN_SPLIT[n][1] - IN_SPLIT[n][0]
        pos[n] = (off, off + wd)
        off += wd
    pos['k_pe'] = (off + NOPE, off + QKD)
    order = sorted(IN_SPLIT, key=lambda n: IN_SPLIT[n][0])
    return jnp.concatenate([cols(*pos[n]) for n in order], axis=1)


def _layer_weights(full, l):
    w = {}
    w_in_p = _permute_w_in(full['w_in'][l])
    w['w_main'] = w_in_p[:, :P_MAIN]
    w['w_l'] = w_in_p[:, P_MAIN:]
    w['w_in_t'] = w_in_p.T
    w['w_uq'] = _pad_heads(full['w_uq'][l], QKD)
    wukv = full['w_ukv'][l].reshape(KVL, H, 2, NOPE)
    kpart = jnp.pad(wukv[:, :, 0, :], ((0, 0), (0, 0), (0, HP - NOPE))).reshape(KVL, H * HP)
    vpart = jnp.pad(wukv[:, :, 1, :], ((0, 0), (0, 0), (0, HP - VD))).reshape(KVL, H * HP)
    w['w_ukv'] = jnp.concatenate([kpart, vpart], axis=1)
    w['w_uq_t'] = w['w_uq'].T
    w['w_ukv_t'] = w['w_ukv'].T
    w['w_mkv'] = full['w_mkv'][l]
    w['w_mkv_t'] = w['w_mkv'].T
    for n in ('w_br_attn', 'w_br_conv', 'w_br_mem', 'w_out'):
        w[n] = full[n][l]
        w[n + '_t'] = w[n].T
    return w


def kernel(x, mem, positions, norm_g, w_in, b_gate, q_norm_g, w_uq, kv_norm_g, w_ukv, q_head_g, k_head_g, conv_w, conv_b, mem_norm_g, w_mkv, mem_q_g, mem_k_g, w_br_attn, w_br_conv, w_br_mem, w_out, loss_target, m_norm_g, m_w_in, m_b_gate, m_q_norm_g, m_w_uq, m_kv_norm_g, m_w_ukv, m_q_head_g, m_k_head_g, m_conv_w, m_conv_b, m_mem_norm_g, m_w_mkv, m_mem_q_g, m_mem_k_g, m_w_br_attn, m_w_br_conv, m_w_br_mem, m_w_out, v_norm_g, v_w_in, v_b_gate, v_q_norm_g, v_w_uq, v_kv_norm_g, v_w_ukv, v_q_head_g, v_k_head_g, v_conv_w, v_conv_b, v_mem_norm_g, v_w_mkv, v_mem_q_g, v_mem_k_g, v_w_br_attn, v_w_br_conv, v_w_br_mem, v_w_out):
    a = dict(zip(INPUTS, (x, mem, positions, norm_g, w_in, b_gate, q_norm_g, w_uq, kv_norm_g, w_ukv, q_head_g, k_head_g, conv_w, conv_b, mem_norm_g, w_mkv, mem_q_g, mem_k_g, w_br_attn, w_br_conv, w_br_mem, w_out, loss_target, m_norm_g, m_w_in, m_b_gate, m_q_norm_g, m_w_uq, m_kv_norm_g, m_w_ukv, m_q_head_g, m_k_head_g, m_conv_w, m_conv_b, m_mem_norm_g, m_w_mkv, m_mem_q_g, m_mem_k_g, m_w_br_attn, m_w_br_conv, m_w_br_mem, m_w_out, v_norm_g, v_w_in, v_b_gate, v_q_norm_g, v_w_uq, v_kv_norm_g, v_w_ukv, v_q_head_g, v_k_head_g, v_conv_w, v_conv_b, v_mem_norm_g, v_w_mkv, v_mem_q_g, v_mem_k_g, v_w_br_attn, v_w_br_conv, v_w_br_mem, v_w_out)))
    x = a['x'][0]
    mem = a['mem'][0]
    tgt = a['loss_target'][0]
    s = x.shape[0]
    t_el = 512
    t_br = 512
    tq_f, tk_f, tq_b, tk_b = 512, 1024, 512, 512

    gathered = _allgather([a[n].astype(BF16) for n in BIG_ORDER] + [a['conv_w']], "ag_weights")
    full = {n: [_unshard_blocks(g8[:, l], BIG[n][1]) for l in range(DEPTH)] for n, g8 in zip(BIG_ORDER, gathered)}
    conv_w = gathered[-1].transpose(1, 2, 0, 3).reshape(DEPTH, 3, CW)

    inv_freq = ROPE_BASE ** (-jnp.arange(0, RP, 2, dtype=F32) / RP)
    ang = a['positions'][0].astype(F32)[:, None] * inv_freq
    cos, sin = jnp.cos(ang), jnp.sin(ang)
    rc = jnp.concatenate([jnp.ones((s, NOPE), F32), cos, cos, jnp.ones((s, HP - QKD), F32)], axis=1)
    rs = jnp.concatenate([jnp.zeros((s, NOPE), F32), -sin, sin, jnp.zeros((s, HP - QKD), F32)], axis=1)

    def small(n, l, width=None):
        v = a[n][l][None, :]
        return v if width is None else jnp.pad(v, ((0, 0), (0, width - v.shape[1])))

    saved = []
    layer_w = [_layer_weights(full, l) for l in range(DEPTH)]
    for l in range(DEPTH):
        w = layer_w[l]
        tag = ""
        h, ht = _rms_h(x, small('norm_g', l), 512, "rms_h" + tag)
        proj = _mm(h, w['w_main'], BF16, 2048, 512, "in_proj" + tag)
        proj_l = _mm(h, w['w_l'], BF16, 512, P_LW, "in_proj_lat" + tag)
        qng, kvng = small('q_norm_g', l), small('kv_norm_g', l)
        qhg, khg = small('q_head_g', l, HP), small('k_head_g', l, HP)
        q, k, v = _mla_prep(proj_l, rc, rs, qng, kvng, qhg, khg, w['w_uq'], w['w_ukv'], t_br, "mla_prep" + tag)
        o, lse = _flash_fwd(q, k, v, tq_f, tk_f, "flash_fwd" + tag)
        mk, mv = _mem_prep(mem, small('mem_norm_g', l), w['w_mkv'], small('mem_k_g', l), "mem_prep" + tag)
        cb, mqg = small('conv_b', l), small('mem_q_g', l)
        oa, oc, om = _branches(proj, o, mk, mv, conv_w[l], cb, mqg, t_br, "branches" + tag)
        x_out, aa, ac, am = _merge_fwd(x, proj, oa, oc, om, small('b_gate', l), w['w_br_attn'], w['w_br_conv'],
                                       w['w_br_mem'], w['w_out'], t_el, "merge" + tag)
        saved.append(dict(x=x, ht=ht, proj=proj, proj_l=proj_l, q=q, k=k, v=v, o=o, lse=lse, mk=mk, mv=mv,
                          oa=oa, oc=oc, om=om, aa=aa, ac=ac, am=am))
        x = x_out

    g, loss_parts = _loss_grad(x, tgt, 512, "loss")
    loss = lax.psum(jnp.sum(loss_parts), ("x", "y", "c"))

    gw = {n: [None] * DEPTH for n in WEIGHTS}
    for l in reversed(range(DEPTH)):
        w = layer_w[l]
        sv = saved[l]
        tag = ""
        tqb = min(tq_b, s)
        y, daa, dac, dam, dr, doa, doc, dom, dbg = _merge_bwd(
            g, sv['proj'], sv['aa'], sv['ac'], sv['am'], small('b_gate', l), w['w_out_t'], w['w_br_attn_t'],
            w['w_br_conv_t'], w['w_br_mem_t'], t_el, "merge_bwd" + tag)
        gw['b_gate'][l] = dbg[0]
        gw['w_out'][l] = _mm_acc(y, g, 1024, D, "dw_out" + tag)
        gw['w_br_attn'][l] = _tn(sv['oa'], daa, 512, D, "dw_attn" + tag)
        gw['w_br_conv'][l] = _tn(sv['oc'], dac, 512, D, "dw_conv" + tag)
        gw['w_br_mem'][l] = _tn(sv['om'], dam, 512, D, "dw_mem" + tag)
        cb, mqg = small('conv_b', l), small('mem_q_g', l)
        dcv, dmm, dga, do, dl, dcw, dcb, dmk, dmv, dmqg = _branches_bwd(
            sv['proj'], sv['o'], sv['mk'], sv['mv'], conv_w[l], cb, mqg, doa, doc, dom, t_br, "branches_bwd" + tag)
        gw['conv_w'][l], gw['conv_b'][l], gw['mem_q_g'][l] = dcw, dcb[0], dmqg[0]
        dwm, dmng, dmkg = _mem_prep_bwd(mem, small('mem_norm_g', l), w['w_mkv'], w['w_mkv_t'], small('mem_k_g', l),
                                        dmk, dmv, "mem_prep_bwd" + tag)
        gw['w_mkv'][l], gw['mem_norm_g'][l], gw['mem_k_g'][l] = dwm, dmng[0], dmkg[0]
        lse_r = sv['lse'].reshape(H, s // tqb, 1, tqb)
        dl_r = dl.reshape(H, s // tqb, 1, tqb)
        dq, dk, dv = _flash_bwd(sv['q'], sv['k'], sv['v'], do, lse_r, dl_r, tq_b, tk_b, "flash_bwd" + tag)
        qng, kvng = small('q_norm_g', l), small('kv_norm_g', l)
        qhg, khg = small('q_head_g', l, HP), small('k_head_g', l, HP)
        dlat, dwuq, dwukv, dqng, dkvng, dqhg, dkhg = _mla_prep_bwd(
            sv['proj_l'], rc, rs, qng, kvng, qhg, khg, w['w_uq'], w['w_ukv'], w['w_uq_t'], w['w_ukv_t'],
            dq, dk, dv, t_br, "mla_prep_bwd" + tag)
        gw['w_uq'][l] = _unpad_heads(dwuq, QKD)
        dwukv = dwukv.reshape(KVL, 2, H, HP)[:, :, :, :NOPE]
        gw['w_ukv'][l] = dwukv.transpose(0, 2, 1, 3).reshape(KVL, H * 2 * NOPE)
        gw['q_norm_g'][l], gw['kv_norm_g'][l] = dqng[0], dkvng[0]
        gw['q_head_g'][l], gw['k_head_g'][l] = dqhg[0, :QKD], dkhg[0, :QKD]
        dpieces = [dcv, dmm, dr, dga, dlat]
        dh = _mm_pieces(dpieces, w['w_in_t'], 512, 512, "d_h" + tag)
        gw['w_in'][l] = _unpermute_w_in([
            _mm_acc(sv['ht'], p, 1024, tn, f"dw_in_{i}" + tag)
            for i, (p, tn) in enumerate(zip(dpieces, (1024, 1024, 1536, 512, P_LW)))])
        g, dng = _rms_in_bwd(sv['x'], g, dh, small('norm_g', l), 512, "rms_bwd" + tag)
        gw['norm_g'][l] = dng[0]
    grad_x = g[None]

    sharded = BIG_ORDER + ['conv_w']
    axis_of = lambda n: 1 if n == 'conv_w' else BIG[n][1]
    send = [jnp.stack([_shard_blocks(gw[n][l], axis_of(n)) for l in range(DEPTH)], axis=2)
            .astype(F32 if n == 'conv_w' else BF16) for n in sharded]
    from_sibling = _pair_exchange(send, "rs_pair")
    my_c = lax.axis_index("c")
    chip_sums = []
    for n, t, got in zip(sharded, send, from_sibling):
        own = lax.dynamic_index_in_dim(t, my_c, axis=0, keepdims=False)
        flat = lambda u: u.reshape(-1, u.shape[-1])
        chip_sums.append(_pair_add(flat(own), flat(got), "rs_add_" + n).reshape(got.shape))
    parts_big = _chip_exchange(chip_sums, "rs_chips")
    small_flat = jnp.concatenate([jnp.stack(gw[n]).reshape(-1) for n in SMALL_ORDER])
    n_small = small_flat.shape[0]
    parts_small = _allgather([_to_rows(small_flat)], "ag_small_grads")[0]

    outs = [{} for _ in range(4)]
    for n, parts in zip(sharded, parts_big):
        loc = a[n].shape
        two_d = lambda t: t.reshape(loc[0] * loc[1], loc[2])
        res = _adamw(parts.reshape(4, loc[0] * loc[1], loc[2]), two_d(a[n]), two_d(a['m_' + n]),
                     two_d(a['v_' + n]), "adamw_" + n)
        for d, r in zip(outs, res):
            d[n] = r.reshape(loc)
    pks = lambda pre: _to_rows(jnp.concatenate([a[pre + n].reshape(-1) for n in SMALL_ORDER]))
    res_small = _adamw(parts_small, pks(''), pks('m_'), pks('v_'), "adamw_small")
    for d, rsm in zip(outs, res_small):
        flat = rsm.reshape(-1)[:n_small]
        off = 0
        for n in SMALL_ORDER:
            d[n] = flat[off:off + DEPTH * SMALL[n]].reshape(DEPTH, SMALL[n])
            off += DEPTH * SMALL[n]
    result = [loss, grad_x]
    for d in outs:
        result += [d[n] for n in WEIGHTS]
    return tuple(result)
```
